```python
import math
import jax, jax.numpy as jnp
from jax import lax
import numpy as np

D_MODEL = 1024
BATCH = 8
SEQ = 2048
DEPTH = 1
DEC_BATCH = 128
DEC_SEQ = 8
PAST_LEN = 16384
PAGE_SIZE = 128

EPS = 1e-5
W_MIX = D_MODEL
W_CONV = W_MIX // 2
W_SSM = W_MIX - W_CONV
CONV_W = 3
SSM_H = 16
N_GROUPS = W_SSM // SSM_H
STATE_P = 64
D_FF = ((8 * D_MODEL // 3 + 255) // 256) * 256

kernel_name = 'hybrid_shortconv_s5_decode_step'


def rms_norm(x, g):
    xf = x.astype(jnp.float32)
    y = xf * lax.rsqrt(jnp.mean(xf * xf, axis=-1, keepdims=True) + EPS)
    return (y * g.astype(jnp.float32)).astype(x.dtype)


def short_conv_group(b_g, c_g, xc, conv_past, conv_w):
    L = xc.shape[1]
    v = c_g * xc
    v_ext = jnp.concatenate([conv_past.astype(v.dtype), v], axis=1)
    conv = sum(conv_w[k] * v_ext[:, k:k + L] for k in range(CONV_W))
    return b_g * conv, v_ext[:, L:]


def _ssm_combine(left, right):
    a_l, b_l = left
    a_r, b_r = right
    return a_r * a_l, a_r * b_l + b_r


def s5_group(u, h0_re, h0_im, lam_re, lam_im, log_dt, b_re, b_im, c_re, c_im, d_skip, w_glu, b_glu):
    f32 = jnp.float32
    bsz, L, _ = u.shape
    uf = u.astype(f32).reshape(bsz, L, N_GROUPS, SSM_H)
    lam = lax.complex(lam_re.astype(f32), lam_im.astype(f32))
    dt = jnp.exp(log_dt.astype(f32))[:, None]
    lam_bar = jnp.exp(lam * dt)
    b_mat = lax.complex(b_re.astype(f32), b_im.astype(f32))
    b_bar = ((lam_bar - 1.0) / lam)[..., None] * b_mat
    c_mat = lax.complex(c_re.astype(f32), c_im.astype(f32))
    bu = jnp.einsum('gph,blgh->blgp', b_bar, uf.astype(jnp.complex64))
    h0 = lax.complex(h0_re.astype(f32), h0_im.astype(f32))
    bu = bu.at[:, 0].add(lam_bar * h0)
    a = jnp.broadcast_to(lam_bar, bu.shape)
    _, h = lax.associative_scan(_ssm_combine, (a, bu), axis=1)
    y = jnp.einsum('ghp,blgp->blgh', c_mat, h).real + d_skip.astype(f32).reshape(N_GROUPS, SSM_H) * uf
    z = jax.nn.gelu(y.reshape(bsz, L, W_SSM))
    gate = jax.nn.sigmoid(z @ w_glu.astype(f32) + b_glu.astype(f32))
    out = (z * gate).astype(u.dtype)
    h_last = h[:, -1]
    return out, jnp.real(h_last).astype(h0_re.dtype), jnp.imag(h_last).astype(h0_re.dtype)


def hybrid_layer(x, conv_past, h_re, h_im, norm_mix, w_in, conv_w, lam_re, lam_im, log_dt,
                 b_re, b_im, c_re, c_im, d_skip, w_glu, b_glu, w_out,
                 norm_ffn, w_gate, w_up, w_down):
    xn = rms_norm(x, norm_mix)
    proj = xn @ w_in
    b_g, c_g, xc, u = jnp.split(proj, [W_CONV, 2 * W_CONV, 3 * W_CONV], axis=-1)
    conv_out, conv_new = short_conv_group(b_g, c_g, xc, conv_past, conv_w)
    ssm_out, hr, hi = s5_group(u, h_re, h_im, lam_re, lam_im, log_dt, b_re, b_im,
                               c_re, c_im, d_skip, w_glu, b_glu)
    x = x + jnp.concatenate([conv_out, ssm_out], axis=-1) @ w_out
    hn = rms_norm(x, norm_ffn)
    x = x + (jax.nn.silu(hn @ w_gate) * (hn @ w_up)) @ w_down
    return x, conv_new, hr, hi


def setup_inputs(seed: int = 0) -> dict:
    key = jax.random.key(seed)
    ks = jax.random.split(key, 26)
    nrm = jax.random.normal
    lam_im_base = math.pi * jnp.arange(STATE_P, dtype=jnp.float32)
    return {
        'x_prompt': nrm(ks[0], (BATCH, SEQ, D_MODEL), jnp.float32),
        'x_sample': nrm(ks[1], (DEC_BATCH, DEC_SEQ, D_MODEL), jnp.float32),
        'cache_conv': nrm(ks[2], (DEPTH, DEC_BATCH, CONV_W - 1, W_CONV), jnp.float32),
        'state_ssm_re': 0.5 * nrm(ks[3], (DEPTH, DEC_BATCH, N_GROUPS, STATE_P), jnp.float32),
        'state_ssm_im': 0.5 * nrm(ks[4], (DEPTH, DEC_BATCH, N_GROUPS, STATE_P), jnp.float32),
        'norm_mix': 1.0 + 0.02 * nrm(ks[5], (DEPTH, D_MODEL), jnp.float32),
        'w_in': nrm(ks[6], (DEPTH, D_MODEL, 3 * W_CONV + W_SSM), jnp.float32) * D_MODEL ** -0.5,
        'conv_w': nrm(ks[7], (DEPTH, CONV_W, W_CONV), jnp.float32) * CONV_W ** -0.5,
        'ssm_lam_re': -0.5 + 0.01 * nrm(ks[8], (DEPTH, N_GROUPS, STATE_P), jnp.float32),
        'ssm_lam_im': lam_im_base + 0.01 * nrm(ks[9], (DEPTH, N_GROUPS, STATE_P), jnp.float32),
        'ssm_log_dt': jax.random.uniform(ks[10], (DEPTH, N_GROUPS), jnp.float32,
                                         minval=math.log(1e-3), maxval=math.log(1e-1)),
        'ssm_b_re': nrm(ks[11], (DEPTH, N_GROUPS, STATE_P, SSM_H), jnp.float32) * (2 * SSM_H) ** -0.5,
        'ssm_b_im': nrm(ks[12], (DEPTH, N_GROUPS, STATE_P, SSM_H), jnp.float32) * (2 * SSM_H) ** -0.5,
        'ssm_c_re': nrm(ks[13], (DEPTH, N_GROUPS, SSM_H, STATE_P), jnp.float32) * STATE_P ** -0.5,
        'ssm_c_im': nrm(ks[14], (DEPTH, N_GROUPS, SSM_H, STATE_P), jnp.float32) * STATE_P ** -0.5,
        'ssm_d': nrm(ks[15], (DEPTH, W_SSM), jnp.float32),
        'w_glu': nrm(ks[16], (DEPTH, W_SSM, W_SSM), jnp.float32) * W_SSM ** -0.5,
        'b_glu': 0.01 * nrm(ks[17], (DEPTH, W_SSM), jnp.float32),
        'w_out': nrm(ks[18], (DEPTH, W_MIX, D_MODEL), jnp.float32) * W_MIX ** -0.5,
        'norm_ffn': 1.0 + 0.02 * nrm(ks[19], (DEPTH, D_MODEL), jnp.float32),
        'w_gate': nrm(ks[20], (DEPTH, D_MODEL, D_FF), jnp.float32) * D_MODEL ** -0.5,
        'w_up': nrm(ks[21], (DEPTH, D_MODEL, D_FF), jnp.float32) * D_MODEL ** -0.5,
        'w_down': nrm(ks[22], (DEPTH, D_FF, D_MODEL), jnp.float32) * D_FF ** -0.5,
        'norm_final': 1.0 + 0.02 * nrm(ks[23], (D_MODEL,), jnp.float32),
    }


def reference(x_prompt, x_sample, cache_conv, state_ssm_re, state_ssm_im,
              norm_mix, w_in, conv_w, ssm_lam_re, ssm_lam_im, ssm_log_dt,
              ssm_b_re, ssm_b_im, ssm_c_re, ssm_c_im, ssm_d, w_glu, b_glu, w_out,
              norm_ffn, w_gate, w_up, w_down, norm_final):
    bp = x_prompt.shape[0]
    xp, xs = x_prompt, x_sample
    conv_p, re_p, im_p, conv_s, re_s, im_s = [], [], [], [], [], []
    for l in range(DEPTH):
        weights = (norm_mix[l], w_in[l], conv_w[l], ssm_lam_re[l], ssm_lam_im[l], ssm_log_dt[l],
                   ssm_b_re[l], ssm_b_im[l], ssm_c_re[l], ssm_c_im[l], ssm_d[l], w_glu[l], b_glu[l],
                   w_out[l], norm_ffn[l], w_gate[l], w_up[l], w_down[l])
        zc = jnp.zeros((bp, CONV_W - 1, W_CONV), xp.dtype)
        zh = jnp.zeros((bp, N_GROUPS, STATE_P), state_ssm_re.dtype)
        xp, c_new, hr, hi = hybrid_layer(xp, zc, zh, zh, *weights)
        conv_p.append(c_new); re_p.append(hr); im_p.append(hi)
        xs, c_new, hr, hi = hybrid_layer(xs, cache_conv[l], state_ssm_re[l], state_ssm_im[l], *weights)
        conv_s.append(c_new); re_s.append(hr); im_s.append(hi)
    y_prompt = rms_norm(xp, norm_final)
    y_sample = rms_norm(xs, norm_final)
    return (y_prompt, y_sample,
            jnp.stack(conv_p), jnp.stack(re_p), jnp.stack(im_p),
            jnp.stack(conv_s), jnp.stack(re_s), jnp.stack(im_s))
```

```python
import functools

import jax
import jax.numpy as jnp
from jax import lax
from jax.experimental import pallas as pl
from jax.experimental.pallas import tpu as pltpu

D_MODEL = 1024
W_CONV = 512
W_SSM = 512
CONV_W = 3
N_GROUPS = 32
SSM_H = 16
STATE_P = 64
D_FF = 2816
EPS = 1e-5

SUBLANES = 8
GROUPS_PER_HALF = 16
N_HALVES = N_GROUPS // GROUPS_PER_HALF
U_HALF = GROUPS_PER_HALF * SSM_H
S_HALF = GROUPS_PER_HALF * STATE_P
S_COLS = 2 * N_GROUPS * STATE_P
SCAN_LANES = 512
FFN_CHUNKS = ((0, 1024), (1024, 1024), (2048, 768))
VMEM_LIMIT = 56 * 1024 * 1024

BF16 = jnp.bfloat16
F32 = jnp.float32


def _rms(x, g):
    y = x * lax.rsqrt(jnp.mean(x * x, axis=-1, keepdims=True) + EPS)
    return y * g


def _dot(a, b):
    return jnp.dot(a, b, preferred_element_type=F32)


def _tile_copies(hbm, buf, sem, step, slot, *, nb, tc, per_seq, to_vmem):
    copies = []
    for k in range(SUBLANES):
        if per_seq:
            h = hbm.at[k, pl.ds(step * tc, tc), :]
            v = buf.at[slot, :, k, :]
        else:
            h = hbm.at[pl.ds(step * nb, nb), k, :]
            v = buf.at[slot, k]
        src, dst = (h, v) if to_vmem else (v, h)
        copies.append(pltpu.make_async_copy(src, dst, sem.at[slot, k]))
    return copies


def _mixer_body(x_hbm, cache_ref, re0_ref, im0_ref, nmix_ref, win_ref, convw_ref, lamr_ref, lami_ref,
                bc_ref, cc_ref, dskip_ref, wglu_ref, bglu_ref, wout_ref,
                x1_hbm, convnew_ref, hre_ref, him_ref,
                xbuf, obuf, hbuf, vbuf, insem, outsem, *, nb, tc, n_steps, carry):
    tm = nb * tc
    i = pl.program_id(0)
    slot = lax.rem(i, 2)
    cp = functools.partial(_tile_copies, nb=nb, tc=tc, per_seq=carry)
    in_copies = functools.partial(cp, x_hbm, xbuf, insem, to_vmem=True)
    out_copies = functools.partial(cp, x1_hbm, obuf, outsem, to_vmem=False)

    @pl.when(i == 0)
    def _():
        for c in in_copies(0, 0):
            c.start()

    for c in in_copies(i, slot):
        c.wait()

    @pl.when(i + 1 < n_steps)
    def _():
        for c in in_copies(i + 1, 1 - slot):
            c.start()

    def load_state():
        vbuf[0:2 * nb, :] = cache_ref[...].reshape(2 * nb, W_CONV)
        for n in range(N_HALVES):
            hbuf[0:nb, 2 * n * S_HALF:(2 * n + 1) * S_HALF] = re0_ref[:, n * S_HALF:(n + 1) * S_HALF]
            hbuf[0:nb, (2 * n + 1) * S_HALF:(2 * n + 2) * S_HALF] = im0_ref[:, n * S_HALF:(n + 1) * S_HALF]

    if carry:
        pl.when(i == 0)(load_state)
    else:
        load_state()

    x = xbuf[slot].reshape(tm, D_MODEL)
    xn = _rms(x, nmix_ref[...]).astype(BF16)
    proj = _dot(xn, win_ref[...])
    b_g = proj[:, 0:W_CONV]
    c_g = proj[:, W_CONV:2 * W_CONV]
    xc = proj[:, 2 * W_CONV:3 * W_CONV]
    u = proj[:, 3 * W_CONV:]

    v = c_g * xc
    vbuf[2 * nb:2 * nb + tm, :] = v
    conv = (convw_ref[0:1, :] * vbuf[0:tm, :] + convw_ref[1:2, :] * vbuf[nb:nb + tm, :]
            + convw_ref[2:3, :] * v)
    conv_out = (b_g * conv).astype(BF16)
    new_cache = vbuf[tm:tm + 2 * nb, :]
    convnew_ref[...] = new_cache.reshape(2, nb, W_CONV)
    vbuf[0:2 * nb, :] = new_cache

    ub = u.astype(BF16)
    for n in range(N_HALVES):
        hbuf[nb:nb + tm, 2 * n * S_HALF:(2 * n + 2) * S_HALF] = _dot(ub[:, n * U_HALF:(n + 1) * U_HALF], bc_ref[n])

    for n in range(N_HALVES):
        for c0 in range(0, S_HALF, SCAN_LANES):
            re_c = 2 * n * S_HALF + c0
            im_c = re_c + S_HALF
            lr = lamr_ref[n, :, c0:c0 + SCAN_LANES]
            li = lami_ref[n, :, c0:c0 + SCAN_LANES]
            for g in range(nb // SUBLANES):
                r0 = g * SUBLANES
                hr0 = hbuf[r0:r0 + SUBLANES, re_c:re_c + SCAN_LANES]
                hi0 = hbuf[r0:r0 + SUBLANES, im_c:im_c + SCAN_LANES]

                def step(t, h, r0=r0, re_c=re_c, im_c=im_c, lr=lr, li=li):
                    hr, hi = h
                    row = pl.multiple_of(nb + t * nb + r0, SUBLANES)
                    nr = lr * hr - li * hi + hbuf[pl.ds(row, SUBLANES), re_c:re_c + SCAN_LANES]
                    ni = lr * hi + li * hr + hbuf[pl.ds(row, SUBLANES), im_c:im_c + SCAN_LANES]
                    hbuf[pl.ds(row, SUBLANES), re_c:re_c + SCAN_LANES] = nr
                    hbuf[pl.ds(row, SUBLANES), im_c:im_c + SCAN_LANES] = ni
                    return nr, ni

                lax.fori_loop(0, tc, step, (hr0, hi0), unroll=8)

    last = hbuf[tm:tm + nb, :]
    for n in range(N_HALVES):
        hre_ref[:, n * S_HALF:(n + 1) * S_HALF] = last[:, 2 * n * S_HALF:(2 * n + 1) * S_HALF]
        him_ref[:, n * S_HALF:(n + 1) * S_HALF] = last[:, (2 * n + 1) * S_HALF:(2 * n + 2) * S_HALF]

    ys = []
    for n in range(N_HALVES):
        hb = hbuf[nb:nb + tm, 2 * n * S_HALF:(2 * n + 2) * S_HALF].astype(BF16)
        ys.append(_dot(hb, cc_ref[n]))
    hbuf[0:nb, :] = last
    y = jnp.concatenate(ys, axis=-1) + dskip_ref[...] * u
    z = jax.nn.gelu(y)
    gate = jax.nn.sigmoid(_dot(z.astype(BF16), wglu_ref[...]) + bglu_ref[...])
    ssm_out = (z * gate).astype(BF16)

    x1 = x + _dot(conv_out, wout_ref[0:W_CONV, :]) + _dot(ssm_out, wout_ref[W_CONV:, :])

    @pl.when(i >= 2)
    def _():
        for c in out_copies(i - 2, slot):
            c.wait()

    obuf[slot] = x1.reshape(tc, nb, D_MODEL)
    for c in out_copies(i, slot):
        c.start()

    @pl.when(i == n_steps - 1)
    def _():
        if n_steps >= 2:
            for c in out_copies(i - 1, 1 - slot):
                c.wait()
        for c in out_copies(i, slot):
            c.wait()


def _resident(shape):
    nd = len(shape)
    return pl.BlockSpec(shape, lambda i: (0,) * nd, pipeline_mode=pl.Buffered(1))


def _mixer(x, cache_t, re0, im0, wts, *, nb, tc, carry):
    n_seq, n_time, _ = x.shape
    if carry:
        assert n_seq == nb == SUBLANES and n_time % tc == 0
        n_steps = n_time // tc
        seq_blk = lambda i: 0
    else:
        assert n_time == tc == SUBLANES and n_seq % nb == 0
        n_steps = n_seq // nb
        seq_blk = lambda i: i
    tm = nb * tc
    n_st = N_GROUPS * STATE_P
    body = functools.partial(_mixer_body, nb=nb, tc=tc, n_steps=n_steps, carry=carry)
    any_spec = pl.BlockSpec(memory_space=pl.ANY)
    cache_spec = pl.BlockSpec((2, nb, W_CONV), lambda i: (0, seq_blk(i), 0))
    st_spec = pl.BlockSpec((nb, n_st), lambda i: (seq_blk(i), 0))
    return pl.pallas_call(
        body,
        grid=(n_steps,),
        in_specs=[any_spec, cache_spec, st_spec, st_spec] + [_resident(w.shape) for w in wts],
        out_specs=[any_spec, cache_spec, st_spec, st_spec],
        out_shape=[jax.ShapeDtypeStruct(x.shape, F32),
                   jax.ShapeDtypeStruct((2, n_seq, W_CONV), F32),
                   jax.ShapeDtypeStruct((n_seq, n_st), F32),
                   jax.ShapeDtypeStruct((n_seq, n_st), F32)],
        scratch_shapes=[pltpu.VMEM((2, tc, nb, D_MODEL), F32),
                        pltpu.VMEM((2, tc, nb, D_MODEL), F32),
                        pltpu.VMEM((nb + tm, S_COLS), F32),
                        pltpu.VMEM((2 * nb + tm, W_CONV), F32),
                        pltpu.SemaphoreType.DMA((2, SUBLANES)),
                        pltpu.SemaphoreType.DMA((2, SUBLANES))],
        compiler_params=pltpu.CompilerParams(dimension_semantics=("arbitrary",),
                                             vmem_limit_bytes=VMEM_LIMIT),
        name="mixer_carry" if carry else "mixer_step",
    )(x, cache_t, re0, im0, *wts)


def _ffn_body(x_ref, nffn_ref, wg_ref, wu_ref, wd_ref, nfin_ref, o_ref):
    x = x_ref[...]
    hn = _rms(x, nffn_ref[...]).astype(BF16)
    acc = x
    for s, w in FFN_CHUNKS:
        g = _dot(hn, wg_ref[:, s:s + w])
        up = _dot(hn, wu_ref[:, s:s + w])
        a = (g * jax.nn.sigmoid(g) * up).astype(BF16)
        acc = acc + _dot(a, wd_ref[s:s + w, :])
    o_ref[...] = _rms(acc, nfin_ref[...])


def _ffn(x2d, wts, *, tm):
    n_rows = x2d.shape[0]
    assert n_rows % tm == 0
    row_spec = pl.BlockSpec((tm, D_MODEL), lambda i: (i, 0))
    return pl.pallas_call(
        _ffn_body,
        grid=(n_rows // tm,),
        in_specs=[row_spec] + [_resident(w.shape) for w in wts],
        out_specs=row_spec,
        out_shape=jax.ShapeDtypeStruct(x2d.shape, F32),
        compiler_params=pltpu.CompilerParams(dimension_semantics=("arbitrary",),
                                             vmem_limit_bytes=VMEM_LIMIT),
        name="ffn",
    )(x2d, *wts)


def _s5_matrices(lam_re, lam_im, log_dt, b_re, b_im, c_re, c_im):
    lam = lax.complex(lam_re, lam_im)
    dt = jnp.exp(log_dt)[:, None]
    lam_bar = jnp.exp(lam * dt)
    b_bar = ((lam_bar - 1.0) / lam)[..., None] * lax.complex(b_re, b_im)
    eye = jnp.eye(GROUPS_PER_HALF, dtype=F32)

    def in_mat(b):
        b = b.reshape(N_HALVES, GROUPS_PER_HALF, STATE_P, SSM_H)
        return jnp.einsum('ngph,gk->nghkp', b, eye).reshape(N_HALVES, U_HALF, S_HALF)

    def out_mat(c):
        c = c.reshape(N_HALVES, GROUPS_PER_HALF, SSM_H, STATE_P)
        return jnp.einsum('nghp,gk->ngpkh', c, eye).reshape(N_HALVES, S_HALF, U_HALF)

    bc = jnp.concatenate([in_mat(jnp.real(b_bar)), in_mat(jnp.imag(b_bar))], axis=-1).astype(BF16)
    cc = jnp.concatenate([out_mat(c_re), out_mat(-c_im)], axis=1).astype(BF16)

    def lanes(a):
        return jnp.broadcast_to(a.reshape(N_HALVES, 1, S_HALF), (N_HALVES, SUBLANES, S_HALF))

    return lanes(jnp.real(lam_bar)), lanes(jnp.imag(lam_bar)), bc, cc


def kernel(x_prompt, x_sample, cache_conv, state_ssm_re, state_ssm_im, norm_mix, w_in, conv_w, ssm_lam_re, ssm_lam_im, ssm_log_dt, ssm_b_re, ssm_b_im, ssm_c_re, ssm_c_im, ssm_d, w_glu, b_glu, w_out, norm_ffn, w_gate, w_up, w_down, norm_final):
    assert norm_mix.shape[0] == 1, "single-layer trunk"
    n_st = N_GROUPS * STATE_P
    lamr, lami, bc, cc = _s5_matrices(ssm_lam_re[0], ssm_lam_im[0], ssm_log_dt[0], ssm_b_re[0], ssm_b_im[0],
                                      ssm_c_re[0], ssm_c_im[0])
    mix_w = (norm_mix, w_in[0].astype(BF16), conv_w[0], lamr, lami, bc, cc, ssm_d, w_glu[0].astype(BF16),
             b_glu, w_out[0].astype(BF16))
    ffn_w = (norm_ffn, w_gate[0].astype(BF16), w_up[0].astype(BF16), w_down[0].astype(BF16),
             norm_final.reshape(1, D_MODEL))

    bp, lp, _ = x_prompt.shape
    bs, ls, _ = x_sample.shape

    zc = jnp.zeros((2, bp, W_CONV), F32)
    zh = jnp.zeros((bp, n_st), F32)
    x1p, convp, rep, imp = _mixer(x_prompt, zc, zh, zh, mix_w, nb=SUBLANES, tc=64, carry=True)
    x1s, convs, res, ims = _mixer(x_sample, cache_conv[0].transpose(1, 0, 2), state_ssm_re[0].reshape(bs, n_st),
                                  state_ssm_im[0].reshape(bs, n_st), mix_w, nb=64, tc=SUBLANES, carry=False)

    yp = _ffn(x1p.reshape(bp * lp, D_MODEL), ffn_w, tm=1024).reshape(bp, lp, D_MODEL)
    ys = _ffn(x1s.reshape(bs * ls, D_MODEL), ffn_w, tm=512).reshape(bs, ls, D_MODEL)

    def cache_out(c):
        return c.transpose(1, 0, 2)[None]

    def state_out(h):
        return h.reshape(1, h.shape[0], N_GROUPS, STATE_P)

    return (yp, ys, cache_out(convp), state_out(rep), state_out(imp),
            cache_out(convs), state_out(res), state_out(ims))
```

```python
import functools

import jax
import jax.numpy as jnp
from jax import lax
from jax.experimental import pallas as pl
from jax.experimental.pallas import tpu as pltpu

D_MODEL = 1024
W_CONV = 512
W_SSM = 512
CONV_W = 3
N_GROUPS = 32
SSM_H = 16
STATE_P = 64
D_FF = 2816
EPS = 1e-5

SUBLANES = 8
GROUPS_PER_HALF = 16
N_HALVES = N_GROUPS // GROUPS_PER_HALF
U_HALF = GROUPS_PER_HALF * SSM_H
S_HALF = GROUPS_PER_HALF * STATE_P
S_COLS = 2 * N_GROUPS * STATE_P
SCAN_LANES = 512
FFN_CHUNKS = ((0, 1024), (1024, 1024), (2048, 768))
VMEM_LIMIT = 56 * 1024 * 1024

BF16 = jnp.bfloat16
F32 = jnp.float32


def _rms(x, g):
    y = x * lax.rsqrt(jnp.mean(x * x, axis=-1, keepdims=True) + EPS)
    return y * g


def _dot(a, b):
    return jnp.dot(a, b, preferred_element_type=F32)


def _tile_copies(hbm, buf, sem, step, slot, *, nb, tc, per_seq, to_vmem):
    copies = []
    for k in range(SUBLANES):
        if per_seq:
            h = hbm.at[k, pl.ds(step * tc, tc), :]
            v = buf.at[slot, :, k, :]
        else:
            h = hbm.at[pl.ds(step * nb, nb), k, :]
            v = buf.at[slot, k]
        src, dst = (h, v) if to_vmem else (v, h)
        copies.append(pltpu.make_async_copy(src, dst, sem.at[slot, k]))
    return copies


def _mixer_body(x_hbm, *refs, nb, tc, n_steps, carry):
    if carry:
        cache_ref = re0_ref = im0_ref = None
    else:
        cache_ref, re0_ref, im0_ref, *refs = refs
    (nmix_ref, win_ref, convw_ref, lamr_ref, lami_ref, bc_ref, cc_ref, dskip_ref, wglu_ref, bglu_ref, wout_ref,
     x1_hbm, convnew_ref, hre_ref, him_ref, xbuf, obuf, hbuf, vbuf, insem, outsem) = refs
    tm = nb * tc
    i = pl.program_id(0)
    slot = lax.rem(i, 2)
    cp = functools.partial(_tile_copies, nb=nb, tc=tc, per_seq=carry)
    in_copies = functools.partial(cp, x_hbm, xbuf, insem, to_vmem=True)
    out_copies = functools.partial(cp, x1_hbm, obuf, outsem, to_vmem=False)

    @pl.when(i == 0)
    def _():
        for c in in_copies(0, 0):
            c.start()

    for c in in_copies(i, slot):
        c.wait()

    @pl.when(i + 1 < n_steps)
    def _():
        for c in in_copies(i + 1, 1 - slot):
            c.start()

    if carry:
        @pl.when(i == 0)
        def _():
            vbuf[0:2 * nb, :] = jnp.zeros((2 * nb, W_CONV), F32)
            hbuf[0:nb, :] = jnp.zeros((nb, S_COLS), F32)
    else:
        vbuf[0:2 * nb, :] = cache_ref[...].reshape(2 * nb, W_CONV)
        for n in range(N_HALVES):
            hbuf[0:nb, 2 * n * S_HALF:(2 * n + 1) * S_HALF] = re0_ref[:, n * S_HALF:(n + 1) * S_HALF]
            hbuf[0:nb, (2 * n + 1) * S_HALF:(2 * n + 2) * S_HALF] = im0_ref[:, n * S_HALF:(n + 1) * S_HALF]

    x = xbuf[slot].reshape(tm, D_MODEL)
    xn = _rms(x, nmix_ref[...]).astype(BF16)
    proj = _dot(xn, win_ref[...])
    b_g = proj[:, 0:W_CONV]
    c_g = proj[:, W_CONV:2 * W_CONV]
    xc = proj[:, 2 * W_CONV:3 * W_CONV]
    u = proj[:, 3 * W_CONV:]

    v = c_g * xc
    vbuf[2 * nb:2 * nb + tm, :] = v
    conv = (convw_ref[0:1, :] * vbuf[0:tm, :] + convw_ref[1:2, :] * vbuf[nb:nb + tm, :]
            + convw_ref[2:3, :] * v)
    conv_out = (b_g * conv).astype(BF16)
    new_cache = vbuf[tm:tm + 2 * nb, :]
    convnew_ref[...] = new_cache.reshape(2, nb, W_CONV)
    vbuf[0:2 * nb, :] = new_cache

    ub = u.astype(BF16)
    for n in range(N_HALVES):
        hbuf[nb:nb + tm, 2 * n * S_HALF:(2 * n + 2) * S_HALF] = _dot(ub[:, n * U_HALF:(n + 1) * U_HALF], bc_ref[n])

    for n in range(N_HALVES):
        for c0 in range(0, S_HALF, SCAN_LANES):
            re_c = 2 * n * S_HALF + c0
            im_c = re_c + S_HALF
            lr = lamr_ref[n, :, c0:c0 + SCAN_LANES]
            li = lami_ref[n, :, c0:c0 + SCAN_LANES]
            for g in range(nb // SUBLANES):
                r0 = g * SUBLANES
                hr0 = hbuf[r0:r0 + SUBLANES, re_c:re_c + SCAN_LANES]
                hi0 = hbuf[r0:r0 + SUBLANES, im_c:im_c + SCAN_LANES]

                def step(t, h, r0=r0, re_c=re_c, im_c=im_c, lr=lr, li=li):
                    hr, hi = h
                    row = pl.multiple_of(nb + t * nb + r0, SUBLANES)
                    nr = lr * hr - li * hi + hbuf[pl.ds(row, SUBLANES), re_c:re_c + SCAN_LANES]
                    ni = lr * hi + li * hr + hbuf[pl.ds(row, SUBLANES), im_c:im_c + SCAN_LANES]
                    hbuf[pl.ds(row, SUBLANES), re_c:re_c + SCAN_LANES] = nr
                    hbuf[pl.ds(row, SUBLANES), im_c:im_c + SCAN_LANES] = ni
                    return nr, ni

                lax.fori_loop(0, tc, step, (hr0, hi0), unroll=8)

    last = hbuf[tm:tm + nb, :]
    for n in range(N_HALVES):
        hre_ref[:, n * S_HALF:(n + 1) * S_HALF] = last[:, 2 * n * S_HALF:(2 * n + 1) * S_HALF]
        him_ref[:, n * S_HALF:(n + 1) * S_HALF] = last[:, (2 * n + 1) * S_HALF:(2 * n + 2) * S_HALF]

    ys = []
    for n in range(N_HALVES):
        hb = hbuf[nb:nb + tm, 2 * n * S_HALF:(2 * n + 2) * S_HALF].astype(BF16)
        ys.append(_dot(hb, cc_ref[n]))
    hbuf[0:nb, :] = last
    y = jnp.concatenate(ys, axis=-1) + dskip_ref[...] * u
    z = jax.nn.gelu(y)
    gate = jax.nn.sigmoid(_dot(z.astype(BF16), wglu_ref[...]) + bglu_ref[...])
    ssm_out = (z * gate).astype(BF16)

    x1 = x + _dot(conv_out, wout_ref[0:W_CONV, :]) + _dot(ssm_out, wout_ref[W_CONV:, :])

    @pl.when(i >= 2)
    def _():
        for c in out_copies(i - 2, slot):
            c.wait()

    obuf[slot] = x1.reshape(tc, nb, D_MODEL)
    for c in out_copies(i, slot):
        c.start()

    @pl.when(i == n_steps - 1)
    def _():
        if n_steps >= 2:
            for c in out_copies(i - 1, 1 - slot):
                c.wait()
        for c in out_copies(i, slot):
            c.wait()


def _resident(shape):
    nd = len(shape)
    return pl.BlockSpec(shape, lambda i: (0,) * nd, pipeline_mode=pl.Buffered(1))


def _mixer(x, wts, state=None, *, nb, tc):
    n_seq, n_time, _ = x.shape
    carry = state is None
    if carry:
        assert n_seq == nb == SUBLANES and n_time % tc == 0
        n_steps = n_time // tc
        seq_blk = lambda i: 0
    else:
        assert n_time == tc == SUBLANES and n_seq % nb == 0
        n_steps = n_seq // nb
        seq_blk = lambda i: i
    tm = nb * tc
    n_st = N_GROUPS * STATE_P
    body = functools.partial(_mixer_body, nb=nb, tc=tc, n_steps=n_steps, carry=carry)
    any_spec = pl.BlockSpec(memory_space=pl.ANY)
    cache_spec = pl.BlockSpec((2, nb, W_CONV), lambda i: (0, seq_blk(i), 0))
    st_spec = pl.BlockSpec((nb, n_st), lambda i: (seq_blk(i), 0))
    state_specs = [] if carry else [cache_spec, st_spec, st_spec]
    return pl.pallas_call(
        body,
        grid=(n_steps,),
        in_specs=[any_spec] + state_specs + [_resident(w.shape) for w in wts],
        out_specs=[any_spec, cache_spec, st_spec, st_spec],
        out_shape=[jax.ShapeDtypeStruct(x.shape, F32),
                   jax.ShapeDtypeStruct((2, n_seq, W_CONV), F32),
                   jax.ShapeDtypeStruct((n_seq, n_st), F32),
                   jax.ShapeDtypeStruct((n_seq, n_st), F32)],
        scratch_shapes=[pltpu.VMEM((2, tc, nb, D_MODEL), F32),
                        pltpu.VMEM((2, tc, nb, D_MODEL), F32),
                        pltpu.VMEM((nb + tm, S_COLS), F32),
                        pltpu.VMEM((2 * nb + tm, W_CONV), F32),
                        pltpu.SemaphoreType.DMA((2, SUBLANES)),
                        pltpu.SemaphoreType.DMA((2, SUBLANES))],
        compiler_params=pltpu.CompilerParams(dimension_semantics=("arbitrary",),
                                             vmem_limit_bytes=VMEM_LIMIT),
        name="mixer_carry" if carry else "mixer_step",
    )(x, *(() if carry else state), *wts)


def _ffn_body(x_ref, nffn_ref, wg_ref, wu_ref, wd_ref, nfin_ref, o_ref):
    x = x_ref[...]
    hn = _rms(x, nffn_ref[...]).astype(BF16)
    acc = x
    for s, w in FFN_CHUNKS:
        g = _dot(hn, wg_ref[:, s:s + w])
        up = _dot(hn, wu_ref[:, s:s + w])
        a = (g * jax.nn.sigmoid(g) * up).astype(BF16)
        acc = acc + _dot(a, wd_ref[s:s + w, :])
    o_ref[...] = _rms(acc, nfin_ref[...])


def _ffn(x2d, wts, *, tm):
    n_rows = x2d.shape[0]
    assert n_rows % tm == 0
    row_spec = pl.BlockSpec((tm, D_MODEL), lambda i: (i, 0))
    return pl.pallas_call(
        _ffn_body,
        grid=(n_rows // tm,),
        in_specs=[row_spec] + [_resident(w.shape) for w in wts],
        out_specs=row_spec,
        out_shape=jax.ShapeDtypeStruct(x2d.shape, F32),
        compiler_params=pltpu.CompilerParams(dimension_semantics=("arbitrary",),
                                             vmem_limit_bytes=VMEM_LIMIT),
        name="ffn",
    )(x2d, *wts)


def _s5_prep_body(lre_ref, lim_ref, logdt_ref, btre_ref, btim_ref, cre_ref, cim_ref,
                  lamr_ref, lami_ref, bc_ref, cc_ref, bscr, cscr):
    lre = lre_ref[...]
    lim = lim_ref[...]
    dt = jnp.exp(logdt_ref[...])
    mag = jnp.exp(lre * dt)
    lbr = mag * jnp.cos(lim * dt)
    lbi = mag * jnp.sin(lim * dt)
    den = lre * lre + lim * lim
    fr = ((lbr - 1.0) * lre + lbi * lim) / den
    fi = (lbi * lre - (lbr - 1.0) * lim) / den
    frb = fr[:, None, :]
    fib = fi[:, None, :]
    bbr = frb * btre_ref[...] - fib * btim_ref[...]
    bbi = frb * btim_ref[...] + fib * btre_ref[...]
    cre = cre_ref[...]
    ncim = -cim_ref[...]

    bscr[...] = jnp.zeros_like(bscr)
    cscr[...] = jnp.zeros_like(cscr)
    for g in range(N_GROUPS):
        n, k = divmod(g, GROUPS_PER_HALF)
        rows = slice(k * SSM_H, (k + 1) * SSM_H)
        re_cols = slice(k * STATE_P, (k + 1) * STATE_P)
        im_cols = slice(S_HALF + k * STATE_P, S_HALF + (k + 1) * STATE_P)
        bscr[n, rows, re_cols] = bbr[g]
        bscr[n, rows, im_cols] = bbi[g]
        cscr[n, rows, re_cols] = cre[g]
        cscr[n, rows, im_cols] = ncim[g]
        lamr_ref[n, :, re_cols] = jnp.broadcast_to(lbr[g:g + 1, :], (SUBLANES, STATE_P))
        lami_ref[n, :, re_cols] = jnp.broadcast_to(lbi[g:g + 1, :], (SUBLANES, STATE_P))
    for n in range(N_HALVES):
        bc_ref[n] = bscr[n].astype(BF16)
        cc_ref[n] = cscr[n].T.astype(BF16)


def _s5_prep(lam_re, lam_im, log_dt, b_re, b_im, c_re, c_im):
    return pl.pallas_call(
        _s5_prep_body,
        out_shape=[jax.ShapeDtypeStruct((N_HALVES, SUBLANES, S_HALF), F32),
                   jax.ShapeDtypeStruct((N_HALVES, SUBLANES, S_HALF), F32),
                   jax.ShapeDtypeStruct((N_HALVES, U_HALF, 2 * S_HALF), BF16),
                   jax.ShapeDtypeStruct((N_HALVES, 2 * S_HALF, U_HALF), BF16)],
        scratch_shapes=[pltpu.VMEM((N_HALVES, U_HALF, 2 * S_HALF), F32),
                        pltpu.VMEM((N_HALVES, U_HALF, 2 * S_HALF), F32)],
        name="s5_prep",
    )(lam_re, lam_im, log_dt.reshape(N_GROUPS, 1), b_re.transpose(0, 2, 1), b_im.transpose(0, 2, 1), c_re, c_im)


def kernel(x_prompt, x_sample, cache_conv, state_ssm_re, state_ssm_im, norm_mix, w_in, conv_w, ssm_lam_re, ssm_lam_im, ssm_log_dt, ssm_b_re, ssm_b_im, ssm_c_re, ssm_c_im, ssm_d, w_glu, b_glu, w_out, norm_ffn, w_gate, w_up, w_down, norm_final):
    assert norm_mix.shape[0] == 1, "single-layer trunk"
    n_st = N_GROUPS * STATE_P
    lamr, lami, bc, cc = _s5_prep(ssm_lam_re[0], ssm_lam_im[0], ssm_log_dt[0], ssm_b_re[0], ssm_b_im[0],
                                  ssm_c_re[0], ssm_c_im[0])
    mix_w = (norm_mix, w_in[0].astype(BF16), conv_w[0], lamr, lami, bc, cc, ssm_d, w_glu[0].astype(BF16),
             b_glu, w_out[0].astype(BF16))
    ffn_w = (norm_ffn, w_gate[0].astype(BF16), w_up[0].astype(BF16), w_down[0].astype(BF16),
             norm_final.reshape(1, D_MODEL))

    bp, lp, _ = x_prompt.shape
    bs, ls, _ = x_sample.shape

    x1p, convp, rep, imp = _mixer(x_prompt, mix_w, nb=SUBLANES, tc=64)
    sample_state = (cache_conv[0].transpose(1, 0, 2), state_ssm_re[0].reshape(bs, n_st),
                    state_ssm_im[0].reshape(bs, n_st))
    x1s, convs, res, ims = _mixer(x_sample, mix_w, sample_state, nb=64, tc=SUBLANES)

    yp = _ffn(x1p.reshape(bp * lp, D_MODEL), ffn_w, tm=1024).reshape(bp, lp, D_MODEL)
    ys = _ffn(x1s.reshape(bs * ls, D_MODEL), ffn_w, tm=512).reshape(bs, ls, D_MODEL)

    def cache_out(c):
        return c.transpose(1, 0, 2)[None]

    def state_out(h):
        return h.reshape(1, h.shape[0], N_GROUPS, STATE_P)

    return (yp, ys, cache_out(convp), state_out(rep), state_out(imp),
            cache_out(convs), state_out(res), state_out(ims))
```

```python
import functools

import jax
import jax.numpy as jnp
from jax import lax
from jax.experimental import pallas as pl
from jax.experimental.pallas import tpu as pltpu

D_MODEL = 1024
W_CONV = 512
W_SSM = 512
CONV_W = 3
N_GROUPS = 32
SSM_H = 16
STATE_P = 64
D_FF = 2816
EPS = 1e-5

SUBLANES = 8
GROUPS_PER_HALF = 16
N_HALVES = N_GROUPS // GROUPS_PER_HALF
U_HALF = GROUPS_PER_HALF * SSM_H
S_HALF = GROUPS_PER_HALF * STATE_P
S_COLS = 2 * N_GROUPS * STATE_P
SCAN_LANES = 512
FFN_CHUNKS = ((0, 1024), (1024, 1024), (2048, 768))
VMEM_LIMIT = 56 * 1024 * 1024

BF16 = jnp.bfloat16
F32 = jnp.float32


def _rms(x, g):
    y = x * lax.rsqrt(jnp.mean(x * x, axis=-1, keepdims=True) + EPS)
    return y * g


def _dot(a, b):
    return jnp.dot(a, b, preferred_element_type=F32)


def _tile_copies(hbm, buf, sem, step, slot, *, nb, tc, per_seq, to_vmem):
    copies = []
    for k in range(SUBLANES):
        if per_seq:
            h = hbm.at[k, pl.ds(step * tc, tc), :]
            v = buf.at[slot, :, k, :]
        else:
            h = hbm.at[pl.ds(step * nb, nb), k, :]
            v = buf.at[slot, k]
        src, dst = (h, v) if to_vmem else (v, h)
        copies.append(pltpu.make_async_copy(src, dst, sem.at[slot, k]))
    return copies


def _mixer_body(x_hbm, *refs, nb, tc, n_steps, carry):
    if carry:
        cache_ref = re0_ref = im0_ref = None
    else:
        cache_ref, re0_ref, im0_ref, *refs = refs
    (nmix_ref, win_ref, convw_ref, lamr_ref, lami_ref, bc_ref, cc_ref, dskip_ref, wglu_ref, bglu_ref, wout_ref,
     x1_hbm, convnew_ref, hre_ref, him_ref, xbuf, obuf, hbuf, vbuf, insem, outsem) = refs
    tm = nb * tc
    i = pl.program_id(0)
    slot = lax.rem(i, 2)
    cp = functools.partial(_tile_copies, nb=nb, tc=tc, per_seq=carry)
    in_copies = functools.partial(cp, x_hbm, xbuf, insem, to_vmem=True)
    out_copies = functools.partial(cp, x1_hbm, obuf, outsem, to_vmem=False)

    @pl.when(i == 0)
    def _():
        for c in in_copies(0, 0):
            c.start()

    for c in in_copies(i, slot):
        c.wait()

    @pl.when(i + 1 < n_steps)
    def _():
        for c in in_copies(i + 1, 1 - slot):
            c.start()

    if carry:
        @pl.when(i == 0)
        def _():
            vbuf[0:2 * nb, :] = jnp.zeros((2 * nb, W_CONV), F32)
            hbuf[0:nb, :] = jnp.zeros((nb, S_COLS), F32)
    else:
        vbuf[0:2 * nb, :] = cache_ref[...].reshape(2 * nb, W_CONV)
        for n in range(N_HALVES):
            hbuf[0:nb, 2 * n * S_HALF:(2 * n + 1) * S_HALF] = re0_ref[:, n * S_HALF:(n + 1) * S_HALF]
            hbuf[0:nb, (2 * n + 1) * S_HALF:(2 * n + 2) * S_HALF] = im0_ref[:, n * S_HALF:(n + 1) * S_HALF]

    x = xbuf[slot].reshape(tm, D_MODEL)
    xn = _rms(x, nmix_ref[...]).astype(BF16)

    u = _dot(xn, win_ref[:, 3 * W_CONV:])
    ub = u.astype(BF16)
    for n in range(N_HALVES):
        hbuf[nb:nb + tm, 2 * n * S_HALF:(2 * n + 2) * S_HALF] = _dot(ub[:, n * U_HALF:(n + 1) * U_HALF], bc_ref[n])

    for n in range(N_HALVES):
        for c0 in range(0, S_HALF, SCAN_LANES):
            re_cols = slice(2 * n * S_HALF + c0, 2 * n * S_HALF + c0 + SCAN_LANES)
            im_cols = slice((2 * n + 1) * S_HALF + c0, (2 * n + 1) * S_HALF + c0 + SCAN_LANES)
            lr = lamr_ref[n, :, c0:c0 + SCAN_LANES]
            li = lami_ref[n, :, c0:c0 + SCAN_LANES]
            for r0 in range(0, nb, SUBLANES):
                hr = hbuf[r0:r0 + SUBLANES, re_cols]
                hi = hbuf[r0:r0 + SUBLANES, im_cols]
                for t in range(tc):
                    rows = slice(nb + t * nb + r0, nb + t * nb + r0 + SUBLANES)
                    hr, hi = (lr * hr - li * hi + hbuf[rows, re_cols],
                              lr * hi + li * hr + hbuf[rows, im_cols])
                    hbuf[rows, re_cols] = hr
                    hbuf[rows, im_cols] = hi

    proj = _dot(xn, win_ref[:, 0:3 * W_CONV])
    b_g = proj[:, 0:W_CONV]
    c_g = proj[:, W_CONV:2 * W_CONV]
    xc = proj[:, 2 * W_CONV:3 * W_CONV]
    v = c_g * xc
    vbuf[2 * nb:2 * nb + tm, :] = v
    conv = (convw_ref[0:1, :] * vbuf[0:tm, :] + convw_ref[1:2, :] * vbuf[nb:nb + tm, :]
            + convw_ref[2:3, :] * v)
    conv_out = (b_g * conv).astype(BF16)
    new_cache = vbuf[tm:tm + 2 * nb, :]
    convnew_ref[...] = new_cache.reshape(2, nb, W_CONV)
    vbuf[0:2 * nb, :] = new_cache
    x1 = x + _dot(conv_out, wout_ref[0:W_CONV, :])

    last = hbuf[tm:tm + nb, :]
    for n in range(N_HALVES):
        hre_ref[:, n * S_HALF:(n + 1) * S_HALF] = last[:, 2 * n * S_HALF:(2 * n + 1) * S_HALF]
        him_ref[:, n * S_HALF:(n + 1) * S_HALF] = last[:, (2 * n + 1) * S_HALF:(2 * n + 2) * S_HALF]

    ys = []
    for n in range(N_HALVES):
        hb = hbuf[nb:nb + tm, 2 * n * S_HALF:(2 * n + 2) * S_HALF].astype(BF16)
        ys.append(_dot(hb, cc_ref[n]))
    hbuf[0:nb, :] = last
    y = jnp.concatenate(ys, axis=-1) + dskip_ref[...] * u
    z = jax.nn.gelu(y)
    gate = jax.nn.sigmoid(_dot(z.astype(BF16), wglu_ref[...]) + bglu_ref[...])
    ssm_out = (z * gate).astype(BF16)

    x1 = x1 + _dot(ssm_out, wout_ref[W_CONV:, :])

    @pl.when(i >= 2)
    def _():
        for c in out_copies(i - 2, slot):
            c.wait()

    obuf[slot] = x1.reshape(tc, nb, D_MODEL)
    for c in out_copies(i, slot):
        c.start()

    @pl.when(i == n_steps - 1)
    def _():
        if n_steps >= 2:
            for c in out_copies(i - 1, 1 - slot):
                c.wait()
        for c in out_copies(i, slot):
            c.wait()


def _resident(shape):
    nd = len(shape)
    return pl.BlockSpec(shape, lambda i: (0,) * nd, pipeline_mode=pl.Buffered(1))


def _mixer(x, wts, state=None, *, nb, tc):
    n_seq, n_time, _ = x.shape
    carry = state is None
    if carry:
        assert n_seq == nb == SUBLANES and n_time % tc == 0
        n_steps = n_time // tc
        seq_blk = lambda i: 0
    else:
        assert n_time == tc == SUBLANES and n_seq % nb == 0
        n_steps = n_seq // nb
        seq_blk = lambda i: i
    tm = nb * tc
    n_st = N_GROUPS * STATE_P
    body = functools.partial(_mixer_body, nb=nb, tc=tc, n_steps=n_steps, carry=carry)
    any_spec = pl.BlockSpec(memory_space=pl.ANY)
    cache_spec = pl.BlockSpec((2, nb, W_CONV), lambda i: (0, seq_blk(i), 0))
    st_spec = pl.BlockSpec((nb, n_st), lambda i: (seq_blk(i), 0))
    state_specs = [] if carry else [cache_spec, st_spec, st_spec]
    return pl.pallas_call(
        body,
        grid=(n_steps,),
        in_specs=[any_spec] + state_specs + [_resident(w.shape) for w in wts],
        out_specs=[any_spec, cache_spec, st_spec, st_spec],
        out_shape=[jax.ShapeDtypeStruct(x.shape, F32),
                   jax.ShapeDtypeStruct((2, n_seq, W_CONV), F32),
                   jax.ShapeDtypeStruct((n_seq, n_st), F32),
                   jax.ShapeDtypeStruct((n_seq, n_st), F32)],
        scratch_shapes=[pltpu.VMEM((2, tc, nb, D_MODEL), F32),
                        pltpu.VMEM((2, tc, nb, D_MODEL), F32),
                        pltpu.VMEM((nb + tm, S_COLS), F32),
                        pltpu.VMEM((2 * nb + tm, W_CONV), F32),
                        pltpu.SemaphoreType.DMA((2, SUBLANES)),
                        pltpu.SemaphoreType.DMA((2, SUBLANES))],
        compiler_params=pltpu.CompilerParams(dimension_semantics=("arbitrary",),
                                             vmem_limit_bytes=VMEM_LIMIT),
        name="mixer_carry" if carry else "mixer_step",
    )(x, *(() if carry else state), *wts)


def _ffn_body(x_ref, nffn_ref, wg_ref, wu_ref, wd_ref, nfin_ref, o_ref):
    x = x_ref[...]
    hn = _rms(x, nffn_ref[...]).astype(BF16)
    acc = x
    for s, w in FFN_CHUNKS:
        g = _dot(hn, wg_ref[:, s:s + w])
        up = _dot(hn, wu_ref[:, s:s + w])
        a = (g * jax.nn.sigmoid(g) * up).astype(BF16)
        acc = acc + _dot(a, wd_ref[s:s + w, :])
    o_ref[...] = _rms(acc, nfin_ref[...])


def _ffn(x2d, wts, *, tm):
    n_rows = x2d.shape[0]
    assert n_rows % tm == 0
    row_spec = pl.BlockSpec((tm, D_MODEL), lambda i: (i, 0))
    return pl.pallas_call(
        _ffn_body,
        grid=(n_rows // tm,),
        in_specs=[row_spec] + [_resident(w.shape) for w in wts],
        out_specs=row_spec,
        out_shape=jax.ShapeDtypeStruct(x2d.shape, F32),
        compiler_params=pltpu.CompilerParams(dimension_semantics=("arbitrary",),
                                             vmem_limit_bytes=VMEM_LIMIT),
        name="ffn",
    )(x2d, *wts)


def _s5_prep_body(lre_ref, lim_ref, logdt_ref, btre_ref, btim_ref, cre_ref, cim_ref,
                  lamr_ref, lami_ref, bc_ref, cc_ref, bscr, cscr):
    lre = lre_ref[...]
    lim = lim_ref[...]
    dt = jnp.exp(logdt_ref[...])
    mag = jnp.exp(lre * dt)
    lbr = mag * jnp.cos(lim * dt)
    lbi = mag * jnp.sin(lim * dt)
    den = lre * lre + lim * lim
    fr = ((lbr - 1.0) * lre + lbi * lim) / den
    fi = (lbi * lre - (lbr - 1.0) * lim) / den
    frb = fr[:, None, :]
    fib = fi[:, None, :]
    bbr = frb * btre_ref[...] - fib * btim_ref[...]
    bbi = frb * btim_ref[...] + fib * btre_ref[...]
    cre = cre_ref[...]
    ncim = -cim_ref[...]

    bscr[...] = jnp.zeros_like(bscr)
    cscr[...] = jnp.zeros_like(cscr)
    for g in range(N_GROUPS):
        n, k = divmod(g, GROUPS_PER_HALF)
        rows = slice(k * SSM_H, (k + 1) * SSM_H)
        re_cols = slice(k * STATE_P, (k + 1) * STATE_P)
        im_cols = slice(S_HALF + k * STATE_P, S_HALF + (k + 1) * STATE_P)
        bscr[n, rows, re_cols] = bbr[g]
        bscr[n, rows, im_cols] = bbi[g]
        cscr[n, rows, re_cols] = cre[g]
        cscr[n, rows, im_cols] = ncim[g]
        lamr_ref[n, :, re_cols] = jnp.broadcast_to(lbr[g:g + 1, :], (SUBLANES, STATE_P))
        lami_ref[n, :, re_cols] = jnp.broadcast_to(lbi[g:g + 1, :], (SUBLANES, STATE_P))
    for n in range(N_HALVES):
        bc_ref[n] = bscr[n].astype(BF16)
        cc_ref[n] = cscr[n].T.astype(BF16)


def _s5_prep(lam_re, lam_im, log_dt, b_re, b_im, c_re, c_im):
    return pl.pallas_call(
        _s5_prep_body,
        out_shape=[jax.ShapeDtypeStruct((N_HALVES, SUBLANES, S_HALF), F32),
                   jax.ShapeDtypeStruct((N_HALVES, SUBLANES, S_HALF), F32),
                   jax.ShapeDtypeStruct((N_HALVES, U_HALF, 2 * S_HALF), BF16),
                   jax.ShapeDtypeStruct((N_HALVES, 2 * S_HALF, U_HALF), BF16)],
        scratch_shapes=[pltpu.VMEM((N_HALVES, U_HALF, 2 * S_HALF), F32),
                        pltpu.VMEM((N_HALVES, U_HALF, 2 * S_HALF), F32)],
        name="s5_prep",
    )(lam_re, lam_im, log_dt.reshape(N_GROUPS, 1), b_re.transpose(0, 2, 1), b_im.transpose(0, 2, 1), c_re, c_im)


def kernel(x_prompt, x_sample, cache_conv, state_ssm_re, state_ssm_im, norm_mix, w_in, conv_w, ssm_lam_re, ssm_lam_im, ssm_log_dt, ssm_b_re, ssm_b_im, ssm_c_re, ssm_c_im, ssm_d, w_glu, b_glu, w_out, norm_ffn, w_gate, w_up, w_down, norm_final):
    assert norm_mix.shape[0] == 1, "single-layer trunk"
    n_st = N_GROUPS * STATE_P
    lamr, lami, bc, cc = _s5_prep(ssm_lam_re[0], ssm_lam_im[0], ssm_log_dt[0], ssm_b_re[0], ssm_b_im[0],
                                  ssm_c_re[0], ssm_c_im[0])
    mix_w = (norm_mix, w_in[0].astype(BF16), conv_w[0], lamr, lami, bc, cc, ssm_d, w_glu[0].astype(BF16),
             b_glu, w_out[0].astype(BF16))
    ffn_w = (norm_ffn, w_gate[0].astype(BF16), w_up[0].astype(BF16), w_down[0].astype(BF16),
             norm_final.reshape(1, D_MODEL))

    bp, lp, _ = x_prompt.shape
    bs, ls, _ = x_sample.shape

    x1p, convp, rep, imp = _mixer(x_prompt, mix_w, nb=SUBLANES, tc=64)
    sample_state = (cache_conv[0].transpose(1, 0, 2), state_ssm_re[0].reshape(bs, n_st),
                    state_ssm_im[0].reshape(bs, n_st))
    x1s, convs, res, ims = _mixer(x_sample, mix_w, sample_state, nb=64, tc=SUBLANES)

    yp = _ffn(x1p.reshape(bp * lp, D_MODEL), ffn_w, tm=1024).reshape(bp, lp, D_MODEL)
    ys = _ffn(x1s.reshape(bs * ls, D_MODEL), ffn_w, tm=512).reshape(bs, ls, D_MODEL)

    def cache_out(c):
        return c.transpose(1, 0, 2)[None]

    def state_out(h):
        return h.reshape(1, h.shape[0], N_GROUPS, STATE_P)

    return (yp, ys, cache_out(convp), state_out(rep), state_out(imp),
            cache_out(convs), state_out(res), state_out(ims))
```

```python
import functools

import jax
import jax.numpy as jnp
from jax import lax
from jax.experimental import pallas as pl
from jax.experimental.pallas import tpu as pltpu

D_MODEL = 1024
W_CONV = 512
W_SSM = 512
CONV_W = 3
N_GROUPS = 32
SSM_H = 16
STATE_P = 64
D_FF = 2816
EPS = 1e-5

SUBLANES = 8
GROUPS_PER_HALF = 16
N_HALVES = N_GROUPS // GROUPS_PER_HALF
U_HALF = GROUPS_PER_HALF * SSM_H
S_HALF = GROUPS_PER_HALF * STATE_P
S_COLS = 2 * N_GROUPS * STATE_P
SCAN_LANES = 512
FFN_CHUNKS = ((0, 1024), (1024, 1024), (2048, 768))
VMEM_LIMIT = 56 * 1024 * 1024

BF16 = jnp.bfloat16
F32 = jnp.float32


def _rms(x, g):
    y = x * lax.rsqrt(jnp.mean(x * x, axis=-1, keepdims=True) + EPS)
    return y * g


def _dot(a, b):
    return jnp.dot(a, b, preferred_element_type=F32)


def _tile_copies(hbm, buf, sem, step, slot, *, nb, tc, per_seq, to_vmem):
    copies = []
    for k in range(SUBLANES):
        if per_seq:
            h = hbm.at[k, pl.ds(step * tc, tc), :]
            v = buf.at[slot, :, k, :]
        else:
            h = hbm.at[pl.ds(step * nb, nb), k, :]
            v = buf.at[slot, k]
        src, dst = (h, v) if to_vmem else (v, h)
        copies.append(pltpu.make_async_copy(src, dst, sem.at[slot, k]))
    return copies


def _mixer_body(x_hbm, *refs, nb, tc, n_steps, carry):
    if carry:
        cache_ref = re0_ref = im0_ref = None
    else:
        cache_ref, re0_ref, im0_ref, *refs = refs
    (nmix_ref, win_ref, convw_ref, lamr_ref, lami_ref, bc_ref, cc_ref, dskip_ref, wglu_ref, bglu_ref, wout_ref,
     x1_hbm, convnew_ref, hre_ref, him_ref, xbuf, obuf, hbuf, vbuf, insem, outsem) = refs
    tm = nb * tc
    i = pl.program_id(0)
    slot = lax.rem(i, 2)
    cp = functools.partial(_tile_copies, nb=nb, tc=tc, per_seq=carry)
    in_copies = functools.partial(cp, x_hbm, xbuf, insem, to_vmem=True)
    out_copies = functools.partial(cp, x1_hbm, obuf, outsem, to_vmem=False)

    @pl.when(i == 0)
    def _():
        for c in in_copies(0, 0):
            c.start()

    for c in in_copies(i, slot):
        c.wait()

    @pl.when(i + 1 < n_steps)
    def _():
        for c in in_copies(i + 1, 1 - slot):
            c.start()

    if carry:
        @pl.when(i == 0)
        def _():
            vbuf[0:2 * nb, :] = jnp.zeros((2 * nb, W_CONV), F32)
            hbuf[0:nb, :] = jnp.zeros((nb, S_COLS), F32)
    else:
        vbuf[0:2 * nb, :] = cache_ref[...].reshape(2 * nb, W_CONV)
        for n in range(N_HALVES):
            hbuf[0:nb, 2 * n * S_HALF:(2 * n + 1) * S_HALF] = re0_ref[:, n * S_HALF:(n + 1) * S_HALF]
            hbuf[0:nb, (2 * n + 1) * S_HALF:(2 * n + 2) * S_HALF] = im0_ref[:, n * S_HALF:(n + 1) * S_HALF]

    x = xbuf[slot].reshape(tm, D_MODEL)
    xn = _rms(x, nmix_ref[...]).astype(BF16)

    u = _dot(xn, win_ref[:, 3 * W_CONV:])
    ub = u.astype(BF16)
    for n in range(N_HALVES):
        hbuf[nb:nb + tm, 2 * n * S_HALF:(2 * n + 2) * S_HALF] = _dot(ub[:, n * U_HALF:(n + 1) * U_HALF], bc_ref[n])

    for n in range(N_HALVES):
        for c0 in range(0, S_HALF, SCAN_LANES):
            re_cols = slice(2 * n * S_HALF + c0, 2 * n * S_HALF + c0 + SCAN_LANES)
            im_cols = slice((2 * n + 1) * S_HALF + c0, (2 * n + 1) * S_HALF + c0 + SCAN_LANES)
            lr = lamr_ref[n, :, c0:c0 + SCAN_LANES]
            li = lami_ref[n, :, c0:c0 + SCAN_LANES]
            for r0 in range(0, nb, SUBLANES):
                hr = hbuf[r0:r0 + SUBLANES, re_cols]
                hi = hbuf[r0:r0 + SUBLANES, im_cols]
                for t in range(tc):
                    rows = slice(nb + t * nb + r0, nb + t * nb + r0 + SUBLANES)
                    hr, hi = (lr * hr - li * hi + hbuf[rows, re_cols],
                              lr * hi + li * hr + hbuf[rows, im_cols])
                    hbuf[rows, re_cols] = hr
                    hbuf[rows, im_cols] = hi

    proj = _dot(xn, win_ref[:, 0:3 * W_CONV])
    b_g = proj[:, 0:W_CONV]
    c_g = proj[:, W_CONV:2 * W_CONV]
    xc = proj[:, 2 * W_CONV:3 * W_CONV]
    v = c_g * xc
    vbuf[2 * nb:2 * nb + tm, :] = v
    conv = (convw_ref[0:1, :] * vbuf[0:tm, :] + convw_ref[1:2, :] * vbuf[nb:nb + tm, :]
            + convw_ref[2:3, :] * v)
    conv_out = (b_g * conv).astype(BF16)
    new_cache = vbuf[tm:tm + 2 * nb, :]
    convnew_ref[...] = new_cache.reshape(2, nb, W_CONV)
    vbuf[0:2 * nb, :] = new_cache
    x1 = x + _dot(conv_out, wout_ref[0:W_CONV, :])

    last = hbuf[tm:tm + nb, :]
    for n in range(N_HALVES):
        hre_ref[:, n * S_HALF:(n + 1) * S_HALF] = last[:, 2 * n * S_HALF:(2 * n + 1) * S_HALF]
        him_ref[:, n * S_HALF:(n + 1) * S_HALF] = last[:, (2 * n + 1) * S_HALF:(2 * n + 2) * S_HALF]

    ys = []
    for n in range(N_HALVES):
        hb = hbuf[nb:nb + tm, 2 * n * S_HALF:(2 * n + 2) * S_HALF].astype(BF16)
        ys.append(_dot(hb, cc_ref[n]))
    hbuf[0:nb, :] = last
    y = jnp.concatenate(ys, axis=-1) + dskip_ref[...] * u
    z = jax.nn.gelu(y)
    gate = jax.nn.sigmoid(_dot(z.astype(BF16), wglu_ref[...]) + bglu_ref[...])
    ssm_out = (z * gate).astype(BF16)

    x1 = x1 + _dot(ssm_out, wout_ref[W_CONV:, :])

    @pl.when(i >= 1)
    def _():
        for c in out_copies(i - 1, 0):
            c.wait()

    obuf[0] = x1.reshape(tc, nb, D_MODEL)
    for c in out_copies(i, 0):
        c.start()

    @pl.when(i == n_steps - 1)
    def _():
        for c in out_copies(i, 0):
            c.wait()


def _resident(shape):
    nd = len(shape)
    return pl.BlockSpec(shape, lambda i: (0,) * nd, pipeline_mode=pl.Buffered(1))


def _mixer(x, wts, state=None, *, nb, tc):
    n_seq, n_time, _ = x.shape
    carry = state is None
    if carry:
        assert n_seq == nb == SUBLANES and n_time % tc == 0
        n_steps = n_time // tc
        seq_blk = lambda i: 0
    else:
        assert n_time == tc == SUBLANES and n_seq % nb == 0
        n_steps = n_seq // nb
        seq_blk = lambda i: i
    tm = nb * tc
    n_st = N_GROUPS * STATE_P
    body = functools.partial(_mixer_body, nb=nb, tc=tc, n_steps=n_steps, carry=carry)
    any_spec = pl.BlockSpec(memory_space=pl.ANY)
    cache_spec = pl.BlockSpec((2, nb, W_CONV), lambda i: (0, seq_blk(i), 0))
    st_spec = pl.BlockSpec((nb, n_st), lambda i: (seq_blk(i), 0))
    state_specs = [] if carry else [cache_spec, st_spec, st_spec]
    return pl.pallas_call(
        body,
        grid=(n_steps,),
        in_specs=[any_spec] + state_specs + [_resident(w.shape) for w in wts],
        out_specs=[any_spec, cache_spec, st_spec, st_spec],
        out_shape=[jax.ShapeDtypeStruct(x.shape, F32),
                   jax.ShapeDtypeStruct((2, n_seq, W_CONV), F32),
                   jax.ShapeDtypeStruct((n_seq, n_st), F32),
                   jax.ShapeDtypeStruct((n_seq, n_st), F32)],
        scratch_shapes=[pltpu.VMEM((2, tc, nb, D_MODEL), F32),
                        pltpu.VMEM((1, tc, nb, D_MODEL), F32),
                        pltpu.VMEM((nb + tm, S_COLS), F32),
                        pltpu.VMEM((2 * nb + tm, W_CONV), F32),
                        pltpu.SemaphoreType.DMA((2, SUBLANES)),
                        pltpu.SemaphoreType.DMA((1, SUBLANES))],
        compiler_params=pltpu.CompilerParams(dimension_semantics=("arbitrary",),
                                             vmem_limit_bytes=VMEM_LIMIT),
        name="mixer_carry" if carry else "mixer_step",
    )(x, *(() if carry else state), *wts)


def _ffn_body(x_ref, nffn_ref, wg_ref, wu_ref, wd_ref, nfin_ref, o_ref):
    x = x_ref[...]
    hn = _rms(x, nffn_ref[...]).astype(BF16)
    acc = x
    for s, w in FFN_CHUNKS:
        g = _dot(hn, wg_ref[:, s:s + w])
        up = _dot(hn, wu_ref[:, s:s + w])
        a = (g * jax.nn.sigmoid(g) * up).astype(BF16)
        acc = acc + _dot(a, wd_ref[s:s + w, :])
    o_ref[...] = _rms(acc, nfin_ref[...])


def _ffn(x2d, wts, *, tm):
    n_rows = x2d.shape[0]
    assert n_rows % tm == 0
    row_spec = pl.BlockSpec((tm, D_MODEL), lambda i: (i, 0))
    return pl.pallas_call(
        _ffn_body,
        grid=(n_rows // tm,),
        in_specs=[row_spec] + [_resident(w.shape) for w in wts],
        out_specs=row_spec,
        out_shape=jax.ShapeDtypeStruct(x2d.shape, F32),
        compiler_params=pltpu.CompilerParams(dimension_semantics=("arbitrary",),
                                             vmem_limit_bytes=VMEM_LIMIT),
        name="ffn",
    )(x2d, *wts)


def _s5_prep_body(lre_ref, lim_ref, logdt_ref, btre_ref, btim_ref, cre_ref, cim_ref,
                  lamr_ref, lami_ref, bc_ref, cc_ref, bscr, cscr):
    lre = lre_ref[...]
    lim = lim_ref[...]
    dt = jnp.exp(logdt_ref[...])
    mag = jnp.exp(lre * dt)
    lbr = mag * jnp.cos(lim * dt)
    lbi = mag * jnp.sin(lim * dt)
    den = lre * lre + lim * lim
    fr = ((lbr - 1.0) * lre + lbi * lim) / den
    fi = (lbi * lre - (lbr - 1.0) * lim) / den
    frb = fr[:, None, :]
    fib = fi[:, None, :]
    bbr = frb * btre_ref[...] - fib * btim_ref[...]
    bbi = frb * btim_ref[...] + fib * btre_ref[...]
    cre = cre_ref[...]
    ncim = -cim_ref[...]

    bscr[...] = jnp.zeros_like(bscr)
    cscr[...] = jnp.zeros_like(cscr)
    for g in range(N_GROUPS):
        n, k = divmod(g, GROUPS_PER_HALF)
        rows = slice(k * SSM_H, (k + 1) * SSM_H)
        re_cols = slice(k * STATE_P, (k + 1) * STATE_P)
        im_cols = slice(S_HALF + k * STATE_P, S_HALF + (k + 1) * STATE_P)
        bscr[n, rows, re_cols] = bbr[g]
        bscr[n, rows, im_cols] = bbi[g]
        cscr[n, rows, re_cols] = cre[g]
        cscr[n, rows, im_cols] = ncim[g]
        lamr_ref[n, :, re_cols] = jnp.broadcast_to(lbr[g:g + 1, :], (SUBLANES, STATE_P))
        lami_ref[n, :, re_cols] = jnp.broadcast_to(lbi[g:g + 1, :], (SUBLANES, STATE_P))
    for n in range(N_HALVES):
        bc_ref[n] = bscr[n].astype(BF16)
        cc_ref[n] = cscr[n].T.astype(BF16)


def _s5_prep(lam_re, lam_im, log_dt, b_re, b_im, c_re, c_im):
    return pl.pallas_call(
        _s5_prep_body,
        out_shape=[jax.ShapeDtypeStruct((N_HALVES, SUBLANES, S_HALF), F32),
                   jax.ShapeDtypeStruct((N_HALVES, SUBLANES, S_HALF), F32),
                   jax.ShapeDtypeStruct((N_HALVES, U_HALF, 2 * S_HALF), BF16),
                   jax.ShapeDtypeStruct((N_HALVES, 2 * S_HALF, U_HALF), BF16)],
        scratch_shapes=[pltpu.VMEM((N_HALVES, U_HALF, 2 * S_HALF), F32),
                        pltpu.VMEM((N_HALVES, U_HALF, 2 * S_HALF), F32)],
        name="s5_prep",
    )(lam_re, lam_im, log_dt.reshape(N_GROUPS, 1), b_re.transpose(0, 2, 1), b_im.transpose(0, 2, 1), c_re, c_im)


def kernel(x_prompt, x_sample, cache_conv, state_ssm_re, state_ssm_im, norm_mix, w_in, conv_w, ssm_lam_re, ssm_lam_im, ssm_log_dt, ssm_b_re, ssm_b_im, ssm_c_re, ssm_c_im, ssm_d, w_glu, b_glu, w_out, norm_ffn, w_gate, w_up, w_down, norm_final):
    assert norm_mix.shape[0] == 1, "single-layer trunk"
    n_st = N_GROUPS * STATE_P
    lamr, lami, bc, cc = _s5_prep(ssm_lam_re[0], ssm_lam_im[0], ssm_log_dt[0], ssm_b_re[0], ssm_b_im[0],
                                  ssm_c_re[0], ssm_c_im[0])
    mix_w = (norm_mix, w_in[0].astype(BF16), conv_w[0], lamr, lami, bc, cc, ssm_d, w_glu[0].astype(BF16),
             b_glu, w_out[0].astype(BF16))
    ffn_w = (norm_ffn, w_gate[0].astype(BF16), w_up[0].astype(BF16), w_down[0].astype(BF16),
             norm_final.reshape(1, D_MODEL))

    bp, lp, _ = x_prompt.shape
    bs, ls, _ = x_sample.shape

    x1p, convp, rep, imp = _mixer(x_prompt, mix_w, nb=SUBLANES, tc=128)
    sample_state = (cache_conv[0].transpose(1, 0, 2), state_ssm_re[0].reshape(bs, n_st),
                    state_ssm_im[0].reshape(bs, n_st))
    x1s, convs, res, ims = _mixer(x_sample, mix_w, sample_state, nb=64, tc=SUBLANES)

    yp = _ffn(x1p.reshape(bp * lp, D_MODEL), ffn_w, tm=1024).reshape(bp, lp, D_MODEL)
    ys = _ffn(x1s.reshape(bs * ls, D_MODEL), ffn_w, tm=512).reshape(bs, ls, D_MODEL)

    def cache_out(c):
        return c.transpose(1, 0, 2)[None]

    def state_out(h):
        return h.reshape(1, h.shape[0], N_GROUPS, STATE_P)

    return (yp, ys, cache_out(convp), state_out(rep), state_out(imp),
            cache_out(convs), state_out(res), state_out(ims))
```

```python
import functools

import jax
import jax.numpy as jnp
from jax import lax
from jax.experimental import pallas as pl
from jax.experimental.pallas import tpu as pltpu

D_MODEL = 1024
W_CONV = 512
W_SSM = 512
CONV_W = 3
N_GROUPS = 32
SSM_H = 16
STATE_P = 64
D_FF = 2816
EPS = 1e-5

SUBLANES = 8
BF16_ROWS = 16
GROUPS_PER_HALF = 16
N_HALVES = N_GROUPS // GROUPS_PER_HALF
U_HALF = GROUPS_PER_HALF * SSM_H
S_HALF = GROUPS_PER_HALF * STATE_P
S_COLS = 2 * N_GROUPS * STATE_P
SCAN_LANES = 512
FFN_CHUNKS = ((0, 1024), (1024, 1024), (2048, 768))
VMEM_LIMIT = 60 * 1024 * 1024

BF16 = jnp.bfloat16
F32 = jnp.float32


def _rms(x, g):
    y = x * lax.rsqrt(jnp.mean(x * x, axis=-1, keepdims=True) + EPS)
    return y * g


def _dot(a, b):
    return jnp.dot(a, b, preferred_element_type=F32)


def _tile_copies(hbm, buf, sem, step, slot, *, nb, tc, per_seq, to_vmem):
    copies = []
    for k in range(SUBLANES):
        if per_seq:
            h = hbm.at[k, pl.ds(step * tc, tc), :]
            v = buf.at[slot, :, k, :]
        else:
            h = hbm.at[pl.ds(step * nb, nb), k, :]
            v = buf.at[slot, k]
        src, dst = (h, v) if to_vmem else (v, h)
        copies.append(pltpu.make_async_copy(src, dst, sem.at[slot, k]))
    return copies


def _mixer_body(x_hbm, *refs, nb, tc, n_steps, carry, n_cast):
    if carry:
        cache_ref = re0_ref = im0_ref = None
    else:
        cache_ref, re0_ref, im0_ref, *refs = refs
    (nmix_ref, win_ref, convw_ref, lamr_ref, lami_ref, bc_ref, cc_ref, dskip_ref, wglu_ref, bglu_ref, wout_ref,
     *refs) = refs
    cast_in, refs = refs[:n_cast], refs[n_cast:]
    x1_hbm, convnew_ref, hre_ref, him_ref, *refs = refs
    cast_out, refs = refs[:n_cast], refs[n_cast:]
    xbuf, obuf, hbuf, vbuf, insem, outsem = refs
    tm = nb * tc

    for src, dst in zip(cast_in, cast_out):
        dst[...] = src[...].astype(BF16)

    i = pl.program_id(0)
    slot = lax.rem(i, 2)
    cp = functools.partial(_tile_copies, nb=nb, tc=tc, per_seq=carry)
    in_copies = functools.partial(cp, x_hbm, xbuf, insem, to_vmem=True)
    out_copies = functools.partial(cp, x1_hbm, obuf, outsem, to_vmem=False)

    @pl.when(i == 0)
    def _():
        for c in in_copies(0, 0):
            c.start()

    for c in in_copies(i, slot):
        c.wait()

    @pl.when(i + 1 < n_steps)
    def _():
        for c in in_copies(i + 1, 1 - slot):
            c.start()

    if carry:
        @pl.when(i == 0)
        def _():
            vbuf[0:2 * nb, :] = jnp.zeros((2 * nb, W_CONV), F32)
            hbuf[0:nb, :] = jnp.zeros((nb, S_COLS), F32)
    else:
        vbuf[0:2 * nb, :] = cache_ref[...].reshape(2 * nb, W_CONV)
        for n in range(N_HALVES):
            hbuf[0:nb, 2 * n * S_HALF:(2 * n + 1) * S_HALF] = re0_ref[:, n * S_HALF:(n + 1) * S_HALF]
            hbuf[0:nb, (2 * n + 1) * S_HALF:(2 * n + 2) * S_HALF] = im0_ref[:, n * S_HALF:(n + 1) * S_HALF]

    x = xbuf[slot].reshape(tm, D_MODEL)
    xn = _rms(x, nmix_ref[...]).astype(BF16)

    u = _dot(xn, win_ref[:, 3 * W_CONV:])
    ub = u.astype(BF16)
    for n in range(N_HALVES):
        hbuf[nb:nb + tm, 2 * n * S_HALF:(2 * n + 2) * S_HALF] = _dot(ub[:, n * U_HALF:(n + 1) * U_HALF], bc_ref[n])

    for n in range(N_HALVES):
        for c0 in range(0, S_HALF, SCAN_LANES):
            re_cols = slice(2 * n * S_HALF + c0, 2 * n * S_HALF + c0 + SCAN_LANES)
            im_cols = slice((2 * n + 1) * S_HALF + c0, (2 * n + 1) * S_HALF + c0 + SCAN_LANES)
            lr = lamr_ref[n, :, c0:c0 + SCAN_LANES]
            li = lami_ref[n, :, c0:c0 + SCAN_LANES]
            for r0 in range(0, nb, SUBLANES):
                hr = hbuf[r0:r0 + SUBLANES, re_cols]
                hi = hbuf[r0:r0 + SUBLANES, im_cols]
                for t in range(tc):
                    rows = slice(nb + t * nb + r0, nb + t * nb + r0 + SUBLANES)
                    hr, hi = (lr * hr - li * hi + hbuf[rows, re_cols],
                              lr * hi + li * hr + hbuf[rows, im_cols])
                    hbuf[rows, re_cols] = hr
                    hbuf[rows, im_cols] = hi

    proj = _dot(xn, win_ref[:, 0:3 * W_CONV])
    b_g = proj[:, 0:W_CONV]
    c_g = proj[:, W_CONV:2 * W_CONV]
    xc = proj[:, 2 * W_CONV:3 * W_CONV]
    v = c_g * xc
    vbuf[2 * nb:2 * nb + tm, :] = v
    conv = (convw_ref[0:1, :] * vbuf[0:tm, :] + convw_ref[1:2, :] * vbuf[nb:nb + tm, :]
            + convw_ref[2:3, :] * v)
    conv_out = (b_g * conv).astype(BF16)
    new_cache = vbuf[tm:tm + 2 * nb, :]
    convnew_ref[...] = new_cache.reshape(2, nb, W_CONV)
    vbuf[0:2 * nb, :] = new_cache
    x1 = x + _dot(conv_out, wout_ref[0:W_CONV, :])

    last = hbuf[tm:tm + nb, :]
    for n in range(N_HALVES):
        hre_ref[:, n * S_HALF:(n + 1) * S_HALF] = last[:, 2 * n * S_HALF:(2 * n + 1) * S_HALF]
        him_ref[:, n * S_HALF:(n + 1) * S_HALF] = last[:, (2 * n + 1) * S_HALF:(2 * n + 2) * S_HALF]

    ys = []
    for n in range(N_HALVES):
        hb = hbuf[nb:nb + tm, 2 * n * S_HALF:(2 * n + 2) * S_HALF].astype(BF16)
        ys.append(_dot(hb, cc_ref[n]))
    hbuf[0:nb, :] = last
    y = jnp.concatenate(ys, axis=-1) + dskip_ref[...] * u
    z = jax.nn.gelu(y)
    gate = jax.nn.sigmoid(_dot(z.astype(BF16), wglu_ref[...]) + bglu_ref[...])
    ssm_out = (z * gate).astype(BF16)

    x1 = x1 + _dot(ssm_out, wout_ref[W_CONV:, :])

    @pl.when(i >= 1)
    def _():
        for c in out_copies(i - 1, 0):
            c.wait()

    obuf[0] = x1.reshape(tc, nb, D_MODEL)
    for c in out_copies(i, 0):
        c.start()

    @pl.when(i == n_steps - 1)
    def _():
        for c in out_copies(i, 0):
            c.wait()


def _resident(shape):
    nd = len(shape)
    return pl.BlockSpec(shape, lambda i: (0,) * nd, pipeline_mode=pl.Buffered(1))


def _mixer(x, wts, state=None, cast=(), *, nb, tc):
    n_seq, n_time, _ = x.shape
    carry = state is None
    if carry:
        assert n_seq == nb == SUBLANES and n_time % tc == 0
        n_steps = n_time // tc
        seq_blk = lambda i: 0
    else:
        assert n_time == tc == SUBLANES and n_seq % nb == 0
        n_steps = n_seq // nb
        seq_blk = lambda i: i
    tm = nb * tc
    n_st = N_GROUPS * STATE_P
    body = functools.partial(_mixer_body, nb=nb, tc=tc, n_steps=n_steps, carry=carry, n_cast=len(cast))
    any_spec = pl.BlockSpec(memory_space=pl.ANY)
    cache_spec = pl.BlockSpec((2, nb, W_CONV), lambda i: (0, seq_blk(i), 0))
    st_spec = pl.BlockSpec((nb, n_st), lambda i: (seq_blk(i), 0))
    state_specs = [] if carry else [cache_spec, st_spec, st_spec]
    for w in cast:
        assert w.shape[0] % (n_steps * BF16_ROWS) == 0, w.shape
    cast_specs = [pl.BlockSpec((w.shape[0] // n_steps, w.shape[1]), lambda i: (i, 0)) for w in cast]
    return pl.pallas_call(
        body,
        grid=(n_steps,),
        in_specs=[any_spec] + state_specs + [_resident(w.shape) for w in wts] + cast_specs,
        out_specs=[any_spec, cache_spec, st_spec, st_spec] + cast_specs,
        out_shape=[jax.ShapeDtypeStruct(x.shape, F32),
                   jax.ShapeDtypeStruct((2, n_seq, W_CONV), F32),
                   jax.ShapeDtypeStruct((n_seq, n_st), F32),
                   jax.ShapeDtypeStruct((n_seq, n_st), F32)]
                  + [jax.ShapeDtypeStruct(w.shape, BF16) for w in cast],
        scratch_shapes=[pltpu.VMEM((2, tc, nb, D_MODEL), F32),
                        pltpu.VMEM((1, tc, nb, D_MODEL), F32),
                        pltpu.VMEM((nb + tm, S_COLS), F32),
                        pltpu.VMEM((2 * nb + tm, W_CONV), F32),
                        pltpu.SemaphoreType.DMA((2, SUBLANES)),
                        pltpu.SemaphoreType.DMA((1, SUBLANES))],
        compiler_params=pltpu.CompilerParams(dimension_semantics=("arbitrary",),
                                             vmem_limit_bytes=VMEM_LIMIT),
        name="mixer_carry" if carry else "mixer_step",
    )(x, *(() if carry else state), *wts, *cast)


def _ffn_body(x_ref, nffn_ref, wg_ref, wu_ref, wd_ref, nfin_ref, o_ref):
    x = x_ref[...]
    hn = _rms(x, nffn_ref[...]).astype(BF16)
    acc = x
    for s, w in FFN_CHUNKS:
        g = _dot(hn, wg_ref[:, s:s + w])
        up = _dot(hn, wu_ref[:, s:s + w])
        a = (g * jax.nn.sigmoid(g) * up).astype(BF16)
        acc = acc + _dot(a, wd_ref[s:s + w, :])
    o_ref[...] = _rms(acc, nfin_ref[...])


def _ffn(x2d, wts, *, tm):
    n_rows = x2d.shape[0]
    assert n_rows % tm == 0
    row_spec = pl.BlockSpec((tm, D_MODEL), lambda i: (i, 0))
    return pl.pallas_call(
        _ffn_body,
        grid=(n_rows // tm,),
        in_specs=[row_spec] + [_resident(w.shape) for w in wts],
        out_specs=row_spec,
        out_shape=jax.ShapeDtypeStruct(x2d.shape, F32),
        compiler_params=pltpu.CompilerParams(dimension_semantics=("arbitrary",),
                                             vmem_limit_bytes=VMEM_LIMIT),
        name="ffn",
    )(x2d, *wts)


def _s5_prep_body(lre_ref, lim_ref, logdt_ref, btre_ref, btim_ref, cre_ref, cim_ref,
                  lamr_ref, lami_ref, bc_ref, cc_ref, bscr, cscr):
    lre = lre_ref[...]
    lim = lim_ref[...]
    dt = jnp.exp(logdt_ref[...])
    mag = jnp.exp(lre * dt)
    lbr = mag * jnp.cos(lim * dt)
    lbi = mag * jnp.sin(lim * dt)
    den = lre * lre + lim * lim
    fr = ((lbr - 1.0) * lre + lbi * lim) / den
    fi = (lbi * lre - (lbr - 1.0) * lim) / den
    frb = fr[:, None, :]
    fib = fi[:, None, :]
    bbr = frb * btre_ref[...] - fib * btim_ref[...]
    bbi = frb * btim_ref[...] + fib * btre_ref[...]
    cre = cre_ref[...]
    ncim = -cim_ref[...]

    bscr[...] = jnp.zeros_like(bscr)
    cscr[...] = jnp.zeros_like(cscr)
    for g in range(N_GROUPS):
        n, k = divmod(g, GROUPS_PER_HALF)
        rows = slice(k * SSM_H, (k + 1) * SSM_H)
        re_cols = slice(k * STATE_P, (k + 1) * STATE_P)
        im_cols = slice(S_HALF + k * STATE_P, S_HALF + (k + 1) * STATE_P)
        bscr[n, rows, re_cols] = bbr[g]
        bscr[n, rows, im_cols] = bbi[g]
        cscr[n, rows, re_cols] = cre[g]
        cscr[n, rows, im_cols] = ncim[g]
        lamr_ref[n, :, re_cols] = jnp.broadcast_to(lbr[g:g + 1, :], (SUBLANES, STATE_P))
        lami_ref[n, :, re_cols] = jnp.broadcast_to(lbi[g:g + 1, :], (SUBLANES, STATE_P))
    for n in range(N_HALVES):
        bc_ref[n] = bscr[n].astype(BF16)
        cc_ref[n] = cscr[n].T.astype(BF16)


def _s5_prep(lam_re, lam_im, log_dt, b_re, b_im, c_re, c_im):
    return pl.pallas_call(
        _s5_prep_body,
        out_shape=[jax.ShapeDtypeStruct((N_HALVES, SUBLANES, S_HALF), F32),
                   jax.ShapeDtypeStruct((N_HALVES, SUBLANES, S_HALF), F32),
                   jax.ShapeDtypeStruct((N_HALVES, U_HALF, 2 * S_HALF), BF16),
                   jax.ShapeDtypeStruct((N_HALVES, 2 * S_HALF, U_HALF), BF16)],
        scratch_shapes=[pltpu.VMEM((N_HALVES, U_HALF, 2 * S_HALF), F32),
                        pltpu.VMEM((N_HALVES, U_HALF, 2 * S_HALF), F32)],
        name="s5_prep",
    )(lam_re, lam_im, log_dt.reshape(N_GROUPS, 1), b_re.transpose(0, 2, 1), b_im.transpose(0, 2, 1), c_re, c_im)


def kernel(x_prompt, x_sample, cache_conv, state_ssm_re, state_ssm_im, norm_mix, w_in, conv_w, ssm_lam_re, ssm_lam_im, ssm_log_dt, ssm_b_re, ssm_b_im, ssm_c_re, ssm_c_im, ssm_d, w_glu, b_glu, w_out, norm_ffn, w_gate, w_up, w_down, norm_final):
    assert norm_mix.shape[0] == 1, "single-layer trunk"
    n_st = N_GROUPS * STATE_P
    lamr, lami, bc, cc = _s5_prep(ssm_lam_re[0], ssm_lam_im[0], ssm_log_dt[0], ssm_b_re[0], ssm_b_im[0],
                                  ssm_c_re[0], ssm_c_im[0])
    mix_w = (norm_mix, w_in[0].astype(BF16), conv_w[0], lamr, lami, bc, cc, ssm_d, w_glu[0].astype(BF16),
             b_glu, w_out[0].astype(BF16))

    bp, lp, _ = x_prompt.shape
    bs, ls, _ = x_sample.shape

    x1p, convp, rep, imp, wg, wu, wd = _mixer(x_prompt, mix_w, cast=(w_gate[0], w_up[0], w_down[0]),
                                              nb=SUBLANES, tc=128)
    ffn_w = (norm_ffn, wg, wu, wd, norm_final.reshape(1, D_MODEL))
    sample_state = (cache_conv[0].transpose(1, 0, 2), state_ssm_re[0].reshape(bs, n_st),
                    state_ssm_im[0].reshape(bs, n_st))
    x1s, convs, res, ims = _mixer(x_sample, mix_w, sample_state, nb=64, tc=SUBLANES)

    yp = _ffn(x1p.reshape(bp * lp, D_MODEL), ffn_w, tm=1024).reshape(bp, lp, D_MODEL)
    ys = _ffn(x1s.reshape(bs * ls, D_MODEL), ffn_w, tm=512).reshape(bs, ls, D_MODEL)

    def cache_out(c):
        return c.transpose(1, 0, 2)[None]

    def state_out(h):
        return h.reshape(1, h.shape[0], N_GROUPS, STATE_P)

    return (yp, ys, cache_out(convp), state_out(rep), state_out(imp),
            cache_out(convs), state_out(res), state_out(ims))
```

```python
import functools

import jax
import jax.numpy as jnp
from jax import lax
from jax.experimental import pallas as pl
from jax.experimental.pallas import tpu as pltpu

D_MODEL = 1024
W_CONV = 512
W_SSM = 512
CONV_W = 3
N_GROUPS = 32
SSM_H = 16
STATE_P = 64
D_FF = 2816
EPS = 1e-5

SUBLANES = 8
BF16_ROWS = 16
GROUPS_PER_HALF = 16
N_HALVES = N_GROUPS // GROUPS_PER_HALF
U_HALF = GROUPS_PER_HALF * SSM_H
S_HALF = GROUPS_PER_HALF * STATE_P
S_COLS = 2 * N_GROUPS * STATE_P
SCAN_LANES = 512
FFN_CHUNKS = ((0, 1024), (1024, 1024), (2048, 768))
VMEM_LIMIT = 60 * 1024 * 1024

BF16 = jnp.bfloat16
F32 = jnp.float32


def _rms(x, g):
    y = x * lax.rsqrt(jnp.mean(x * x, axis=-1, keepdims=True) + EPS)
    return y * g


def _dot(a, b):
    return jnp.dot(a, b, preferred_element_type=F32)


def _tile_copies(hbm, buf, sem, step, slot, *, nb, tc, per_seq, to_vmem):
    copies = []
    for k in range(SUBLANES):
        if per_seq:
            h = hbm.at[k, pl.ds(step * tc, tc), :]
            v = buf.at[slot, :, k, :]
        else:
            h = hbm.at[pl.ds(step * nb, nb), k, :]
            v = buf.at[slot, k]
        src, dst = (h, v) if to_vmem else (v, h)
        copies.append(pltpu.make_async_copy(src, dst, sem.at[slot, k]))
    return copies


def _mixer_body(x_hbm, *refs, nb, tc, n_steps, carry, n_cast):
    if carry:
        cache_ref = re0_ref = im0_ref = None
    else:
        cache_ref, re0_ref, im0_ref, *refs = refs
    (nmix_ref, win_ref, convw_ref, lamr_ref, lami_ref, bc_ref, cc_ref, dskip_ref, wglu_ref, bglu_ref, wout_ref,
     *refs) = refs
    cast_in, refs = refs[:n_cast], refs[n_cast:]
    x1_hbm, convnew_hbm, hre_ref, him_ref, *refs = refs
    cast_out, refs = refs[:n_cast], refs[n_cast:]
    xbuf, obuf, hbuf, vbuf, insem, outsem, cachesem = refs
    tm = nb * tc

    for src, dst in zip(cast_in, cast_out):
        dst[...] = src[...].astype(BF16)

    i = pl.program_id(0)
    slot = lax.rem(i, 2)
    cp = functools.partial(_tile_copies, nb=nb, tc=tc, per_seq=carry)
    in_copies = functools.partial(cp, x_hbm, xbuf, insem, to_vmem=True)
    out_copies = functools.partial(cp, x1_hbm, obuf, outsem, to_vmem=False)

    def cache_copies(hbm, row0, to_vmem):
        copies = []
        for k in range(CONV_W - 1):
            h = hbm.at[0, pl.ds(0 if carry else i * nb, nb), k, :]
            v = vbuf.at[pl.ds(row0 + k * nb, nb), :]
            src, dst = (h, v) if to_vmem else (v, h)
            copies.append(pltpu.make_async_copy(src, dst, cachesem.at[int(to_vmem), k]))
        return copies

    @pl.when(i == 0)
    def _():
        for c in in_copies(0, 0):
            c.start()

    for c in in_copies(i, slot):
        c.wait()

    @pl.when(i + 1 < n_steps)
    def _():
        for c in in_copies(i + 1, 1 - slot):
            c.start()

    if carry:
        @pl.when(i == 0)
        def _():
            vbuf[0:2 * nb, :] = jnp.zeros((2 * nb, W_CONV), F32)
            hbuf[0:nb, :] = jnp.zeros((nb, S_COLS), F32)
    else:
        for c in cache_copies(cache_ref, 0, True):
            c.start()
        for n in range(N_HALVES):
            hbuf[0:nb, 2 * n * S_HALF:(2 * n + 1) * S_HALF] = re0_ref[:, n * S_HALF:(n + 1) * S_HALF]
            hbuf[0:nb, (2 * n + 1) * S_HALF:(2 * n + 2) * S_HALF] = im0_ref[:, n * S_HALF:(n + 1) * S_HALF]

    x = xbuf[slot].reshape(tm, D_MODEL)
    xn = _rms(x, nmix_ref[...]).astype(BF16)

    u = _dot(xn, win_ref[:, 3 * W_CONV:])
    ub = u.astype(BF16)
    for n in range(N_HALVES):
        hbuf[nb:nb + tm, 2 * n * S_HALF:(2 * n + 2) * S_HALF] = _dot(ub[:, n * U_HALF:(n + 1) * U_HALF], bc_ref[n])

    for n in range(N_HALVES):
        for c0 in range(0, S_HALF, SCAN_LANES):
            re_cols = slice(2 * n * S_HALF + c0, 2 * n * S_HALF + c0 + SCAN_LANES)
            im_cols = slice((2 * n + 1) * S_HALF + c0, (2 * n + 1) * S_HALF + c0 + SCAN_LANES)
            lr = lamr_ref[n, :, c0:c0 + SCAN_LANES]
            li = lami_ref[n, :, c0:c0 + SCAN_LANES]
            for r0 in range(0, nb, SUBLANES):
                hr = hbuf[r0:r0 + SUBLANES, re_cols]
                hi = hbuf[r0:r0 + SUBLANES, im_cols]
                for t in range(tc):
                    rows = slice(nb + t * nb + r0, nb + t * nb + r0 + SUBLANES)
                    hr, hi = (lr * hr - li * hi + hbuf[rows, re_cols],
                              lr * hi + li * hr + hbuf[rows, im_cols])
                    hbuf[rows, re_cols] = hr
                    hbuf[rows, im_cols] = hi

    proj = _dot(xn, win_ref[:, 0:3 * W_CONV])
    b_g = proj[:, 0:W_CONV]
    c_g = proj[:, W_CONV:2 * W_CONV]
    xc = proj[:, 2 * W_CONV:3 * W_CONV]
    v = c_g * xc
    vbuf[2 * nb:2 * nb + tm, :] = v
    if not carry:
        for c in cache_copies(cache_ref, 0, True):
            c.wait()
    conv = (convw_ref[0:1, :] * vbuf[0:tm, :] + convw_ref[1:2, :] * vbuf[nb:nb + tm, :]
            + convw_ref[2:3, :] * v)
    conv_out = (b_g * conv).astype(BF16)
    if carry:
        vbuf[0:2 * nb, :] = vbuf[tm:tm + 2 * nb, :]
    else:
        for c in cache_copies(convnew_hbm, tm, False):
            c.start()
    x1 = x + _dot(conv_out, wout_ref[0:W_CONV, :])

    last = hbuf[tm:tm + nb, :]
    for n in range(N_HALVES):
        hre_ref[:, n * S_HALF:(n + 1) * S_HALF] = last[:, 2 * n * S_HALF:(2 * n + 1) * S_HALF]
        him_ref[:, n * S_HALF:(n + 1) * S_HALF] = last[:, (2 * n + 1) * S_HALF:(2 * n + 2) * S_HALF]

    ys = []
    for n in range(N_HALVES):
        hb = hbuf[nb:nb + tm, 2 * n * S_HALF:(2 * n + 2) * S_HALF].astype(BF16)
        ys.append(_dot(hb, cc_ref[n]))
    hbuf[0:nb, :] = last
    y = jnp.concatenate(ys, axis=-1) + dskip_ref[...] * u
    z = jax.nn.gelu(y)
    gate = jax.nn.sigmoid(_dot(z.astype(BF16), wglu_ref[...]) + bglu_ref[...])
    ssm_out = (z * gate).astype(BF16)

    x1 = x1 + _dot(ssm_out, wout_ref[W_CONV:, :])

    @pl.when(i >= 1)
    def _():
        for c in out_copies(i - 1, 0):
            c.wait()

    obuf[0] = x1.reshape(tc, nb, D_MODEL)
    for c in out_copies(i, 0):
        c.start()

    if not carry:
        for c in cache_copies(convnew_hbm, tm, False):
            c.wait()

    @pl.when(i == n_steps - 1)
    def _():
        if carry:
            for c in cache_copies(convnew_hbm, tm, False):
                c.start()
            for c in cache_copies(convnew_hbm, tm, False):
                c.wait()
        for c in out_copies(i, 0):
            c.wait()


def _resident(shape):
    nd = len(shape)
    return pl.BlockSpec(shape, lambda i: (0,) * nd, pipeline_mode=pl.Buffered(1))


def _mixer(x, wts, state=None, cast=(), *, nb, tc):
    n_seq, n_time, _ = x.shape
    carry = state is None
    if carry:
        assert n_seq == nb == SUBLANES and n_time % tc == 0
        n_steps = n_time // tc
        seq_blk = lambda i: 0
    else:
        assert n_time == tc == SUBLANES and n_seq % nb == 0
        n_steps = n_seq // nb
        seq_blk = lambda i: i
    tm = nb * tc
    n_st = N_GROUPS * STATE_P
    body = functools.partial(_mixer_body, nb=nb, tc=tc, n_steps=n_steps, carry=carry, n_cast=len(cast))
    any_spec = pl.BlockSpec(memory_space=pl.ANY)
    st_spec = pl.BlockSpec((nb, n_st), lambda i: (seq_blk(i), 0))
    state_specs = [] if carry else [any_spec, st_spec, st_spec]
    for w in cast:
        assert w.shape[0] % (n_steps * BF16_ROWS) == 0, w.shape
    cast_specs = [pl.BlockSpec((w.shape[0] // n_steps, w.shape[1]), lambda i: (i, 0)) for w in cast]
    return pl.pallas_call(
        body,
        grid=(n_steps,),
        in_specs=[any_spec] + state_specs + [_resident(w.shape) for w in wts] + cast_specs,
        out_specs=[any_spec, any_spec, st_spec, st_spec] + cast_specs,
        out_shape=[jax.ShapeDtypeStruct(x.shape, F32),
                   jax.ShapeDtypeStruct((1, n_seq, CONV_W - 1, W_CONV), F32),
                   jax.ShapeDtypeStruct((n_seq, n_st), F32),
                   jax.ShapeDtypeStruct((n_seq, n_st), F32)]
                  + [jax.ShapeDtypeStruct(w.shape, BF16) for w in cast],
        scratch_shapes=[pltpu.VMEM((2, tc, nb, D_MODEL), F32),
                        pltpu.VMEM((1, tc, nb, D_MODEL), F32),
                        pltpu.VMEM((nb + tm, S_COLS), F32),
                        pltpu.VMEM((2 * nb + tm, W_CONV), F32),
                        pltpu.SemaphoreType.DMA((2, SUBLANES)),
                        pltpu.SemaphoreType.DMA((1, SUBLANES)),
                        pltpu.SemaphoreType.DMA((2, CONV_W - 1))],
        compiler_params=pltpu.CompilerParams(dimension_semantics=("arbitrary",),
                                             vmem_limit_bytes=VMEM_LIMIT),
        name="mixer_carry" if carry else "mixer_step",
    )(x, *(() if carry else state), *wts, *cast)


def _ffn_body(x_ref, nffn_ref, wg_ref, wu_ref, wd_ref, nfin_ref, o_ref):
    x = x_ref[...]
    hn = _rms(x, nffn_ref[...]).astype(BF16)
    acc = x
    for s, w in FFN_CHUNKS:
        g = _dot(hn, wg_ref[:, s:s + w])
        up = _dot(hn, wu_ref[:, s:s + w])
        a = (g * jax.nn.sigmoid(g) * up).astype(BF16)
        acc = acc + _dot(a, wd_ref[s:s + w, :])
    o_ref[...] = _rms(acc, nfin_ref[...])


def _ffn(x2d, wts, *, tm):
    n_rows = x2d.shape[0]
    assert n_rows % tm == 0
    row_spec = pl.BlockSpec((tm, D_MODEL), lambda i: (i, 0))
    return pl.pallas_call(
        _ffn_body,
        grid=(n_rows // tm,),
        in_specs=[row_spec] + [_resident(w.shape) for w in wts],
        out_specs=row_spec,
        out_shape=jax.ShapeDtypeStruct(x2d.shape, F32),
        compiler_params=pltpu.CompilerParams(dimension_semantics=("arbitrary",),
                                             vmem_limit_bytes=VMEM_LIMIT),
        name="ffn",
    )(x2d, *wts)


def _s5_prep_body(lre_ref, lim_ref, logdt_ref, bre_ref, bim_ref, cre_ref, cim_ref,
                  lamr_ref, lami_ref, bc_ref, cc_ref, fscr, brscr, biscr, cscr):
    lre = lre_ref[...]
    lim = lim_ref[...]
    dt = jnp.exp(logdt_ref[...])
    mag = jnp.exp(lre * dt)
    lbr = mag * jnp.cos(lim * dt)
    lbi = mag * jnp.sin(lim * dt)
    den = lre * lre + lim * lim
    fr = ((lbr - 1.0) * lre + lbi * lim) / den
    fi = (lbi * lre - (lbr - 1.0) * lim) / den
    cre = cre_ref[...]
    ncim = -cim_ref[...]

    def lanes(a, g):
        return jnp.broadcast_to(a[g:g + 1, :], (SUBLANES, STATE_P))

    brscr[...] = jnp.zeros_like(brscr)
    biscr[...] = jnp.zeros_like(biscr)
    cscr[...] = jnp.zeros_like(cscr)
    for g in range(N_GROUPS):
        n, k = divmod(g, GROUPS_PER_HALF)
        chans = slice(k * SSM_H, (k + 1) * SSM_H)
        states = slice(k * STATE_P, (k + 1) * STATE_P)
        im_states = slice(S_HALF + k * STATE_P, S_HALF + (k + 1) * STATE_P)
        lamr_ref[n, :, states] = lanes(lbr, g)
        lami_ref[n, :, states] = lanes(lbi, g)
        fscr[0, n, :, states] = lanes(fr, g)
        fscr[1, n, :, states] = lanes(fi, g)
        brscr[n, states, chans] = bre_ref[g]
        biscr[n, states, chans] = bim_ref[g]
        cscr[n, chans, states] = cre[g]
        cscr[n, chans, im_states] = ncim[g]
    for n in range(N_HALVES):
        br = brscr[n].T
        bi = biscr[n].T
        frl = fscr[0, n, 0:1, :]
        fil = fscr[1, n, 0:1, :]
        bc_ref[n, :, 0:S_HALF] = (frl * br - fil * bi).astype(BF16)
        bc_ref[n, :, S_HALF:] = (frl * bi + fil * br).astype(BF16)
        cc_ref[n] = cscr[n].T.astype(BF16)


def _s5_prep(lam_re, lam_im, log_dt, b_re, b_im, c_re, c_im):
    return pl.pallas_call(
        _s5_prep_body,
        out_shape=[jax.ShapeDtypeStruct((N_HALVES, SUBLANES, S_HALF), F32),
                   jax.ShapeDtypeStruct((N_HALVES, SUBLANES, S_HALF), F32),
                   jax.ShapeDtypeStruct((N_HALVES, U_HALF, 2 * S_HALF), BF16),
                   jax.ShapeDtypeStruct((N_HALVES, 2 * S_HALF, U_HALF), BF16)],
        scratch_shapes=[pltpu.VMEM((2, N_HALVES, SUBLANES, S_HALF), F32),
                        pltpu.VMEM((N_HALVES, S_HALF, U_HALF), F32),
                        pltpu.VMEM((N_HALVES, S_HALF, U_HALF), F32),
                        pltpu.VMEM((N_HALVES, U_HALF, 2 * S_HALF), F32)],
        name="s5_prep",
    )(lam_re, lam_im, log_dt.reshape(N_GROUPS, 1), b_re, b_im, c_re, c_im)


def kernel(x_prompt, x_sample, cache_conv, state_ssm_re, state_ssm_im, norm_mix, w_in, conv_w, ssm_lam_re, ssm_lam_im, ssm_log_dt, ssm_b_re, ssm_b_im, ssm_c_re, ssm_c_im, ssm_d, w_glu, b_glu, w_out, norm_ffn, w_gate, w_up, w_down, norm_final):
    assert norm_mix.shape[0] == 1, "single-layer trunk"
    n_st = N_GROUPS * STATE_P
    lamr, lami, bc, cc = _s5_prep(ssm_lam_re[0], ssm_lam_im[0], ssm_log_dt[0], ssm_b_re[0], ssm_b_im[0],
                                  ssm_c_re[0], ssm_c_im[0])
    mix_w = (norm_mix, w_in[0].astype(BF16), conv_w[0], lamr, lami, bc, cc, ssm_d, w_glu[0].astype(BF16),
             b_glu, w_out[0].astype(BF16))

    bp, lp, _ = x_prompt.shape
    bs, ls, _ = x_sample.shape

    x1p, convp, rep, imp, wg, wu, wd = _mixer(x_prompt, mix_w, cast=(w_gate[0], w_up[0], w_down[0]),
                                              nb=SUBLANES, tc=128)
    ffn_w = (norm_ffn, wg, wu, wd, norm_final.reshape(1, D_MODEL))
    sample_state = (cache_conv, state_ssm_re[0].reshape(bs, n_st),
                    state_ssm_im[0].reshape(bs, n_st))
    x1s, convs, res, ims = _mixer(x_sample, mix_w, sample_state, nb=64, tc=SUBLANES)

    yp = _ffn(x1p.reshape(bp * lp, D_MODEL), ffn_w, tm=1024).reshape(bp, lp, D_MODEL)
    ys = _ffn(x1s.reshape(bs * ls, D_MODEL), ffn_w, tm=512).reshape(bs, ls, D_MODEL)

    def state_out(h):
        return h.reshape(1, h.shape[0], N_GROUPS, STATE_P)

    return (yp, ys, convp, state_out(rep), state_out(imp),
            convs, state_out(res), state_out(ims))
```

```python
import functools

import jax
import jax.numpy as jnp
from jax import lax
from jax.experimental import pallas as pl
from jax.experimental.pallas import tpu as pltpu

D_MODEL = 1024
W_CONV = 512
W_SSM = 512
CONV_W = 3
N_GROUPS = 32
SSM_H = 16
STATE_P = 64
D_FF = 2816
EPS = 1e-5

SUBLANES = 8
BF16_ROWS = 16
GROUPS_PER_HALF = 16
N_HALVES = N_GROUPS // GROUPS_PER_HALF
U_HALF = GROUPS_PER_HALF * SSM_H
S_HALF = GROUPS_PER_HALF * STATE_P
S_COLS = 2 * N_GROUPS * STATE_P
SCAN_LANES = 512
FFN_CHUNKS = ((0, 1024), (1024, 1024), (2048, 768))
VMEM_LIMIT = 60 * 1024 * 1024

BF16 = jnp.bfloat16
F32 = jnp.float32


def _rms(x, g):
    y = x * lax.rsqrt(jnp.mean(x * x, axis=-1, keepdims=True) + EPS)
    return y * g


def _dot(a, b):
    return jnp.dot(a, b, preferred_element_type=F32)


def _tile_copies(hbm, buf, sem, step, slot, *, nb, tc, per_seq, to_vmem):
    copies = []
    for k in range(SUBLANES):
        if per_seq:
            h = hbm.at[k, pl.ds(step * tc, tc), :]
            v = buf.at[slot, :, k, :]
        else:
            h = hbm.at[pl.ds(step * nb, nb), k, :]
            v = buf.at[slot, k]
        src, dst = (h, v) if to_vmem else (v, h)
        copies.append(pltpu.make_async_copy(src, dst, sem.at[slot, k]))
    return copies


def _mixer_body(x_hbm, *refs, nb, tc, n_steps, carry, n_cast):
    if carry:
        cache_ref = re0_ref = im0_ref = None
    else:
        cache_ref, re0_ref, im0_ref, *refs = refs
    (nmix_ref, win_ref, convw_ref, lamr_ref, lami_ref, bc_ref, cc_ref, dskip_ref, wglu_ref, bglu_ref, wout_ref,
     *refs) = refs
    cast_in, refs = refs[:n_cast], refs[n_cast:]
    x1_hbm, convnew_hbm, hre_ref, him_ref, *refs = refs
    cast_out, refs = refs[:n_cast], refs[n_cast:]
    xbuf, obuf, hbuf, vbuf, insem, outsem, cachesem = refs
    tm = nb * tc

    for src, dst in zip(cast_in, cast_out):
        dst[...] = src[...].astype(BF16)

    i = pl.program_id(0)
    slot = lax.rem(i, 2)
    cp = functools.partial(_tile_copies, nb=nb, tc=tc, per_seq=carry)
    in_copies = functools.partial(cp, x_hbm, xbuf, insem, to_vmem=True)
    out_copies = functools.partial(cp, x1_hbm, obuf, outsem, to_vmem=False)

    def cache_copies(hbm, row0, to_vmem):
        copies = []
        for k in range(CONV_W - 1):
            h = hbm.at[0, pl.ds(0 if carry else i * nb, nb), k, :]
            v = vbuf.at[pl.ds(row0 + k * nb, nb), :]
            src, dst = (h, v) if to_vmem else (v, h)
            copies.append(pltpu.make_async_copy(src, dst, cachesem.at[int(to_vmem), k]))
        return copies

    @pl.when(i == 0)
    def _():
        for c in in_copies(0, 0):
            c.start()

    for c in in_copies(i, slot):
        c.wait()

    @pl.when(i + 1 < n_steps)
    def _():
        for c in in_copies(i + 1, 1 - slot):
            c.start()

    if carry:
        @pl.when(i == 0)
        def _():
            vbuf[0:2 * nb, :] = jnp.zeros((2 * nb, W_CONV), F32)
            hbuf[0:nb, :] = jnp.zeros((nb, S_COLS), F32)
    else:
        for c in cache_copies(cache_ref, 0, True):
            c.start()
        for n in range(N_HALVES):
            hbuf[0:nb, 2 * n * S_HALF:(2 * n + 1) * S_HALF] = re0_ref[n * S_HALF:(n + 1) * S_HALF, :].T
            hbuf[0:nb, (2 * n + 1) * S_HALF:(2 * n + 2) * S_HALF] = im0_ref[n * S_HALF:(n + 1) * S_HALF, :].T

    x = xbuf[slot].reshape(tm, D_MODEL)
    xn = _rms(x, nmix_ref[...]).astype(BF16)

    u = _dot(xn, win_ref[:, 3 * W_CONV:])
    ub = u.astype(BF16)
    for n in range(N_HALVES):
        hbuf[nb:nb + tm, 2 * n * S_HALF:(2 * n + 2) * S_HALF] = _dot(ub[:, n * U_HALF:(n + 1) * U_HALF], bc_ref[n])

    for n in range(N_HALVES):
        for c0 in range(0, S_HALF, SCAN_LANES):
            re_cols = slice(2 * n * S_HALF + c0, 2 * n * S_HALF + c0 + SCAN_LANES)
            im_cols = slice((2 * n + 1) * S_HALF + c0, (2 * n + 1) * S_HALF + c0 + SCAN_LANES)
            lr = lamr_ref[n, :, c0:c0 + SCAN_LANES]
            li = lami_ref[n, :, c0:c0 + SCAN_LANES]
            for r0 in range(0, nb, SUBLANES):
                hr = hbuf[r0:r0 + SUBLANES, re_cols]
                hi = hbuf[r0:r0 + SUBLANES, im_cols]
                for t in range(tc):
                    rows = slice(nb + t * nb + r0, nb + t * nb + r0 + SUBLANES)
                    hr, hi = (lr * hr - li * hi + hbuf[rows, re_cols],
                              lr * hi + li * hr + hbuf[rows, im_cols])
                    hbuf[rows, re_cols] = hr
                    hbuf[rows, im_cols] = hi

    proj = _dot(xn, win_ref[:, 0:3 * W_CONV])
    b_g = proj[:, 0:W_CONV]
    c_g = proj[:, W_CONV:2 * W_CONV]
    xc = proj[:, 2 * W_CONV:3 * W_CONV]
    v = c_g * xc
    vbuf[2 * nb:2 * nb + tm, :] = v
    if not carry:
        for c in cache_copies(cache_ref, 0, True):
            c.wait()
    conv = convw_ref[0] * vbuf[0:tm, :] + convw_ref[1] * vbuf[nb:nb + tm, :] + convw_ref[2] * v
    conv_out = (b_g * conv).astype(BF16)
    if carry:
        vbuf[0:2 * nb, :] = vbuf[tm:tm + 2 * nb, :]
    else:
        for c in cache_copies(convnew_hbm, tm, False):
            c.start()
    x1 = x + _dot(conv_out, wout_ref[0:W_CONV, :])

    last = hbuf[tm:tm + nb, :]
    for n in range(N_HALVES):
        last_re = last[:, 2 * n * S_HALF:(2 * n + 1) * S_HALF]
        last_im = last[:, (2 * n + 1) * S_HALF:(2 * n + 2) * S_HALF]
        if carry:
            hre_ref[:, n * S_HALF:(n + 1) * S_HALF] = last_re
            him_ref[:, n * S_HALF:(n + 1) * S_HALF] = last_im
        else:
            hre_ref[n * S_HALF:(n + 1) * S_HALF, :] = last_re.T
            him_ref[n * S_HALF:(n + 1) * S_HALF, :] = last_im.T

    ys = []
    for n in range(N_HALVES):
        hb = hbuf[nb:nb + tm, 2 * n * S_HALF:(2 * n + 2) * S_HALF].astype(BF16)
        ys.append(_dot(hb, cc_ref[n]))
    hbuf[0:nb, :] = last
    y = jnp.concatenate(ys, axis=-1) + dskip_ref[...] * u
    z = jax.nn.gelu(y)
    gate = jax.nn.sigmoid(_dot(z.astype(BF16), wglu_ref[...]) + bglu_ref[...])
    ssm_out = (z * gate).astype(BF16)

    x1 = x1 + _dot(ssm_out, wout_ref[W_CONV:, :])

    @pl.when(i >= 1)
    def _():
        for c in out_copies(i - 1, 0):
            c.wait()

    obuf[0] = x1.reshape(tc, nb, D_MODEL)
    for c in out_copies(i, 0):
        c.start()

    if not carry:
        for c in cache_copies(convnew_hbm, tm, False):
            c.wait()

    @pl.when(i == n_steps - 1)
    def _():
        if carry:
            for c in cache_copies(convnew_hbm, tm, False):
                c.start()
            for c in cache_copies(convnew_hbm, tm, False):
                c.wait()
        for c in out_copies(i, 0):
            c.wait()


def _resident(shape):
    nd = len(shape)
    return pl.BlockSpec(shape, lambda i: (0,) * nd, pipeline_mode=pl.Buffered(1))


def _mixer(x, wts, state=None, cast=(), *, nb, tc):
    n_seq, n_time, _ = x.shape
    carry = state is None
    if carry:
        assert n_seq == nb == SUBLANES and n_time % tc == 0
        n_steps = n_time // tc
    else:
        assert n_time == tc == SUBLANES and n_seq % nb == 0
        n_steps = n_seq // nb
    tm = nb * tc
    n_st = N_GROUPS * STATE_P
    body = functools.partial(_mixer_body, nb=nb, tc=tc, n_steps=n_steps, carry=carry, n_cast=len(cast))
    any_spec = pl.BlockSpec(memory_space=pl.ANY)
    if carry:
        st_spec = pl.BlockSpec((nb, n_st), lambda i: (0, 0))
        st_shape = jax.ShapeDtypeStruct((n_seq, n_st), F32)
    else:
        st_spec = pl.BlockSpec((n_st, nb), lambda i: (0, i))
        st_shape = jax.ShapeDtypeStruct((n_st, n_seq), F32)
    state_specs = [] if carry else [any_spec, st_spec, st_spec]
    for w in cast:
        assert w.shape[0] % (n_steps * BF16_ROWS) == 0, w.shape
    cast_specs = [pl.BlockSpec((w.shape[0] // n_steps, w.shape[1]), lambda i: (i, 0)) for w in cast]
    return pl.pallas_call(
        body,
        grid=(n_steps,),
        in_specs=[any_spec] + state_specs + [_resident(w.shape) for w in wts] + cast_specs,
        out_specs=[any_spec, any_spec, st_spec, st_spec] + cast_specs,
        out_shape=[jax.ShapeDtypeStruct(x.shape, F32),
                   jax.ShapeDtypeStruct((1, n_seq, CONV_W - 1, W_CONV), F32),
                   st_shape, st_shape]
                  + [jax.ShapeDtypeStruct(w.shape, BF16) for w in cast],
        scratch_shapes=[pltpu.VMEM((2, tc, nb, D_MODEL), F32),
                        pltpu.VMEM((1, tc, nb, D_MODEL), F32),
                        pltpu.VMEM((nb + tm, S_COLS), F32),
                        pltpu.VMEM((2 * nb + tm, W_CONV), F32),
                        pltpu.SemaphoreType.DMA((2, SUBLANES)),
                        pltpu.SemaphoreType.DMA((1, SUBLANES)),
                        pltpu.SemaphoreType.DMA((2, CONV_W - 1))],
        compiler_params=pltpu.CompilerParams(dimension_semantics=("arbitrary",),
                                             vmem_limit_bytes=VMEM_LIMIT),
        name="mixer_carry" if carry else "mixer_step",
    )(x, *(() if carry else state), *wts, *cast)


def _ffn_body(x_ref, nffn_ref, wg_ref, wu_ref, wd_ref, nfin_ref, o_ref):
    x = x_ref[...]
    hn = _rms(x, nffn_ref[...]).astype(BF16)
    acc = x
    for s, w in FFN_CHUNKS:
        g = _dot(hn, wg_ref[:, s:s + w])
        up = _dot(hn, wu_ref[:, s:s + w])
        a = (g * jax.nn.sigmoid(g) * up).astype(BF16)
        acc = acc + _dot(a, wd_ref[s:s + w, :])
    o_ref[...] = _rms(acc, nfin_ref[...])


def _ffn(x2d, wts, *, tm):
    n_rows = x2d.shape[0]
    assert n_rows % tm == 0
    row_spec = pl.BlockSpec((tm, D_MODEL), lambda i: (i, 0))
    return pl.pallas_call(
        _ffn_body,
        grid=(n_rows // tm,),
        in_specs=[row_spec] + [_resident(w.shape) for w in wts],
        out_specs=row_spec,
        out_shape=jax.ShapeDtypeStruct(x2d.shape, F32),
        compiler_params=pltpu.CompilerParams(dimension_semantics=("arbitrary",),
                                             vmem_limit_bytes=VMEM_LIMIT),
        name="ffn",
    )(x2d, *wts)


def _s5_prep_body(lre_ref, lim_ref, logdt_ref, btre_ref, btim_ref, cre_ref, cim_ref,
                  lamr_ref, lami_ref, bc_ref, cc_ref, bscr, cscr):
    lre = lre_ref[...]
    lim = lim_ref[...]
    grp = (lax.broadcasted_iota(jnp.int32, (N_GROUPS, N_GROUPS), 0)
           == lax.broadcasted_iota(jnp.int32, (N_GROUPS, N_GROUPS), 1))
    logdt = jnp.sum(jnp.where(grp, jnp.broadcast_to(logdt_ref[...], (N_GROUPS, N_GROUPS)), 0.0),
                    axis=1, keepdims=True)
    dt = jnp.exp(logdt)
    mag = jnp.exp(lre * dt)
    lbr = mag * jnp.cos(lim * dt)
    lbi = mag * jnp.sin(lim * dt)
    den = lre * lre + lim * lim
    fr = ((lbr - 1.0) * lre + lbi * lim) / den
    fi = (lbi * lre - (lbr - 1.0) * lim) / den
    frb = fr[:, None, :]
    fib = fi[:, None, :]
    bbr = frb * btre_ref[...] - fib * btim_ref[...]
    bbi = frb * btim_ref[...] + fib * btre_ref[...]
    cre = cre_ref[...]
    ncim = -cim_ref[...]

    bscr[...] = jnp.zeros_like(bscr)
    cscr[...] = jnp.zeros_like(cscr)
    for g in range(N_GROUPS):
        n, k = divmod(g, GROUPS_PER_HALF)
        chans = slice(k * SSM_H, (k + 1) * SSM_H)
        states = slice(k * STATE_P, (k + 1) * STATE_P)
        im_states = slice(S_HALF + k * STATE_P, S_HALF + (k + 1) * STATE_P)
        bscr[n, chans, states] = bbr[g]
        bscr[n, chans, im_states] = bbi[g]
        cscr[n, chans, states] = cre[g]
        cscr[n, chans, im_states] = ncim[g]
        lamr_ref[n, :, states] = jnp.broadcast_to(lbr[g:g + 1, :], (SUBLANES, STATE_P))
        lami_ref[n, :, states] = jnp.broadcast_to(lbi[g:g + 1, :], (SUBLANES, STATE_P))
    for n in range(N_HALVES):
        bc_ref[n] = bscr[n].astype(BF16)
        cc_ref[n] = cscr[n].T.astype(BF16)


def _s5_prep(lam_re, lam_im, log_dt, b_re, b_im, c_re, c_im):
    return pl.pallas_call(
        _s5_prep_body,
        out_shape=[jax.ShapeDtypeStruct((N_HALVES, SUBLANES, S_HALF), F32),
                   jax.ShapeDtypeStruct((N_HALVES, SUBLANES, S_HALF), F32),
                   jax.ShapeDtypeStruct((N_HALVES, U_HALF, 2 * S_HALF), BF16),
                   jax.ShapeDtypeStruct((N_HALVES, 2 * S_HALF, U_HALF), BF16)],
        scratch_shapes=[pltpu.VMEM((N_HALVES, U_HALF, 2 * S_HALF), F32),
                        pltpu.VMEM((N_HALVES, U_HALF, 2 * S_HALF), F32)],
        name="s5_prep",
    )(lam_re, lam_im, log_dt.reshape(1, N_GROUPS), b_re.transpose(0, 2, 1), b_im.transpose(0, 2, 1), c_re, c_im)


def kernel(x_prompt, x_sample, cache_conv, state_ssm_re, state_ssm_im, norm_mix, w_in, conv_w, ssm_lam_re, ssm_lam_im, ssm_log_dt, ssm_b_re, ssm_b_im, ssm_c_re, ssm_c_im, ssm_d, w_glu, b_glu, w_out, norm_ffn, w_gate, w_up, w_down, norm_final):
    assert norm_mix.shape[0] == 1, "single-layer trunk"
    n_st = N_GROUPS * STATE_P
    lamr, lami, bc, cc = _s5_prep(ssm_lam_re[0], ssm_lam_im[0], ssm_log_dt[0], ssm_b_re[0], ssm_b_im[0],
                                  ssm_c_re[0], ssm_c_im[0])
    mix_w = (norm_mix, w_in[0].astype(BF16), conv_w[0][:, None, :], lamr, lami, bc, cc, ssm_d, w_glu[0].astype(BF16),
             b_glu, w_out[0].astype(BF16))

    bp, lp, _ = x_prompt.shape
    bs, ls, _ = x_sample.shape

    x1p, convp, rep, imp, wg, wu, wd = _mixer(x_prompt, mix_w, cast=(w_gate[0], w_up[0], w_down[0]),
                                              nb=SUBLANES, tc=128)
    ffn_w = (norm_ffn, wg, wu, wd, norm_final.reshape(1, D_MODEL))
    def state_major(h):
        return h[0].transpose(1, 2, 0).reshape(n_st, h.shape[1])

    sample_state = (cache_conv, state_major(state_ssm_re), state_major(state_ssm_im))
    x1s, convs, res, ims = _mixer(x_sample, mix_w, sample_state, nb=128, tc=SUBLANES)

    yp = _ffn(x1p.reshape(bp * lp, D_MODEL), ffn_w, tm=1024).reshape(bp, lp, D_MODEL)
    ys = _ffn(x1s.reshape(bs * ls, D_MODEL), ffn_w, tm=512).reshape(bs, ls, D_MODEL)

    def seq_major(h):
        return h.reshape(N_GROUPS, STATE_P, h.shape[1]).transpose(2, 0, 1)[None]

    return (yp, ys, convp, rep.reshape(1, bp, N_GROUPS, STATE_P), imp.reshape(1, bp, N_GROUPS, STATE_P),
            convs, seq_major(res), seq_major(ims))
```

```python
import functools

import jax
import jax.numpy as jnp
from jax import lax
from jax.experimental import pallas as pl
from jax.experimental.pallas import tpu as pltpu

D_MODEL = 1024
W_CONV = 512
W_SSM = 512
CONV_W = 3
N_GROUPS = 32
SSM_H = 16
STATE_P = 64
D_FF = 2816
EPS = 1e-5

SUBLANES = 8
BF16_ROWS = 16
GROUPS_PER_HALF = 16
N_HALVES = N_GROUPS // GROUPS_PER_HALF
U_HALF = GROUPS_PER_HALF * SSM_H
S_HALF = GROUPS_PER_HALF * STATE_P
S_COLS = 2 * N_GROUPS * STATE_P
SCAN_LANES = 512
FFN_CHUNKS = ((0, 1024), (1024, 1024), (2048, 768))
VMEM_LIMIT = 60 * 1024 * 1024

BF16 = jnp.bfloat16
F32 = jnp.float32


def _rms(x, g):
    y = x * lax.rsqrt(jnp.mean(x * x, axis=-1, keepdims=True) + EPS)
    return y * g


def _dot(a, b):
    return jnp.dot(a, b, preferred_element_type=F32)


def _tile_copies(hbm, buf, sem, step, slot, *, nb, tc, per_seq, to_vmem):
    copies = []
    for k in range(SUBLANES):
        if per_seq:
            h = hbm.at[k, pl.ds(step * tc, tc), :]
            v = buf.at[slot, :, k, :]
        else:
            h = hbm.at[pl.ds(step * nb, nb), k, :]
            v = buf.at[slot, k]
        src, dst = (h, v) if to_vmem else (v, h)
        copies.append(pltpu.make_async_copy(src, dst, sem.at[slot, k]))
    return copies


def _mixer_body(x_hbm, *refs, nb, tc, n_steps, carry, n_cast):
    if carry:
        cache_ref = re0_ref = im0_ref = None
    else:
        cache_ref, re0_ref, im0_ref, *refs = refs
    (nmix_ref, win_ref, convw_ref, lamr_ref, lami_ref, bc_ref, cc_ref, dskip_ref, wglu_ref, bglu_ref, wout_ref,
     *refs) = refs
    cast_in, refs = refs[:n_cast], refs[n_cast:]
    x1_hbm, convnew_hbm, hre_ref, him_ref, *refs = refs
    cast_out, refs = refs[:n_cast], refs[n_cast:]
    xbuf, obuf, hbuf, vbuf, insem, outsem, cachesem = refs
    tm = nb * tc

    for src, dst in zip(cast_in, cast_out):
        dst[...] = src[...].astype(BF16)

    i = pl.program_id(0)
    slot = lax.rem(i, 2)
    cp = functools.partial(_tile_copies, nb=nb, tc=tc, per_seq=carry)
    in_copies = functools.partial(cp, x_hbm, xbuf, insem, to_vmem=True)
    out_copies = functools.partial(cp, x1_hbm, obuf, outsem, to_vmem=False)

    def cache_copies(hbm, row0, to_vmem):
        copies = []
        for k in range(CONV_W - 1):
            h = hbm.at[0, pl.ds(0 if carry else i * nb, nb), k, :]
            v = vbuf.at[pl.ds(row0 + k * nb, nb), :]
            src, dst = (h, v) if to_vmem else (v, h)
            copies.append(pltpu.make_async_copy(src, dst, cachesem.at[int(to_vmem), k]))
        return copies

    @pl.when(i == 0)
    def _():
        for c in in_copies(0, 0):
            c.start()

    for c in in_copies(i, slot):
        c.wait()

    @pl.when(i + 1 < n_steps)
    def _():
        for c in in_copies(i + 1, 1 - slot):
            c.start()

    if carry:
        @pl.when(i == 0)
        def _():
            vbuf[0:2 * nb, :] = jnp.zeros((2 * nb, W_CONV), F32)
            hbuf[0:nb, :] = jnp.zeros((nb, S_COLS), F32)
    else:
        for c in cache_copies(cache_ref, 0, True):
            c.start()
        for n in range(N_HALVES):
            hbuf[0:nb, 2 * n * S_HALF:(2 * n + 1) * S_HALF] = re0_ref[n * S_HALF:(n + 1) * S_HALF, :].T
            hbuf[0:nb, (2 * n + 1) * S_HALF:(2 * n + 2) * S_HALF] = im0_ref[n * S_HALF:(n + 1) * S_HALF, :].T

    x = xbuf[slot].reshape(tm, D_MODEL)
    xn = _rms(x, nmix_ref[...]).astype(BF16)

    u = _dot(xn, win_ref[:, 3 * W_CONV:])
    ub = u.astype(BF16)
    for n in range(N_HALVES):
        hbuf[nb:nb + tm, 2 * n * S_HALF:(2 * n + 2) * S_HALF] = _dot(ub[:, n * U_HALF:(n + 1) * U_HALF], bc_ref[n])

    for n in range(N_HALVES):
        for c0 in range(0, S_HALF, SCAN_LANES):
            re_cols = slice(2 * n * S_HALF + c0, 2 * n * S_HALF + c0 + SCAN_LANES)
            im_cols = slice((2 * n + 1) * S_HALF + c0, (2 * n + 1) * S_HALF + c0 + SCAN_LANES)
            lr = lamr_ref[n, :, c0:c0 + SCAN_LANES]
            li = lami_ref[n, :, c0:c0 + SCAN_LANES]
            for r0 in range(0, nb, SUBLANES):
                hr = hbuf[r0:r0 + SUBLANES, re_cols]
                hi = hbuf[r0:r0 + SUBLANES, im_cols]
                for t in range(tc):
                    rows = slice(nb + t * nb + r0, nb + t * nb + r0 + SUBLANES)
                    hr, hi = (lr * hr - li * hi + hbuf[rows, re_cols],
                              lr * hi + li * hr + hbuf[rows, im_cols])
                    hbuf[rows, re_cols] = hr
                    hbuf[rows, im_cols] = hi

    proj = _dot(xn, win_ref[:, 0:3 * W_CONV])
    b_g = proj[:, 0:W_CONV]
    c_g = proj[:, W_CONV:2 * W_CONV]
    xc = proj[:, 2 * W_CONV:3 * W_CONV]
    v = c_g * xc
    vbuf[2 * nb:2 * nb + tm, :] = v
    if not carry:
        for c in cache_copies(cache_ref, 0, True):
            c.wait()
    conv = convw_ref[0] * vbuf[0:tm, :] + convw_ref[1] * vbuf[nb:nb + tm, :] + convw_ref[2] * v
    conv_out = (b_g * conv).astype(BF16)
    if carry:
        vbuf[0:2 * nb, :] = vbuf[tm:tm + 2 * nb, :]
    else:
        for c in cache_copies(convnew_hbm, tm, False):
            c.start()
    x1 = x + _dot(conv_out, wout_ref[0:W_CONV, :])

    last = hbuf[tm:tm + nb, :]
    for n in range(N_HALVES):
        last_re = last[:, 2 * n * S_HALF:(2 * n + 1) * S_HALF]
        last_im = last[:, (2 * n + 1) * S_HALF:(2 * n + 2) * S_HALF]
        if carry:
            hre_ref[:, n * S_HALF:(n + 1) * S_HALF] = last_re
            him_ref[:, n * S_HALF:(n + 1) * S_HALF] = last_im
        else:
            hre_ref[n * S_HALF:(n + 1) * S_HALF, :] = last_re.T
            him_ref[n * S_HALF:(n + 1) * S_HALF, :] = last_im.T

    ys = []
    for n in range(N_HALVES):
        hb = hbuf[nb:nb + tm, 2 * n * S_HALF:(2 * n + 2) * S_HALF].astype(BF16)
        ys.append(_dot(hb, cc_ref[n]))
    hbuf[0:nb, :] = last
    y = jnp.concatenate(ys, axis=-1) + dskip_ref[...] * u
    z = jax.nn.gelu(y)
    gate = jax.nn.sigmoid(_dot(z.astype(BF16), wglu_ref[...]) + bglu_ref[...])
    ssm_out = (z * gate).astype(BF16)

    x1 = x1 + _dot(ssm_out, wout_ref[W_CONV:, :])

    @pl.when(i >= 1)
    def _():
        for c in out_copies(i - 1, 0):
            c.wait()

    obuf[0] = x1.reshape(tc, nb, D_MODEL)
    for c in out_copies(i, 0):
        c.start()

    if not carry:
        for c in cache_copies(convnew_hbm, tm, False):
            c.wait()

    @pl.when(i == n_steps - 1)
    def _():
        if carry:
            for c in cache_copies(convnew_hbm, tm, False):
                c.start()
            for c in cache_copies(convnew_hbm, tm, False):
                c.wait()
        for c in out_copies(i, 0):
            c.wait()


def _resident(shape):
    nd = len(shape)
    return pl.BlockSpec(shape, lambda i: (0,) * nd, pipeline_mode=pl.Buffered(1))


def _mixer(x, wts, state=None, cast=(), *, nb, tc):
    n_seq, n_time, _ = x.shape
    carry = state is None
    if carry:
        assert n_seq == nb == SUBLANES and n_time % tc == 0
        n_steps = n_time // tc
    else:
        assert n_time == tc == SUBLANES and n_seq % nb == 0
        n_steps = n_seq // nb
    tm = nb * tc
    n_st = N_GROUPS * STATE_P
    body = functools.partial(_mixer_body, nb=nb, tc=tc, n_steps=n_steps, carry=carry, n_cast=len(cast))
    any_spec = pl.BlockSpec(memory_space=pl.ANY)
    if carry:
        st_spec = pl.BlockSpec((nb, n_st), lambda i: (0, 0))
        st_shape = jax.ShapeDtypeStruct((n_seq, n_st), F32)
    else:
        st_spec = pl.BlockSpec((n_st, nb), lambda i: (0, i))
        st_shape = jax.ShapeDtypeStruct((n_st, n_seq), F32)
    state_specs = [] if carry else [any_spec, st_spec, st_spec]
    for w in cast:
        assert w.shape[0] % (n_steps * BF16_ROWS) == 0, w.shape
    cast_specs = [pl.BlockSpec((w.shape[0] // n_steps, w.shape[1]), lambda i: (i, 0)) for w in cast]
    return pl.pallas_call(
        body,
        grid=(n_steps,),
        in_specs=[any_spec] + state_specs + [_resident(w.shape) for w in wts] + cast_specs,
        out_specs=[any_spec, any_spec, st_spec, st_spec] + cast_specs,
        out_shape=[jax.ShapeDtypeStruct(x.shape, F32),
                   jax.ShapeDtypeStruct((1, n_seq, CONV_W - 1, W_CONV), F32),
                   st_shape, st_shape]
                  + [jax.ShapeDtypeStruct(w.shape, BF16) for w in cast],
        scratch_shapes=[pltpu.VMEM((2, tc, nb, D_MODEL), F32),
                        pltpu.VMEM((1, tc, nb, D_MODEL), F32),
                        pltpu.VMEM((nb + tm, S_COLS), F32),
                        pltpu.VMEM((2 * nb + tm, W_CONV), F32),
                        pltpu.SemaphoreType.DMA((2, SUBLANES)),
                        pltpu.SemaphoreType.DMA((1, SUBLANES)),
                        pltpu.SemaphoreType.DMA((2, CONV_W - 1))],
        compiler_params=pltpu.CompilerParams(dimension_semantics=("arbitrary",),
                                             vmem_limit_bytes=VMEM_LIMIT),
        name="mixer_carry" if carry else "mixer_step",
    )(x, *(() if carry else state), *wts, *cast)


def _ffn_body(xp_hbm, xs_hbm, nffn_ref, wg_ref, wu_ref, wd_ref, nfin_ref, yp_hbm, ys_hbm,
              xbuf, obuf, insem, outsem, *, n_p, n_s, tm):
    i = pl.program_id(0)
    n_steps = n_p + n_s
    slot = lax.rem(i, 2)

    def tile_copy(step, buf_slot, to_vmem, first):
        hbm = (xp_hbm if first else xs_hbm) if to_vmem else (yp_hbm if first else ys_hbm)
        h = hbm.at[pl.ds((step if first else step - n_p) * tm, tm), :]
        if to_vmem:
            return pltpu.make_async_copy(h, xbuf.at[buf_slot], insem.at[buf_slot])
        return pltpu.make_async_copy(obuf.at[buf_slot], h, outsem.at[buf_slot])

    def start(step, buf_slot, to_vmem):
        pl.when(step < n_p)(lambda: tile_copy(step, buf_slot, to_vmem, True).start())
        pl.when(step >= n_p)(lambda: tile_copy(step, buf_slot, to_vmem, False).start())

    def wait(buf_slot, to_vmem):
        tile_copy(0, buf_slot, to_vmem, True).wait()

    pl.when(i == 0)(lambda: start(0, 0, True))
    wait(slot, True)
    pl.when(i + 1 < n_steps)(lambda: start(i + 1, 1 - slot, True))

    x = xbuf[slot]
    hn = _rms(x, nffn_ref[...]).astype(BF16)
    acc = x
    for s, w in FFN_CHUNKS:
        g = _dot(hn, wg_ref[:, s:s + w])
        up = _dot(hn, wu_ref[:, s:s + w])
        a = (g * jax.nn.sigmoid(g) * up).astype(BF16)
        acc = acc + _dot(a, wd_ref[s:s + w, :])

    pl.when(i >= 2)(lambda: wait(slot, False))
    obuf[slot] = _rms(acc, nfin_ref[...])
    start(i, slot, False)

    @pl.when(i == n_steps - 1)
    def _():
        if n_steps >= 2:
            wait(1 - slot, False)
        wait(slot, False)


def _ffn(xp, xs, wts, *, tm):
    n_p, n_s = xp.shape[0] // tm, xs.shape[0] // tm
    assert xp.shape[0] == n_p * tm and xs.shape[0] == n_s * tm
    any_spec = pl.BlockSpec(memory_space=pl.ANY)
    return pl.pallas_call(
        functools.partial(_ffn_body, n_p=n_p, n_s=n_s, tm=tm),
        grid=(n_p + n_s,),
        in_specs=[any_spec, any_spec] + [_resident(w.shape) for w in wts],
        out_specs=[any_spec, any_spec],
        out_shape=[jax.ShapeDtypeStruct(xp.shape, F32), jax.ShapeDtypeStruct(xs.shape, F32)],
        scratch_shapes=[pltpu.VMEM((2, tm, D_MODEL), F32),
                        pltpu.VMEM((2, tm, D_MODEL), F32),
                        pltpu.SemaphoreType.DMA((2,)),
                        pltpu.SemaphoreType.DMA((2,))],
        compiler_params=pltpu.CompilerParams(dimension_semantics=("arbitrary",),
                                             vmem_limit_bytes=VMEM_LIMIT),
        name="ffn",
    )(xp, xs, *wts)


def _prep_body(lre_ref, lim_ref, logdt_ref, btre_ref, btim_ref, cre_ref, cim_ref, *refs):
    n_cast = (len(refs) - 6) // 2
    cast_in, refs = refs[:n_cast], refs[n_cast:]
    s5_out, refs = refs[:4], refs[4:]
    cast_out, s5_scratch = refs[:n_cast], refs[n_cast:]
    for src, dst in zip(cast_in, cast_out):
        dst[...] = src[...].astype(BF16)
    pl.when(pl.program_id(0) == 0)(functools.partial(
        _s5_matrices, lre_ref, lim_ref, logdt_ref, btre_ref, btim_ref, cre_ref, cim_ref, *s5_out, *s5_scratch))


def _s5_matrices(lre_ref, lim_ref, logdt_ref, btre_ref, btim_ref, cre_ref, cim_ref,
                 lamr_ref, lami_ref, bc_ref, cc_ref, bscr, cscr):
    lre = lre_ref[...]
    lim = lim_ref[...]
    grp = (lax.broadcasted_iota(jnp.int32, (N_GROUPS, N_GROUPS), 0)
           == lax.broadcasted_iota(jnp.int32, (N_GROUPS, N_GROUPS), 1))
    logdt = jnp.sum(jnp.where(grp, jnp.broadcast_to(logdt_ref[...], (N_GROUPS, N_GROUPS)), 0.0),
                    axis=1, keepdims=True)
    dt = jnp.exp(logdt)
    mag = jnp.exp(lre * dt)
    lbr = mag * jnp.cos(lim * dt)
    lbi = mag * jnp.sin(lim * dt)
    den = lre * lre + lim * lim
    fr = ((lbr - 1.0) * lre + lbi * lim) / den
    fi = (lbi * lre - (lbr - 1.0) * lim) / den
    frb = fr[:, None, :]
    fib = fi[:, None, :]
    bbr = frb * btre_ref[...] - fib * btim_ref[...]
    bbi = frb * btim_ref[...] + fib * btre_ref[...]
    cre = cre_ref[...]
    ncim = -cim_ref[...]

    bscr[...] = jnp.zeros_like(bscr)
    cscr[...] = jnp.zeros_like(cscr)
    for g in range(N_GROUPS):
        n, k = divmod(g, GROUPS_PER_HALF)
        chans = slice(k * SSM_H, (k + 1) * SSM_H)
        states = slice(k * STATE_P, (k + 1) * STATE_P)
        im_states = slice(S_HALF + k * STATE_P, S_HALF + (k + 1) * STATE_P)
        bscr[n, chans, states] = bbr[g]
        bscr[n, chans, im_states] = bbi[g]
        cscr[n, chans, states] = cre[g]
        cscr[n, chans, im_states] = ncim[g]
        lamr_ref[n, :, states] = jnp.broadcast_to(lbr[g:g + 1, :], (SUBLANES, STATE_P))
        lami_ref[n, :, states] = jnp.broadcast_to(lbi[g:g + 1, :], (SUBLANES, STATE_P))
    for n in range(N_HALVES):
        bc_ref[n] = bscr[n].astype(BF16)
        cc_ref[n] = cscr[n].T.astype(BF16)


PREP_STEPS = 4


def _prep(lam_re, lam_im, log_dt, b_re, b_im, c_re, c_im, cast):
    s5_in = (lam_re, lam_im, log_dt.reshape(1, N_GROUPS), b_re.transpose(0, 2, 1), b_im.transpose(0, 2, 1),
             c_re, c_im)
    s5_shapes = [jax.ShapeDtypeStruct((N_HALVES, SUBLANES, S_HALF), F32),
                 jax.ShapeDtypeStruct((N_HALVES, SUBLANES, S_HALF), F32),
                 jax.ShapeDtypeStruct((N_HALVES, U_HALF, 2 * S_HALF), BF16),
                 jax.ShapeDtypeStruct((N_HALVES, 2 * S_HALF, U_HALF), BF16)]
    for w in cast:
        assert w.shape[0] % (PREP_STEPS * BF16_ROWS) == 0, w.shape
    cast_specs = [pl.BlockSpec((w.shape[0] // PREP_STEPS, w.shape[1]), lambda i: (i, 0)) for w in cast]
    whole = lambda a: pl.BlockSpec(a.shape, lambda i: (0,) * len(a.shape))
    return pl.pallas_call(
        _prep_body,
        grid=(PREP_STEPS,),
        in_specs=[whole(a) for a in s5_in] + cast_specs,
        out_specs=[whole(a) for a in s5_shapes] + cast_specs,
        out_shape=s5_shapes + [jax.ShapeDtypeStruct(w.shape, BF16) for w in cast],
        scratch_shapes=[pltpu.VMEM((N_HALVES, U_HALF, 2 * S_HALF), F32),
                        pltpu.VMEM((N_HALVES, U_HALF, 2 * S_HALF), F32)],
        compiler_params=pltpu.CompilerParams(dimension_semantics=("arbitrary",)),
        name="prep",
    )(*s5_in, *cast)


def kernel(x_prompt, x_sample, cache_conv, state_ssm_re, state_ssm_im, norm_mix, w_in, conv_w, ssm_lam_re, ssm_lam_im, ssm_log_dt, ssm_b_re, ssm_b_im, ssm_c_re, ssm_c_im, ssm_d, w_glu, b_glu, w_out, norm_ffn, w_gate, w_up, w_down, norm_final):
    assert norm_mix.shape[0] == 1, "single-layer trunk"
    n_st = N_GROUPS * STATE_P
    lamr, lami, bc, cc, w_in_b, w_glu_b, w_out_b = _prep(
        ssm_lam_re[0], ssm_lam_im[0], ssm_log_dt[0], ssm_b_re[0], ssm_b_im[0], ssm_c_re[0], ssm_c_im[0],
        cast=(w_in[0], w_glu[0], w_out[0]))
    mix_w = (norm_mix, w_in_b, conv_w[0][:, None, :], lamr, lami, bc, cc, ssm_d, w_glu_b, b_glu, w_out_b)

    bp, lp, _ = x_prompt.shape
    bs, ls, _ = x_sample.shape

    x1p, convp, rep, imp, wg, wu, wd = _mixer(x_prompt, mix_w, cast=(w_gate[0], w_up[0], w_down[0]),
                                              nb=SUBLANES, tc=128)
    ffn_w = (norm_ffn, wg, wu, wd, norm_final.reshape(1, D_MODEL))
    def state_major(h):
        return h[0].transpose(1, 2, 0).reshape(n_st, h.shape[1])

    sample_state = (cache_conv, state_major(state_ssm_re), state_major(state_ssm_im))
    x1s, convs, res, ims = _mixer(x_sample, mix_w, sample_state, nb=128, tc=SUBLANES)

    yp, ys = _ffn(x1p.reshape(bp * lp, D_MODEL), x1s.reshape(bs * ls, D_MODEL), ffn_w, tm=1024)
    yp = yp.reshape(bp, lp, D_MODEL)
    ys = ys.reshape(bs, ls, D_MODEL)

    def seq_major(h):
        return h.reshape(N_GROUPS, STATE_P, h.shape[1]).transpose(2, 0, 1)[None]

    return (yp, ys, convp, rep.reshape(1, bp, N_GROUPS, STATE_P), imp.reshape(1, bp, N_GROUPS, STATE_P),
            convs, seq_major(res), seq_major(ims))
```

```python
import functools

import jax
import jax.numpy as jnp
from jax import lax
from jax.experimental import pallas as pl
from jax.experimental.pallas import tpu as pltpu

D_MODEL = 1024
W_CONV = 512
W_SSM = 512
CONV_W = 3
N_GROUPS = 32
SSM_H = 16
STATE_P = 64
D_FF = 2816
EPS = 1e-5

SUBLANES = 8
BF16_ROWS = 16
GROUPS_PER_HALF = 16
N_HALVES = N_GROUPS // GROUPS_PER_HALF
U_HALF = GROUPS_PER_HALF * SSM_H
S_HALF = GROUPS_PER_HALF * STATE_P
S_COLS = 2 * N_GROUPS * STATE_P
SCAN_LANES = 512
MXU_COLS = 256
FFN_CHUNKS = tuple((s, 256) for s in range(0, D_FF, 256))
VMEM_LIMIT = 60 * 1024 * 1024

BF16 = jnp.bfloat16
F32 = jnp.float32


def _rms(x, g):
    y = x * lax.rsqrt(jnp.mean(x * x, axis=-1, keepdims=True) + EPS)
    return y * g


def _dot(a, b):
    return jnp.dot(a, b, preferred_element_type=F32)


def _tile_copies(hbm, buf, sem, step, slot, *, nb, tc, per_seq, to_vmem):
    copies = []
    for k in range(SUBLANES):
        if per_seq:
            h = hbm.at[k, pl.ds(step * tc, tc), :]
            v = buf.at[slot, :, k, :]
        else:
            h = hbm.at[pl.ds(step * nb, nb), k, :]
            v = buf.at[slot, k]
        src, dst = (h, v) if to_vmem else (v, h)
        copies.append(pltpu.make_async_copy(src, dst, sem.at[slot, k]))
    return copies


def _mixer_body(x_hbm, *refs, nb, tc, n_steps, carry, n_cast):
    if carry:
        cache_ref = re0_ref = im0_ref = None
    else:
        cache_ref, re0_ref, im0_ref, *refs = refs
    (nmix_ref, win_ref, convw_ref, lamr_ref, lami_ref, bc_ref, cc_ref, dskip_ref, wglu_ref, bglu_ref, wout_ref,
     *refs) = refs
    cast_in, refs = refs[:n_cast], refs[n_cast:]
    x1_hbm, convnew_hbm, hre_ref, him_ref, *refs = refs
    cast_out, refs = refs[:n_cast], refs[n_cast:]
    xbuf, obuf, hbuf, vbuf, insem, outsem, cachesem = refs
    tm = nb * tc

    for src, dst in zip(cast_in, cast_out):
        dst[...] = src[...].astype(BF16)

    i = pl.program_id(0)
    slot = lax.rem(i, 2)
    cp = functools.partial(_tile_copies, nb=nb, tc=tc, per_seq=carry)
    in_copies = functools.partial(cp, x_hbm, xbuf, insem, to_vmem=True)
    out_copies = functools.partial(cp, x1_hbm, obuf, outsem, to_vmem=False)

    def cache_copies(hbm, row0, to_vmem):
        copies = []
        for k in range(CONV_W - 1):
            h = hbm.at[0, pl.ds(0 if carry else i * nb, nb), k, :]
            v = vbuf.at[pl.ds(row0 + k * nb, nb), :]
            src, dst = (h, v) if to_vmem else (v, h)
            copies.append(pltpu.make_async_copy(src, dst, cachesem.at[int(to_vmem), k]))
        return copies

    @pl.when(i == 0)
    def _():
        for c in in_copies(0, 0):
            c.start()

    for c in in_copies(i, slot):
        c.wait()

    @pl.when(i + 1 < n_steps)
    def _():
        for c in in_copies(i + 1, 1 - slot):
            c.start()

    if carry:
        @pl.when(i == 0)
        def _():
            vbuf[0:2 * nb, :] = jnp.zeros((2 * nb, W_CONV), F32)
            hbuf[0:nb, :] = jnp.zeros((nb, S_COLS), F32)
    else:
        for c in cache_copies(cache_ref, 0, True):
            c.start()
        for n in range(N_HALVES):
            hbuf[0:nb, 2 * n * S_HALF:(2 * n + 1) * S_HALF] = re0_ref[n * S_HALF:(n + 1) * S_HALF, :].T
            hbuf[0:nb, (2 * n + 1) * S_HALF:(2 * n + 2) * S_HALF] = im0_ref[n * S_HALF:(n + 1) * S_HALF, :].T

    x = xbuf[slot].reshape(tm, D_MODEL)
    xn = _rms(x, nmix_ref[...]).astype(BF16)

    u = _dot(xn, win_ref[:, 3 * W_CONV:])
    ub = u.astype(BF16)
    for n in range(N_HALVES):
        hbuf[nb:nb + tm, 2 * n * S_HALF:(2 * n + 2) * S_HALF] = _dot(ub[:, n * U_HALF:(n + 1) * U_HALF], bc_ref[n])

    for n in range(N_HALVES):
        for c0 in range(0, S_HALF, SCAN_LANES):
            re_cols = slice(2 * n * S_HALF + c0, 2 * n * S_HALF + c0 + SCAN_LANES)
            im_cols = slice((2 * n + 1) * S_HALF + c0, (2 * n + 1) * S_HALF + c0 + SCAN_LANES)
            lr = lamr_ref[n, :, c0:c0 + SCAN_LANES]
            li = lami_ref[n, :, c0:c0 + SCAN_LANES]
            for r0 in range(0, nb, SUBLANES):
                hr = hbuf[r0:r0 + SUBLANES, re_cols]
                hi = hbuf[r0:r0 + SUBLANES, im_cols]
                for t in range(tc):
                    rows = slice(nb + t * nb + r0, nb + t * nb + r0 + SUBLANES)
                    hr, hi = (lr * hr - li * hi + hbuf[rows, re_cols],
                              lr * hi + li * hr + hbuf[rows, im_cols])
                    hbuf[rows, re_cols] = hr
                    hbuf[rows, im_cols] = hi

    if not carry:
        for c in cache_copies(cache_ref, 0, True):
            c.wait()
    conv_chunks = []
    for c0 in range(0, W_CONV, MXU_COLS):
        cols = slice(c0, c0 + MXU_COLS)
        b_g = _dot(xn, win_ref[:, c0:c0 + MXU_COLS])
        c_g = _dot(xn, win_ref[:, W_CONV + c0:W_CONV + c0 + MXU_COLS])
        xc = _dot(xn, win_ref[:, 2 * W_CONV + c0:2 * W_CONV + c0 + MXU_COLS])
        v = c_g * xc
        vbuf[2 * nb:2 * nb + tm, cols] = v
        conv = (convw_ref[0, :, cols] * vbuf[0:tm, cols] + convw_ref[1, :, cols] * vbuf[nb:nb + tm, cols]
                + convw_ref[2, :, cols] * v)
        conv_chunks.append((b_g * conv).astype(BF16))
    conv_out = jnp.concatenate(conv_chunks, axis=-1)
    if carry:
        vbuf[0:2 * nb, :] = vbuf[tm:tm + 2 * nb, :]
    else:
        for c in cache_copies(convnew_hbm, tm, False):
            c.start()
    x1 = x + _dot(conv_out, wout_ref[0:W_CONV, :])

    last = hbuf[tm:tm + nb, :]
    for n in range(N_HALVES):
        last_re = last[:, 2 * n * S_HALF:(2 * n + 1) * S_HALF]
        last_im = last[:, (2 * n + 1) * S_HALF:(2 * n + 2) * S_HALF]
        if carry:
            hre_ref[:, n * S_HALF:(n + 1) * S_HALF] = last_re
            him_ref[:, n * S_HALF:(n + 1) * S_HALF] = last_im
        else:
            hre_ref[n * S_HALF:(n + 1) * S_HALF, :] = last_re.T
            him_ref[n * S_HALF:(n + 1) * S_HALF, :] = last_im.T

    zs = []
    for n in range(N_HALVES):
        cols = slice(n * U_HALF, (n + 1) * U_HALF)
        hb = hbuf[nb:nb + tm, 2 * n * S_HALF:(2 * n + 2) * S_HALF].astype(BF16)
        y = _dot(hb, cc_ref[n]) + dskip_ref[:, cols] * u[:, cols]
        zs.append(jax.nn.gelu(y))
    hbuf[0:nb, :] = last
    zb = jnp.concatenate([z.astype(BF16) for z in zs], axis=-1)
    ssm_chunks = []
    for n, z in enumerate(zs):
        cols = slice(n * U_HALF, (n + 1) * U_HALF)
        gate = jax.nn.sigmoid(_dot(zb, wglu_ref[:, cols]) + bglu_ref[:, cols])
        ssm_chunks.append((z * gate).astype(BF16))
    ssm_out = jnp.concatenate(ssm_chunks, axis=-1)

    x1 = x1 + _dot(ssm_out, wout_ref[W_CONV:, :])

    @pl.when(i >= 1)
    def _():
        for c in out_copies(i - 1, 0):
            c.wait()

    obuf[0] = x1.reshape(tc, nb, D_MODEL)
    for c in out_copies(i, 0):
        c.start()

    if not carry:
        for c in cache_copies(convnew_hbm, tm, False):
            c.wait()

    @pl.when(i == n_steps - 1)
    def _():
        if carry:
            for c in cache_copies(convnew_hbm, tm, False):
                c.start()
            for c in cache_copies(convnew_hbm, tm, False):
                c.wait()
        for c in out_copies(i, 0):
            c.wait()


def _resident(shape):
    nd = len(shape)
    return pl.BlockSpec(shape, lambda i: (0,) * nd, pipeline_mode=pl.Buffered(1))


def _mixer(x, wts, state=None, cast=(), *, nb, tc):
    n_seq, n_time, _ = x.shape
    carry = state is None
    if carry:
        assert n_seq == nb == SUBLANES and n_time % tc == 0
        n_steps = n_time // tc
    else:
        assert n_time == tc == SUBLANES and n_seq % nb == 0
        n_steps = n_seq // nb
    tm = nb * tc
    n_st = N_GROUPS * STATE_P
    body = functools.partial(_mixer_body, nb=nb, tc=tc, n_steps=n_steps, carry=carry, n_cast=len(cast))
    any_spec = pl.BlockSpec(memory_space=pl.ANY)
    if carry:
        st_spec = pl.BlockSpec((nb, n_st), lambda i: (0, 0))
        st_shape = jax.ShapeDtypeStruct((n_seq, n_st), F32)
    else:
        st_spec = pl.BlockSpec((n_st, nb), lambda i: (0, i))
        st_shape = jax.ShapeDtypeStruct((n_st, n_seq), F32)
    state_specs = [] if carry else [any_spec, st_spec, st_spec]
    for w in cast:
        assert w.shape[0] % (n_steps * BF16_ROWS) == 0, w.shape
    cast_specs = [pl.BlockSpec((w.shape[0] // n_steps, w.shape[1]), lambda i: (i, 0)) for w in cast]
    return pl.pallas_call(
        body,
        grid=(n_steps,),
        in_specs=[any_spec] + state_specs + [_resident(w.shape) for w in wts] + cast_specs,
        out_specs=[any_spec, any_spec, st_spec, st_spec] + cast_specs,
        out_shape=[jax.ShapeDtypeStruct(x.shape, F32),
                   jax.ShapeDtypeStruct((1, n_seq, CONV_W - 1, W_CONV), F32),
                   st_shape, st_shape]
                  + [jax.ShapeDtypeStruct(w.shape, BF16) for w in cast],
        scratch_shapes=[pltpu.VMEM((2, tc, nb, D_MODEL), F32),
                        pltpu.VMEM((1, tc, nb, D_MODEL), F32),
                        pltpu.VMEM((nb + tm, S_COLS), F32),
                        pltpu.VMEM((2 * nb + tm, W_CONV), F32),
                        pltpu.SemaphoreType.DMA((2, SUBLANES)),
                        pltpu.SemaphoreType.DMA((1, SUBLANES)),
                        pltpu.SemaphoreType.DMA((2, CONV_W - 1))],
        compiler_params=pltpu.CompilerParams(dimension_semantics=("arbitrary",),
                                             vmem_limit_bytes=VMEM_LIMIT),
        name="mixer_carry" if carry else "mixer_step",
    )(x, *(() if carry else state), *wts, *cast)


def _ffn_body(xp_hbm, xs_hbm, nffn_ref, wg_ref, wu_ref, wd_ref, nfin_ref, yp_hbm, ys_hbm,
              xbuf, obuf, abuf, insem, outsem, *, n_p, n_s, tm):
    i = pl.program_id(0)
    n_steps = n_p + n_s
    slot = lax.rem(i, 2)

    def tile_copy(step, buf_slot, to_vmem, first):
        hbm = (xp_hbm if first else xs_hbm) if to_vmem else (yp_hbm if first else ys_hbm)
        h = hbm.at[pl.ds((step if first else step - n_p) * tm, tm), :]
        if to_vmem:
            return pltpu.make_async_copy(h, xbuf.at[buf_slot], insem.at[buf_slot])
        return pltpu.make_async_copy(obuf.at[buf_slot], h, outsem.at[buf_slot])

    def start(step, buf_slot, to_vmem):
        pl.when(step < n_p)(lambda: tile_copy(step, buf_slot, to_vmem, True).start())
        pl.when(step >= n_p)(lambda: tile_copy(step, buf_slot, to_vmem, False).start())

    def wait(buf_slot, to_vmem):
        tile_copy(0, buf_slot, to_vmem, True).wait()

    pl.when(i == 0)(lambda: start(0, 0, True))
    wait(slot, True)
    pl.when(i + 1 < n_steps)(lambda: start(i + 1, 1 - slot, True))

    x = xbuf[slot]
    hn = _rms(x, nffn_ref[...]).astype(BF16)
    for s, w in FFN_CHUNKS:
        g = _dot(hn, wg_ref[:, s:s + w])
        up = _dot(hn, wu_ref[:, s:s + w])
        abuf[:, s:s + w] = (g * jax.nn.sigmoid(g) * up).astype(BF16)
    acc = x + _dot(abuf[...], wd_ref[...])

    pl.when(i >= 2)(lambda: wait(slot, False))
    obuf[slot] = _rms(acc, nfin_ref[...])
    start(i, slot, False)

    @pl.when(i == n_steps - 1)
    def _():
        if n_steps >= 2:
            wait(1 - slot, False)
        wait(slot, False)


def _ffn(xp, xs, wts, *, tm):
    n_p, n_s = xp.shape[0] // tm, xs.shape[0] // tm
    assert xp.shape[0] == n_p * tm and xs.shape[0] == n_s * tm
    any_spec = pl.BlockSpec(memory_space=pl.ANY)
    return pl.pallas_call(
        functools.partial(_ffn_body, n_p=n_p, n_s=n_s, tm=tm),
        grid=(n_p + n_s,),
        in_specs=[any_spec, any_spec] + [_resident(w.shape) for w in wts],
        out_specs=[any_spec, any_spec],
        out_shape=[jax.ShapeDtypeStruct(xp.shape, F32), jax.ShapeDtypeStruct(xs.shape, F32)],
        scratch_shapes=[pltpu.VMEM((2, tm, D_MODEL), F32),
                        pltpu.VMEM((2, tm, D_MODEL), F32),
                        pltpu.VMEM((tm, D_FF), BF16),
                        pltpu.SemaphoreType.DMA((2,)),
                        pltpu.SemaphoreType.DMA((2,))],
        compiler_params=pltpu.CompilerParams(dimension_semantics=("arbitrary",),
                                             vmem_limit_bytes=VMEM_LIMIT),
        name="ffn",
    )(xp, xs, *wts)


def _prep_body(lre_ref, lim_ref, logdt_ref, btre_ref, btim_ref, cre_ref, cim_ref, *refs):
    n_cast = (len(refs) - 6) // 2
    cast_in, refs = refs[:n_cast], refs[n_cast:]
    s5_out, refs = refs[:4], refs[4:]
    cast_out, s5_scratch = refs[:n_cast], refs[n_cast:]
    for src, dst in zip(cast_in, cast_out):
        dst[...] = src[...].astype(BF16)
    pl.when(pl.program_id(0) == 0)(functools.partial(
        _s5_matrices, lre_ref, lim_ref, logdt_ref, btre_ref, btim_ref, cre_ref, cim_ref, *s5_out, *s5_scratch))


def _s5_matrices(lre_ref, lim_ref, logdt_ref, btre_ref, btim_ref, cre_ref, cim_ref,
                 lamr_ref, lami_ref, bc_ref, cc_ref, bscr, cscr):
    lre = lre_ref[...]
    lim = lim_ref[...]
    grp = (lax.broadcasted_iota(jnp.int32, (N_GROUPS, N_GROUPS), 0)
           == lax.broadcasted_iota(jnp.int32, (N_GROUPS, N_GROUPS), 1))
    logdt = jnp.sum(jnp.where(grp, jnp.broadcast_to(logdt_ref[...], (N_GROUPS, N_GROUPS)), 0.0),
                    axis=1, keepdims=True)
    dt = jnp.exp(logdt)
    mag = jnp.exp(lre * dt)
    lbr = mag * jnp.cos(lim * dt)
    lbi = mag * jnp.sin(lim * dt)
    den = lre * lre + lim * lim
    fr = ((lbr - 1.0) * lre + lbi * lim) / den
    fi = (lbi * lre - (lbr - 1.0) * lim) / den
    frb = fr[:, None, :]
    fib = fi[:, None, :]
    bbr = frb * btre_ref[...] - fib * btim_ref[...]
    bbi = frb * btim_ref[...] + fib * btre_ref[...]
    cre = cre_ref[...]
    ncim = -cim_ref[...]

    bscr[...] = jnp.zeros_like(bscr)
    cscr[...] = jnp.zeros_like(cscr)
    for g in range(N_GROUPS):
        n, k = divmod(g, GROUPS_PER_HALF)
        chans = slice(k * SSM_H, (k + 1) * SSM_H)
        states = slice(k * STATE_P, (k + 1) * STATE_P)
        im_states = slice(S_HALF + k * STATE_P, S_HALF + (k + 1) * STATE_P)
        bscr[n, chans, states] = bbr[g]
        bscr[n, chans, im_states] = bbi[g]
        cscr[n, chans, states] = cre[g]
        cscr[n, chans, im_states] = ncim[g]
        lamr_ref[n, :, states] = jnp.broadcast_to(lbr[g:g + 1, :], (SUBLANES, STATE_P))
        lami_ref[n, :, states] = jnp.broadcast_to(lbi[g:g + 1, :], (SUBLANES, STATE_P))
    for n in range(N_HALVES):
        bc_ref[n] = bscr[n].astype(BF16)
        cc_ref[n] = cscr[n].T.astype(BF16)


PREP_STEPS = 4


def _prep(lam_re, lam_im, log_dt, b_re, b_im, c_re, c_im, cast):
    s5_in = (lam_re, lam_im, log_dt.reshape(1, N_GROUPS), b_re.transpose(0, 2, 1), b_im.transpose(0, 2, 1),
             c_re, c_im)
    s5_shapes = [jax.ShapeDtypeStruct((N_HALVES, SUBLANES, S_HALF), F32),
                 jax.ShapeDtypeStruct((N_HALVES, SUBLANES, S_HALF), F32),
                 jax.ShapeDtypeStruct((N_HALVES, U_HALF, 2 * S_HALF), BF16),
                 jax.ShapeDtypeStruct((N_HALVES, 2 * S_HALF, U_HALF), BF16)]
    for w in cast:
        assert w.shape[0] % (PREP_STEPS * BF16_ROWS) == 0, w.shape
    cast_specs = [pl.BlockSpec((w.shape[0] // PREP_STEPS, w.shape[1]), lambda i: (i, 0)) for w in cast]
    whole = lambda a: pl.BlockSpec(a.shape, lambda i: (0,) * len(a.shape))
    return pl.pallas_call(
        _prep_body,
        grid=(PREP_STEPS,),
        in_specs=[whole(a) for a in s5_in] + cast_specs,
        out_specs=[whole(a) for a in s5_shapes] + cast_specs,
        out_shape=s5_shapes + [jax.ShapeDtypeStruct(w.shape, BF16) for w in cast],
        scratch_shapes=[pltpu.VMEM((N_HALVES, U_HALF, 2 * S_HALF), F32),
                        pltpu.VMEM((N_HALVES, U_HALF, 2 * S_HALF), F32)],
        compiler_params=pltpu.CompilerParams(dimension_semantics=("arbitrary",)),
        name="prep",
    )(*s5_in, *cast)


def kernel(x_prompt, x_sample, cache_conv, state_ssm_re, state_ssm_im, norm_mix, w_in, conv_w, ssm_lam_re, ssm_lam_im, ssm_log_dt, ssm_b_re, ssm_b_im, ssm_c_re, ssm_c_im, ssm_d, w_glu, b_glu, w_out, norm_ffn, w_gate, w_up, w_down, norm_final):
    assert norm_mix.shape[0] == 1, "single-layer trunk"
    n_st = N_GROUPS * STATE_P
    lamr, lami, bc, cc, w_in_b, w_glu_b, w_out_b = _prep(
        ssm_lam_re[0], ssm_lam_im[0], ssm_log_dt[0], ssm_b_re[0], ssm_b_im[0], ssm_c_re[0], ssm_c_im[0],
        cast=(w_in[0], w_glu[0], w_out[0]))
    mix_w = (norm_mix, w_in_b, conv_w[0][:, None, :], lamr, lami, bc, cc, ssm_d, w_glu_b, b_glu, w_out_b)

    bp, lp, _ = x_prompt.shape
    bs, ls, _ = x_sample.shape

    x1p, convp, rep, imp, wg, wu, wd = _mixer(x_prompt, mix_w, cast=(w_gate[0], w_up[0], w_down[0]),
                                              nb=SUBLANES, tc=128)
    ffn_w = (norm_ffn, wg, wu, wd, norm_final.reshape(1, D_MODEL))
    def state_major(h):
        return h[0].transpose(1, 2, 0).reshape(n_st, h.shape[1])

    sample_state = (cache_conv, state_major(state_ssm_re), state_major(state_ssm_im))
    x1s, convs, res, ims = _mixer(x_sample, mix_w, sample_state, nb=128, tc=SUBLANES)

    yp, ys = _ffn(x1p.reshape(bp * lp, D_MODEL), x1s.reshape(bs * ls, D_MODEL), ffn_w, tm=1024)
    yp = yp.reshape(bp, lp, D_MODEL)
    ys = ys.reshape(bs, ls, D_MODEL)

    def seq_major(h):
        return h.reshape(N_GROUPS, STATE_P, h.shape[1]).transpose(2, 0, 1)[None]

    return (yp, ys, convp, rep.reshape(1, bp, N_GROUPS, STATE_P), imp.reshape(1, bp, N_GROUPS, STATE_P),
            convs, seq_major(res), seq_major(ims))
```

```python
import functools

import jax
import jax.numpy as jnp
from jax import lax
from jax.experimental import pallas as pl
from jax.experimental.pallas import tpu as pltpu

D_MODEL = 1024
W_CONV = 512
W_SSM = 512
CONV_W = 3
N_GROUPS = 32
SSM_H = 16
STATE_P = 64
D_FF = 2816
EPS = 1e-5

SUBLANES = 8
BF16_ROWS = 16
GROUPS_PER_HALF = 16
N_HALVES = N_GROUPS // GROUPS_PER_HALF
U_HALF = GROUPS_PER_HALF * SSM_H
S_HALF = GROUPS_PER_HALF * STATE_P
S_COLS = 2 * N_GROUPS * STATE_P
SCAN_LANES = 512
MXU_COLS = 256
FFN_CHUNKS = tuple((s, 256) for s in range(0, D_FF, 256))
VMEM_LIMIT = 60 * 1024 * 1024

BF16 = jnp.bfloat16
F32 = jnp.float32


def _rms(x, g):
    y = x * lax.rsqrt(jnp.mean(x * x, axis=-1, keepdims=True) + EPS)
    return y * g


def _dot(a, b):
    return jnp.dot(a, b, preferred_element_type=F32)


def _tile_copies(hbm, buf, sem, step, slot, *, nb, tc, per_seq, to_vmem):
    copies = []
    for k in range(SUBLANES):
        if per_seq:
            h = hbm.at[k, pl.ds(step * tc, tc), :]
            v = buf.at[slot, :, k, :]
        else:
            h = hbm.at[pl.ds(step * nb, nb), k, :]
            v = buf.at[slot, k]
        src, dst = (h, v) if to_vmem else (v, h)
        copies.append(pltpu.make_async_copy(src, dst, sem.at[slot, k]))
    return copies


def _mixer_body(x_hbm, *refs, nb, tc, n_steps, carry, n_cast):
    if carry:
        cache_ref = re0_ref = im0_ref = None
    else:
        cache_ref, re0_ref, im0_ref, *refs = refs
    (nmix_ref, win_ref, convw_ref, lamr_ref, lami_ref, bc_ref, cc_ref, dskip_ref, wglu_ref, bglu_ref, wout_ref,
     *refs) = refs
    cast_in, refs = refs[:n_cast], refs[n_cast:]
    x1_hbm, convnew_hbm, hre_ref, him_ref, *refs = refs
    cast_out, refs = refs[:n_cast], refs[n_cast:]
    xbuf, obuf, bubuf, hbf, hstate, vbuf, insem, outsem, cachesem = refs
    tm = nb * tc

    for src, dst in zip(cast_in, cast_out):
        dst[...] = src[...].astype(BF16)

    i = pl.program_id(0)
    slot = lax.rem(i, 2)
    cp = functools.partial(_tile_copies, nb=nb, tc=tc, per_seq=carry)
    in_copies = functools.partial(cp, x_hbm, xbuf, insem, to_vmem=True)
    out_copies = functools.partial(cp, x1_hbm, obuf, outsem, to_vmem=False)

    def cache_copies(hbm, row0, to_vmem):
        copies = []
        for k in range(CONV_W - 1):
            h = hbm.at[0, pl.ds(0 if carry else i * nb, nb), k, :]
            v = vbuf.at[pl.ds(row0 + k * nb, nb), :]
            src, dst = (h, v) if to_vmem else (v, h)
            copies.append(pltpu.make_async_copy(src, dst, cachesem.at[int(to_vmem), k]))
        return copies

    @pl.when(i == 0)
    def _():
        for c in in_copies(0, 0):
            c.start()

    for c in in_copies(i, slot):
        c.wait()

    @pl.when(i + 1 < n_steps)
    def _():
        for c in in_copies(i + 1, 1 - slot):
            c.start()

    if carry:
        @pl.when(i == 0)
        def _():
            vbuf[0:2 * nb, :] = jnp.zeros((2 * nb, W_CONV), F32)
            hstate[...] = jnp.zeros((nb, S_COLS), F32)
    else:
        for c in cache_copies(cache_ref, 0, True):
            c.start()
        for n in range(N_HALVES):
            hstate[:, 2 * n * S_HALF:(2 * n + 1) * S_HALF] = re0_ref[n * S_HALF:(n + 1) * S_HALF, :].T
            hstate[:, (2 * n + 1) * S_HALF:(2 * n + 2) * S_HALF] = im0_ref[n * S_HALF:(n + 1) * S_HALF, :].T

    x = xbuf[slot].reshape(tm, D_MODEL)
    xn = _rms(x, nmix_ref[...]).astype(BF16)

    u = _dot(xn, win_ref[:, 3 * W_CONV:])
    ub = u.astype(BF16)
    for n in range(N_HALVES):
        bubuf[:, 2 * n * S_HALF:(2 * n + 2) * S_HALF] = _dot(ub[:, n * U_HALF:(n + 1) * U_HALF], bc_ref[n])

    def scan_step(h, rows, lr, li, re_cols, im_cols):
        hr, hi = h
        return (lr * hr - li * hi + bubuf[rows, re_cols], lr * hi + li * hr + bubuf[rows, im_cols])

    def store_pair(row0, pair, re_cols, im_cols):
        rows = slice(row0, row0 + BF16_ROWS)
        hbf[rows, re_cols] = jnp.concatenate([h[0] for h in pair], axis=0).astype(BF16)
        hbf[rows, im_cols] = jnp.concatenate([h[1] for h in pair], axis=0).astype(BF16)

    for n in range(N_HALVES):
        for c0 in range(0, S_HALF, SCAN_LANES):
            re_cols = slice(2 * n * S_HALF + c0, 2 * n * S_HALF + c0 + SCAN_LANES)
            im_cols = slice((2 * n + 1) * S_HALF + c0, (2 * n + 1) * S_HALF + c0 + SCAN_LANES)
            lr = lamr_ref[n, :, c0:c0 + SCAN_LANES]
            li = lami_ref[n, :, c0:c0 + SCAN_LANES]
            step = functools.partial(scan_step, lr=lr, li=li, re_cols=re_cols, im_cols=im_cols)
            if nb == SUBLANES:
                h = (hstate[:, re_cols], hstate[:, im_cols])
                for t in range(0, tc, 2):
                    h1 = step(h, slice(t * nb, (t + 1) * nb))
                    h = step(h1, slice((t + 1) * nb, (t + 2) * nb))
                    store_pair(t * nb, (h1, h), re_cols, im_cols)
                hstate[:, re_cols], hstate[:, im_cols] = h
            else:
                for r0 in range(0, nb, BF16_ROWS):
                    seqs = [slice(r0 + k * SUBLANES, r0 + (k + 1) * SUBLANES) for k in range(2)]
                    pair = [(hstate[sq, re_cols], hstate[sq, im_cols]) for sq in seqs]
                    for t in range(tc):
                        pair = [step(h, slice(t * nb + sq.start, t * nb + sq.stop)) for h, sq in zip(pair, seqs)]
                        store_pair(t * nb + r0, pair, re_cols, im_cols)
                    for h, sq in zip(pair, seqs):
                        hstate[sq, re_cols], hstate[sq, im_cols] = h

    if not carry:
        for c in cache_copies(cache_ref, 0, True):
            c.wait()
    conv_chunks = []
    for c0 in range(0, W_CONV, MXU_COLS):
        cols = slice(c0, c0 + MXU_COLS)
        b_g = _dot(xn, win_ref[:, c0:c0 + MXU_COLS])
        c_g = _dot(xn, win_ref[:, W_CONV + c0:W_CONV + c0 + MXU_COLS])
        xc = _dot(xn, win_ref[:, 2 * W_CONV + c0:2 * W_CONV + c0 + MXU_COLS])
        v = c_g * xc
        vbuf[2 * nb:2 * nb + tm, cols] = v
        conv = (convw_ref[0, :, cols] * vbuf[0:tm, cols] + convw_ref[1, :, cols] * vbuf[nb:nb + tm, cols]
                + convw_ref[2, :, cols] * v)
        conv_chunks.append((b_g * conv).astype(BF16))
    conv_out = jnp.concatenate(conv_chunks, axis=-1)
    if carry:
        vbuf[0:2 * nb, :] = vbuf[tm:tm + 2 * nb, :]
    else:
        for c in cache_copies(convnew_hbm, tm, False):
            c.start()
    x1 = x + _dot(conv_out, wout_ref[0:W_CONV, :])

    last = hstate[...]
    for n in range(N_HALVES):
        last_re = last[:, 2 * n * S_HALF:(2 * n + 1) * S_HALF]
        last_im = last[:, (2 * n + 1) * S_HALF:(2 * n + 2) * S_HALF]
        if carry:
            hre_ref[:, n * S_HALF:(n + 1) * S_HALF] = last_re
            him_ref[:, n * S_HALF:(n + 1) * S_HALF] = last_im
        else:
            hre_ref[n * S_HALF:(n + 1) * S_HALF, :] = last_re.T
            him_ref[n * S_HALF:(n + 1) * S_HALF, :] = last_im.T

    zs = []
    for n in range(N_HALVES):
        cols = slice(n * U_HALF, (n + 1) * U_HALF)
        y = _dot(hbf[:, 2 * n * S_HALF:(2 * n + 2) * S_HALF], cc_ref[n]) + dskip_ref[:, cols] * u[:, cols]
        zs.append(jax.nn.gelu(y))
    zb = jnp.concatenate([z.astype(BF16) for z in zs], axis=-1)
    ssm_chunks = []
    for n, z in enumerate(zs):
        cols = slice(n * U_HALF, (n + 1) * U_HALF)
        gate = jax.nn.sigmoid(_dot(zb, wglu_ref[:, cols]) + bglu_ref[:, cols])
        ssm_chunks.append((z * gate).astype(BF16))
    ssm_out = jnp.concatenate(ssm_chunks, axis=-1)

    x1 = x1 + _dot(ssm_out, wout_ref[W_CONV:, :])

    @pl.when(i >= 1)
    def _():
        for c in out_copies(i - 1, 0):
            c.wait()

    obuf[0] = x1.reshape(tc, nb, D_MODEL)
    for c in out_copies(i, 0):
        c.start()

    if not carry:
        for c in cache_copies(convnew_hbm, tm, False):
            c.wait()

    @pl.when(i == n_steps - 1)
    def _():
        if carry:
            for c in cache_copies(convnew_hbm, tm, False):
                c.start()
            for c in cache_copies(convnew_hbm, tm, False):
                c.wait()
        for c in out_copies(i, 0):
            c.wait()


def _resident(shape):
    nd = len(shape)
    return pl.BlockSpec(shape, lambda i: (0,) * nd, pipeline_mode=pl.Buffered(1))


def _mixer(x, wts, state=None, cast=(), *, nb, tc):
    n_seq, n_time, _ = x.shape
    carry = state is None
    if carry:
        assert n_seq == nb == SUBLANES and n_time % tc == 0
        n_steps = n_time // tc
    else:
        assert n_time == tc == SUBLANES and n_seq % nb == 0
        n_steps = n_seq // nb
    tm = nb * tc
    n_st = N_GROUPS * STATE_P
    body = functools.partial(_mixer_body, nb=nb, tc=tc, n_steps=n_steps, carry=carry, n_cast=len(cast))
    any_spec = pl.BlockSpec(memory_space=pl.ANY)
    if carry:
        st_spec = pl.BlockSpec((nb, n_st), lambda i: (0, 0))
        st_shape = jax.ShapeDtypeStruct((n_seq, n_st), F32)
    else:
        st_spec = pl.BlockSpec((n_st, nb), lambda i: (0, i))
        st_shape = jax.ShapeDtypeStruct((n_st, n_seq), F32)
    state_specs = [] if carry else [any_spec, st_spec, st_spec]
    for w in cast:
        assert w.shape[0] % (n_steps * BF16_ROWS) == 0, w.shape
    cast_specs = [pl.BlockSpec((w.shape[0] // n_steps, w.shape[1]), lambda i: (i, 0)) for w in cast]
    return pl.pallas_call(
        body,
        grid=(n_steps,),
        in_specs=[any_spec] + state_specs + [_resident(w.shape) for w in wts] + cast_specs,
        out_specs=[any_spec, any_spec, st_spec, st_spec] + cast_specs,
        out_shape=[jax.ShapeDtypeStruct(x.shape, F32),
                   jax.ShapeDtypeStruct((1, n_seq, CONV_W - 1, W_CONV), F32),
                   st_shape, st_shape]
                  + [jax.ShapeDtypeStruct(w.shape, BF16) for w in cast],
        scratch_shapes=[pltpu.VMEM((2, tc, nb, D_MODEL), F32),
                        pltpu.VMEM((1, tc, nb, D_MODEL), F32),
                        pltpu.VMEM((tm, S_COLS), F32),
                        pltpu.VMEM((tm, S_COLS), BF16),
                        pltpu.VMEM((nb, S_COLS), F32),
                        pltpu.VMEM((2 * nb + tm, W_CONV), F32),
                        pltpu.SemaphoreType.DMA((2, SUBLANES)),
                        pltpu.SemaphoreType.DMA((1, SUBLANES)),
                        pltpu.SemaphoreType.DMA((2, CONV_W - 1))],
        compiler_params=pltpu.CompilerParams(dimension_semantics=("arbitrary",),
                                             vmem_limit_bytes=VMEM_LIMIT),
        name="mixer_carry" if carry else "mixer_step",
    )(x, *(() if carry else state), *wts, *cast)


def _ffn_body(xp_hbm, xs_hbm, nffn_ref, wg_ref, wu_ref, wd_ref, nfin_ref, yp_hbm, ys_hbm,
              xbuf, obuf, abuf, insem, outsem, *, n_p, n_s, tm):
    i = pl.program_id(0)
    n_steps = n_p + n_s
    slot = lax.rem(i, 2)

    def tile_copy(step, buf_slot, to_vmem, first):
        hbm = (xp_hbm if first else xs_hbm) if to_vmem else (yp_hbm if first else ys_hbm)
        h = hbm.at[pl.ds((step if first else step - n_p) * tm, tm), :]
        if to_vmem:
            return pltpu.make_async_copy(h, xbuf.at[buf_slot], insem.at[buf_slot])
        return pltpu.make_async_copy(obuf.at[buf_slot], h, outsem.at[buf_slot])

    def start(step, buf_slot, to_vmem):
        pl.when(step < n_p)(lambda: tile_copy(step, buf_slot, to_vmem, True).start())
        pl.when(step >= n_p)(lambda: tile_copy(step, buf_slot, to_vmem, False).start())

    def wait(buf_slot, to_vmem):
        tile_copy(0, buf_slot, to_vmem, True).wait()

    pl.when(i == 0)(lambda: start(0, 0, True))
    wait(slot, True)
    pl.when(i + 1 < n_steps)(lambda: start(i + 1, 1 - slot, True))

    x = xbuf[slot]
    hn = _rms(x, nffn_ref[...]).astype(BF16)
    for s, w in FFN_CHUNKS:
        g = _dot(hn, wg_ref[:, s:s + w])
        up = _dot(hn, wu_ref[:, s:s + w])
        abuf[:, s:s + w] = (g * jax.nn.sigmoid(g) * up).astype(BF16)
    acc = x + _dot(abuf[...], wd_ref[...])

    pl.when(i >= 2)(lambda: wait(slot, False))
    obuf[slot] = _rms(acc, nfin_ref[...])
    start(i, slot, False)

    @pl.when(i == n_steps - 1)
    def _():
        if n_steps >= 2:
            wait(1 - slot, False)
        wait(slot, False)


def _ffn(xp, xs, wts, *, tm):
    n_p, n_s = xp.shape[0] // tm, xs.shape[0] // tm
    assert xp.shape[0] == n_p * tm and xs.shape[0] == n_s * tm
    any_spec = pl.BlockSpec(memory_space=pl.ANY)
    return pl.pallas_call(
        functools.partial(_ffn_body, n_p=n_p, n_s=n_s, tm=tm),
        grid=(n_p + n_s,),
        in_specs=[any_spec, any_spec] + [_resident(w.shape) for w in wts],
        out_specs=[any_spec, any_spec],
        out_shape=[jax.ShapeDtypeStruct(xp.shape, F32), jax.ShapeDtypeStruct(xs.shape, F32)],
        scratch_shapes=[pltpu.VMEM((2, tm, D_MODEL), F32),
                        pltpu.VMEM((2, tm, D_MODEL), F32),
                        pltpu.VMEM((tm, D_FF), BF16),
                        pltpu.SemaphoreType.DMA((2,)),
                        pltpu.SemaphoreType.DMA((2,))],
        compiler_params=pltpu.CompilerParams(dimension_semantics=("arbitrary",),
                                             vmem_limit_bytes=VMEM_LIMIT),
        name="ffn",
    )(xp, xs, *wts)


def _prep_body(lre_ref, lim_ref, logdt_ref, btre_ref, btim_ref, cre_ref, cim_ref, *refs):
    n_cast = (len(refs) - 6) // 2
    cast_in, refs = refs[:n_cast], refs[n_cast:]
    s5_out, refs = refs[:4], refs[4:]
    cast_out, s5_scratch = refs[:n_cast], refs[n_cast:]
    for src, dst in zip(cast_in, cast_out):
        dst[...] = src[...].astype(BF16)
    pl.when(pl.program_id(0) == 0)(functools.partial(
        _s5_matrices, lre_ref, lim_ref, logdt_ref, btre_ref, btim_ref, cre_ref, cim_ref, *s5_out, *s5_scratch))


def _s5_matrices(lre_ref, lim_ref, logdt_ref, btre_ref, btim_ref, cre_ref, cim_ref,
                 lamr_ref, lami_ref, bc_ref, cc_ref, bscr, cscr):
    lre = lre_ref[...]
    lim = lim_ref[...]
    grp = (lax.broadcasted_iota(jnp.int32, (N_GROUPS, N_GROUPS), 0)
           == lax.broadcasted_iota(jnp.int32, (N_GROUPS, N_GROUPS), 1))
    logdt = jnp.sum(jnp.where(grp, jnp.broadcast_to(logdt_ref[...], (N_GROUPS, N_GROUPS)), 0.0),
                    axis=1, keepdims=True)
    dt = jnp.exp(logdt)
    mag = jnp.exp(lre * dt)
    lbr = mag * jnp.cos(lim * dt)
    lbi = mag * jnp.sin(lim * dt)
    den = lre * lre + lim * lim
    fr = ((lbr - 1.0) * lre + lbi * lim) / den
    fi = (lbi * lre - (lbr - 1.0) * lim) / den
    frb = fr[:, None, :]
    fib = fi[:, None, :]
    bbr = frb * btre_ref[...] - fib * btim_ref[...]
    bbi = frb * btim_ref[...] + fib * btre_ref[...]
    cre = cre_ref[...]
    ncim = -cim_ref[...]

    bscr[...] = jnp.zeros_like(bscr)
    cscr[...] = jnp.zeros_like(cscr)
    for g in range(N_GROUPS):
        n, k = divmod(g, GROUPS_PER_HALF)
        chans = slice(k * SSM_H, (k + 1) * SSM_H)
        states = slice(k * STATE_P, (k + 1) * STATE_P)
        im_states = slice(S_HALF + k * STATE_P, S_HALF + (k + 1) * STATE_P)
        bscr[n, chans, states] = bbr[g]
        bscr[n, chans, im_states] = bbi[g]
        cscr[n, chans, states] = cre[g]
        cscr[n, chans, im_states] = ncim[g]
        lamr_ref[n, :, states] = jnp.broadcast_to(lbr[g:g + 1, :], (SUBLANES, STATE_P))
        lami_ref[n, :, states] = jnp.broadcast_to(lbi[g:g + 1, :], (SUBLANES, STATE_P))
    for n in range(N_HALVES):
        bc_ref[n] = bscr[n].astype(BF16)
        cc_ref[n] = cscr[n].T.astype(BF16)


PREP_STEPS = 4


def _prep(lam_re, lam_im, log_dt, b_re, b_im, c_re, c_im, cast):
    s5_in = (lam_re, lam_im, log_dt.reshape(1, N_GROUPS), b_re.transpose(0, 2, 1), b_im.transpose(0, 2, 1),
             c_re, c_im)
    s5_shapes = [jax.ShapeDtypeStruct((N_HALVES, SUBLANES, S_HALF), F32),
                 jax.ShapeDtypeStruct((N_HALVES, SUBLANES, S_HALF), F32),
                 jax.ShapeDtypeStruct((N_HALVES, U_HALF, 2 * S_HALF), BF16),
                 jax.ShapeDtypeStruct((N_HALVES, 2 * S_HALF, U_HALF), BF16)]
    for w in cast:
        assert w.shape[0] % (PREP_STEPS * BF16_ROWS) == 0, w.shape
    cast_specs = [pl.BlockSpec((w.shape[0] // PREP_STEPS, w.shape[1]), lambda i: (i, 0)) for w in cast]
    whole = lambda a: pl.BlockSpec(a.shape, lambda i: (0,) * len(a.shape))
    return pl.pallas_call(
        _prep_body,
        grid=(PREP_STEPS,),
        in_specs=[whole(a) for a in s5_in] + cast_specs,
        out_specs=[whole(a) for a in s5_shapes] + cast_specs,
        out_shape=s5_shapes + [jax.ShapeDtypeStruct(w.shape, BF16) for w in cast],
        scratch_shapes=[pltpu.VMEM((N_HALVES, U_HALF, 2 * S_HALF), F32),
                        pltpu.VMEM((N_HALVES, U_HALF, 2 * S_HALF), F32)],
        compiler_params=pltpu.CompilerParams(dimension_semantics=("arbitrary",)),
        name="prep",
    )(*s5_in, *cast)


def kernel(x_prompt, x_sample, cache_conv, state_ssm_re, state_ssm_im, norm_mix, w_in, conv_w, ssm_lam_re, ssm_lam_im, ssm_log_dt, ssm_b_re, ssm_b_im, ssm_c_re, ssm_c_im, ssm_d, w_glu, b_glu, w_out, norm_ffn, w_gate, w_up, w_down, norm_final):
    assert norm_mix.shape[0] == 1, "single-layer trunk"
    n_st = N_GROUPS * STATE_P
    lamr, lami, bc, cc, w_in_b, w_glu_b, w_out_b = _prep(
        ssm_lam_re[0], ssm_lam_im[0], ssm_log_dt[0], ssm_b_re[0], ssm_b_im[0], ssm_c_re[0], ssm_c_im[0],
        cast=(w_in[0], w_glu[0], w_out[0]))
    mix_w = (norm_mix, w_in_b, conv_w[0][:, None, :], lamr, lami, bc, cc, ssm_d, w_glu_b, b_glu, w_out_b)

    bp, lp, _ = x_prompt.shape
    bs, ls, _ = x_sample.shape

    x1p, convp, rep, imp, wg, wu, wd = _mixer(x_prompt, mix_w, cast=(w_gate[0], w_up[0], w_down[0]),
                                              nb=SUBLANES, tc=128)
    ffn_w = (norm_ffn, wg, wu, wd, norm_final.reshape(1, D_MODEL))
    def state_major(h):
        return h[0].transpose(1, 2, 0).reshape(n_st, h.shape[1])

    sample_state = (cache_conv, state_major(state_ssm_re), state_major(state_ssm_im))
    x1s, convs, res, ims = _mixer(x_sample, mix_w, sample_state, nb=128, tc=SUBLANES)

    yp, ys = _ffn(x1p.reshape(bp * lp, D_MODEL), x1s.reshape(bs * ls, D_MODEL), ffn_w, tm=1024)
    yp = yp.reshape(bp, lp, D_MODEL)
    ys = ys.reshape(bs, ls, D_MODEL)

    def seq_major(h):
        return h.reshape(N_GROUPS, STATE_P, h.shape[1]).transpose(2, 0, 1)[None]

    return (yp, ys, convp, rep.reshape(1, bp, N_GROUPS, STATE_P), imp.reshape(1, bp, N_GROUPS, STATE_P),
            convs, seq_major(res), seq_major(ims))
```

```python
import functools

import jax
import jax.numpy as jnp
from jax import lax
from jax.experimental import pallas as pl
from jax.experimental.pallas import tpu as pltpu

D_MODEL = 1024
W_CONV = 512
W_SSM = 512
CONV_W = 3
N_GROUPS = 32
SSM_H = 16
STATE_P = 64
D_FF = 2816
EPS = 1e-5

SUBLANES = 8
BF16_ROWS = 16
GROUPS_PER_HALF = 16
N_HALVES = N_GROUPS // GROUPS_PER_HALF
U_HALF = GROUPS_PER_HALF * SSM_H
S_HALF = GROUPS_PER_HALF * STATE_P
S_COLS = 2 * N_GROUPS * STATE_P
SCAN_LANES = 512
MXU_COLS = 256
FFN_CHUNKS = tuple((s, 256) for s in range(0, D_FF, 256))
VMEM_LIMIT = 60 * 1024 * 1024

BF16 = jnp.bfloat16
F32 = jnp.float32


def _rms(x, g):
    y = x * lax.rsqrt(jnp.mean(x * x, axis=-1, keepdims=True) + EPS)
    return y * g


def _dot(a, b):
    return jnp.dot(a, b, preferred_element_type=F32)


def _tile_copies(hbm, buf, sem, step, slot, *, nb, tc, per_seq, to_vmem):
    copies = []
    for k in range(SUBLANES):
        if per_seq:
            h = hbm.at[k, pl.ds(step * tc, tc), :]
            v = buf.at[slot, :, k, :]
        else:
            h = hbm.at[pl.ds(step * nb, nb), k, :]
            v = buf.at[slot, k]
        src, dst = (h, v) if to_vmem else (v, h)
        copies.append(pltpu.make_async_copy(src, dst, sem.at[slot, k]))
    return copies


def _mixer_body(x_hbm, *refs, nb, tc, n_steps, carry, n_cast):
    if carry:
        cache_ref = re0_ref = im0_ref = None
    else:
        cache_ref, re0_ref, im0_ref, *refs = refs
    (nmix_ref, win_ref, convw_ref, lamr_ref, lami_ref, bc_ref, cc_ref, dskip_ref, wglu_ref, bglu_ref, wout_ref,
     *refs) = refs
    cast_in, refs = refs[:n_cast], refs[n_cast:]
    x1_hbm, convnew_hbm, hre_ref, him_ref, *refs = refs
    cast_out, refs = refs[:n_cast], refs[n_cast:]
    xbuf, obuf, bubuf, hbf, hstate, vbuf, insem, outsem, cachesem = refs
    tm = nb * tc

    for src, dst in zip(cast_in, cast_out):
        dst[...] = src[...].astype(BF16)

    i = pl.program_id(0)
    slot = lax.rem(i, 2)
    cp = functools.partial(_tile_copies, nb=nb, tc=tc, per_seq=carry)
    in_copies = functools.partial(cp, x_hbm, xbuf, insem, to_vmem=True)
    out_copies = functools.partial(cp, x1_hbm, obuf, outsem, to_vmem=False)

    def cache_copies(hbm, row0, to_vmem):
        copies = []
        for k in range(CONV_W - 1):
            h = hbm.at[0, pl.ds(0 if carry else i * nb, nb), k, :]
            v = vbuf.at[pl.ds(row0 + k * nb, nb), :]
            src, dst = (h, v) if to_vmem else (v, h)
            copies.append(pltpu.make_async_copy(src, dst, cachesem.at[int(to_vmem), k]))
        return copies

    @pl.when(i == 0)
    def _():
        for c in in_copies(0, 0):
            c.start()

    for c in in_copies(i, slot):
        c.wait()

    @pl.when(i + 1 < n_steps)
    def _():
        for c in in_copies(i + 1, 1 - slot):
            c.start()

    @pl.when(i >= 2)
    def _():
        for c in out_copies(i - 2, slot):
            c.wait()

    if carry:
        @pl.when(i == 0)
        def _():
            vbuf[0:2 * nb, :] = jnp.zeros((2 * nb, W_CONV), F32)
            hstate[...] = jnp.zeros((nb, S_COLS), F32)
    else:
        for c in cache_copies(cache_ref, 0, True):
            c.start()
        for n in range(N_HALVES):
            hstate[:, 2 * n * S_HALF:(2 * n + 1) * S_HALF] = re0_ref[n * S_HALF:(n + 1) * S_HALF, :].T
            hstate[:, (2 * n + 1) * S_HALF:(2 * n + 2) * S_HALF] = im0_ref[n * S_HALF:(n + 1) * S_HALF, :].T

    x = xbuf[slot].reshape(tm, D_MODEL)
    xn = _rms(x, nmix_ref[...]).astype(BF16)

    u = _dot(xn, win_ref[:, 3 * W_CONV:])
    ub = u.astype(BF16)
    for n in range(N_HALVES):
        bubuf[:, 2 * n * S_HALF:(2 * n + 2) * S_HALF] = _dot(ub[:, n * U_HALF:(n + 1) * U_HALF], bc_ref[n])

    def scan_step(h, rows, lr, li, re_cols, im_cols):
        hr, hi = h
        return (lr * hr - li * hi + bubuf[rows, re_cols], lr * hi + li * hr + bubuf[rows, im_cols])

    def store_pair(row0, pair, re_cols, im_cols):
        rows = slice(row0, row0 + BF16_ROWS)
        hbf[rows, re_cols] = jnp.concatenate([h[0] for h in pair], axis=0).astype(BF16)
        hbf[rows, im_cols] = jnp.concatenate([h[1] for h in pair], axis=0).astype(BF16)

    for n in range(N_HALVES):
        for c0 in range(0, S_HALF, SCAN_LANES):
            re_cols = slice(2 * n * S_HALF + c0, 2 * n * S_HALF + c0 + SCAN_LANES)
            im_cols = slice((2 * n + 1) * S_HALF + c0, (2 * n + 1) * S_HALF + c0 + SCAN_LANES)
            lr = lamr_ref[n, :, c0:c0 + SCAN_LANES]
            li = lami_ref[n, :, c0:c0 + SCAN_LANES]
            step = functools.partial(scan_step, lr=lr, li=li, re_cols=re_cols, im_cols=im_cols)
            if nb == SUBLANES:
                h = (hstate[:, re_cols], hstate[:, im_cols])
                for t in range(0, tc, 2):
                    h1 = step(h, slice(t * nb, (t + 1) * nb))
                    h = step(h1, slice((t + 1) * nb, (t + 2) * nb))
                    store_pair(t * nb, (h1, h), re_cols, im_cols)
                hstate[:, re_cols], hstate[:, im_cols] = h
            else:
                for r0 in range(0, nb, BF16_ROWS):
                    seqs = [slice(r0 + k * SUBLANES, r0 + (k + 1) * SUBLANES) for k in range(2)]
                    pair = [(hstate[sq, re_cols], hstate[sq, im_cols]) for sq in seqs]
                    for t in range(tc):
                        pair = [step(h, slice(t * nb + sq.start, t * nb + sq.stop)) for h, sq in zip(pair, seqs)]
                        store_pair(t * nb + r0, pair, re_cols, im_cols)
                    for h, sq in zip(pair, seqs):
                        hstate[sq, re_cols], hstate[sq, im_cols] = h

    if not carry:
        for c in cache_copies(cache_ref, 0, True):
            c.wait()
    conv_chunks = []
    for c0 in range(0, W_CONV, MXU_COLS):
        cols = slice(c0, c0 + MXU_COLS)
        b_g = _dot(xn, win_ref[:, c0:c0 + MXU_COLS])
        c_g = _dot(xn, win_ref[:, W_CONV + c0:W_CONV + c0 + MXU_COLS])
        xc = _dot(xn, win_ref[:, 2 * W_CONV + c0:2 * W_CONV + c0 + MXU_COLS])
        v = c_g * xc
        vbuf[2 * nb:2 * nb + tm, cols] = v
        conv = (convw_ref[0, :, cols] * vbuf[0:tm, cols] + convw_ref[1, :, cols] * vbuf[nb:nb + tm, cols]
                + convw_ref[2, :, cols] * v)
        conv_chunks.append((b_g * conv).astype(BF16))
    conv_out = jnp.concatenate(conv_chunks, axis=-1)
    if carry:
        vbuf[0:2 * nb, :] = vbuf[tm:tm + 2 * nb, :]
    else:
        for c in cache_copies(convnew_hbm, tm, False):
            c.start()
    x1 = x + _dot(conv_out, wout_ref[0:W_CONV, :])

    last = hstate[...]
    for n in range(N_HALVES):
        last_re = last[:, 2 * n * S_HALF:(2 * n + 1) * S_HALF]
        last_im = last[:, (2 * n + 1) * S_HALF:(2 * n + 2) * S_HALF]
        if carry:
            hre_ref[:, n * S_HALF:(n + 1) * S_HALF] = last_re
            him_ref[:, n * S_HALF:(n + 1) * S_HALF] = last_im
        else:
            hre_ref[n * S_HALF:(n + 1) * S_HALF, :] = last_re.T
            him_ref[n * S_HALF:(n + 1) * S_HALF, :] = last_im.T

    zs = []
    for n in range(N_HALVES):
        cols = slice(n * U_HALF, (n + 1) * U_HALF)
        y = _dot(hbf[:, 2 * n * S_HALF:(2 * n + 2) * S_HALF], cc_ref[n]) + dskip_ref[:, cols] * u[:, cols]
        zs.append(jax.nn.gelu(y))
    zb = jnp.concatenate([z.astype(BF16) for z in zs], axis=-1)
    ssm_chunks = []
    for n, z in enumerate(zs):
        cols = slice(n * U_HALF, (n + 1) * U_HALF)
        gate = jax.nn.sigmoid(_dot(zb, wglu_ref[:, cols]) + bglu_ref[:, cols])
        ssm_chunks.append((z * gate).astype(BF16))
    ssm_out = jnp.concatenate(ssm_chunks, axis=-1)

    x1 = x1 + _dot(ssm_out, wout_ref[W_CONV:, :])

    obuf[slot] = x1.reshape(tc, nb, D_MODEL)
    for c in out_copies(i, slot):
        c.start()

    if not carry:
        for c in cache_copies(convnew_hbm, tm, False):
            c.wait()

    @pl.when(i == n_steps - 1)
    def _():
        if carry:
            for c in cache_copies(convnew_hbm, tm, False):
                c.start()
            for c in cache_copies(convnew_hbm, tm, False):
                c.wait()
        if n_steps >= 2:
            for c in out_copies(i - 1, 1 - slot):
                c.wait()
        for c in out_copies(i, slot):
            c.wait()


def _resident(shape):
    nd = len(shape)
    return pl.BlockSpec(shape, lambda i: (0,) * nd, pipeline_mode=pl.Buffered(1))


def _mixer(x, wts, state=None, cast=(), *, nb, tc):
    n_seq, n_time, _ = x.shape
    carry = state is None
    if carry:
        assert n_seq == nb == SUBLANES and n_time % tc == 0
        n_steps = n_time // tc
    else:
        assert n_time == tc == SUBLANES and n_seq % nb == 0
        n_steps = n_seq // nb
    tm = nb * tc
    n_st = N_GROUPS * STATE_P
    body = functools.partial(_mixer_body, nb=nb, tc=tc, n_steps=n_steps, carry=carry, n_cast=len(cast))
    any_spec = pl.BlockSpec(memory_space=pl.ANY)
    if carry:
        st_spec = pl.BlockSpec((nb, n_st), lambda i: (0, 0))
        st_shape = jax.ShapeDtypeStruct((n_seq, n_st), F32)
    else:
        st_spec = pl.BlockSpec((n_st, nb), lambda i: (0, i))
        st_shape = jax.ShapeDtypeStruct((n_st, n_seq), F32)
    state_specs = [] if carry else [any_spec, st_spec, st_spec]
    for w in cast:
        assert w.shape[0] % (n_steps * BF16_ROWS) == 0, w.shape
    cast_specs = [pl.BlockSpec((w.shape[0] // n_steps, w.shape[1]), lambda i: (i, 0)) for w in cast]
    return pl.pallas_call(
        body,
        grid=(n_steps,),
        in_specs=[any_spec] + state_specs + [_resident(w.shape) for w in wts] + cast_specs,
        out_specs=[any_spec, any_spec, st_spec, st_spec] + cast_specs,
        out_shape=[jax.ShapeDtypeStruct(x.shape, F32),
                   jax.ShapeDtypeStruct((1, n_seq, CONV_W - 1, W_CONV), F32),
                   st_shape, st_shape]
                  + [jax.ShapeDtypeStruct(w.shape, BF16) for w in cast],
        scratch_shapes=[pltpu.VMEM((2, tc, nb, D_MODEL), F32),
                        pltpu.VMEM((2, tc, nb, D_MODEL), F32),
                        pltpu.VMEM((tm, S_COLS), F32),
                        pltpu.VMEM((tm, S_COLS), BF16),
                        pltpu.VMEM((nb, S_COLS), F32),
                        pltpu.VMEM((2 * nb + tm, W_CONV), F32),
                        pltpu.SemaphoreType.DMA((2, SUBLANES)),
                        pltpu.SemaphoreType.DMA((2, SUBLANES)),
                        pltpu.SemaphoreType.DMA((2, CONV_W - 1))],
        compiler_params=pltpu.CompilerParams(dimension_semantics=("arbitrary",),
                                             vmem_limit_bytes=VMEM_LIMIT),
        name="mixer_carry" if carry else "mixer_step",
    )(x, *(() if carry else state), *wts, *cast)


def _ffn_body(xp_hbm, xs_hbm, nffn_ref, wg_ref, wu_ref, wd_ref, nfin_ref, yp_hbm, ys_hbm,
              xbuf, obuf, abuf, insem, outsem, *, n_p, n_s, tm):
    i = pl.program_id(0)
    n_steps = n_p + n_s
    slot = lax.rem(i, 2)

    def tile_copy(step, buf_slot, to_vmem, first):
        hbm = (xp_hbm if first else xs_hbm) if to_vmem else (yp_hbm if first else ys_hbm)
        h = hbm.at[pl.ds((step if first else step - n_p) * tm, tm), :]
        if to_vmem:
            return pltpu.make_async_copy(h, xbuf.at[buf_slot], insem.at[buf_slot])
        return pltpu.make_async_copy(obuf.at[buf_slot], h, outsem.at[buf_slot])

    def start(step, buf_slot, to_vmem):
        pl.when(step < n_p)(lambda: tile_copy(step, buf_slot, to_vmem, True).start())
        pl.when(step >= n_p)(lambda: tile_copy(step, buf_slot, to_vmem, False).start())

    def wait(buf_slot, to_vmem):
        tile_copy(0, buf_slot, to_vmem, True).wait()

    pl.when(i == 0)(lambda: start(0, 0, True))
    wait(slot, True)
    pl.when(i + 1 < n_steps)(lambda: start(i + 1, 1 - slot, True))
    pl.when(i >= 2)(lambda: wait(slot, False))

    x = xbuf[slot]
    hn = _rms(x, nffn_ref[...]).astype(BF16)
    for s, w in FFN_CHUNKS:
        g = _dot(hn, wg_ref[:, s:s + w])
        up = _dot(hn, wu_ref[:, s:s + w])
        abuf[:, s:s + w] = (g * jax.nn.sigmoid(g) * up).astype(BF16)
    acc = x + _dot(abuf[...], wd_ref[...])
    obuf[slot] = _rms(acc, nfin_ref[...])
    start(i, slot, False)

    @pl.when(i == n_steps - 1)
    def _():
        if n_steps >= 2:
            wait(1 - slot, False)
        wait(slot, False)


def _ffn(xp, xs, wts, *, tm):
    n_p, n_s = xp.shape[0] // tm, xs.shape[0] // tm
    assert xp.shape[0] == n_p * tm and xs.shape[0] == n_s * tm
    any_spec = pl.BlockSpec(memory_space=pl.ANY)
    return pl.pallas_call(
        functools.partial(_ffn_body, n_p=n_p, n_s=n_s, tm=tm),
        grid=(n_p + n_s,),
        in_specs=[any_spec, any_spec] + [_resident(w.shape) for w in wts],
        out_specs=[any_spec, any_spec],
        out_shape=[jax.ShapeDtypeStruct(xp.shape, F32), jax.ShapeDtypeStruct(xs.shape, F32)],
        scratch_shapes=[pltpu.VMEM((2, tm, D_MODEL), F32),
                        pltpu.VMEM((2, tm, D_MODEL), F32),
                        pltpu.VMEM((tm, D_FF), BF16),
                        pltpu.SemaphoreType.DMA((2,)),
                        pltpu.SemaphoreType.DMA((2,))],
        compiler_params=pltpu.CompilerParams(dimension_semantics=("arbitrary",),
                                             vmem_limit_bytes=VMEM_LIMIT),
        name="ffn",
    )(xp, xs, *wts)


def _prep_body(lre_ref, lim_ref, logdt_ref, btre_ref, btim_ref, cre_ref, cim_ref, *refs):
    n_cast = (len(refs) - 6) // 2
    cast_in, refs = refs[:n_cast], refs[n_cast:]
    s5_out, refs = refs[:4], refs[4:]
    cast_out, s5_scratch = refs[:n_cast], refs[n_cast:]
    for src, dst in zip(cast_in, cast_out):
        dst[...] = src[...].astype(BF16)
    pl.when(pl.program_id(0) == 0)(functools.partial(
        _s5_matrices, lre_ref, lim_ref, logdt_ref, btre_ref, btim_ref, cre_ref, cim_ref, *s5_out, *s5_scratch))


def _s5_matrices(lre_ref, lim_ref, logdt_ref, btre_ref, btim_ref, cre_ref, cim_ref,
                 lamr_ref, lami_ref, bc_ref, cc_ref, bscr, cscr):
    lre = lre_ref[...]
    lim = lim_ref[...]
    grp = (lax.broadcasted_iota(jnp.int32, (N_GROUPS, N_GROUPS), 0)
           == lax.broadcasted_iota(jnp.int32, (N_GROUPS, N_GROUPS), 1))
    logdt = jnp.sum(jnp.where(grp, jnp.broadcast_to(logdt_ref[...], (N_GROUPS, N_GROUPS)), 0.0),
                    axis=1, keepdims=True)
    dt = jnp.exp(logdt)
    mag = jnp.exp(lre * dt)
    lbr = mag * jnp.cos(lim * dt)
    lbi = mag * jnp.sin(lim * dt)
    den = lre * lre + lim * lim
    fr = ((lbr - 1.0) * lre + lbi * lim) / den
    fi = (lbi * lre - (lbr - 1.0) * lim) / den
    frb = fr[:, None, :]
    fib = fi[:, None, :]
    bbr = frb * btre_ref[...] - fib * btim_ref[...]
    bbi = frb * btim_ref[...] + fib * btre_ref[...]
    cre = cre_ref[...]
    ncim = -cim_ref[...]

    bscr[...] = jnp.zeros_like(bscr)
    cscr[...] = jnp.zeros_like(cscr)
    for g in range(N_GROUPS):
        n, k = divmod(g, GROUPS_PER_HALF)
        chans = slice(k * SSM_H, (k + 1) * SSM_H)
        states = slice(k * STATE_P, (k + 1) * STATE_P)
        im_states = slice(S_HALF + k * STATE_P, S_HALF + (k + 1) * STATE_P)
        bscr[n, chans, states] = bbr[g]
        bscr[n, chans, im_states] = bbi[g]
        cscr[n, chans, states] = cre[g]
        cscr[n, chans, im_states] = ncim[g]
        lamr_ref[n, :, states] = jnp.broadcast_to(lbr[g:g + 1, :], (SUBLANES, STATE_P))
        lami_ref[n, :, states] = jnp.broadcast_to(lbi[g:g + 1, :], (SUBLANES, STATE_P))
    for n in range(N_HALVES):
        bc_ref[n] = bscr[n].astype(BF16)
        cc_ref[n] = cscr[n].T.astype(BF16)


PREP_STEPS = 4


def _prep(lam_re, lam_im, log_dt, b_re, b_im, c_re, c_im, cast):
    s5_in = (lam_re, lam_im, log_dt.reshape(1, N_GROUPS), b_re.transpose(0, 2, 1), b_im.transpose(0, 2, 1),
             c_re, c_im)
    s5_shapes = [jax.ShapeDtypeStruct((N_HALVES, SUBLANES, S_HALF), F32),
                 jax.ShapeDtypeStruct((N_HALVES, SUBLANES, S_HALF), F32),
                 jax.ShapeDtypeStruct((N_HALVES, U_HALF, 2 * S_HALF), BF16),
                 jax.ShapeDtypeStruct((N_HALVES, 2 * S_HALF, U_HALF), BF16)]
    for w in cast:
        assert w.shape[0] % (PREP_STEPS * BF16_ROWS) == 0, w.shape
    cast_specs = [pl.BlockSpec((w.shape[0] // PREP_STEPS, w.shape[1]), lambda i: (i, 0)) for w in cast]
    whole = lambda a: pl.BlockSpec(a.shape, lambda i: (0,) * len(a.shape))
    return pl.pallas_call(
        _prep_body,
        grid=(PREP_STEPS,),
        in_specs=[whole(a) for a in s5_in] + cast_specs,
        out_specs=[whole(a) for a in s5_shapes] + cast_specs,
        out_shape=s5_shapes + [jax.ShapeDtypeStruct(w.shape, BF16) for w in cast],
        scratch_shapes=[pltpu.VMEM((N_HALVES, U_HALF, 2 * S_HALF), F32),
                        pltpu.VMEM((N_HALVES, U_HALF, 2 * S_HALF), F32)],
        compiler_params=pltpu.CompilerParams(dimension_semantics=("arbitrary",)),
        name="prep",
    )(*s5_in, *cast)


def kernel(x_prompt, x_sample, cache_conv, state_ssm_re, state_ssm_im, norm_mix, w_in, conv_w, ssm_lam_re, ssm_lam_im, ssm_log_dt, ssm_b_re, ssm_b_im, ssm_c_re, ssm_c_im, ssm_d, w_glu, b_glu, w_out, norm_ffn, w_gate, w_up, w_down, norm_final):
    assert norm_mix.shape[0] == 1, "single-layer trunk"
    n_st = N_GROUPS * STATE_P
    lamr, lami, bc, cc, w_in_b, w_glu_b, w_out_b = _prep(
        ssm_lam_re[0], ssm_lam_im[0], ssm_log_dt[0], ssm_b_re[0], ssm_b_im[0], ssm_c_re[0], ssm_c_im[0],
        cast=(w_in[0], w_glu[0], w_out[0]))
    mix_w = (norm_mix, w_in_b, conv_w[0][:, None, :], lamr, lami, bc, cc, ssm_d, w_glu_b, b_glu, w_out_b)

    bp, lp, _ = x_prompt.shape
    bs, ls, _ = x_sample.shape

    x1p, convp, rep, imp, wg, wu, wd = _mixer(x_prompt, mix_w, cast=(w_gate[0], w_up[0], w_down[0]),
                                              nb=SUBLANES, tc=128)
    ffn_w = (norm_ffn, wg, wu, wd, norm_final.reshape(1, D_MODEL))
    def state_major(h):
        return h[0].transpose(1, 2, 0).reshape(n_st, h.shape[1])

    sample_state = (cache_conv, state_major(state_ssm_re), state_major(state_ssm_im))
    x1s, convs, res, ims = _mixer(x_sample, mix_w, sample_state, nb=128, tc=SUBLANES)

    yp, ys = _ffn(x1p.reshape(bp * lp, D_MODEL), x1s.reshape(bs * ls, D_MODEL), ffn_w, tm=1024)
    yp = yp.reshape(bp, lp, D_MODEL)
    ys = ys.reshape(bs, ls, D_MODEL)

    def seq_major(h):
        return h.reshape(N_GROUPS, STATE_P, h.shape[1]).transpose(2, 0, 1)[None]

    return (yp, ys, convp, rep.reshape(1, bp, N_GROUPS, STATE_P), imp.reshape(1, bp, N_GROUPS, STATE_P),
            convs, seq_major(res), seq_major(ims))
```

```python
import functools

import jax
import jax.numpy as jnp
from jax import lax
from jax.experimental import pallas as pl
from jax.experimental.pallas import tpu as pltpu

D_MODEL = 1024
W_CONV = 512
W_SSM = 512
CONV_W = 3
N_GROUPS = 32
SSM_H = 16
STATE_P = 64
D_FF = 2816
EPS = 1e-5

SUBLANES = 8
BF16_ROWS = 16
GROUPS_PER_HALF = 16
N_HALVES = N_GROUPS // GROUPS_PER_HALF
U_HALF = GROUPS_PER_HALF * SSM_H
S_HALF = GROUPS_PER_HALF * STATE_P
S_COLS = 2 * N_GROUPS * STATE_P
SCAN_LANES = 512
MXU_COLS = 256
FFN_CHUNKS = tuple((s, 256) for s in range(0, D_FF, 256))
VMEM_LIMIT = 60 * 1024 * 1024

BF16 = jnp.bfloat16
F32 = jnp.float32


def _rms(x, g):
    y = x * lax.rsqrt(jnp.mean(x * x, axis=-1, keepdims=True) + EPS)
    return y * g


def _dot(a, b):
    return jnp.dot(a, b, preferred_element_type=F32)


def _tile_copies(hbm, buf, sem, step, slot, *, nb, tc, per_seq, to_vmem):
    copies = []
    for k in range(SUBLANES):
        if per_seq:
            h = hbm.at[k, pl.ds(step * tc, tc), :]
            v = buf.at[slot, :, k, :]
        else:
            h = hbm.at[pl.ds(step * nb, nb), k, :]
            v = buf.at[slot, k]
        src, dst = (h, v) if to_vmem else (v, h)
        copies.append(pltpu.make_async_copy(src, dst, sem.at[slot, k]))
    return copies


def _mixer_body(x_hbm, *refs, nb, tc, n_steps, carry, n_cast):
    if carry:
        cache_ref = re0_ref = im0_ref = None
    else:
        cache_ref, re0_ref, im0_ref, *refs = refs
    (nmix_ref, win_ref, convw_ref, lam2r_ref, lam2i_ref, bc2_ref, cc_ref, ccl_ref, d0_ref, dskip_ref,
     wglu_ref, bglu_ref, wout_ref, *refs) = refs
    cast_in, refs = refs[:n_cast], refs[n_cast:]
    x1_hbm, convnew_hbm, hre_ref, him_ref, *refs = refs
    cast_out, refs = refs[:n_cast], refs[n_cast:]
    xbuf, obuf, gbuf, hb_odd, hb_prev, hstate, vbuf, insem, outsem, cachesem = refs
    tm = nb * tc
    n_pair = tc // 2
    tmh = n_pair * nb

    i = pl.program_id(0)
    slot = lax.rem(i, 2)
    cp = functools.partial(_tile_copies, nb=nb, tc=tc, per_seq=carry)
    in_copies = functools.partial(cp, x_hbm, xbuf, insem, to_vmem=True)
    out_copies = functools.partial(cp, x1_hbm, obuf, outsem, to_vmem=False)

    def cache_copies(hbm, row0, to_vmem):
        copies = []
        for k in range(CONV_W - 1):
            h = hbm.at[0, pl.ds(0 if carry else i * nb, nb), k, :]
            v = vbuf.at[pl.ds(row0 + k * nb, nb), :]
            src, dst = (h, v) if to_vmem else (v, h)
            copies.append(pltpu.make_async_copy(src, dst, cachesem.at[int(to_vmem), k]))
        return copies

    @pl.when(i == 0)
    def _():
        for c in in_copies(0, 0):
            c.start()

    if not carry:
        for c in cache_copies(cache_ref, 0, True):
            c.start()
    for c in in_copies(i, slot):
        c.wait()
    if not carry:
        for c in cache_copies(cache_ref, 0, True):
            c.wait()

    @pl.when(i + 1 < n_steps)
    def _():
        for c in in_copies(i + 1, 1 - slot):
            c.start()

    @pl.when(i >= 2)
    def _():
        for c in out_copies(i - 2, slot):
            c.wait()

    if carry:
        @pl.when(i == 0)
        def _():
            vbuf[0:2 * nb, :] = jnp.zeros((2 * nb, W_CONV), F32)
            hstate[...] = jnp.zeros((nb, S_COLS), F32)
    else:
        for n in range(N_HALVES):
            hstate[:, 2 * n * S_HALF:(2 * n + 1) * S_HALF] = re0_ref[n * S_HALF:(n + 1) * S_HALF, :].T
            hstate[:, (2 * n + 1) * S_HALF:(2 * n + 2) * S_HALF] = im0_ref[n * S_HALF:(n + 1) * S_HALF, :].T

    x = xbuf[slot].reshape(tm, D_MODEL)
    xn = _rms(x, nmix_ref[...]).astype(BF16)

    u = _dot(xn, win_ref[:, 3 * W_CONV:])
    u_pairs = u.reshape(n_pair, 2, nb, W_SSM)
    u_e = u_pairs[:, 0].reshape(tmh, W_SSM)
    u_o = u_pairs[:, 1].reshape(tmh, W_SSM)
    ub_e = u_e.astype(BF16)
    ub_o = u_o.astype(BF16)
    for n in range(N_HALVES):
        cols = slice(n * U_HALF, (n + 1) * U_HALF)
        gbuf[:, 2 * n * S_HALF:(2 * n + 2) * S_HALF] = _dot(
            jnp.concatenate([ub_e[:, cols], ub_o[:, cols]], axis=1), bc2_ref[n])

    for src, dst in zip(cast_in, cast_out):
        dst[...] = src[...].astype(BF16)

    def scan_step(h, rows, lr, li, re_cols, im_cols):
        hr, hi = h
        return (lr * hr - li * hi + gbuf[rows, re_cols], lr * hi + li * hr + gbuf[rows, im_cols])

    def store_pair(buf, row0, pair, re_cols, im_cols):
        rows = slice(row0, row0 + BF16_ROWS)
        buf[rows, re_cols] = jnp.concatenate([h[0] for h in pair], axis=0).astype(BF16)
        buf[rows, im_cols] = jnp.concatenate([h[1] for h in pair], axis=0).astype(BF16)

    for n in range(N_HALVES):
        for c0 in range(0, S_HALF, SCAN_LANES):
            re_cols = slice(2 * n * S_HALF + c0, 2 * n * S_HALF + c0 + SCAN_LANES)
            im_cols = slice((2 * n + 1) * S_HALF + c0, (2 * n + 1) * S_HALF + c0 + SCAN_LANES)
            lr = lam2r_ref[n, :, c0:c0 + SCAN_LANES]
            li = lam2i_ref[n, :, c0:c0 + SCAN_LANES]
            step = functools.partial(scan_step, lr=lr, li=li, re_cols=re_cols, im_cols=im_cols)
            if nb == SUBLANES:
                h = (hstate[:, re_cols], hstate[:, im_cols])
                for k in range(0, n_pair, 2):
                    h0 = step(h, slice(k * nb, (k + 1) * nb))
                    h1 = step(h0, slice((k + 1) * nb, (k + 2) * nb))
                    store_pair(hb_odd, k * nb, (h0, h1), re_cols, im_cols)
                    store_pair(hb_prev, k * nb, (h, h0), re_cols, im_cols)
                    h = h1
                hstate[:, re_cols], hstate[:, im_cols] = h
            else:
                for r0 in range(0, nb, BF16_ROWS):
                    seqs = [slice(r0 + j * SUBLANES, r0 + (j + 1) * SUBLANES) for j in range(2)]
                    pair = [(hstate[sq, re_cols], hstate[sq, im_cols]) for sq in seqs]
                    for k in range(n_pair):
                        store_pair(hb_prev, k * nb + r0, pair, re_cols, im_cols)
                        pair = [step(h, slice(k * nb + sq.start, k * nb + sq.stop)) for h, sq in zip(pair, seqs)]
                        store_pair(hb_odd, k * nb + r0, pair, re_cols, im_cols)
                    for h, sq in zip(pair, seqs):
                        hstate[sq, re_cols], hstate[sq, im_cols] = h

    conv_chunks = []
    for c0 in range(0, W_CONV, MXU_COLS):
        cols = slice(c0, c0 + MXU_COLS)
        b_g = _dot(xn, win_ref[:, c0:c0 + MXU_COLS])
        c_g = _dot(xn, win_ref[:, W_CONV + c0:W_CONV + c0 + MXU_COLS])
        xc = _dot(xn, win_ref[:, 2 * W_CONV + c0:2 * W_CONV + c0 + MXU_COLS])
        v = c_g * xc
        vbuf[2 * nb:2 * nb + tm, cols] = v
        conv = (convw_ref[0, :, cols] * vbuf[0:tm, cols] + convw_ref[1, :, cols] * vbuf[nb:nb + tm, cols]
                + convw_ref[2, :, cols] * v)
        conv_chunks.append((b_g * conv).astype(BF16))
    conv_out = jnp.concatenate(conv_chunks, axis=-1)
    if carry:
        vbuf[0:2 * nb, :] = vbuf[tm:tm + 2 * nb, :]
    x1 = x + _dot(conv_out, wout_ref[0:W_CONV, :])

    last = hstate[...]
    for n in range(N_HALVES):
        last_re = last[:, 2 * n * S_HALF:(2 * n + 1) * S_HALF]
        last_im = last[:, (2 * n + 1) * S_HALF:(2 * n + 2) * S_HALF]
        if carry:
            hre_ref[:, n * S_HALF:(n + 1) * S_HALF] = last_re
            him_ref[:, n * S_HALF:(n + 1) * S_HALF] = last_im
        else:
            hre_ref[n * S_HALF:(n + 1) * S_HALF, :] = last_re.T
            him_ref[n * S_HALF:(n + 1) * S_HALF, :] = last_im.T

    zs = []
    for n in range(N_HALVES):
        cols = slice(n * U_HALF, (n + 1) * U_HALF)
        st_cols = slice(2 * n * S_HALF, (2 * n + 2) * S_HALF)
        y_o = _dot(hb_odd[:, st_cols], cc_ref[n]) + dskip_ref[:, cols] * u_o[:, cols]
        y_e = (_dot(hb_prev[:, st_cols], ccl_ref[n]) + _dot(ub_e[:, cols], d0_ref[n])
               + dskip_ref[:, cols] * u_e[:, cols])
        y = jnp.stack([y_e.reshape(n_pair, nb, U_HALF), y_o.reshape(n_pair, nb, U_HALF)],
                      axis=1).reshape(tm, U_HALF)
        zs.append(jax.nn.gelu(y))
    zb = jnp.concatenate([z.astype(BF16) for z in zs], axis=-1)
    ssm_chunks = []
    for n, z in enumerate(zs):
        cols = slice(n * U_HALF, (n + 1) * U_HALF)
        gate = jax.nn.sigmoid(_dot(zb, wglu_ref[:, cols]) + bglu_ref[:, cols])
        ssm_chunks.append((z * gate).astype(BF16))
    ssm_out = jnp.concatenate(ssm_chunks, axis=-1)

    x1 = x1 + _dot(ssm_out, wout_ref[W_CONV:, :])

    obuf[slot] = x1.reshape(tc, nb, D_MODEL)
    for c in out_copies(i, slot):
        c.start()

    if not carry:
        for c in cache_copies(convnew_hbm, tm, False):
            c.start()
        for c in cache_copies(convnew_hbm, tm, False):
            c.wait()

    @pl.when(i == n_steps - 1)
    def _():
        if carry:
            for c in cache_copies(convnew_hbm, tm, False):
                c.start()
            for c in cache_copies(convnew_hbm, tm, False):
                c.wait()
        if n_steps >= 2:
            for c in out_copies(i - 1, 1 - slot):
                c.wait()
        for c in out_copies(i, slot):
            c.wait()


def _resident(shape):
    nd = len(shape)
    return pl.BlockSpec(shape, lambda i: (0,) * nd, pipeline_mode=pl.Buffered(1))


def _mixer(x, wts, state=None, cast=(), *, nb, tc):
    n_seq, n_time, _ = x.shape
    carry = state is None
    if carry:
        assert n_seq == nb == SUBLANES and n_time % tc == 0 and tc % 4 == 0
        n_steps = n_time // tc
    else:
        assert n_time == tc == SUBLANES and n_seq % nb == 0 and nb % BF16_ROWS == 0
        n_steps = n_seq // nb
    tm = nb * tc
    n_st = N_GROUPS * STATE_P
    body = functools.partial(_mixer_body, nb=nb, tc=tc, n_steps=n_steps, carry=carry, n_cast=len(cast))
    any_spec = pl.BlockSpec(memory_space=pl.ANY)
    if carry:
        st_spec = pl.BlockSpec((nb, n_st), lambda i: (0, 0))
        st_shape = jax.ShapeDtypeStruct((n_seq, n_st), F32)
    else:
        st_spec = pl.BlockSpec((n_st, nb), lambda i: (0, i))
        st_shape = jax.ShapeDtypeStruct((n_st, n_seq), F32)
    state_specs = [] if carry else [any_spec, st_spec, st_spec]
    for w in cast:
        assert w.shape[0] % (n_steps * BF16_ROWS) == 0, w.shape
    cast_specs = [pl.BlockSpec((w.shape[0] // n_steps, w.shape[1]), lambda i: (i, 0)) for w in cast]
    return pl.pallas_call(
        body,
        grid=(n_steps,),
        in_specs=[any_spec] + state_specs + [_resident(w.shape) for w in wts] + cast_specs,
        out_specs=[any_spec, any_spec, st_spec, st_spec] + cast_specs,
        out_shape=[jax.ShapeDtypeStruct(x.shape, F32),
                   jax.ShapeDtypeStruct((1, n_seq, CONV_W - 1, W_CONV), F32),
                   st_shape, st_shape]
                  + [jax.ShapeDtypeStruct(w.shape, BF16) for w in cast],
        scratch_shapes=[pltpu.VMEM((2, tc, nb, D_MODEL), F32),
                        pltpu.VMEM((2, tc, nb, D_MODEL), F32),
                        pltpu.VMEM((tm // 2, S_COLS), F32),
                        pltpu.VMEM((tm // 2, S_COLS), BF16),
                        pltpu.VMEM((tm // 2, S_COLS), BF16),
                        pltpu.VMEM((nb, S_COLS), F32),
                        pltpu.VMEM((2 * nb + tm, W_CONV), F32),
                        pltpu.SemaphoreType.DMA((2, SUBLANES)),
                        pltpu.SemaphoreType.DMA((2, SUBLANES)),
                        pltpu.SemaphoreType.DMA((2, CONV_W - 1))],
        compiler_params=pltpu.CompilerParams(dimension_semantics=("arbitrary",),
                                             vmem_limit_bytes=VMEM_LIMIT),
        name="mixer_carry" if carry else "mixer_step",
    )(x, *(() if carry else state), *wts, *cast)


def _ffn_body(xp_hbm, xs_hbm, nffn_ref, wg_ref, wu_ref, wd_ref, nfin_ref, yp_hbm, ys_hbm,
              xbuf, obuf, abuf, insem, outsem, *, n_p, n_s, tm):
    i = pl.program_id(0)
    n_steps = n_p + n_s
    slot = lax.rem(i, 2)

    def tile_copy(step, buf_slot, to_vmem, first):
        hbm = (xp_hbm if first else xs_hbm) if to_vmem else (yp_hbm if first else ys_hbm)
        h = hbm.at[pl.ds((step if first else step - n_p) * tm, tm), :]
        if to_vmem:
            return pltpu.make_async_copy(h, xbuf.at[buf_slot], insem.at[buf_slot])
        return pltpu.make_async_copy(obuf.at[buf_slot], h, outsem.at[buf_slot])

    def start(step, buf_slot, to_vmem):
        pl.when(step < n_p)(lambda: tile_copy(step, buf_slot, to_vmem, True).start())
        pl.when(step >= n_p)(lambda: tile_copy(step, buf_slot, to_vmem, False).start())

    def wait(buf_slot, to_vmem):
        tile_copy(0, buf_slot, to_vmem, True).wait()

    pl.when(i == 0)(lambda: start(0, 0, True))
    wait(slot, True)
    pl.when(i + 1 < n_steps)(lambda: start(i + 1, 1 - slot, True))
    pl.when(i >= 2)(lambda: wait(slot, False))

    x = xbuf[slot]
    hn = _rms(x, nffn_ref[...]).astype(BF16)
    for s, w in FFN_CHUNKS:
        g = _dot(hn, wg_ref[:, s:s + w])
        up = _dot(hn, wu_ref[:, s:s + w])
        abuf[:, s:s + w] = (g * jax.nn.sigmoid(g) * up).astype(BF16)
    acc = x + _dot(abuf[...], wd_ref[...])
    obuf[slot] = _rms(acc, nfin_ref[...])
    start(i, slot, False)

    @pl.when(i == n_steps - 1)
    def _():
        if n_steps >= 2:
            wait(1 - slot, False)
        wait(slot, False)


def _ffn(xp, xs, wts, *, tm):
    n_p, n_s = xp.shape[0] // tm, xs.shape[0] // tm
    assert xp.shape[0] == n_p * tm and xs.shape[0] == n_s * tm
    any_spec = pl.BlockSpec(memory_space=pl.ANY)
    return pl.pallas_call(
        functools.partial(_ffn_body, n_p=n_p, n_s=n_s, tm=tm),
        grid=(n_p + n_s,),
        in_specs=[any_spec, any_spec] + [_resident(w.shape) for w in wts],
        out_specs=[any_spec, any_spec],
        out_shape=[jax.ShapeDtypeStruct(xp.shape, F32), jax.ShapeDtypeStruct(xs.shape, F32)],
        scratch_shapes=[pltpu.VMEM((2, tm, D_MODEL), F32),
                        pltpu.VMEM((2, tm, D_MODEL), F32),
                        pltpu.VMEM((tm, D_FF), BF16),
                        pltpu.SemaphoreType.DMA((2,)),
                        pltpu.SemaphoreType.DMA((2,))],
        compiler_params=pltpu.CompilerParams(dimension_semantics=("arbitrary",),
                                             vmem_limit_bytes=VMEM_LIMIT),
        name="ffn",
    )(xp, xs, *wts)


N_S5_OUT = 6
N_S5_SCRATCH = 4


def _prep_body(lre_ref, lim_ref, logdt_ref, btre_ref, btim_ref, cre_ref, cim_ref, *refs):
    n_cast = (len(refs) - N_S5_OUT - N_S5_SCRATCH) // 2
    cast_in, refs = refs[:n_cast], refs[n_cast:]
    s5_out, refs = refs[:N_S5_OUT], refs[N_S5_OUT:]
    cast_out, s5_scratch = refs[:n_cast], refs[n_cast:]
    for src, dst in zip(cast_in, cast_out):
        dst[...] = src[...].astype(BF16)
    pl.when(pl.program_id(0) == 0)(functools.partial(
        _s5_matrices, lre_ref, lim_ref, logdt_ref, btre_ref, btim_ref, cre_ref, cim_ref, *s5_out, *s5_scratch))


def _s5_matrices(lre_ref, lim_ref, logdt_ref, btre_ref, btim_ref, cre_ref, cim_ref,
                 lam2r_ref, lam2i_ref, bc2_ref, cc_ref, ccl_ref, d0_ref, bscr, lbscr, cscr, clscr):
    lre = lre_ref[...]
    lim = lim_ref[...]
    grp = (lax.broadcasted_iota(jnp.int32, (N_GROUPS, N_GROUPS), 0)
           == lax.broadcasted_iota(jnp.int32, (N_GROUPS, N_GROUPS), 1))
    logdt = jnp.sum(jnp.where(grp, jnp.broadcast_to(logdt_ref[...], (N_GROUPS, N_GROUPS)), 0.0),
                    axis=1, keepdims=True)
    dt = jnp.exp(logdt)
    mag = jnp.exp(lre * dt)
    lbr = mag * jnp.cos(lim * dt)
    lbi = mag * jnp.sin(lim * dt)
    l2r = lbr * lbr - lbi * lbi
    l2i = 2.0 * (lbr * lbi)
    den = lre * lre + lim * lim
    fr = ((lbr - 1.0) * lre + lbi * lim) / den
    fi = (lbi * lre - (lbr - 1.0) * lim) / den
    frb, fib = fr[:, None, :], fi[:, None, :]
    lrb, lib = lbr[:, None, :], lbi[:, None, :]
    bbr = frb * btre_ref[...] - fib * btim_ref[...]
    bbi = frb * btim_ref[...] + fib * btre_ref[...]
    lbbr = lrb * bbr - lib * bbi
    lbbi = lrb * bbi + lib * bbr
    cre = cre_ref[...]
    cim = cim_ref[...]
    clr = cre * lrb - cim * lib
    cli = cre * lib + cim * lrb

    for scr in (bscr, lbscr, cscr, clscr):
        scr[...] = jnp.zeros_like(scr)
    for g in range(N_GROUPS):
        n, k = divmod(g, GROUPS_PER_HALF)
        chans = slice(k * SSM_H, (k + 1) * SSM_H)
        states = slice(k * STATE_P, (k + 1) * STATE_P)
        im_states = slice(S_HALF + k * STATE_P, S_HALF + (k + 1) * STATE_P)
        for scr, re, im in ((bscr, bbr, bbi), (lbscr, lbbr, lbbi), (cscr, cre, -cim), (clscr, clr, -cli)):
            scr[n, chans, states] = re[g]
            scr[n, chans, im_states] = im[g]
        lam2r_ref[n, :, states] = jnp.broadcast_to(l2r[g:g + 1, :], (SUBLANES, STATE_P))
        lam2i_ref[n, :, states] = jnp.broadcast_to(l2i[g:g + 1, :], (SUBLANES, STATE_P))
    for n in range(N_HALVES):
        bc2_ref[n, 0:U_HALF, :] = lbscr[n].astype(BF16)
        bc2_ref[n, U_HALF:, :] = bscr[n].astype(BF16)
        c_t = cscr[n].T
        cc_ref[n] = c_t.astype(BF16)
        ccl_ref[n] = clscr[n].T.astype(BF16)
        d0_ref[n] = jnp.dot(bscr[n], c_t, precision=lax.Precision.HIGHEST,
                            preferred_element_type=F32).astype(BF16)


PREP_STEPS = 4


def _prep(lam_re, lam_im, log_dt, b_re, b_im, c_re, c_im, cast):
    s5_in = (lam_re, lam_im, log_dt.reshape(1, N_GROUPS), b_re.transpose(0, 2, 1), b_im.transpose(0, 2, 1),
             c_re, c_im)
    s5_shapes = [jax.ShapeDtypeStruct((N_HALVES, SUBLANES, S_HALF), F32),
                 jax.ShapeDtypeStruct((N_HALVES, SUBLANES, S_HALF), F32),
                 jax.ShapeDtypeStruct((N_HALVES, 2 * U_HALF, 2 * S_HALF), BF16),
                 jax.ShapeDtypeStruct((N_HALVES, 2 * S_HALF, U_HALF), BF16),
                 jax.ShapeDtypeStruct((N_HALVES, 2 * S_HALF, U_HALF), BF16),
                 jax.ShapeDtypeStruct((N_HALVES, U_HALF, U_HALF), BF16)]
    assert len(s5_shapes) == N_S5_OUT
    for w in cast:
        assert w.shape[0] % (PREP_STEPS * BF16_ROWS) == 0, w.shape
    cast_specs = [pl.BlockSpec((w.shape[0] // PREP_STEPS, w.shape[1]), lambda i: (i, 0)) for w in cast]
    whole = lambda a: pl.BlockSpec(a.shape, lambda i: (0,) * len(a.shape))
    return pl.pallas_call(
        _prep_body,
        grid=(PREP_STEPS,),
        in_specs=[whole(a) for a in s5_in] + cast_specs,
        out_specs=[whole(a) for a in s5_shapes] + cast_specs,
        out_shape=s5_shapes + [jax.ShapeDtypeStruct(w.shape, BF16) for w in cast],
        scratch_shapes=[pltpu.VMEM((N_HALVES, U_HALF, 2 * S_HALF), F32)] * N_S5_SCRATCH,
        compiler_params=pltpu.CompilerParams(dimension_semantics=("arbitrary",)),
        name="prep",
    )(*s5_in, *cast)


def kernel(x_prompt, x_sample, cache_conv, state_ssm_re, state_ssm_im, norm_mix, w_in, conv_w, ssm_lam_re, ssm_lam_im, ssm_log_dt, ssm_b_re, ssm_b_im, ssm_c_re, ssm_c_im, ssm_d, w_glu, b_glu, w_out, norm_ffn, w_gate, w_up, w_down, norm_final):
    assert norm_mix.shape[0] == 1, "single-layer trunk"
    n_st = N_GROUPS * STATE_P
    lam2r, lam2i, bc2, cc, ccl, d0, w_in_b, w_glu_b, w_out_b = _prep(
        ssm_lam_re[0], ssm_lam_im[0], ssm_log_dt[0], ssm_b_re[0], ssm_b_im[0], ssm_c_re[0], ssm_c_im[0],
        cast=(w_in[0], w_glu[0], w_out[0]))
    mix_w = (norm_mix, w_in_b, conv_w[0][:, None, :], lam2r, lam2i, bc2, cc, ccl, d0, ssm_d, w_glu_b, b_glu,
             w_out_b)

    bp, lp, _ = x_prompt.shape
    bs, ls, _ = x_sample.shape

    x1p, convp, rep, imp, wg, wu, wd = _mixer(x_prompt, mix_w, cast=(w_gate[0], w_up[0], w_down[0]),
                                              nb=SUBLANES, tc=128)
    ffn_w = (norm_ffn, wg, wu, wd, norm_final.reshape(1, D_MODEL))
    def state_major(h):
        return h[0].transpose(1, 2, 0).reshape(n_st, h.shape[1])

    sample_state = (cache_conv, state_major(state_ssm_re), state_major(state_ssm_im))
    x1s, convs, res, ims = _mixer(x_sample, mix_w, sample_state, nb=128, tc=SUBLANES)

    yp, ys = _ffn(x1p.reshape(bp * lp, D_MODEL), x1s.reshape(bs * ls, D_MODEL), ffn_w, tm=1024)
    yp = yp.reshape(bp, lp, D_MODEL)
    ys = ys.reshape(bs, ls, D_MODEL)

    def seq_major(h):
        return h.reshape(N_GROUPS, STATE_P, h.shape[1]).transpose(2, 0, 1)[None]

    return (yp, ys, convp, rep.reshape(1, bp, N_GROUPS, STATE_P), imp.reshape(1, bp, N_GROUPS, STATE_P),
            convs, seq_major(res), seq_major(ims))
```

```python
import functools

import jax
import jax.numpy as jnp
from jax import lax
from jax.experimental import pallas as pl
from jax.experimental.pallas import tpu as pltpu

D_MODEL = 1024
W_CONV = 512
W_SSM = 512
CONV_W = 3
N_GROUPS = 32
SSM_H = 16
STATE_P = 64
D_FF = 2816
EPS = 1e-5

SUBLANES = 8
BF16_ROWS = 16
GROUPS_PER_HALF = 16
N_HALVES = N_GROUPS // GROUPS_PER_HALF
U_HALF = GROUPS_PER_HALF * SSM_H
S_HALF = GROUPS_PER_HALF * STATE_P
S_COLS = 2 * N_GROUPS * STATE_P
SCAN_LANES = 512
MXU_COLS = 256
FFN_CHUNKS = tuple((s, 256) for s in range(0, D_FF, 256))
VMEM_LIMIT = 60 * 1024 * 1024

BF16 = jnp.bfloat16
F32 = jnp.float32


def _rms(x, g):
    y = x * lax.rsqrt(jnp.mean(x * x, axis=-1, keepdims=True) + EPS)
    return y * g


def _dot(a, b):
    return jnp.dot(a, b, preferred_element_type=F32)


def _split_bf16(a):
    hi = a.astype(BF16)
    return hi, (a - hi.astype(F32)).astype(BF16)


def _tile_copies(hbm, buf, sem, step, slot, *, nb, tc, per_seq, to_vmem):
    copies = []
    for k in range(SUBLANES):
        if per_seq:
            h = hbm.at[k, pl.ds(step * tc, tc), :]
            v = buf.at[slot, :, k, :]
        else:
            h = hbm.at[pl.ds(step * nb, nb), k, :]
            v = buf.at[slot, k]
        src, dst = (h, v) if to_vmem else (v, h)
        copies.append(pltpu.make_async_copy(src, dst, sem.at[slot, k]))
    return copies


def _mixer_body(x_hbm, *refs, nb, tc, n_steps, carry, n_cast):
    if carry:
        cache_ref = re0_ref = im0_ref = None
    else:
        cache_ref, re0_ref, im0_ref, *refs = refs
    (nmix_ref, win_ref, convw_ref, lam2r_ref, lam2i_ref, bc2_ref, cc_ref, ccl_ref, d0_ref, dskip_ref,
     wglu_ref, bglu_ref, wout_ref, *refs) = refs
    cast_in, refs = refs[:n_cast], refs[n_cast:]
    x1_hbm, convnew_hbm, hre_ref, him_ref, *refs = refs
    cast_out, refs = refs[:n_cast], refs[n_cast:]
    xbuf, obuf, gbuf, hb_odd, hb_prev, hstate, vbuf, insem, outsem, cachesem = refs
    tm = nb * tc
    n_pair = tc // 2
    tmh = n_pair * nb

    i = pl.program_id(0)
    slot = lax.rem(i, 2)
    cp = functools.partial(_tile_copies, nb=nb, tc=tc, per_seq=carry)
    in_copies = functools.partial(cp, x_hbm, xbuf, insem, to_vmem=True)
    out_copies = functools.partial(cp, x1_hbm, obuf, outsem, to_vmem=False)

    def cache_copies(hbm, row0, to_vmem):
        copies = []
        for k in range(CONV_W - 1):
            h = hbm.at[0, pl.ds(0 if carry else i * nb, nb), k, :]
            v = vbuf.at[pl.ds(row0 + k * nb, nb), :]
            src, dst = (h, v) if to_vmem else (v, h)
            copies.append(pltpu.make_async_copy(src, dst, cachesem.at[int(to_vmem), k]))
        return copies

    @pl.when(i == 0)
    def _():
        for c in in_copies(0, 0):
            c.start()

    if not carry:
        for c in cache_copies(cache_ref, 0, True):
            c.start()
    for c in in_copies(i, slot):
        c.wait()
    if not carry:
        for c in cache_copies(cache_ref, 0, True):
            c.wait()

    @pl.when(i + 1 < n_steps)
    def _():
        for c in in_copies(i + 1, 1 - slot):
            c.start()

    @pl.when(i >= 2)
    def _():
        for c in out_copies(i - 2, slot):
            c.wait()

    if carry:
        @pl.when(i == 0)
        def _():
            vbuf[0:2 * nb, :] = jnp.zeros((2 * nb, W_CONV), F32)
            hstate[...] = jnp.zeros((nb, S_COLS), F32)
    else:
        for n in range(N_HALVES):
            hstate[:, 2 * n * S_HALF:(2 * n + 1) * S_HALF] = re0_ref[n * S_HALF:(n + 1) * S_HALF, :].T
            hstate[:, (2 * n + 1) * S_HALF:(2 * n + 2) * S_HALF] = im0_ref[n * S_HALF:(n + 1) * S_HALF, :].T

    x = xbuf[slot].reshape(tm, D_MODEL)
    xn = _rms(x, nmix_ref[...]).astype(BF16)

    u = _dot(xn, win_ref[:, 3 * W_CONV:])
    u_pairs = u.reshape(n_pair, 2, nb, W_SSM)
    u_e = u_pairs[:, 0].reshape(tmh, W_SSM)
    u_o = u_pairs[:, 1].reshape(tmh, W_SSM)
    ub_e = u_e.astype(BF16)
    ub_o = u_o.astype(BF16)
    for n in range(N_HALVES):
        cols = slice(n * U_HALF, (n + 1) * U_HALF)
        gbuf[:, 2 * n * S_HALF:(2 * n + 2) * S_HALF] = _dot(
            jnp.concatenate([ub_e[:, cols], ub_o[:, cols]], axis=1), bc2_ref[n])

    for src, dst in zip(cast_in, cast_out):
        dst[...] = src[...].astype(BF16)

    def scan_step(h, rows, lr, li, re_cols, im_cols):
        hr, hi = h
        return (lr * hr - li * hi + gbuf[rows, re_cols], lr * hi + li * hr + gbuf[rows, im_cols])

    def store_pair(buf, row0, pair, re_cols, im_cols):
        rows = slice(row0, row0 + BF16_ROWS)
        buf[rows, re_cols] = jnp.concatenate([h[0] for h in pair], axis=0).astype(BF16)
        buf[rows, im_cols] = jnp.concatenate([h[1] for h in pair], axis=0).astype(BF16)

    for n in range(N_HALVES):
        for c0 in range(0, S_HALF, SCAN_LANES):
            re_cols = slice(2 * n * S_HALF + c0, 2 * n * S_HALF + c0 + SCAN_LANES)
            im_cols = slice((2 * n + 1) * S_HALF + c0, (2 * n + 1) * S_HALF + c0 + SCAN_LANES)
            lr = lam2r_ref[n, :, c0:c0 + SCAN_LANES]
            li = lam2i_ref[n, :, c0:c0 + SCAN_LANES]
            step = functools.partial(scan_step, lr=lr, li=li, re_cols=re_cols, im_cols=im_cols)
            if nb == SUBLANES:
                h = (hstate[:, re_cols], hstate[:, im_cols])
                for k in range(0, n_pair, 2):
                    h0 = step(h, slice(k * nb, (k + 1) * nb))
                    h1 = step(h0, slice((k + 1) * nb, (k + 2) * nb))
                    store_pair(hb_odd, k * nb, (h0, h1), re_cols, im_cols)
                    store_pair(hb_prev, k * nb, (h, h0), re_cols, im_cols)
                    h = h1
                hstate[:, re_cols], hstate[:, im_cols] = h
            else:
                for r0 in range(0, nb, BF16_ROWS):
                    seqs = [slice(r0 + j * SUBLANES, r0 + (j + 1) * SUBLANES) for j in range(2)]
                    pair = [(hstate[sq, re_cols], hstate[sq, im_cols]) for sq in seqs]
                    for k in range(n_pair):
                        store_pair(hb_prev, k * nb + r0, pair, re_cols, im_cols)
                        pair = [step(h, slice(k * nb + sq.start, k * nb + sq.stop)) for h, sq in zip(pair, seqs)]
                        store_pair(hb_odd, k * nb + r0, pair, re_cols, im_cols)
                    for h, sq in zip(pair, seqs):
                        hstate[sq, re_cols], hstate[sq, im_cols] = h

    conv_chunks = []
    for c0 in range(0, W_CONV, MXU_COLS):
        cols = slice(c0, c0 + MXU_COLS)
        b_g = _dot(xn, win_ref[:, c0:c0 + MXU_COLS])
        c_g = _dot(xn, win_ref[:, W_CONV + c0:W_CONV + c0 + MXU_COLS])
        xc = _dot(xn, win_ref[:, 2 * W_CONV + c0:2 * W_CONV + c0 + MXU_COLS])
        v = c_g * xc
        vbuf[2 * nb:2 * nb + tm, cols] = v
        conv = (convw_ref[0, :, cols] * vbuf[0:tm, cols] + convw_ref[1, :, cols] * vbuf[nb:nb + tm, cols]
                + convw_ref[2, :, cols] * v)
        conv_chunks.append((b_g * conv).astype(BF16))
    conv_out = jnp.concatenate(conv_chunks, axis=-1)
    if carry:
        vbuf[0:2 * nb, :] = vbuf[tm:tm + 2 * nb, :]
    x1 = x + _dot(conv_out, wout_ref[0:W_CONV, :])

    last = hstate[...]
    for n in range(N_HALVES):
        last_re = last[:, 2 * n * S_HALF:(2 * n + 1) * S_HALF]
        last_im = last[:, (2 * n + 1) * S_HALF:(2 * n + 2) * S_HALF]
        if carry:
            hre_ref[:, n * S_HALF:(n + 1) * S_HALF] = last_re
            him_ref[:, n * S_HALF:(n + 1) * S_HALF] = last_im
        else:
            hre_ref[n * S_HALF:(n + 1) * S_HALF, :] = last_re.T
            him_ref[n * S_HALF:(n + 1) * S_HALF, :] = last_im.T

    zs = []
    for n in range(N_HALVES):
        cols = slice(n * U_HALF, (n + 1) * U_HALF)
        st_cols = slice(2 * n * S_HALF, (2 * n + 2) * S_HALF)
        y_o = _dot(hb_odd[:, st_cols], cc_ref[n]) + dskip_ref[:, cols] * u_o[:, cols]
        y_e = (_dot(hb_prev[:, st_cols], ccl_ref[n]) + _dot(ub_e[:, cols], d0_ref[n])
               + dskip_ref[:, cols] * u_e[:, cols])
        y = jnp.stack([y_e.reshape(n_pair, nb, U_HALF), y_o.reshape(n_pair, nb, U_HALF)],
                      axis=1).reshape(tm, U_HALF)
        zs.append(jax.nn.gelu(y))
    zb = jnp.concatenate([z.astype(BF16) for z in zs], axis=-1)
    ssm_chunks = []
    for n, z in enumerate(zs):
        cols = slice(n * U_HALF, (n + 1) * U_HALF)
        gate = jax.nn.sigmoid(_dot(zb, wglu_ref[:, cols]) + bglu_ref[:, cols])
        ssm_chunks.append((z * gate).astype(BF16))
    ssm_out = jnp.concatenate(ssm_chunks, axis=-1)

    x1 = x1 + _dot(ssm_out, wout_ref[W_CONV:, :])

    obuf[slot] = x1.reshape(tc, nb, D_MODEL)
    for c in out_copies(i, slot):
        c.start()

    if not carry:
        for c in cache_copies(convnew_hbm, tm, False):
            c.start()
        for c in cache_copies(convnew_hbm, tm, False):
            c.wait()

    @pl.when(i == n_steps - 1)
    def _():
        if carry:
            for c in cache_copies(convnew_hbm, tm, False):
                c.start()
            for c in cache_copies(convnew_hbm, tm, False):
                c.wait()
        if n_steps >= 2:
            for c in out_copies(i - 1, 1 - slot):
                c.wait()
        for c in out_copies(i, slot):
            c.wait()


def _resident(shape):
    nd = len(shape)
    return pl.BlockSpec(shape, lambda i: (0,) * nd, pipeline_mode=pl.Buffered(1))


def _mixer(x, wts, state=None, cast=(), *, nb, tc):
    n_seq, n_time, _ = x.shape
    carry = state is None
    if carry:
        assert n_seq == nb == SUBLANES and n_time % tc == 0 and tc % 4 == 0
        n_steps = n_time // tc
    else:
        assert n_time == tc == SUBLANES and n_seq % nb == 0 and nb % BF16_ROWS == 0
        n_steps = n_seq // nb
    tm = nb * tc
    n_st = N_GROUPS * STATE_P
    body = functools.partial(_mixer_body, nb=nb, tc=tc, n_steps=n_steps, carry=carry, n_cast=len(cast))
    any_spec = pl.BlockSpec(memory_space=pl.ANY)
    if carry:
        st_spec = pl.BlockSpec((nb, n_st), lambda i: (0, 0))
        st_shape = jax.ShapeDtypeStruct((n_seq, n_st), F32)
    else:
        st_spec = pl.BlockSpec((n_st, nb), lambda i: (0, i))
        st_shape = jax.ShapeDtypeStruct((n_st, n_seq), F32)
    state_specs = [] if carry else [any_spec, st_spec, st_spec]
    for w in cast:
        assert w.shape[0] % (n_steps * BF16_ROWS) == 0, w.shape
    cast_specs = [pl.BlockSpec((w.shape[0] // n_steps, w.shape[1]), lambda i: (i, 0)) for w in cast]
    return pl.pallas_call(
        body,
        grid=(n_steps,),
        in_specs=[any_spec] + state_specs + [_resident(w.shape) for w in wts] + cast_specs,
        out_specs=[any_spec, any_spec, st_spec, st_spec] + cast_specs,
        out_shape=[jax.ShapeDtypeStruct(x.shape, F32),
                   jax.ShapeDtypeStruct((1, n_seq, CONV_W - 1, W_CONV), F32),
                   st_shape, st_shape]
                  + [jax.ShapeDtypeStruct(w.shape, BF16) for w in cast],
        scratch_shapes=[pltpu.VMEM((2, tc, nb, D_MODEL), F32),
                        pltpu.VMEM((2, tc, nb, D_MODEL), F32),
                        pltpu.VMEM((tm // 2, S_COLS), F32),
                        pltpu.VMEM((tm // 2, S_COLS), BF16),
                        pltpu.VMEM((tm // 2, S_COLS), BF16),
                        pltpu.VMEM((nb, S_COLS), F32),
                        pltpu.VMEM((2 * nb + tm, W_CONV), F32),
                        pltpu.SemaphoreType.DMA((2, SUBLANES)),
                        pltpu.SemaphoreType.DMA((2, SUBLANES)),
                        pltpu.SemaphoreType.DMA((2, CONV_W - 1))],
        compiler_params=pltpu.CompilerParams(dimension_semantics=("arbitrary",),
                                             vmem_limit_bytes=VMEM_LIMIT),
        name="mixer_carry" if carry else "mixer_step",
    )(x, *(() if carry else state), *wts, *cast)


def _ffn_body(xp_hbm, xs_hbm, nffn_ref, wg_ref, wu_ref, wd_ref, nfin_ref, yp_hbm, ys_hbm,
              xbuf, obuf, abuf, insem, outsem, *, n_p, n_s, tm):
    i = pl.program_id(0)
    n_steps = n_p + n_s
    slot = lax.rem(i, 2)

    def tile_copy(step, buf_slot, to_vmem, first):
        hbm = (xp_hbm if first else xs_hbm) if to_vmem else (yp_hbm if first else ys_hbm)
        h = hbm.at[pl.ds((step if first else step - n_p) * tm, tm), :]
        if to_vmem:
            return pltpu.make_async_copy(h, xbuf.at[buf_slot], insem.at[buf_slot])
        return pltpu.make_async_copy(obuf.at[buf_slot], h, outsem.at[buf_slot])

    def start(step, buf_slot, to_vmem):
        pl.when(step < n_p)(lambda: tile_copy(step, buf_slot, to_vmem, True).start())
        pl.when(step >= n_p)(lambda: tile_copy(step, buf_slot, to_vmem, False).start())

    def wait(buf_slot, to_vmem):
        tile_copy(0, buf_slot, to_vmem, True).wait()

    pl.when(i == 0)(lambda: start(0, 0, True))
    wait(slot, True)
    pl.when(i + 1 < n_steps)(lambda: start(i + 1, 1 - slot, True))
    pl.when(i >= 2)(lambda: wait(slot, False))

    x = xbuf[slot]
    hn = _rms(x, nffn_ref[...]).astype(BF16)
    for s, w in FFN_CHUNKS:
        g = _dot(hn, wg_ref[:, s:s + w])
        up = _dot(hn, wu_ref[:, s:s + w])
        abuf[:, s:s + w] = (g * jax.nn.sigmoid(g) * up).astype(BF16)
    acc = x + _dot(abuf[...], wd_ref[...])
    obuf[slot] = _rms(acc, nfin_ref[...])
    start(i, slot, False)

    @pl.when(i == n_steps - 1)
    def _():
        if n_steps >= 2:
            wait(1 - slot, False)
        wait(slot, False)


def _ffn(xp, xs, wts, *, tm):
    n_p, n_s = xp.shape[0] // tm, xs.shape[0] // tm
    assert xp.shape[0] == n_p * tm and xs.shape[0] == n_s * tm
    any_spec = pl.BlockSpec(memory_space=pl.ANY)
    return pl.pallas_call(
        functools.partial(_ffn_body, n_p=n_p, n_s=n_s, tm=tm),
        grid=(n_p + n_s,),
        in_specs=[any_spec, any_spec] + [_resident(w.shape) for w in wts],
        out_specs=[any_spec, any_spec],
        out_shape=[jax.ShapeDtypeStruct(xp.shape, F32), jax.ShapeDtypeStruct(xs.shape, F32)],
        scratch_shapes=[pltpu.VMEM((2, tm, D_MODEL), F32),
                        pltpu.VMEM((2, tm, D_MODEL), F32),
                        pltpu.VMEM((tm, D_FF), BF16),
                        pltpu.SemaphoreType.DMA((2,)),
                        pltpu.SemaphoreType.DMA((2,))],
        compiler_params=pltpu.CompilerParams(dimension_semantics=("arbitrary",),
                                             vmem_limit_bytes=VMEM_LIMIT),
        name="ffn",
    )(xp, xs, *wts)


N_S5_OUT = 6
N_S5_SCRATCH = 4


def _prep_body(lre_ref, lim_ref, logdt_ref, btre_ref, btim_ref, cre_ref, cim_ref, *refs):
    n_cast = (len(refs) - N_S5_OUT - N_S5_SCRATCH) // 2
    cast_in, refs = refs[:n_cast], refs[n_cast:]
    s5_out, refs = refs[:N_S5_OUT], refs[N_S5_OUT:]
    cast_out, s5_scratch = refs[:n_cast], refs[n_cast:]
    for src, dst in zip(cast_in, cast_out):
        dst[...] = src[...].astype(BF16)
    pl.when(pl.program_id(0) == 0)(functools.partial(
        _s5_matrices, lre_ref, lim_ref, logdt_ref, btre_ref, btim_ref, cre_ref, cim_ref, *s5_out, *s5_scratch))


def _s5_matrices(lre_ref, lim_ref, logdt_ref, btre_ref, btim_ref, cre_ref, cim_ref,
                 lam2r_ref, lam2i_ref, bc2_ref, cc_ref, ccl_ref, d0_ref, bscr, lbscr, cscr, clscr):
    lre = lre_ref[...]
    lim = lim_ref[...]
    grp = (lax.broadcasted_iota(jnp.int32, (N_GROUPS, N_GROUPS), 0)
           == lax.broadcasted_iota(jnp.int32, (N_GROUPS, N_GROUPS), 1))
    logdt = jnp.sum(jnp.where(grp, jnp.broadcast_to(logdt_ref[...], (N_GROUPS, N_GROUPS)), 0.0),
                    axis=1, keepdims=True)
    dt = jnp.exp(logdt)
    mag = jnp.exp(lre * dt)
    lbr = mag * jnp.cos(lim * dt)
    lbi = mag * jnp.sin(lim * dt)
    l2r = lbr * lbr - lbi * lbi
    l2i = 2.0 * (lbr * lbi)
    den = lre * lre + lim * lim
    fr = ((lbr - 1.0) * lre + lbi * lim) / den
    fi = (lbi * lre - (lbr - 1.0) * lim) / den
    frb, fib = fr[:, None, :], fi[:, None, :]
    lrb, lib = lbr[:, None, :], lbi[:, None, :]
    bbr = frb * btre_ref[...] - fib * btim_ref[...]
    bbi = frb * btim_ref[...] + fib * btre_ref[...]
    lbbr = lrb * bbr - lib * bbi
    lbbi = lrb * bbi + lib * bbr
    cre = cre_ref[...]
    cim = cim_ref[...]
    clr = cre * lrb - cim * lib
    cli = cre * lib + cim * lrb

    for scr in (bscr, lbscr, cscr, clscr):
        scr[...] = jnp.zeros_like(scr)
    for g in range(N_GROUPS):
        n, k = divmod(g, GROUPS_PER_HALF)
        chans = slice(k * SSM_H, (k + 1) * SSM_H)
        states = slice(k * STATE_P, (k + 1) * STATE_P)
        im_states = slice(S_HALF + k * STATE_P, S_HALF + (k + 1) * STATE_P)
        for scr, re, im in ((bscr, bbr, bbi), (lbscr, lbbr, lbbi), (cscr, cre, -cim), (clscr, clr, -cli)):
            scr[n, chans, states] = re[g]
            scr[n, chans, im_states] = im[g]
        lam2r_ref[n, :, states] = jnp.broadcast_to(l2r[g:g + 1, :], (SUBLANES, STATE_P))
        lam2i_ref[n, :, states] = jnp.broadcast_to(l2i[g:g + 1, :], (SUBLANES, STATE_P))
    for n in range(N_HALVES):
        bc2_ref[n, 0:U_HALF, :] = lbscr[n].astype(BF16)
        bc2_ref[n, U_HALF:, :] = bscr[n].astype(BF16)
        c_t = cscr[n].T
        cc_ref[n] = c_t.astype(BF16)
        ccl_ref[n] = clscr[n].T.astype(BF16)
        b_hi, b_lo = _split_bf16(bscr[n])
        c_hi, c_lo = _split_bf16(c_t)
        d0_ref[n] = (_dot(b_hi, c_hi) + _dot(b_hi, c_lo) + _dot(b_lo, c_hi)).astype(BF16)


PREP_STEPS = 8


def _prep(lam_re, lam_im, log_dt, b_re, b_im, c_re, c_im, cast):
    s5_in = (lam_re, lam_im, log_dt.reshape(1, N_GROUPS), b_re.transpose(0, 2, 1), b_im.transpose(0, 2, 1),
             c_re, c_im)
    s5_shapes = [jax.ShapeDtypeStruct((N_HALVES, SUBLANES, S_HALF), F32),
                 jax.ShapeDtypeStruct((N_HALVES, SUBLANES, S_HALF), F32),
                 jax.ShapeDtypeStruct((N_HALVES, 2 * U_HALF, 2 * S_HALF), BF16),
                 jax.ShapeDtypeStruct((N_HALVES, 2 * S_HALF, U_HALF), BF16),
                 jax.ShapeDtypeStruct((N_HALVES, 2 * S_HALF, U_HALF), BF16),
                 jax.ShapeDtypeStruct((N_HALVES, U_HALF, U_HALF), BF16)]
    assert len(s5_shapes) == N_S5_OUT
    for w in cast:
        assert w.shape[0] % (PREP_STEPS * BF16_ROWS) == 0, w.shape
    cast_specs = [pl.BlockSpec((w.shape[0] // PREP_STEPS, w.shape[1]), lambda i: (i, 0)) for w in cast]
    whole = lambda a: pl.BlockSpec(a.shape, lambda i: (0,) * len(a.shape))
    return pl.pallas_call(
        _prep_body,
        grid=(PREP_STEPS,),
        in_specs=[whole(a) for a in s5_in] + cast_specs,
        out_specs=[whole(a) for a in s5_shapes] + cast_specs,
        out_shape=s5_shapes + [jax.ShapeDtypeStruct(w.shape, BF16) for w in cast],
        scratch_shapes=[pltpu.VMEM((N_HALVES, U_HALF, 2 * S_HALF), F32)] * N_S5_SCRATCH,
        compiler_params=pltpu.CompilerParams(dimension_semantics=("arbitrary",)),
        name="prep",
    )(*s5_in, *cast)


def kernel(x_prompt, x_sample, cache_conv, state_ssm_re, state_ssm_im, norm_mix, w_in, conv_w, ssm_lam_re, ssm_lam_im, ssm_log_dt, ssm_b_re, ssm_b_im, ssm_c_re, ssm_c_im, ssm_d, w_glu, b_glu, w_out, norm_ffn, w_gate, w_up, w_down, norm_final):
    assert norm_mix.shape[0] == 1, "single-layer trunk"
    n_st = N_GROUPS * STATE_P
    lam2r, lam2i, bc2, cc, ccl, d0, w_in_b, w_glu_b, w_out_b = _prep(
        ssm_lam_re[0], ssm_lam_im[0], ssm_log_dt[0], ssm_b_re[0], ssm_b_im[0], ssm_c_re[0], ssm_c_im[0],
        cast=(w_in[0], w_glu[0], w_out[0]))
    mix_w = (norm_mix, w_in_b, conv_w[0][:, None, :], lam2r, lam2i, bc2, cc, ccl, d0, ssm_d, w_glu_b, b_glu,
             w_out_b)

    bp, lp, _ = x_prompt.shape
    bs, ls, _ = x_sample.shape

    x1p, convp, rep, imp, wg, wu, wd = _mixer(x_prompt, mix_w, cast=(w_gate[0], w_up[0], w_down[0]),
                                              nb=SUBLANES, tc=128)
    ffn_w = (norm_ffn, wg, wu, wd, norm_final.reshape(1, D_MODEL))
    def state_major(h):
        return h[0].transpose(1, 2, 0).reshape(n_st, h.shape[1])

    sample_state = (cache_conv, state_major(state_ssm_re), state_major(state_ssm_im))
    x1s, convs, res, ims = _mixer(x_sample, mix_w, sample_state, nb=128, tc=SUBLANES)

    yp, ys = _ffn(x1p.reshape(bp * lp, D_MODEL), x1s.reshape(bs * ls, D_MODEL), ffn_w, tm=1024)
    yp = yp.reshape(bp, lp, D_MODEL)
    ys = ys.reshape(bs, ls, D_MODEL)

    def seq_major(h):
        return h.reshape(N_GROUPS, STATE_P, h.shape[1]).transpose(2, 0, 1)[None]

    return (yp, ys, convp, rep.reshape(1, bp, N_GROUPS, STATE_P), imp.reshape(1, bp, N_GROUPS, STATE_P),
            convs, seq_major(res), seq_major(ims))
```

```python
import functools

import jax
import jax.numpy as jnp
from jax import lax
from jax.experimental import pallas as pl
from jax.experimental.pallas import tpu as pltpu

D_MODEL = 1024
W_CONV = 512
W_SSM = 512
CONV_W = 3
N_GROUPS = 32
SSM_H = 16
STATE_P = 64
D_FF = 2816
EPS = 1e-5

SUBLANES = 8
BF16_ROWS = 16
GROUPS_PER_HALF = 16
N_HALVES = N_GROUPS // GROUPS_PER_HALF
U_HALF = GROUPS_PER_HALF * SSM_H
S_HALF = GROUPS_PER_HALF * STATE_P
S_COLS = 2 * N_GROUPS * STATE_P
SCAN_LANES = 512
MXU_COLS = 256
FFN_CHUNKS = tuple((s, 256) for s in range(0, D_FF, 256))
VMEM_LIMIT = 60 * 1024 * 1024

BF16 = jnp.bfloat16
F32 = jnp.float32


def _rms(x, g):
    y = x * lax.rsqrt(jnp.mean(x * x, axis=-1, keepdims=True) + EPS)
    return y * g


def _dot(a, b):
    return jnp.dot(a, b, preferred_element_type=F32)


def _split_bf16(a):
    hi = a.astype(BF16)
    return hi, (a - hi.astype(F32)).astype(BF16)


def _tile_copies(hbm, buf, sem, step, slot, *, nb, tc, per_seq, to_vmem):
    copies = []
    for k in range(SUBLANES):
        if per_seq:
            h = hbm.at[k, pl.ds(step * tc, tc), :]
            v = buf.at[slot, :, k, :]
        else:
            h = hbm.at[pl.ds(step * nb, nb), k, :]
            v = buf.at[slot, k]
        src, dst = (h, v) if to_vmem else (v, h)
        copies.append(pltpu.make_async_copy(src, dst, sem.at[slot, k]))
    return copies


def _mixer_body(x_hbm, *refs, nb, tc, n_steps, carry, n_cast):
    if carry:
        cache_ref = re0_ref = im0_ref = None
    else:
        cache_ref, re0_ref, im0_ref, *refs = refs
    (nmix_ref, win_ref, convw_ref, lam2r_ref, lam2i_ref, bc2_ref, cc_ref, ccl_ref, d0_ref, dskip_ref,
     wglu_ref, bglu_ref, wout_ref, *refs) = refs
    cast_in, refs = refs[:n_cast], refs[n_cast:]
    x1_hbm, convnew_hbm, hre_ref, him_ref, *refs = refs
    cast_out, refs = refs[:n_cast], refs[n_cast:]
    xbuf, obuf, gbuf, hb_odd, hb_prev, hstate, vbuf, insem, outsem, cachesem = refs
    tm = nb * tc
    n_pair = tc // 2
    tmh = n_pair * nb

    i = pl.program_id(0)
    slot = lax.rem(i, 2)
    cp = functools.partial(_tile_copies, nb=nb, tc=tc, per_seq=carry)
    in_copies = functools.partial(cp, x_hbm, xbuf, insem, to_vmem=True)
    out_copies = functools.partial(cp, x1_hbm, obuf, outsem, to_vmem=False)

    def cache_copies(hbm, row0, to_vmem):
        copies = []
        for k in range(CONV_W - 1):
            h = hbm.at[0, pl.ds(0 if carry else i * nb, nb), k, :]
            v = vbuf.at[pl.ds(row0 + k * nb, nb), :]
            src, dst = (h, v) if to_vmem else (v, h)
            copies.append(pltpu.make_async_copy(src, dst, cachesem.at[int(to_vmem), k]))
        return copies

    @pl.when(i == 0)
    def _():
        for c in in_copies(0, 0):
            c.start()

    if not carry:
        for c in cache_copies(cache_ref, 0, True):
            c.start()
    for c in in_copies(i, slot):
        c.wait()
    if not carry:
        for c in cache_copies(cache_ref, 0, True):
            c.wait()

    @pl.when(i + 1 < n_steps)
    def _():
        for c in in_copies(i + 1, 1 - slot):
            c.start()

    @pl.when(i >= 2)
    def _():
        for c in out_copies(i - 2, slot):
            c.wait()

    if carry:
        @pl.when(i == 0)
        def _():
            vbuf[0:2 * nb, :] = jnp.zeros((2 * nb, W_CONV), F32)
            hstate[...] = jnp.zeros((nb, S_COLS), F32)
    else:
        for n in range(N_HALVES):
            hstate[:, 2 * n * S_HALF:(2 * n + 1) * S_HALF] = re0_ref[n * S_HALF:(n + 1) * S_HALF, :].T
            hstate[:, (2 * n + 1) * S_HALF:(2 * n + 2) * S_HALF] = im0_ref[n * S_HALF:(n + 1) * S_HALF, :].T

    x = xbuf[slot].reshape(tm, D_MODEL)
    xn = _rms(x, nmix_ref[...]).astype(BF16)

    u = _dot(xn, win_ref[:, 3 * W_CONV:])
    u_pairs = u.reshape(n_pair, 2, nb, W_SSM)
    u_e = u_pairs[:, 0].reshape(tmh, W_SSM)
    u_o = u_pairs[:, 1].reshape(tmh, W_SSM)
    ub_e = u_e.astype(BF16)
    ub_o = u_o.astype(BF16)
    for n in range(N_HALVES):
        cols = slice(n * U_HALF, (n + 1) * U_HALF)
        gbuf[:, 2 * n * S_HALF:(2 * n + 2) * S_HALF] = _dot(
            jnp.concatenate([ub_e[:, cols], ub_o[:, cols]], axis=1), bc2_ref[n])

    for src, dst in zip(cast_in, cast_out):
        dst[...] = src[...].astype(BF16)

    def scan_step(h, rows, lr, li, re_cols, im_cols):
        hr, hi = h
        return (lr * hr - li * hi + gbuf[rows, re_cols], lr * hi + li * hr + gbuf[rows, im_cols])

    def store_pair(buf, row0, pair, re_cols, im_cols):
        rows = slice(row0, row0 + BF16_ROWS)
        buf[rows, re_cols] = jnp.concatenate([h[0] for h in pair], axis=0).astype(BF16)
        buf[rows, im_cols] = jnp.concatenate([h[1] for h in pair], axis=0).astype(BF16)

    for n in range(N_HALVES):
        for c0 in range(0, S_HALF, SCAN_LANES):
            re_cols = slice(2 * n * S_HALF + c0, 2 * n * S_HALF + c0 + SCAN_LANES)
            im_cols = slice((2 * n + 1) * S_HALF + c0, (2 * n + 1) * S_HALF + c0 + SCAN_LANES)
            lr = lam2r_ref[n, :, c0:c0 + SCAN_LANES]
            li = lam2i_ref[n, :, c0:c0 + SCAN_LANES]
            step = functools.partial(scan_step, lr=lr, li=li, re_cols=re_cols, im_cols=im_cols)
            if nb == SUBLANES:
                h = (hstate[:, re_cols], hstate[:, im_cols])
                for k in range(0, n_pair, 2):
                    h0 = step(h, slice(k * nb, (k + 1) * nb))
                    h1 = step(h0, slice((k + 1) * nb, (k + 2) * nb))
                    store_pair(hb_odd, k * nb, (h0, h1), re_cols, im_cols)
                    store_pair(hb_prev, k * nb, (h, h0), re_cols, im_cols)
                    h = h1
                hstate[:, re_cols], hstate[:, im_cols] = h
            else:
                for r0 in range(0, nb, BF16_ROWS):
                    seqs = [slice(r0 + j * SUBLANES, r0 + (j + 1) * SUBLANES) for j in range(2)]
                    pair = [(hstate[sq, re_cols], hstate[sq, im_cols]) for sq in seqs]
                    for k in range(n_pair):
                        store_pair(hb_prev, k * nb + r0, pair, re_cols, im_cols)
                        pair = [step(h, slice(k * nb + sq.start, k * nb + sq.stop)) for h, sq in zip(pair, seqs)]
                        store_pair(hb_odd, k * nb + r0, pair, re_cols, im_cols)
                    for h, sq in zip(pair, seqs):
                        hstate[sq, re_cols], hstate[sq, im_cols] = h

    conv_chunks = []
    for c0 in range(0, W_CONV, MXU_COLS):
        cols = slice(c0, c0 + MXU_COLS)
        b_g = _dot(xn, win_ref[:, c0:c0 + MXU_COLS])
        c_g = _dot(xn, win_ref[:, W_CONV + c0:W_CONV + c0 + MXU_COLS])
        xc = _dot(xn, win_ref[:, 2 * W_CONV + c0:2 * W_CONV + c0 + MXU_COLS])
        v = c_g * xc
        vbuf[2 * nb:2 * nb + tm, cols] = v
        conv = (convw_ref[0, :, cols] * vbuf[0:tm, cols] + convw_ref[1, :, cols] * vbuf[nb:nb + tm, cols]
                + convw_ref[2, :, cols] * v)
        conv_chunks.append((b_g * conv).astype(BF16))
    conv_out = jnp.concatenate(conv_chunks, axis=-1)
    if carry:
        vbuf[0:2 * nb, :] = vbuf[tm:tm + 2 * nb, :]
    x1 = x + _dot(conv_out, wout_ref[0:W_CONV, :])

    last = hstate[...]
    for n in range(N_HALVES):
        last_re = last[:, 2 * n * S_HALF:(2 * n + 1) * S_HALF]
        last_im = last[:, (2 * n + 1) * S_HALF:(2 * n + 2) * S_HALF]
        if carry:
            hre_ref[:, n * S_HALF:(n + 1) * S_HALF] = last_re
            him_ref[:, n * S_HALF:(n + 1) * S_HALF] = last_im
        else:
            hre_ref[n * S_HALF:(n + 1) * S_HALF, :] = last_re.T
            him_ref[n * S_HALF:(n + 1) * S_HALF, :] = last_im.T

    zs = []
    for n in range(N_HALVES):
        cols = slice(n * U_HALF, (n + 1) * U_HALF)
        st_cols = slice(2 * n * S_HALF, (2 * n + 2) * S_HALF)
        y_o = _dot(hb_odd[:, st_cols], cc_ref[n]) + dskip_ref[:, cols] * u_o[:, cols]
        y_e = (_dot(hb_prev[:, st_cols], ccl_ref[n]) + _dot(ub_e[:, cols], d0_ref[n])
               + dskip_ref[:, cols] * u_e[:, cols])
        y = jnp.stack([y_e.reshape(n_pair, nb, U_HALF), y_o.reshape(n_pair, nb, U_HALF)],
                      axis=1).reshape(tm, U_HALF)
        zs.append(jax.nn.gelu(y))
    zb = jnp.concatenate([z.astype(BF16) for z in zs], axis=-1)
    ssm_chunks = []
    for n, z in enumerate(zs):
        cols = slice(n * U_HALF, (n + 1) * U_HALF)
        gate = jax.nn.sigmoid(_dot(zb, wglu_ref[:, cols]) + bglu_ref[:, cols])
        ssm_chunks.append((z * gate).astype(BF16))
    ssm_out = jnp.concatenate(ssm_chunks, axis=-1)

    x1 = x1 + _dot(ssm_out, wout_ref[W_CONV:, :])

    obuf[slot] = x1.reshape(tc, nb, D_MODEL)
    for c in out_copies(i, slot):
        c.start()

    if not carry:
        for c in cache_copies(convnew_hbm, tm, False):
            c.start()
        for c in cache_copies(convnew_hbm, tm, False):
            c.wait()

    @pl.when(i == n_steps - 1)
    def _():
        if carry:
            for c in cache_copies(convnew_hbm, tm, False):
                c.start()
            for c in cache_copies(convnew_hbm, tm, False):
                c.wait()
        if n_steps >= 2:
            for c in out_copies(i - 1, 1 - slot):
                c.wait()
        for c in out_copies(i, slot):
            c.wait()


def _resident(shape):
    nd = len(shape)
    return pl.BlockSpec(shape, lambda i: (0,) * nd, pipeline_mode=pl.Buffered(1))


def _mixer(x, wts, state=None, cast=(), *, nb, tc):
    n_seq, n_time, _ = x.shape
    carry = state is None
    if carry:
        assert n_seq == nb == SUBLANES and n_time % tc == 0 and tc % 4 == 0
        n_steps = n_time // tc
    else:
        assert n_time == tc == SUBLANES and n_seq % nb == 0 and nb % BF16_ROWS == 0
        n_steps = n_seq // nb
    tm = nb * tc
    n_st = N_GROUPS * STATE_P
    body = functools.partial(_mixer_body, nb=nb, tc=tc, n_steps=n_steps, carry=carry, n_cast=len(cast))
    any_spec = pl.BlockSpec(memory_space=pl.ANY)
    if carry:
        st_spec = pl.BlockSpec((nb, n_st), lambda i: (0, 0))
        st_shape = jax.ShapeDtypeStruct((n_seq, n_st), F32)
    else:
        st_spec = pl.BlockSpec((n_st, nb), lambda i: (0, i))
        st_shape = jax.ShapeDtypeStruct((n_st, n_seq), F32)
    state_specs = [] if carry else [any_spec, st_spec, st_spec]
    for w in cast:
        assert w.shape[0] % (n_steps * BF16_ROWS) == 0, w.shape
    cast_specs = [pl.BlockSpec((w.shape[0] // n_steps, w.shape[1]), lambda i: (i, 0)) for w in cast]
    return pl.pallas_call(
        body,
        grid=(n_steps,),
        in_specs=[any_spec] + state_specs + [_resident(w.shape) for w in wts] + cast_specs,
        out_specs=[any_spec, any_spec, st_spec, st_spec] + cast_specs,
        out_shape=[jax.ShapeDtypeStruct(x.shape, F32),
                   jax.ShapeDtypeStruct((1, n_seq, CONV_W - 1, W_CONV), F32),
                   st_shape, st_shape]
                  + [jax.ShapeDtypeStruct(w.shape, BF16) for w in cast],
        scratch_shapes=[pltpu.VMEM((2, tc, nb, D_MODEL), F32),
                        pltpu.VMEM((2, tc, nb, D_MODEL), F32),
                        pltpu.VMEM((tm // 2, S_COLS), F32),
                        pltpu.VMEM((tm // 2, S_COLS), BF16),
                        pltpu.VMEM((tm // 2, S_COLS), BF16),
                        pltpu.VMEM((nb, S_COLS), F32),
                        pltpu.VMEM((2 * nb + tm, W_CONV), F32),
                        pltpu.SemaphoreType.DMA((2, SUBLANES)),
                        pltpu.SemaphoreType.DMA((2, SUBLANES)),
                        pltpu.SemaphoreType.DMA((2, CONV_W - 1))],
        compiler_params=pltpu.CompilerParams(dimension_semantics=("arbitrary",),
                                             vmem_limit_bytes=VMEM_LIMIT),
        name="mixer_carry" if carry else "mixer_step",
    )(x, *(() if carry else state), *wts, *cast)


def _ffn_body(xp_hbm, xs_hbm, nffn_ref, wg_ref, wu_ref, wd_ref, nfin_ref, yp_hbm, ys_hbm,
              xbuf, obuf, abuf, insem, outsem, *, n_p, n_s, tm):
    i = pl.program_id(0)
    n_steps = n_p + n_s
    slot = lax.rem(i, 2)

    def tile_copy(step, buf_slot, to_vmem, first):
        hbm = (xp_hbm if first else xs_hbm) if to_vmem else (yp_hbm if first else ys_hbm)
        h = hbm.at[pl.ds((step if first else step - n_p) * tm, tm), :]
        if to_vmem:
            return pltpu.make_async_copy(h, xbuf.at[buf_slot], insem.at[buf_slot])
        return pltpu.make_async_copy(obuf.at[buf_slot], h, outsem.at[buf_slot])

    def start(step, buf_slot, to_vmem):
        pl.when(step < n_p)(lambda: tile_copy(step, buf_slot, to_vmem, True).start())
        pl.when(step >= n_p)(lambda: tile_copy(step, buf_slot, to_vmem, False).start())

    def wait(buf_slot, to_vmem):
        tile_copy(0, buf_slot, to_vmem, True).wait()

    pl.when(i == 0)(lambda: start(0, 0, True))
    wait(slot, True)
    pl.when(i + 1 < n_steps)(lambda: start(i + 1, 1 - slot, True))
    pl.when(i >= 2)(lambda: wait(slot, False))

    x = xbuf[slot]
    hn = _rms(x, nffn_ref[...]).astype(BF16)
    for s, w in FFN_CHUNKS:
        g = _dot(hn, wg_ref[:, s:s + w])
        up = _dot(hn, wu_ref[:, s:s + w])
        abuf[:, s:s + w] = (g * jax.nn.sigmoid(g) * up).astype(BF16)
    acc = x + _dot(abuf[...], wd_ref[...])
    obuf[slot] = _rms(acc, nfin_ref[...])
    start(i, slot, False)

    @pl.when(i == n_steps - 1)
    def _():
        if n_steps >= 2:
            wait(1 - slot, False)
        wait(slot, False)


def _ffn(xp, xs, wts, *, tm):
    n_p, n_s = xp.shape[0] // tm, xs.shape[0] // tm
    assert xp.shape[0] == n_p * tm and xs.shape[0] == n_s * tm
    any_spec = pl.BlockSpec(memory_space=pl.ANY)
    return pl.pallas_call(
        functools.partial(_ffn_body, n_p=n_p, n_s=n_s, tm=tm),
        grid=(n_p + n_s,),
        in_specs=[any_spec, any_spec] + [_resident(w.shape) for w in wts],
        out_specs=[any_spec, any_spec],
        out_shape=[jax.ShapeDtypeStruct(xp.shape, F32), jax.ShapeDtypeStruct(xs.shape, F32)],
        scratch_shapes=[pltpu.VMEM((2, tm, D_MODEL), F32),
                        pltpu.VMEM((2, tm, D_MODEL), F32),
                        pltpu.VMEM((tm, D_FF), BF16),
                        pltpu.SemaphoreType.DMA((2,)),
                        pltpu.SemaphoreType.DMA((2,))],
        compiler_params=pltpu.CompilerParams(dimension_semantics=("arbitrary",),
                                             vmem_limit_bytes=VMEM_LIMIT),
        name="ffn",
    )(xp, xs, *wts)


N_S5_OUT = 6
N_S5_SCRATCH = 4


def _prep_body(lre_ref, lim_ref, logdt_ref, btre_ref, btim_ref, cre_ref, cim_ref, *refs):
    n_cast = (len(refs) - 2 * N_S5_OUT - N_S5_SCRATCH - 1) // 2
    cast_in, refs = refs[:n_cast], refs[n_cast:]
    s5_hbm, refs = refs[:N_S5_OUT], refs[N_S5_OUT:]
    cast_out, refs = refs[:n_cast], refs[n_cast:]
    s5_vmem, refs = refs[:N_S5_OUT], refs[N_S5_OUT:]
    s5_scratch, sem = refs[:N_S5_SCRATCH], refs[N_S5_SCRATCH]
    i = pl.program_id(0)
    writes = [pltpu.make_async_copy(v, h, sem.at[j]) for j, (v, h) in enumerate(zip(s5_vmem, s5_hbm))]

    @pl.when(i == 0)
    def _():
        _s5_matrices(lre_ref, lim_ref, logdt_ref, btre_ref, btim_ref, cre_ref, cim_ref, *s5_vmem, *s5_scratch)
        for c in writes:
            c.start()

    for src, dst in zip(cast_in, cast_out):
        dst[...] = src[...].astype(BF16)

    @pl.when(i == PREP_STEPS - 1)
    def _():
        for c in writes:
            c.wait()


def _s5_matrices(lre_ref, lim_ref, logdt_ref, btre_ref, btim_ref, cre_ref, cim_ref,
                 lam2r_ref, lam2i_ref, bc2_ref, cc_ref, ccl_ref, d0_ref, bscr, lbscr, cscr, clscr):
    lre = lre_ref[...]
    lim = lim_ref[...]
    grp = (lax.broadcasted_iota(jnp.int32, (N_GROUPS, N_GROUPS), 0)
           == lax.broadcasted_iota(jnp.int32, (N_GROUPS, N_GROUPS), 1))
    logdt = jnp.sum(jnp.where(grp, jnp.broadcast_to(logdt_ref[...], (N_GROUPS, N_GROUPS)), 0.0),
                    axis=1, keepdims=True)
    dt = jnp.exp(logdt)
    mag = jnp.exp(lre * dt)
    lbr = mag * jnp.cos(lim * dt)
    lbi = mag * jnp.sin(lim * dt)
    l2r = lbr * lbr - lbi * lbi
    l2i = 2.0 * (lbr * lbi)
    den = lre * lre + lim * lim
    fr = ((lbr - 1.0) * lre + lbi * lim) / den
    fi = (lbi * lre - (lbr - 1.0) * lim) / den
    frb, fib = fr[:, None, :], fi[:, None, :]
    lrb, lib = lbr[:, None, :], lbi[:, None, :]
    bbr = frb * btre_ref[...] - fib * btim_ref[...]
    bbi = frb * btim_ref[...] + fib * btre_ref[...]
    lbbr = lrb * bbr - lib * bbi
    lbbi = lrb * bbi + lib * bbr
    cre = cre_ref[...]
    cim = cim_ref[...]
    clr = cre * lrb - cim * lib
    cli = cre * lib + cim * lrb

    for scr in (bscr, lbscr, cscr, clscr):
        scr[...] = jnp.zeros_like(scr)
    for g in range(N_GROUPS):
        n, k = divmod(g, GROUPS_PER_HALF)
        chans = slice(k * SSM_H, (k + 1) * SSM_H)
        states = slice(k * STATE_P, (k + 1) * STATE_P)
        im_states = slice(S_HALF + k * STATE_P, S_HALF + (k + 1) * STATE_P)
        for scr, re, im in ((bscr, bbr, bbi), (lbscr, lbbr, lbbi), (cscr, cre, -cim), (clscr, clr, -cli)):
            scr[n, chans, states] = re[g]
            scr[n, chans, im_states] = im[g]
        lam2r_ref[n, :, states] = jnp.broadcast_to(l2r[g:g + 1, :], (SUBLANES, STATE_P))
        lam2i_ref[n, :, states] = jnp.broadcast_to(l2i[g:g + 1, :], (SUBLANES, STATE_P))
    for n in range(N_HALVES):
        bc2_ref[n, 0:U_HALF, :] = lbscr[n].astype(BF16)
        bc2_ref[n, U_HALF:, :] = bscr[n].astype(BF16)
        c_t = cscr[n].T
        cc_ref[n] = c_t.astype(BF16)
        ccl_ref[n] = clscr[n].T.astype(BF16)
        b_hi, b_lo = _split_bf16(bscr[n])
        c_hi, c_lo = _split_bf16(c_t)
        d0_ref[n] = (_dot(b_hi, c_hi) + _dot(b_hi, c_lo) + _dot(b_lo, c_hi)).astype(BF16)


PREP_STEPS = 8


def _prep(lam_re, lam_im, log_dt, b_re, b_im, c_re, c_im, cast):
    s5_in = (lam_re, lam_im, log_dt.reshape(1, N_GROUPS), b_re.transpose(0, 2, 1), b_im.transpose(0, 2, 1),
             c_re, c_im)
    s5_shapes = [jax.ShapeDtypeStruct((N_HALVES, SUBLANES, S_HALF), F32),
                 jax.ShapeDtypeStruct((N_HALVES, SUBLANES, S_HALF), F32),
                 jax.ShapeDtypeStruct((N_HALVES, 2 * U_HALF, 2 * S_HALF), BF16),
                 jax.ShapeDtypeStruct((N_HALVES, 2 * S_HALF, U_HALF), BF16),
                 jax.ShapeDtypeStruct((N_HALVES, 2 * S_HALF, U_HALF), BF16),
                 jax.ShapeDtypeStruct((N_HALVES, U_HALF, U_HALF), BF16)]
    assert len(s5_shapes) == N_S5_OUT
    for w in cast:
        assert w.shape[0] % (PREP_STEPS * BF16_ROWS) == 0, w.shape
    cast_specs = [pl.BlockSpec((w.shape[0] // PREP_STEPS, w.shape[1]), lambda i: (i, 0)) for w in cast]
    whole = lambda a: pl.BlockSpec(a.shape, lambda i: (0,) * len(a.shape))
    return pl.pallas_call(
        _prep_body,
        grid=(PREP_STEPS,),
        in_specs=[whole(a) for a in s5_in] + cast_specs,
        out_specs=[pl.BlockSpec(memory_space=pl.ANY)] * N_S5_OUT + cast_specs,
        out_shape=s5_shapes + [jax.ShapeDtypeStruct(w.shape, BF16) for w in cast],
        scratch_shapes=[pltpu.VMEM(a.shape, a.dtype) for a in s5_shapes]
                       + [pltpu.VMEM((N_HALVES, U_HALF, 2 * S_HALF), F32)] * N_S5_SCRATCH
                       + [pltpu.SemaphoreType.DMA((N_S5_OUT,))],
        compiler_params=pltpu.CompilerParams(dimension_semantics=("arbitrary",)),
        name="prep",
    )(*s5_in, *cast)


def kernel(x_prompt, x_sample, cache_conv, state_ssm_re, state_ssm_im, norm_mix, w_in, conv_w, ssm_lam_re, ssm_lam_im, ssm_log_dt, ssm_b_re, ssm_b_im, ssm_c_re, ssm_c_im, ssm_d, w_glu, b_glu, w_out, norm_ffn, w_gate, w_up, w_down, norm_final):
    assert norm_mix.shape[0] == 1, "single-layer trunk"
    n_st = N_GROUPS * STATE_P
    lam2r, lam2i, bc2, cc, ccl, d0, w_in_b, w_glu_b, w_out_b = _prep(
        ssm_lam_re[0], ssm_lam_im[0], ssm_log_dt[0], ssm_b_re[0], ssm_b_im[0], ssm_c_re[0], ssm_c_im[0],
        cast=(w_in[0], w_glu[0], w_out[0]))
    mix_w = (norm_mix, w_in_b, conv_w[0][:, None, :], lam2r, lam2i, bc2, cc, ccl, d0, ssm_d, w_glu_b, b_glu,
             w_out_b)

    bp, lp, _ = x_prompt.shape
    bs, ls, _ = x_sample.shape

    x1p, convp, rep, imp, wg, wu, wd = _mixer(x_prompt, mix_w, cast=(w_gate[0], w_up[0], w_down[0]),
                                              nb=SUBLANES, tc=128)
    ffn_w = (norm_ffn, wg, wu, wd, norm_final.reshape(1, D_MODEL))
    def state_major(h):
        return h[0].transpose(1, 2, 0).reshape(n_st, h.shape[1])

    sample_state = (cache_conv, state_major(state_ssm_re), state_major(state_ssm_im))
    x1s, convs, res, ims = _mixer(x_sample, mix_w, sample_state, nb=128, tc=SUBLANES)

    yp, ys = _ffn(x1p.reshape(bp * lp, D_MODEL), x1s.reshape(bs * ls, D_MODEL), ffn_w, tm=1024)
    yp = yp.reshape(bp, lp, D_MODEL)
    ys = ys.reshape(bs, ls, D_MODEL)

    def seq_major(h):
        return h.reshape(N_GROUPS, STATE_P, h.shape[1]).transpose(2, 0, 1)[None]

    return (yp, ys, convp, rep.reshape(1, bp, N_GROUPS, STATE_P), imp.reshape(1, bp, N_GROUPS, STATE_P),
            convs, seq_major(res), seq_major(ims))
```

```python
import functools

import jax
import jax.numpy as jnp
from jax import lax
from jax.experimental import pallas as pl
from jax.experimental.pallas import tpu as pltpu

D_MODEL = 1024
W_CONV = 512
W_SSM = 512
CONV_W = 3
N_GROUPS = 32
SSM_H = 16
STATE_P = 64
D_FF = 2816
EPS = 1e-5

SUBLANES = 8
BF16_ROWS = 16
GROUPS_PER_HALF = 16
N_HALVES = N_GROUPS // GROUPS_PER_HALF
U_HALF = GROUPS_PER_HALF * SSM_H
S_HALF = GROUPS_PER_HALF * STATE_P
S_COLS = 2 * N_GROUPS * STATE_P
SCAN_LANES = 512
MXU_COLS = 256
FFN_CHUNKS = tuple((s, MXU_COLS) for s in range(0, D_FF, MXU_COLS))
VMEM_LIMIT = 60 * 1024 * 1024

BF16 = jnp.bfloat16
F32 = jnp.float32


def _rms(x, g):
    y = x * lax.rsqrt(jnp.mean(x * x, axis=-1, keepdims=True) + EPS)
    return y * g


def _dot(a, b):
    return jnp.dot(a, b, preferred_element_type=F32)


def _split_bf16(a):
    hi = a.astype(BF16)
    return hi, (a - hi.astype(F32)).astype(BF16)


def _tile_copies(hbm, buf, sem, step, slot, *, nb, tc, per_seq, to_vmem):
    copies = []
    for k in range(SUBLANES):
        if per_seq:
            h = hbm.at[k, pl.ds(step * tc, tc), :]
            v = buf.at[slot, :, k, :]
        else:
            h = hbm.at[pl.ds(step * nb, nb), k, :]
            v = buf.at[slot, k]
        src, dst = (h, v) if to_vmem else (v, h)
        copies.append(pltpu.make_async_copy(src, dst, sem.at[slot, k]))
    return copies


def _mixer_body(x_hbm, *refs, nb, tc, n_steps, carry, n_cast):
    if carry:
        cache_ref = re0_ref = im0_ref = None
    else:
        cache_ref, re0_ref, im0_ref, *refs = refs
    (nmix_ref, win_ref, convw_ref, lam2r_ref, lam2i_ref, bc2_ref, cc_ref, ccl_ref, d0_ref, dskip_ref,
     wglu_ref, bglu_ref, wout_ref, *refs) = refs
    cast_in, refs = refs[:n_cast], refs[n_cast:]
    x1_hbm, convnew_hbm, hre_ref, him_ref, *refs = refs
    cast_out, refs = refs[:n_cast], refs[n_cast:]
    xbuf, obuf, gbuf, hb_odd, hb_prev, hstate, vbuf, insem, outsem, cachesem = refs
    tm = nb * tc
    n_pair = tc // 2
    tmh = n_pair * nb

    i = pl.program_id(0)
    slot = lax.rem(i, 2)
    cp = functools.partial(_tile_copies, nb=nb, tc=tc, per_seq=carry)
    in_copies = functools.partial(cp, x_hbm, xbuf, insem, to_vmem=True)
    out_copies = functools.partial(cp, x1_hbm, obuf, outsem, to_vmem=False)

    def cache_copies(hbm, row0, to_vmem):
        copies = []
        for k in range(CONV_W - 1):
            h = hbm.at[0, pl.ds(0 if carry else i * nb, nb), k, :]
            v = vbuf.at[pl.ds(row0 + k * nb, nb), :]
            src, dst = (h, v) if to_vmem else (v, h)
            copies.append(pltpu.make_async_copy(src, dst, cachesem.at[int(to_vmem), k]))
        return copies

    @pl.when(i == 0)
    def _():
        for c in in_copies(0, 0):
            c.start()

    if not carry:
        for c in cache_copies(cache_ref, 0, True):
            c.start()
    for c in in_copies(i, slot):
        c.wait()
    if not carry:
        for c in cache_copies(cache_ref, 0, True):
            c.wait()

    @pl.when(i + 1 < n_steps)
    def _():
        for c in in_copies(i + 1, 1 - slot):
            c.start()

    @pl.when(i >= 2)
    def _():
        for c in out_copies(i - 2, slot):
            c.wait()

    if carry:
        @pl.when(i == 0)
        def _():
            vbuf[0:2 * nb, :] = jnp.zeros((2 * nb, W_CONV), F32)
            hstate[...] = jnp.zeros((nb, S_COLS), F32)
    else:
        for n in range(N_HALVES):
            hstate[:, 2 * n * S_HALF:(2 * n + 1) * S_HALF] = re0_ref[n * S_HALF:(n + 1) * S_HALF, :].T
            hstate[:, (2 * n + 1) * S_HALF:(2 * n + 2) * S_HALF] = im0_ref[n * S_HALF:(n + 1) * S_HALF, :].T

    x = xbuf[slot].reshape(tm, D_MODEL)
    xn = _rms(x, nmix_ref[...]).astype(BF16)

    u = _dot(xn, win_ref[:, 3 * W_CONV:])
    u_pairs = u.reshape(n_pair, 2, nb, W_SSM)
    u_e = u_pairs[:, 0].reshape(tmh, W_SSM)
    u_o = u_pairs[:, 1].reshape(tmh, W_SSM)
    ub_e = u_e.astype(BF16)
    ub_o = u_o.astype(BF16)
    for n in range(N_HALVES):
        cols = slice(n * U_HALF, (n + 1) * U_HALF)
        gbuf[:, 2 * n * S_HALF:(2 * n + 2) * S_HALF] = _dot(
            jnp.concatenate([ub_e[:, cols], ub_o[:, cols]], axis=1), bc2_ref[n])

    for src, dst in zip(cast_in, cast_out):
        dst[...] = src[...].astype(BF16)

    def scan_step(h, rows, lr, li, re_cols, im_cols):
        hr, hi = h
        return (lr * hr - li * hi + gbuf[rows, re_cols], lr * hi + li * hr + gbuf[rows, im_cols])

    def store_pair(buf, row0, pair, re_cols, im_cols):
        rows = slice(row0, row0 + BF16_ROWS)
        buf[rows, re_cols] = jnp.concatenate([h[0] for h in pair], axis=0).astype(BF16)
        buf[rows, im_cols] = jnp.concatenate([h[1] for h in pair], axis=0).astype(BF16)

    for n in range(N_HALVES):
        for c0 in range(0, S_HALF, SCAN_LANES):
            re_cols = slice(2 * n * S_HALF + c0, 2 * n * S_HALF + c0 + SCAN_LANES)
            im_cols = slice((2 * n + 1) * S_HALF + c0, (2 * n + 1) * S_HALF + c0 + SCAN_LANES)
            lr = lam2r_ref[n, :, c0:c0 + SCAN_LANES]
            li = lam2i_ref[n, :, c0:c0 + SCAN_LANES]
            step = functools.partial(scan_step, lr=lr, li=li, re_cols=re_cols, im_cols=im_cols)
            if nb == SUBLANES:
                h = (hstate[:, re_cols], hstate[:, im_cols])
                for k in range(0, n_pair, 2):
                    h0 = step(h, slice(k * nb, (k + 1) * nb))
                    h1 = step(h0, slice((k + 1) * nb, (k + 2) * nb))
                    store_pair(hb_odd, k * nb, (h0, h1), re_cols, im_cols)
                    store_pair(hb_prev, k * nb, (h, h0), re_cols, im_cols)
                    h = h1
                hstate[:, re_cols], hstate[:, im_cols] = h
            else:
                for r0 in range(0, nb, BF16_ROWS):
                    seqs = [slice(r0 + j * SUBLANES, r0 + (j + 1) * SUBLANES) for j in range(2)]
                    pair = [(hstate[sq, re_cols], hstate[sq, im_cols]) for sq in seqs]
                    for k in range(n_pair):
                        store_pair(hb_prev, k * nb + r0, pair, re_cols, im_cols)
                        pair = [step(h, slice(k * nb + sq.start, k * nb + sq.stop)) for h, sq in zip(pair, seqs)]
                        store_pair(hb_odd, k * nb + r0, pair, re_cols, im_cols)
                    for h, sq in zip(pair, seqs):
                        hstate[sq, re_cols], hstate[sq, im_cols] = h

    conv_chunks = []
    for c0 in range(0, W_CONV, MXU_COLS):
        cols = slice(c0, c0 + MXU_COLS)
        b_g = _dot(xn, win_ref[:, c0:c0 + MXU_COLS])
        c_g = _dot(xn, win_ref[:, W_CONV + c0:W_CONV + c0 + MXU_COLS])
        xc = _dot(xn, win_ref[:, 2 * W_CONV + c0:2 * W_CONV + c0 + MXU_COLS])
        v = c_g * xc
        vbuf[2 * nb:2 * nb + tm, cols] = v
        conv = (convw_ref[0, :, cols] * vbuf[0:tm, cols] + convw_ref[1, :, cols] * vbuf[nb:nb + tm, cols]
                + convw_ref[2, :, cols] * v)
        conv_chunks.append((b_g * conv).astype(BF16))
    conv_out = jnp.concatenate(conv_chunks, axis=-1)
    if carry:
        vbuf[0:2 * nb, :] = vbuf[tm:tm + 2 * nb, :]
    x1 = x + _dot(conv_out, wout_ref[0:W_CONV, :])

    last = hstate[...]
    for n in range(N_HALVES):
        last_re = last[:, 2 * n * S_HALF:(2 * n + 1) * S_HALF]
        last_im = last[:, (2 * n + 1) * S_HALF:(2 * n + 2) * S_HALF]
        if carry:
            hre_ref[:, n * S_HALF:(n + 1) * S_HALF] = last_re
            him_ref[:, n * S_HALF:(n + 1) * S_HALF] = last_im
        else:
            hre_ref[n * S_HALF:(n + 1) * S_HALF, :] = last_re.T
            him_ref[n * S_HALF:(n + 1) * S_HALF, :] = last_im.T

    zs = []
    for n in range(N_HALVES):
        cols = slice(n * U_HALF, (n + 1) * U_HALF)
        st_cols = slice(2 * n * S_HALF, (2 * n + 2) * S_HALF)
        y_o = _dot(hb_odd[:, st_cols], cc_ref[n]) + dskip_ref[:, cols] * u_o[:, cols]
        y_e = (_dot(hb_prev[:, st_cols], ccl_ref[n]) + _dot(ub_e[:, cols], d0_ref[n])
               + dskip_ref[:, cols] * u_e[:, cols])
        y = jnp.stack([y_e.reshape(n_pair, nb, U_HALF), y_o.reshape(n_pair, nb, U_HALF)],
                      axis=1).reshape(tm, U_HALF)
        zs.append(jax.nn.gelu(y))
    zb = jnp.concatenate([z.astype(BF16) for z in zs], axis=-1)
    ssm_chunks = []
    for n, z in enumerate(zs):
        cols = slice(n * U_HALF, (n + 1) * U_HALF)
        gate = jax.nn.sigmoid(_dot(zb, wglu_ref[:, cols]) + bglu_ref[:, cols])
        ssm_chunks.append((z * gate).astype(BF16))
    ssm_out = jnp.concatenate(ssm_chunks, axis=-1)

    x1 = x1 + _dot(ssm_out, wout_ref[W_CONV:, :])

    obuf[slot] = x1.reshape(tc, nb, D_MODEL)
    for c in out_copies(i, slot):
        c.start()

    if not carry:
        for c in cache_copies(convnew_hbm, tm, False):
            c.start()
        for c in cache_copies(convnew_hbm, tm, False):
            c.wait()

    @pl.when(i == n_steps - 1)
    def _():
        if carry:
            for c in cache_copies(convnew_hbm, tm, False):
                c.start()
            for c in cache_copies(convnew_hbm, tm, False):
                c.wait()
        if n_steps >= 2:
            for c in out_copies(i - 1, 1 - slot):
                c.wait()
        for c in out_copies(i, slot):
            c.wait()


def _resident(shape):
    nd = len(shape)
    return pl.BlockSpec(shape, lambda i: (0,) * nd, pipeline_mode=pl.Buffered(1))


def _mixer(x, wts, state=None, cast=(), *, nb, tc):
    n_seq, n_time, _ = x.shape
    carry = state is None
    if carry:
        assert n_seq == nb == SUBLANES and n_time % tc == 0 and tc % 4 == 0
        n_steps = n_time // tc
    else:
        assert n_time == tc == SUBLANES and n_seq % nb == 0 and nb % BF16_ROWS == 0
        n_steps = n_seq // nb
    tm = nb * tc
    n_st = N_GROUPS * STATE_P
    body = functools.partial(_mixer_body, nb=nb, tc=tc, n_steps=n_steps, carry=carry, n_cast=len(cast))
    any_spec = pl.BlockSpec(memory_space=pl.ANY)
    if carry:
        st_spec = pl.BlockSpec((nb, n_st), lambda i: (0, 0))
        st_shape = jax.ShapeDtypeStruct((n_seq, n_st), F32)
    else:
        st_spec = pl.BlockSpec((n_st, nb), lambda i: (0, i))
        st_shape = jax.ShapeDtypeStruct((n_st, n_seq), F32)
    state_specs = [] if carry else [any_spec, st_spec, st_spec]
    for w in cast:
        assert w.shape[0] % (n_steps * BF16_ROWS) == 0, w.shape
    cast_specs = [pl.BlockSpec((w.shape[0] // n_steps, w.shape[1]), lambda i: (i, 0)) for w in cast]
    return pl.pallas_call(
        body,
        grid=(n_steps,),
        in_specs=[any_spec] + state_specs + [_resident(w.shape) for w in wts] + cast_specs,
        out_specs=[any_spec, any_spec, st_spec, st_spec] + cast_specs,
        out_shape=[jax.ShapeDtypeStruct(x.shape, F32),
                   jax.ShapeDtypeStruct((1, n_seq, CONV_W - 1, W_CONV), F32),
                   st_shape, st_shape]
                  + [jax.ShapeDtypeStruct(w.shape, BF16) for w in cast],
        scratch_shapes=[pltpu.VMEM((2, tc, nb, D_MODEL), F32),
                        pltpu.VMEM((2, tc, nb, D_MODEL), F32),
                        pltpu.VMEM((tm // 2, S_COLS), F32),
                        pltpu.VMEM((tm // 2, S_COLS), BF16),
                        pltpu.VMEM((tm // 2, S_COLS), BF16),
                        pltpu.VMEM((nb, S_COLS), F32),
                        pltpu.VMEM((2 * nb + tm, W_CONV), F32),
                        pltpu.SemaphoreType.DMA((2, SUBLANES)),
                        pltpu.SemaphoreType.DMA((2, SUBLANES)),
                        pltpu.SemaphoreType.DMA((2, CONV_W - 1))],
        compiler_params=pltpu.CompilerParams(dimension_semantics=("arbitrary",),
                                             vmem_limit_bytes=VMEM_LIMIT),
        name="mixer_carry" if carry else "mixer_step",
    )(x, *(() if carry else state), *wts, *cast)


def _ffn_body(xp_hbm, xs_hbm, nffn_ref, wg_ref, wu_ref, wd_ref, nfin_ref, yp_hbm, ys_hbm,
              xbuf, obuf, abuf, insem, outsem, *, n_p, n_s, tm):
    i = pl.program_id(0)
    n_steps = n_p + n_s
    slot = lax.rem(i, 2)

    def tile_copy(step, buf_slot, to_vmem, first):
        hbm = (xp_hbm if first else xs_hbm) if to_vmem else (yp_hbm if first else ys_hbm)
        h = hbm.at[pl.ds((step if first else step - n_p) * tm, tm), :]
        if to_vmem:
            return pltpu.make_async_copy(h, xbuf.at[buf_slot], insem.at[buf_slot])
        return pltpu.make_async_copy(obuf.at[buf_slot], h, outsem.at[buf_slot])

    def start(step, buf_slot, to_vmem):
        pl.when(step < n_p)(lambda: tile_copy(step, buf_slot, to_vmem, True).start())
        pl.when(step >= n_p)(lambda: tile_copy(step, buf_slot, to_vmem, False).start())

    def wait(buf_slot, to_vmem):
        tile_copy(0, buf_slot, to_vmem, True).wait()

    pl.when(i == 0)(lambda: start(0, 0, True))
    wait(slot, True)
    pl.when(i + 1 < n_steps)(lambda: start(i + 1, 1 - slot, True))
    pl.when(i >= 2)(lambda: wait(slot, False))

    x = xbuf[slot]
    hn = _rms(x, nffn_ref[...]).astype(BF16)
    for s, w in FFN_CHUNKS:
        g = _dot(hn, wg_ref[:, s:s + w])
        up = _dot(hn, wu_ref[:, s:s + w])
        abuf[:, s:s + w] = (g * jax.nn.sigmoid(g) * up).astype(BF16)
    acc = x + _dot(abuf[...], wd_ref[...])
    obuf[slot] = _rms(acc, nfin_ref[...])
    start(i, slot, False)

    @pl.when(i == n_steps - 1)
    def _():
        if n_steps >= 2:
            wait(1 - slot, False)
        wait(slot, False)


def _ffn(xp, xs, wts, *, tm):
    n_p, n_s = xp.shape[0] // tm, xs.shape[0] // tm
    assert xp.shape[0] == n_p * tm and xs.shape[0] == n_s * tm
    any_spec = pl.BlockSpec(memory_space=pl.ANY)
    return pl.pallas_call(
        functools.partial(_ffn_body, n_p=n_p, n_s=n_s, tm=tm),
        grid=(n_p + n_s,),
        in_specs=[any_spec, any_spec] + [_resident(w.shape) for w in wts],
        out_specs=[any_spec, any_spec],
        out_shape=[jax.ShapeDtypeStruct(xp.shape, F32), jax.ShapeDtypeStruct(xs.shape, F32)],
        scratch_shapes=[pltpu.VMEM((2, tm, D_MODEL), F32),
                        pltpu.VMEM((2, tm, D_MODEL), F32),
                        pltpu.VMEM((tm, D_FF), BF16),
                        pltpu.SemaphoreType.DMA((2,)),
                        pltpu.SemaphoreType.DMA((2,))],
        compiler_params=pltpu.CompilerParams(dimension_semantics=("arbitrary",),
                                             vmem_limit_bytes=VMEM_LIMIT),
        name="ffn",
    )(xp, xs, *wts)


N_S5_OUT = 6
N_S5_SCRATCH = 4


def _prep_body(lre_ref, lim_ref, logdt_ref, btre_ref, btim_ref, cre_ref, cim_ref, *refs):
    n_cast = (len(refs) - N_S5_OUT - N_S5_SCRATCH) // 2
    cast_in, refs = refs[:n_cast], refs[n_cast:]
    s5_out, refs = refs[:N_S5_OUT], refs[N_S5_OUT:]
    cast_out, s5_scratch = refs[:n_cast], refs[n_cast:]
    for src, dst in zip(cast_in, cast_out):
        dst[...] = src[...].astype(BF16)
    pl.when(pl.program_id(0) == 0)(functools.partial(
        _s5_matrices, lre_ref, lim_ref, logdt_ref, btre_ref, btim_ref, cre_ref, cim_ref, *s5_out, *s5_scratch))


def _s5_matrices(lre_ref, lim_ref, logdt_ref, btre_ref, btim_ref, cre_ref, cim_ref,
                 lam2r_ref, lam2i_ref, bc2_ref, cc_ref, ccl_ref, d0_ref, bscr, lbscr, cscr, clscr):
    lre = lre_ref[...]
    lim = lim_ref[...]
    grp = (lax.broadcasted_iota(jnp.int32, (N_GROUPS, N_GROUPS), 0)
           == lax.broadcasted_iota(jnp.int32, (N_GROUPS, N_GROUPS), 1))
    logdt = jnp.sum(jnp.where(grp, jnp.broadcast_to(logdt_ref[...], (N_GROUPS, N_GROUPS)), 0.0),
                    axis=1, keepdims=True)
    dt = jnp.exp(logdt)
    mag = jnp.exp(lre * dt)
    lbr = mag * jnp.cos(lim * dt)
    lbi = mag * jnp.sin(lim * dt)
    l2r = lbr * lbr - lbi * lbi
    l2i = 2.0 * (lbr * lbi)
    den = lre * lre + lim * lim
    fr = ((lbr - 1.0) * lre + lbi * lim) / den
    fi = (lbi * lre - (lbr - 1.0) * lim) / den
    frb, fib = fr[:, None, :], fi[:, None, :]
    lrb, lib = lbr[:, None, :], lbi[:, None, :]
    bbr = frb * btre_ref[...] - fib * btim_ref[...]
    bbi = frb * btim_ref[...] + fib * btre_ref[...]
    lbbr = lrb * bbr - lib * bbi
    lbbi = lrb * bbi + lib * bbr
    cre = cre_ref[...]
    cim = cim_ref[...]
    clr = cre * lrb - cim * lib
    cli = cre * lib + cim * lrb

    for scr in (bscr, lbscr, cscr, clscr):
        scr[...] = jnp.zeros_like(scr)
    for g in range(N_GROUPS):
        n, k = divmod(g, GROUPS_PER_HALF)
        chans = slice(k * SSM_H, (k + 1) * SSM_H)
        states = slice(k * STATE_P, (k + 1) * STATE_P)
        im_states = slice(S_HALF + k * STATE_P, S_HALF + (k + 1) * STATE_P)
        for scr, re, im in ((bscr, bbr, bbi), (lbscr, lbbr, lbbi), (cscr, cre, -cim), (clscr, clr, -cli)):
            scr[n, chans, states] = re[g]
            scr[n, chans, im_states] = im[g]
        lam2r_ref[n, :, states] = jnp.broadcast_to(l2r[g:g + 1, :], (SUBLANES, STATE_P))
        lam2i_ref[n, :, states] = jnp.broadcast_to(l2i[g:g + 1, :], (SUBLANES, STATE_P))
    for n in range(N_HALVES):
        bc2_ref[n, 0:U_HALF, :] = lbscr[n].astype(BF16)
        bc2_ref[n, U_HALF:, :] = bscr[n].astype(BF16)
        c_t = cscr[n].T
        cc_ref[n] = c_t.astype(BF16)
        ccl_ref[n] = clscr[n].T.astype(BF16)
        b_hi, b_lo = _split_bf16(bscr[n])
        c_hi, c_lo = _split_bf16(c_t)
        d0_ref[n] = (_dot(b_hi, c_hi) + _dot(b_hi, c_lo) + _dot(b_lo, c_hi)).astype(BF16)


PREP_STEPS = 8


def _prep(lam_re, lam_im, log_dt, b_re, b_im, c_re, c_im, cast):
    s5_in = (lam_re, lam_im, log_dt.reshape(1, N_GROUPS), b_re.transpose(0, 2, 1), b_im.transpose(0, 2, 1),
             c_re, c_im)
    s5_shapes = [jax.ShapeDtypeStruct((N_HALVES, SUBLANES, S_HALF), F32),
                 jax.ShapeDtypeStruct((N_HALVES, SUBLANES, S_HALF), F32),
                 jax.ShapeDtypeStruct((N_HALVES, 2 * U_HALF, 2 * S_HALF), BF16),
                 jax.ShapeDtypeStruct((N_HALVES, 2 * S_HALF, U_HALF), BF16),
                 jax.ShapeDtypeStruct((N_HALVES, 2 * S_HALF, U_HALF), BF16),
                 jax.ShapeDtypeStruct((N_HALVES, U_HALF, U_HALF), BF16)]
    assert len(s5_shapes) == N_S5_OUT
    for w in cast:
        assert w.shape[0] % (PREP_STEPS * BF16_ROWS) == 0, w.shape
    cast_specs = [pl.BlockSpec((w.shape[0] // PREP_STEPS, w.shape[1]), lambda i: (i, 0)) for w in cast]
    whole = lambda a: pl.BlockSpec(a.shape, lambda i: (0,) * len(a.shape))
    return pl.pallas_call(
        _prep_body,
        grid=(PREP_STEPS,),
        in_specs=[whole(a) for a in s5_in] + cast_specs,
        out_specs=[whole(a) for a in s5_shapes] + cast_specs,
        out_shape=s5_shapes + [jax.ShapeDtypeStruct(w.shape, BF16) for w in cast],
        scratch_shapes=[pltpu.VMEM((N_HALVES, U_HALF, 2 * S_HALF), F32)] * N_S5_SCRATCH,
        compiler_params=pltpu.CompilerParams(dimension_semantics=("arbitrary",)),
        name="prep",
    )(*s5_in, *cast)


def kernel(x_prompt, x_sample, cache_conv, state_ssm_re, state_ssm_im, norm_mix, w_in, conv_w, ssm_lam_re, ssm_lam_im, ssm_log_dt, ssm_b_re, ssm_b_im, ssm_c_re, ssm_c_im, ssm_d, w_glu, b_glu, w_out, norm_ffn, w_gate, w_up, w_down, norm_final):
    assert norm_mix.shape[0] == 1, "single-layer trunk"
    n_st = N_GROUPS * STATE_P
    lam2r, lam2i, bc2, cc, ccl, d0, w_in_b, w_glu_b, w_out_b = _prep(
        ssm_lam_re[0], ssm_lam_im[0], ssm_log_dt[0], ssm_b_re[0], ssm_b_im[0], ssm_c_re[0], ssm_c_im[0],
        cast=(w_in[0], w_glu[0], w_out[0]))
    mix_w = (norm_mix, w_in_b, conv_w[0][:, None, :], lam2r, lam2i, bc2, cc, ccl, d0, ssm_d, w_glu_b, b_glu,
             w_out_b)

    bp, lp, _ = x_prompt.shape
    bs, ls, _ = x_sample.shape

    x1p, convp, rep, imp, wg, wu, wd = _mixer(x_prompt, mix_w, cast=(w_gate[0], w_up[0], w_down[0]),
                                              nb=SUBLANES, tc=128)
    ffn_w = (norm_ffn, wg, wu, wd, norm_final.reshape(1, D_MODEL))
    def state_major(h):
        return h[0].transpose(1, 2, 0).reshape(n_st, h.shape[1])

    sample_state = (cache_conv, state_major(state_ssm_re), state_major(state_ssm_im))
    x1s, convs, res, ims = _mixer(x_sample, mix_w, sample_state, nb=128, tc=SUBLANES)

    yp, ys = _ffn(x1p.reshape(bp * lp, D_MODEL), x1s.reshape(bs * ls, D_MODEL), ffn_w, tm=1024)
    yp = yp.reshape(bp, lp, D_MODEL)
    ys = ys.reshape(bs, ls, D_MODEL)

    def seq_major(h):
        return h.reshape(N_GROUPS, STATE_P, h.shape[1]).transpose(2, 0, 1)[None]

    return (yp, ys, convp, rep.reshape(1, bp, N_GROUPS, STATE_P), imp.reshape(1, bp, N_GROUPS, STATE_P),
            convs, seq_major(res), seq_major(ims))
```

```python
import functools

import jax
import jax.numpy as jnp
from jax import lax
from jax.experimental import pallas as pl
from jax.experimental.pallas import tpu as pltpu

D_MODEL = 1024
W_CONV = 512
W_SSM = 512
CONV_W = 3
N_GROUPS = 32
SSM_H = 16
STATE_P = 64
D_FF = 2816
EPS = 1e-5

SUBLANES = 8
BF16_ROWS = 16
GROUPS_PER_HALF = 16
N_HALVES = N_GROUPS // GROUPS_PER_HALF
U_HALF = GROUPS_PER_HALF * SSM_H
S_HALF = GROUPS_PER_HALF * STATE_P
S_COLS = 2 * N_GROUPS * STATE_P
SCAN_LANES = 512
MXU_COLS = 256
STAGE_ROWS = 128
FFN_CHUNKS = tuple((s, MXU_COLS) for s in range(0, D_FF, MXU_COLS))
VMEM_LIMIT = 60 * 1024 * 1024

BF16 = jnp.bfloat16
F32 = jnp.float32


def _rms(x, g):
    y = x * lax.rsqrt(jnp.mean(x * x, axis=-1, keepdims=True) + EPS)
    return y * g


def _dot(a, b):
    return jnp.dot(a, b, preferred_element_type=F32)


def _split_bf16(a):
    hi = a.astype(BF16)
    return hi, (a - hi.astype(F32)).astype(BF16)


def _tile_copies(hbm, buf, sem, step, slot, *, nb, tc, per_seq, to_vmem):
    copies = []
    for k in range(SUBLANES):
        if per_seq:
            h = hbm.at[k, pl.ds(step * tc, tc), :]
            v = buf.at[slot, :, k, :]
        else:
            h = hbm.at[pl.ds(step * nb, nb), k, :]
            v = buf.at[slot, k]
        src, dst = (h, v) if to_vmem else (v, h)
        copies.append(pltpu.make_async_copy(src, dst, sem.at[slot, k]))
    return copies


def _mixer_body(x_hbm, *refs, nb, tc, n_steps, carry, n_cast):
    if carry:
        cache_ref = re0_ref = im0_ref = None
    else:
        cache_ref, re0_ref, im0_ref, *refs = refs
    (nmix_ref, win_ref, convw_ref, lam2r_ref, lam2i_ref, bc2_ref, cc_ref, ccl_ref, d0_ref, dskip_ref,
     wglu_ref, bglu_ref, wout_ref, *refs) = refs
    cast_in, refs = refs[:n_cast], refs[n_cast:]
    x1_hbm, convnew_hbm, hre_ref, him_ref, *refs = refs
    cast_out, refs = refs[:n_cast], refs[n_cast:]
    if carry:
        w_f32 = (win_ref, wglu_ref, wout_ref)
        w_bf16_hbm, refs = refs[:len(w_f32)], refs[len(w_f32):]
        *refs, win_ref, wglu_ref, wout_ref, stage, stagesem, wbsem = refs
    xbuf, obuf, gbuf, hb_odd, hb_prev, hstate, vbuf, insem, outsem, cachesem = refs
    tm = nb * tc
    n_pair = tc // 2
    tmh = n_pair * nb

    i = pl.program_id(0)
    slot = lax.rem(i, 2)
    cp = functools.partial(_tile_copies, nb=nb, tc=tc, per_seq=carry)
    in_copies = functools.partial(cp, x_hbm, xbuf, insem, to_vmem=True)
    out_copies = functools.partial(cp, x1_hbm, obuf, outsem, to_vmem=False)

    def cache_copies(hbm, row0, to_vmem):
        copies = []
        for k in range(CONV_W - 1):
            h = hbm.at[0, pl.ds(0 if carry else i * nb, nb), k, :]
            v = vbuf.at[pl.ds(row0 + k * nb, nb), :]
            src, dst = (h, v) if to_vmem else (v, h)
            copies.append(pltpu.make_async_copy(src, dst, cachesem.at[int(to_vmem), k]))
        return copies

    @pl.when(i == 0)
    def _():
        for c in in_copies(0, 0):
            c.start()

    if not carry:
        for c in cache_copies(cache_ref, 0, True):
            c.start()
    for c in in_copies(i, slot):
        c.wait()
    if not carry:
        for c in cache_copies(cache_ref, 0, True):
            c.wait()

    @pl.when(i + 1 < n_steps)
    def _():
        for c in in_copies(i + 1, 1 - slot):
            c.start()

    @pl.when(i >= 2)
    def _():
        for c in out_copies(i - 2, slot):
            c.wait()

    if carry:
        w_bf16 = (win_ref, wglu_ref, wout_ref)
        writes = [pltpu.make_async_copy(v, h, wbsem.at[j]) for j, (v, h) in enumerate(zip(w_bf16, w_bf16_hbm))]

        @pl.when(i == 0)
        def _():
            vbuf[0:2 * nb, :] = jnp.zeros((2 * nb, W_CONV), F32)
            hstate[...] = jnp.zeros((nb, S_COLS), F32)
            for w_hbm, w_res in zip(w_f32, w_bf16):
                rows, cols = w_res.shape
                n_chunks = rows // STAGE_ROWS

                def load(j, k, w_hbm=w_hbm, cols=cols):
                    return pltpu.make_async_copy(w_hbm.at[pl.ds(j * STAGE_ROWS, STAGE_ROWS), :],
                                                 stage.at[k, :, pl.ds(0, cols)], stagesem.at[k])

                def chunk(j, carry_, load=load, w_res=w_res, cols=cols, n_chunks=n_chunks):
                    k = lax.rem(j, 2)
                    pl.when(j + 1 < n_chunks)(lambda: load(j + 1, 1 - k).start())
                    load(j, k).wait()
                    dst = pl.ds(pl.multiple_of(j * STAGE_ROWS, STAGE_ROWS), STAGE_ROWS)
                    w_res[dst, :] = stage[k, :, 0:cols].astype(BF16)
                    return carry_

                load(0, 0).start()
                lax.fori_loop(0, n_chunks, chunk, 0)
            for c in writes:
                c.start()
    else:
        for n in range(N_HALVES):
            hstate[:, 2 * n * S_HALF:(2 * n + 1) * S_HALF] = re0_ref[n * S_HALF:(n + 1) * S_HALF, :].T
            hstate[:, (2 * n + 1) * S_HALF:(2 * n + 2) * S_HALF] = im0_ref[n * S_HALF:(n + 1) * S_HALF, :].T

    x = xbuf[slot].reshape(tm, D_MODEL)
    xn = _rms(x, nmix_ref[...]).astype(BF16)

    u = _dot(xn, win_ref[:, 3 * W_CONV:])
    u_pairs = u.reshape(n_pair, 2, nb, W_SSM)
    u_e = u_pairs[:, 0].reshape(tmh, W_SSM)
    u_o = u_pairs[:, 1].reshape(tmh, W_SSM)
    ub_e = u_e.astype(BF16)
    ub_o = u_o.astype(BF16)
    for n in range(N_HALVES):
        cols = slice(n * U_HALF, (n + 1) * U_HALF)
        gbuf[:, 2 * n * S_HALF:(2 * n + 2) * S_HALF] = _dot(
            jnp.concatenate([ub_e[:, cols], ub_o[:, cols]], axis=1), bc2_ref[n])

    for src, dst in zip(cast_in, cast_out):
        dst[...] = src[...].astype(BF16)

    def scan_step(h, rows, lr, li, re_cols, im_cols):
        hr, hi = h
        return (lr * hr - li * hi + gbuf[rows, re_cols], lr * hi + li * hr + gbuf[rows, im_cols])

    def store_pair(buf, row0, pair, re_cols, im_cols):
        rows = slice(row0, row0 + BF16_ROWS)
        buf[rows, re_cols] = jnp.concatenate([h[0] for h in pair], axis=0).astype(BF16)
        buf[rows, im_cols] = jnp.concatenate([h[1] for h in pair], axis=0).astype(BF16)

    for n in range(N_HALVES):
        for c0 in range(0, S_HALF, SCAN_LANES):
            re_cols = slice(2 * n * S_HALF + c0, 2 * n * S_HALF + c0 + SCAN_LANES)
            im_cols = slice((2 * n + 1) * S_HALF + c0, (2 * n + 1) * S_HALF + c0 + SCAN_LANES)
            lr = lam2r_ref[n, :, c0:c0 + SCAN_LANES]
            li = lam2i_ref[n, :, c0:c0 + SCAN_LANES]
            step = functools.partial(scan_step, lr=lr, li=li, re_cols=re_cols, im_cols=im_cols)
            if nb == SUBLANES:
                h = (hstate[:, re_cols], hstate[:, im_cols])
                for k in range(0, n_pair, 2):
                    h0 = step(h, slice(k * nb, (k + 1) * nb))
                    h1 = step(h0, slice((k + 1) * nb, (k + 2) * nb))
                    store_pair(hb_odd, k * nb, (h0, h1), re_cols, im_cols)
                    store_pair(hb_prev, k * nb, (h, h0), re_cols, im_cols)
                    h = h1
                hstate[:, re_cols], hstate[:, im_cols] = h
            else:
                for r0 in range(0, nb, BF16_ROWS):
                    seqs = [slice(r0 + j * SUBLANES, r0 + (j + 1) * SUBLANES) for j in range(2)]
                    pair = [(hstate[sq, re_cols], hstate[sq, im_cols]) for sq in seqs]
                    for k in range(n_pair):
                        store_pair(hb_prev, k * nb + r0, pair, re_cols, im_cols)
                        pair = [step(h, slice(k * nb + sq.start, k * nb + sq.stop)) for h, sq in zip(pair, seqs)]
                        store_pair(hb_odd, k * nb + r0, pair, re_cols, im_cols)
                    for h, sq in zip(pair, seqs):
                        hstate[sq, re_cols], hstate[sq, im_cols] = h

    conv_chunks = []
    for c0 in range(0, W_CONV, MXU_COLS):
        cols = slice(c0, c0 + MXU_COLS)
        b_g = _dot(xn, win_ref[:, c0:c0 + MXU_COLS])
        c_g = _dot(xn, win_ref[:, W_CONV + c0:W_CONV + c0 + MXU_COLS])
        xc = _dot(xn, win_ref[:, 2 * W_CONV + c0:2 * W_CONV + c0 + MXU_COLS])
        v = c_g * xc
        vbuf[2 * nb:2 * nb + tm, cols] = v
        conv = (convw_ref[0, :, cols] * vbuf[0:tm, cols] + convw_ref[1, :, cols] * vbuf[nb:nb + tm, cols]
                + convw_ref[2, :, cols] * v)
        conv_chunks.append((b_g * conv).astype(BF16))
    conv_out = jnp.concatenate(conv_chunks, axis=-1)
    if carry:
        vbuf[0:2 * nb, :] = vbuf[tm:tm + 2 * nb, :]
    x1 = x + _dot(conv_out, wout_ref[0:W_CONV, :])

    last = hstate[...]
    for n in range(N_HALVES):
        last_re = last[:, 2 * n * S_HALF:(2 * n + 1) * S_HALF]
        last_im = last[:, (2 * n + 1) * S_HALF:(2 * n + 2) * S_HALF]
        if carry:
            hre_ref[:, n * S_HALF:(n + 1) * S_HALF] = last_re
            him_ref[:, n * S_HALF:(n + 1) * S_HALF] = last_im
        else:
            hre_ref[n * S_HALF:(n + 1) * S_HALF, :] = last_re.T
            him_ref[n * S_HALF:(n + 1) * S_HALF, :] = last_im.T

    zs = []
    for n in range(N_HALVES):
        cols = slice(n * U_HALF, (n + 1) * U_HALF)
        st_cols = slice(2 * n * S_HALF, (2 * n + 2) * S_HALF)
        y_o = _dot(hb_odd[:, st_cols], cc_ref[n]) + dskip_ref[:, cols] * u_o[:, cols]
        y_e = (_dot(hb_prev[:, st_cols], ccl_ref[n]) + _dot(ub_e[:, cols], d0_ref[n])
               + dskip_ref[:, cols] * u_e[:, cols])
        y = jnp.stack([y_e.reshape(n_pair, nb, U_HALF), y_o.reshape(n_pair, nb, U_HALF)],
                      axis=1).reshape(tm, U_HALF)
        zs.append(jax.nn.gelu(y))
    zb = jnp.concatenate([z.astype(BF16) for z in zs], axis=-1)
    ssm_chunks = []
    for n, z in enumerate(zs):
        cols = slice(n * U_HALF, (n + 1) * U_HALF)
        gate = jax.nn.sigmoid(_dot(zb, wglu_ref[:, cols]) + bglu_ref[:, cols])
        ssm_chunks.append((z * gate).astype(BF16))
    ssm_out = jnp.concatenate(ssm_chunks, axis=-1)

    x1 = x1 + _dot(ssm_out, wout_ref[W_CONV:, :])

    obuf[slot] = x1.reshape(tc, nb, D_MODEL)
    for c in out_copies(i, slot):
        c.start()

    if not carry:
        for c in cache_copies(convnew_hbm, tm, False):
            c.start()
        for c in cache_copies(convnew_hbm, tm, False):
            c.wait()

    @pl.when(i == n_steps - 1)
    def _():
        if carry:
            for c in cache_copies(convnew_hbm, tm, False):
                c.start()
            for c in cache_copies(convnew_hbm, tm, False):
                c.wait()
            for c in writes:
                c.wait()
        if n_steps >= 2:
            for c in out_copies(i - 1, 1 - slot):
                c.wait()
        for c in out_copies(i, slot):
            c.wait()


def _resident(shape):
    nd = len(shape)
    return pl.BlockSpec(shape, lambda i: (0,) * nd, pipeline_mode=pl.Buffered(1))


def _mixer(x, wts, state=None, cast=(), *, nb, tc):
    n_seq, n_time, _ = x.shape
    carry = state is None
    if carry:
        assert n_seq == nb == SUBLANES and n_time % tc == 0 and tc % 4 == 0
        n_steps = n_time // tc
    else:
        assert n_time == tc == SUBLANES and n_seq % nb == 0 and nb % BF16_ROWS == 0
        n_steps = n_seq // nb
    tm = nb * tc
    n_st = N_GROUPS * STATE_P
    body = functools.partial(_mixer_body, nb=nb, tc=tc, n_steps=n_steps, carry=carry, n_cast=len(cast))
    any_spec = pl.BlockSpec(memory_space=pl.ANY)
    if carry:
        st_spec = pl.BlockSpec((nb, n_st), lambda i: (0, 0))
        st_shape = jax.ShapeDtypeStruct((n_seq, n_st), F32)
    else:
        st_spec = pl.BlockSpec((n_st, nb), lambda i: (0, i))
        st_shape = jax.ShapeDtypeStruct((n_st, n_seq), F32)
    state_specs = [] if carry else [any_spec, st_spec, st_spec]
    for w in cast:
        assert w.shape[0] % (n_steps * BF16_ROWS) == 0, w.shape
    cast_specs = [pl.BlockSpec((w.shape[0] // n_steps, w.shape[1]), lambda i: (i, 0)) for w in cast]
    staged = [w for w in wts if carry and w.dtype == F32 and w.ndim == 2 and w.shape[0] >= STAGE_ROWS]
    for w in staged:
        assert w.shape[0] % STAGE_ROWS == 0, w.shape
    w_specs = [any_spec if any(w is v for v in staged) else _resident(w.shape) for w in wts]
    return pl.pallas_call(
        body,
        grid=(n_steps,),
        in_specs=[any_spec] + state_specs + w_specs + cast_specs,
        out_specs=[any_spec, any_spec, st_spec, st_spec] + cast_specs + [any_spec] * len(staged),
        out_shape=[jax.ShapeDtypeStruct(x.shape, F32),
                   jax.ShapeDtypeStruct((1, n_seq, CONV_W - 1, W_CONV), F32),
                   st_shape, st_shape]
                  + [jax.ShapeDtypeStruct(w.shape, BF16) for w in cast]
                  + [jax.ShapeDtypeStruct(w.shape, BF16) for w in staged],
        scratch_shapes=[pltpu.VMEM((2, tc, nb, D_MODEL), F32),
                        pltpu.VMEM((2, tc, nb, D_MODEL), F32),
                        pltpu.VMEM((tm // 2, S_COLS), F32),
                        pltpu.VMEM((tm // 2, S_COLS), BF16),
                        pltpu.VMEM((tm // 2, S_COLS), BF16),
                        pltpu.VMEM((nb, S_COLS), F32),
                        pltpu.VMEM((2 * nb + tm, W_CONV), F32),
                        pltpu.SemaphoreType.DMA((2, SUBLANES)),
                        pltpu.SemaphoreType.DMA((2, SUBLANES)),
                        pltpu.SemaphoreType.DMA((2, CONV_W - 1))]
                       + ([pltpu.VMEM(w.shape, BF16) for w in staged]
                          + [pltpu.VMEM((2, STAGE_ROWS, max(w.shape[1] for w in staged)), F32),
                             pltpu.SemaphoreType.DMA((2,)), pltpu.SemaphoreType.DMA((len(staged),))]
                          if staged else []),
        compiler_params=pltpu.CompilerParams(dimension_semantics=("arbitrary",),
                                             vmem_limit_bytes=VMEM_LIMIT),
        name="mixer_carry" if carry else "mixer_step",
    )(x, *(() if carry else state), *wts, *cast)


def _ffn_body(xp_hbm, xs_hbm, nffn_ref, wg_ref, wu_ref, wd_ref, nfin_ref, yp_hbm, ys_hbm,
              xbuf, obuf, abuf, insem, outsem, *, n_p, n_s, tm):
    i = pl.program_id(0)
    n_steps = n_p + n_s
    slot = lax.rem(i, 2)

    def tile_copy(step, buf_slot, to_vmem, first):
        hbm = (xp_hbm if first else xs_hbm) if to_vmem else (yp_hbm if first else ys_hbm)
        h = hbm.at[pl.ds((step if first else step - n_p) * tm, tm), :]
        if to_vmem:
            return pltpu.make_async_copy(h, xbuf.at[buf_slot], insem.at[buf_slot])
        return pltpu.make_async_copy(obuf.at[buf_slot], h, outsem.at[buf_slot])

    def start(step, buf_slot, to_vmem):
        pl.when(step < n_p)(lambda: tile_copy(step, buf_slot, to_vmem, True).start())
        pl.when(step >= n_p)(lambda: tile_copy(step, buf_slot, to_vmem, False).start())

    def wait(buf_slot, to_vmem):
        tile_copy(0, buf_slot, to_vmem, True).wait()

    pl.when(i == 0)(lambda: start(0, 0, True))
    wait(slot, True)
    pl.when(i + 1 < n_steps)(lambda: start(i + 1, 1 - slot, True))
    pl.when(i >= 2)(lambda: wait(slot, False))

    x = xbuf[slot]
    hn = _rms(x, nffn_ref[...]).astype(BF16)
    for s, w in FFN_CHUNKS:
        g = _dot(hn, wg_ref[:, s:s + w])
        up = _dot(hn, wu_ref[:, s:s + w])
        abuf[:, s:s + w] = (g * jax.nn.sigmoid(g) * up).astype(BF16)
    acc = x + _dot(abuf[...], wd_ref[...])
    obuf[slot] = _rms(acc, nfin_ref[...])
    start(i, slot, False)

    @pl.when(i == n_steps - 1)
    def _():
        if n_steps >= 2:
            wait(1 - slot, False)
        wait(slot, False)


def _ffn(xp, xs, wts, *, tm):
    n_p, n_s = xp.shape[0] // tm, xs.shape[0] // tm
    assert xp.shape[0] == n_p * tm and xs.shape[0] == n_s * tm
    any_spec = pl.BlockSpec(memory_space=pl.ANY)
    return pl.pallas_call(
        functools.partial(_ffn_body, n_p=n_p, n_s=n_s, tm=tm),
        grid=(n_p + n_s,),
        in_specs=[any_spec, any_spec] + [_resident(w.shape) for w in wts],
        out_specs=[any_spec, any_spec],
        out_shape=[jax.ShapeDtypeStruct(xp.shape, F32), jax.ShapeDtypeStruct(xs.shape, F32)],
        scratch_shapes=[pltpu.VMEM((2, tm, D_MODEL), F32),
                        pltpu.VMEM((2, tm, D_MODEL), F32),
                        pltpu.VMEM((tm, D_FF), BF16),
                        pltpu.SemaphoreType.DMA((2,)),
                        pltpu.SemaphoreType.DMA((2,))],
        compiler_params=pltpu.CompilerParams(dimension_semantics=("arbitrary",),
                                             vmem_limit_bytes=VMEM_LIMIT),
        name="ffn",
    )(xp, xs, *wts)


N_S5_OUT = 6
N_S5_SCRATCH = 4


def _prep_body(lre_ref, lim_ref, logdt_ref, btre_ref, btim_ref, cre_ref, cim_ref, *refs):
    n_cast = (len(refs) - N_S5_OUT - N_S5_SCRATCH) // 2
    cast_in, refs = refs[:n_cast], refs[n_cast:]
    s5_out, refs = refs[:N_S5_OUT], refs[N_S5_OUT:]
    cast_out, s5_scratch = refs[:n_cast], refs[n_cast:]
    for src, dst in zip(cast_in, cast_out):
        dst[...] = src[...].astype(BF16)
    pl.when(pl.program_id(0) == 0)(functools.partial(
        _s5_matrices, lre_ref, lim_ref, logdt_ref, btre_ref, btim_ref, cre_ref, cim_ref, *s5_out, *s5_scratch))


def _s5_matrices(lre_ref, lim_ref, logdt_ref, btre_ref, btim_ref, cre_ref, cim_ref,
                 lam2r_ref, lam2i_ref, bc2_ref, cc_ref, ccl_ref, d0_ref, bscr, lbscr, cscr, clscr):
    lre = lre_ref[...]
    lim = lim_ref[...]
    grp = (lax.broadcasted_iota(jnp.int32, (N_GROUPS, N_GROUPS), 0)
           == lax.broadcasted_iota(jnp.int32, (N_GROUPS, N_GROUPS), 1))
    logdt = jnp.sum(jnp.where(grp, jnp.broadcast_to(logdt_ref[...], (N_GROUPS, N_GROUPS)), 0.0),
                    axis=1, keepdims=True)
    dt = jnp.exp(logdt)
    mag = jnp.exp(lre * dt)
    lbr = mag * jnp.cos(lim * dt)
    lbi = mag * jnp.sin(lim * dt)
    l2r = lbr * lbr - lbi * lbi
    l2i = 2.0 * (lbr * lbi)
    den = lre * lre + lim * lim
    fr = ((lbr - 1.0) * lre + lbi * lim) / den
    fi = (lbi * lre - (lbr - 1.0) * lim) / den
    frb, fib = fr[:, None, :], fi[:, None, :]
    lrb, lib = lbr[:, None, :], lbi[:, None, :]
    bbr = frb * btre_ref[...] - fib * btim_ref[...]
    bbi = frb * btim_ref[...] + fib * btre_ref[...]
    lbbr = lrb * bbr - lib * bbi
    lbbi = lrb * bbi + lib * bbr
    cre = cre_ref[...]
    cim = cim_ref[...]
    clr = cre * lrb - cim * lib
    cli = cre * lib + cim * lrb

    for scr in (bscr, lbscr, cscr, clscr):
        scr[...] = jnp.zeros_like(scr)
    for g in range(N_GROUPS):
        n, k = divmod(g, GROUPS_PER_HALF)
        chans = slice(k * SSM_H, (k + 1) * SSM_H)
        states = slice(k * STATE_P, (k + 1) * STATE_P)
        im_states = slice(S_HALF + k * STATE_P, S_HALF + (k + 1) * STATE_P)
        for scr, re, im in ((bscr, bbr, bbi), (lbscr, lbbr, lbbi), (cscr, cre, -cim), (clscr, clr, -cli)):
            scr[n, chans, states] = re[g]
            scr[n, chans, im_states] = im[g]
        lam2r_ref[n, :, states] = jnp.broadcast_to(l2r[g:g + 1, :], (SUBLANES, STATE_P))
        lam2i_ref[n, :, states] = jnp.broadcast_to(l2i[g:g + 1, :], (SUBLANES, STATE_P))
    for n in range(N_HALVES):
        bc2_ref[n, 0:U_HALF, :] = lbscr[n].astype(BF16)
        bc2_ref[n, U_HALF:, :] = bscr[n].astype(BF16)
        c_t = cscr[n].T
        cc_ref[n] = c_t.astype(BF16)
        ccl_ref[n] = clscr[n].T.astype(BF16)
        b_hi, b_lo = _split_bf16(bscr[n])
        c_hi, c_lo = _split_bf16(c_t)
        d0_ref[n] = (_dot(b_hi, c_hi) + _dot(b_hi, c_lo) + _dot(b_lo, c_hi)).astype(BF16)


PREP_STEPS = 8


def _prep(lam_re, lam_im, log_dt, b_re, b_im, c_re, c_im, cast):
    s5_in = (lam_re, lam_im, log_dt.reshape(1, N_GROUPS), b_re.transpose(0, 2, 1), b_im.transpose(0, 2, 1),
             c_re, c_im)
    s5_shapes = [jax.ShapeDtypeStruct((N_HALVES, SUBLANES, S_HALF), F32),
                 jax.ShapeDtypeStruct((N_HALVES, SUBLANES, S_HALF), F32),
                 jax.ShapeDtypeStruct((N_HALVES, 2 * U_HALF, 2 * S_HALF), BF16),
                 jax.ShapeDtypeStruct((N_HALVES, 2 * S_HALF, U_HALF), BF16),
                 jax.ShapeDtypeStruct((N_HALVES, 2 * S_HALF, U_HALF), BF16),
                 jax.ShapeDtypeStruct((N_HALVES, U_HALF, U_HALF), BF16)]
    assert len(s5_shapes) == N_S5_OUT
    for w in cast:
        assert w.shape[0] % (PREP_STEPS * BF16_ROWS) == 0, w.shape
    cast_specs = [pl.BlockSpec((w.shape[0] // PREP_STEPS, w.shape[1]), lambda i: (i, 0)) for w in cast]
    whole = lambda a: pl.BlockSpec(a.shape, lambda i: (0,) * len(a.shape))
    return pl.pallas_call(
        _prep_body,
        grid=(PREP_STEPS if cast else 1,),
        in_specs=[whole(a) for a in s5_in] + cast_specs,
        out_specs=[whole(a) for a in s5_shapes] + cast_specs,
        out_shape=s5_shapes + [jax.ShapeDtypeStruct(w.shape, BF16) for w in cast],
        scratch_shapes=[pltpu.VMEM((N_HALVES, U_HALF, 2 * S_HALF), F32)] * N_S5_SCRATCH,
        compiler_params=pltpu.CompilerParams(dimension_semantics=("arbitrary",)),
        name="prep",
    )(*s5_in, *cast)


def kernel(x_prompt, x_sample, cache_conv, state_ssm_re, state_ssm_im, norm_mix, w_in, conv_w, ssm_lam_re, ssm_lam_im, ssm_log_dt, ssm_b_re, ssm_b_im, ssm_c_re, ssm_c_im, ssm_d, w_glu, b_glu, w_out, norm_ffn, w_gate, w_up, w_down, norm_final):
    assert norm_mix.shape[0] == 1, "single-layer trunk"
    n_st = N_GROUPS * STATE_P
    lam2r, lam2i, bc2, cc, ccl, d0 = _prep(
        ssm_lam_re[0], ssm_lam_im[0], ssm_log_dt[0], ssm_b_re[0], ssm_b_im[0], ssm_c_re[0], ssm_c_im[0], cast=())

    def mixer_weights(w_in_, w_glu_, w_out_):
        return (norm_mix, w_in_, conv_w[0][:, None, :], lam2r, lam2i, bc2, cc, ccl, d0, ssm_d, w_glu_, b_glu,
                w_out_)

    bp, lp, _ = x_prompt.shape
    bs, ls, _ = x_sample.shape

    x1p, convp, rep, imp, wg, wu, wd, w_in_b, w_glu_b, w_out_b = _mixer(
        x_prompt, mixer_weights(w_in[0], w_glu[0], w_out[0]), cast=(w_gate[0], w_up[0], w_down[0]),
        nb=SUBLANES, tc=128)
    mix_w = mixer_weights(w_in_b, w_glu_b, w_out_b)
    ffn_w = (norm_ffn, wg, wu, wd, norm_final.reshape(1, D_MODEL))
    def state_major(h):
        return h[0].transpose(1, 2, 0).reshape(n_st, h.shape[1])

    sample_state = (cache_conv, state_major(state_ssm_re), state_major(state_ssm_im))
    x1s, convs, res, ims = _mixer(x_sample, mix_w, sample_state, nb=128, tc=SUBLANES)

    yp, ys = _ffn(x1p.reshape(bp * lp, D_MODEL), x1s.reshape(bs * ls, D_MODEL), ffn_w, tm=1024)
    yp = yp.reshape(bp, lp, D_MODEL)
    ys = ys.reshape(bs, ls, D_MODEL)

    def seq_major(h):
        return h.reshape(N_GROUPS, STATE_P, h.shape[1]).transpose(2, 0, 1)[None]

    return (yp, ys, convp, rep.reshape(1, bp, N_GROUPS, STATE_P), imp.reshape(1, bp, N_GROUPS, STATE_P),
            convs, seq_major(res), seq_major(ims))
```

```python
import functools

import jax
import jax.numpy as jnp
from jax import lax
from jax.experimental import pallas as pl
from jax.experimental.pallas import tpu as pltpu

D_MODEL = 1024
W_CONV = 512
W_SSM = 512
CONV_W = 3
N_GROUPS = 32
SSM_H = 16
STATE_P = 64
D_FF = 2816
EPS = 1e-5

SUBLANES = 8
BF16_ROWS = 16
GROUPS_PER_HALF = 16
N_HALVES = N_GROUPS // GROUPS_PER_HALF
U_HALF = GROUPS_PER_HALF * SSM_H
S_HALF = GROUPS_PER_HALF * STATE_P
S_COLS = 2 * N_GROUPS * STATE_P
SCAN_LANES = 512
MXU_COLS = 256
FFN_CHUNKS = tuple((s, MXU_COLS) for s in range(0, D_FF, MXU_COLS))
VMEM_LIMIT = 60 * 1024 * 1024

BF16 = jnp.bfloat16
F32 = jnp.float32


def _rms(x, g):
    y = x * lax.rsqrt(jnp.mean(x * x, axis=-1, keepdims=True) + EPS)
    return y * g


def _dot(a, b):
    return jnp.dot(a, b, preferred_element_type=F32)


def _split_bf16(a):
    hi = a.astype(BF16)
    return hi, (a - hi.astype(F32)).astype(BF16)


def _tile_copies(hbm, buf, sem, step, slot, *, nb, tc, per_seq, to_vmem):
    copies = []
    for k in range(SUBLANES):
        if per_seq:
            h = hbm.at[k, pl.ds(step * tc, tc), :]
            v = buf.at[slot, :, k, :]
        else:
            h = hbm.at[pl.ds(step * nb, nb), k, :]
            v = buf.at[slot, k]
        src, dst = (h, v) if to_vmem else (v, h)
        copies.append(pltpu.make_async_copy(src, dst, sem.at[slot, k]))
    return copies


def _mixer_body(x_hbm, *refs, nb, tc, n_steps, carry, n_cast):
    if carry:
        cache_ref = re0_ref = im0_ref = None
    else:
        cache_ref, re0_ref, im0_ref, *refs = refs
    (nmix_ref, win_ref, convw_ref, lam2r_ref, lam2i_ref, bc2_ref, cc_ref, ccl_ref, d0_ref, dskip_ref,
     wglu_ref, bglu_ref, wout_ref, *refs) = refs
    cast_in, refs = refs[:n_cast], refs[n_cast:]
    x1_hbm, convnew_hbm, hre_ref, him_ref, *refs = refs
    cast_out, refs = refs[:n_cast], refs[n_cast:]
    xbuf, obuf, gbuf, hb_odd, hb_prev, hstate, vbuf, insem, outsem, cachesem = refs
    tm = nb * tc
    n_pair = tc // 2
    tmh = n_pair * nb

    i = pl.program_id(0)
    slot = lax.rem(i, 2)
    cp = functools.partial(_tile_copies, nb=nb, tc=tc, per_seq=carry)
    in_copies = functools.partial(cp, x_hbm, xbuf, insem, to_vmem=True)
    out_copies = functools.partial(cp, x1_hbm, obuf, outsem, to_vmem=False)

    def cache_copies(hbm, row0, to_vmem):
        copies = []
        for k in range(CONV_W - 1):
            h = hbm.at[0, pl.ds(0 if carry else i * nb, nb), k, :]
            v = vbuf.at[pl.ds(row0 + k * nb, nb), :]
            src, dst = (h, v) if to_vmem else (v, h)
            copies.append(pltpu.make_async_copy(src, dst, cachesem.at[int(to_vmem), k]))
        return copies

    @pl.when(i == 0)
    def _():
        for c in in_copies(0, 0):
            c.start()

    if not carry:
        for c in cache_copies(cache_ref, 0, True):
            c.start()
    for c in in_copies(i, slot):
        c.wait()
    if not carry:
        for c in cache_copies(cache_ref, 0, True):
            c.wait()

    @pl.when(i + 1 < n_steps)
    def _():
        for c in in_copies(i + 1, 1 - slot):
            c.start()

    @pl.when(i >= 2)
    def _():
        for c in out_copies(i - 2, slot):
            c.wait()

    if carry:
        @pl.when(i == 0)
        def _():
            vbuf[0:2 * nb, :] = jnp.zeros((2 * nb, W_CONV), F32)
            hstate[...] = jnp.zeros((nb, S_COLS), F32)
    else:
        for n in range(N_HALVES):
            hstate[:, 2 * n * S_HALF:(2 * n + 1) * S_HALF] = re0_ref[n * S_HALF:(n + 1) * S_HALF, :].T
            hstate[:, (2 * n + 1) * S_HALF:(2 * n + 2) * S_HALF] = im0_ref[n * S_HALF:(n + 1) * S_HALF, :].T

    x = xbuf[slot].reshape(tm, D_MODEL)
    xn = _rms(x, nmix_ref[...]).astype(BF16)

    u = _dot(xn, win_ref[:, 3 * W_CONV:])
    u_pairs = u.reshape(n_pair, 2, nb, W_SSM)
    u_e = u_pairs[:, 0].reshape(tmh, W_SSM)
    u_o = u_pairs[:, 1].reshape(tmh, W_SSM)
    ub_e = u_e.astype(BF16)
    ub_o = u_o.astype(BF16)
    for n in range(N_HALVES):
        cols = slice(n * U_HALF, (n + 1) * U_HALF)
        gbuf[:, 2 * n * S_HALF:(2 * n + 2) * S_HALF] = _dot(
            jnp.concatenate([ub_e[:, cols], ub_o[:, cols]], axis=1), bc2_ref[n])

    for src, dst in zip(cast_in, cast_out):
        dst[...] = src[...].astype(BF16)

    def scan_step(h, rows, lr, li, re_cols, im_cols):
        hr, hi = h
        return (lr * hr - li * hi + gbuf[rows, re_cols], lr * hi + li * hr + gbuf[rows, im_cols])

    def store_pair(buf, row0, pair, re_cols, im_cols):
        rows = slice(row0, row0 + BF16_ROWS)
        buf[rows, re_cols] = jnp.concatenate([h[0] for h in pair], axis=0).astype(BF16)
        buf[rows, im_cols] = jnp.concatenate([h[1] for h in pair], axis=0).astype(BF16)

    for n in range(N_HALVES):
        for c0 in range(0, S_HALF, SCAN_LANES):
            re_cols = slice(2 * n * S_HALF + c0, 2 * n * S_HALF + c0 + SCAN_LANES)
            im_cols = slice((2 * n + 1) * S_HALF + c0, (2 * n + 1) * S_HALF + c0 + SCAN_LANES)
            lr = lam2r_ref[n, :, c0:c0 + SCAN_LANES]
            li = lam2i_ref[n, :, c0:c0 + SCAN_LANES]
            step = functools.partial(scan_step, lr=lr, li=li, re_cols=re_cols, im_cols=im_cols)
            if nb == SUBLANES:
                h = (hstate[:, re_cols], hstate[:, im_cols])
                for k in range(0, n_pair, 2):
                    h0 = step(h, slice(k * nb, (k + 1) * nb))
                    h1 = step(h0, slice((k + 1) * nb, (k + 2) * nb))
                    store_pair(hb_odd, k * nb, (h0, h1), re_cols, im_cols)
                    store_pair(hb_prev, k * nb, (h, h0), re_cols, im_cols)
                    h = h1
                hstate[:, re_cols], hstate[:, im_cols] = h
            else:
                for r0 in range(0, nb, BF16_ROWS):
                    seqs = [slice(r0 + j * SUBLANES, r0 + (j + 1) * SUBLANES) for j in range(2)]
                    pair = [(hstate[sq, re_cols], hstate[sq, im_cols]) for sq in seqs]
                    for k in range(n_pair):
                        store_pair(hb_prev, k * nb + r0, pair, re_cols, im_cols)
                        pair = [step(h, slice(k * nb + sq.start, k * nb + sq.stop)) for h, sq in zip(pair, seqs)]
                        store_pair(hb_odd, k * nb + r0, pair, re_cols, im_cols)
                    for h, sq in zip(pair, seqs):
                        hstate[sq, re_cols], hstate[sq, im_cols] = h

    conv_chunks = []
    for c0 in range(0, W_CONV, MXU_COLS):
        cols = slice(c0, c0 + MXU_COLS)
        b_g = _dot(xn, win_ref[:, c0:c0 + MXU_COLS])
        c_g = _dot(xn, win_ref[:, W_CONV + c0:W_CONV + c0 + MXU_COLS])
        xc = _dot(xn, win_ref[:, 2 * W_CONV + c0:2 * W_CONV + c0 + MXU_COLS])
        v = c_g * xc
        vbuf[2 * nb:2 * nb + tm, cols] = v
        conv = (convw_ref[0, :, cols] * vbuf[0:tm, cols] + convw_ref[1, :, cols] * vbuf[nb:nb + tm, cols]
                + convw_ref[2, :, cols] * v)
        conv_chunks.append((b_g * conv).astype(BF16))
    conv_out = jnp.concatenate(conv_chunks, axis=-1)
    if carry:
        vbuf[0:2 * nb, :] = vbuf[tm:tm + 2 * nb, :]
    x1 = x + _dot(conv_out, wout_ref[0:W_CONV, :])

    last = hstate[...]
    for n in range(N_HALVES):
        last_re = last[:, 2 * n * S_HALF:(2 * n + 1) * S_HALF]
        last_im = last[:, (2 * n + 1) * S_HALF:(2 * n + 2) * S_HALF]
        if carry:
            hre_ref[:, n * S_HALF:(n + 1) * S_HALF] = last_re
            him_ref[:, n * S_HALF:(n + 1) * S_HALF] = last_im
        else:
            hre_ref[n * S_HALF:(n + 1) * S_HALF, :] = last_re.T
            him_ref[n * S_HALF:(n + 1) * S_HALF, :] = last_im.T

    zs = []
    for n in range(N_HALVES):
        cols = slice(n * U_HALF, (n + 1) * U_HALF)
        st_cols = slice(2 * n * S_HALF, (2 * n + 2) * S_HALF)
        y_o = _dot(hb_odd[:, st_cols], cc_ref[n]) + dskip_ref[:, cols] * u_o[:, cols]
        y_e = (_dot(hb_prev[:, st_cols], ccl_ref[n]) + _dot(ub_e[:, cols], d0_ref[n])
               + dskip_ref[:, cols] * u_e[:, cols])
        y = jnp.stack([y_e.reshape(n_pair, nb, U_HALF), y_o.reshape(n_pair, nb, U_HALF)],
                      axis=1).reshape(tm, U_HALF)
        zs.append(jax.nn.gelu(y))
    zb = jnp.concatenate([z.astype(BF16) for z in zs], axis=-1)
    ssm_chunks = []
    for n, z in enumerate(zs):
        cols = slice(n * U_HALF, (n + 1) * U_HALF)
        gate = jax.nn.sigmoid(_dot(zb, wglu_ref[:, cols]) + bglu_ref[:, cols])
        ssm_chunks.append((z * gate).astype(BF16))
    ssm_out = jnp.concatenate(ssm_chunks, axis=-1)

    x1 = x1 + _dot(ssm_out, wout_ref[W_CONV:, :])

    obuf[slot] = x1.reshape(tc, nb, D_MODEL)
    for c in out_copies(i, slot):
        c.start()

    if not carry:
        for c in cache_copies(convnew_hbm, tm, False):
            c.start()
        for c in cache_copies(convnew_hbm, tm, False):
            c.wait()

    @pl.when(i == n_steps - 1)
    def _():
        if carry:
            for c in cache_copies(convnew_hbm, tm, False):
                c.start()
            for c in cache_copies(convnew_hbm, tm, False):
                c.wait()
        if n_steps >= 2:
            for c in out_copies(i - 1, 1 - slot):
                c.wait()
        for c in out_copies(i, slot):
            c.wait()


def _resident(shape):
    nd = len(shape)
    return pl.BlockSpec(shape, lambda i: (0,) * nd, pipeline_mode=pl.Buffered(1))


def _mixer(x, wts, state=None, cast=(), *, nb, tc):
    n_seq, n_time, _ = x.shape
    carry = state is None
    if carry:
        assert n_seq == nb == SUBLANES and n_time % tc == 0 and tc % 4 == 0
        n_steps = n_time // tc
    else:
        assert n_time == tc == SUBLANES and n_seq % nb == 0 and nb % BF16_ROWS == 0
        n_steps = n_seq // nb
    tm = nb * tc
    n_st = N_GROUPS * STATE_P
    body = functools.partial(_mixer_body, nb=nb, tc=tc, n_steps=n_steps, carry=carry, n_cast=len(cast))
    any_spec = pl.BlockSpec(memory_space=pl.ANY)
    if carry:
        st_spec = pl.BlockSpec((nb, n_st), lambda i: (0, 0))
        st_shape = jax.ShapeDtypeStruct((n_seq, n_st), F32)
    else:
        st_spec = pl.BlockSpec((n_st, nb), lambda i: (0, i))
        st_shape = jax.ShapeDtypeStruct((n_st, n_seq), F32)
    state_specs = [] if carry else [any_spec, st_spec, st_spec]
    for w in cast:
        assert w.shape[0] % (n_steps * BF16_ROWS) == 0, w.shape
    cast_specs = [pl.BlockSpec((w.shape[0] // n_steps, w.shape[1]), lambda i: (i, 0)) for w in cast]
    return pl.pallas_call(
        body,
        grid=(n_steps,),
        in_specs=[any_spec] + state_specs + [_resident(w.shape) for w in wts] + cast_specs,
        out_specs=[any_spec, any_spec, st_spec, st_spec] + cast_specs,
        out_shape=[jax.ShapeDtypeStruct(x.shape, F32),
                   jax.ShapeDtypeStruct((1, n_seq, CONV_W - 1, W_CONV), F32),
                   st_shape, st_shape]
                  + [jax.ShapeDtypeStruct(w.shape, BF16) for w in cast],
        scratch_shapes=[pltpu.VMEM((2, tc, nb, D_MODEL), F32),
                        pltpu.VMEM((2, tc, nb, D_MODEL), F32),
                        pltpu.VMEM((tm // 2, S_COLS), F32),
                        pltpu.VMEM((tm // 2, S_COLS), BF16),
                        pltpu.VMEM((tm // 2, S_COLS), BF16),
                        pltpu.VMEM((nb, S_COLS), F32),
                        pltpu.VMEM((2 * nb + tm, W_CONV), F32),
                        pltpu.SemaphoreType.DMA((2, SUBLANES)),
                        pltpu.SemaphoreType.DMA((2, SUBLANES)),
                        pltpu.SemaphoreType.DMA((2, CONV_W - 1))],
        compiler_params=pltpu.CompilerParams(dimension_semantics=("arbitrary",),
                                             vmem_limit_bytes=VMEM_LIMIT),
        name="mixer_carry" if carry else "mixer_step",
    )(x, *(() if carry else state), *wts, *cast)


def _ffn_body(xp_hbm, xs_hbm, nffn_ref, wg_ref, wu_ref, wd_ref, nfin_ref, yp_hbm, ys_hbm,
              xbuf, obuf, abuf, insem, outsem, *, n_p, n_s, tm):
    i = pl.program_id(0)
    n_steps = n_p + n_s
    slot = lax.rem(i, 2)

    def tile_copy(step, buf_slot, to_vmem, first):
        hbm = (xp_hbm if first else xs_hbm) if to_vmem else (yp_hbm if first else ys_hbm)
        h = hbm.at[pl.ds((step if first else step - n_p) * tm, tm), :]
        if to_vmem:
            return pltpu.make_async_copy(h, xbuf.at[buf_slot], insem.at[buf_slot])
        return pltpu.make_async_copy(obuf.at[buf_slot], h, outsem.at[buf_slot])

    def start(step, buf_slot, to_vmem):
        pl.when(step < n_p)(lambda: tile_copy(step, buf_slot, to_vmem, True).start())
        pl.when(step >= n_p)(lambda: tile_copy(step, buf_slot, to_vmem, False).start())

    def wait(buf_slot, to_vmem):
        tile_copy(0, buf_slot, to_vmem, True).wait()

    pl.when(i == 0)(lambda: start(0, 0, True))
    wait(slot, True)
    pl.when(i + 1 < n_steps)(lambda: start(i + 1, 1 - slot, True))
    pl.when(i >= 2)(lambda: wait(slot, False))

    x = xbuf[slot]
    hn = _rms(x, nffn_ref[...]).astype(BF16)
    for s, w in FFN_CHUNKS:
        g = _dot(hn, wg_ref[:, s:s + w])
        up = _dot(hn, wu_ref[:, s:s + w])
        abuf[:, s:s + w] = (g * jax.nn.sigmoid(g) * up).astype(BF16)
    acc = x + _dot(abuf[...], wd_ref[...])
    obuf[slot] = _rms(acc, nfin_ref[...])
    start(i, slot, False)

    @pl.when(i == n_steps - 1)
    def _():
        if n_steps >= 2:
            wait(1 - slot, False)
        wait(slot, False)


def _ffn(xp, xs, wts, *, tm):
    n_p, n_s = xp.shape[0] // tm, xs.shape[0] // tm
    assert xp.shape[0] == n_p * tm and xs.shape[0] == n_s * tm
    any_spec = pl.BlockSpec(memory_space=pl.ANY)
    return pl.pallas_call(
        functools.partial(_ffn_body, n_p=n_p, n_s=n_s, tm=tm),
        grid=(n_p + n_s,),
        in_specs=[any_spec, any_spec] + [_resident(w.shape) for w in wts],
        out_specs=[any_spec, any_spec],
        out_shape=[jax.ShapeDtypeStruct(xp.shape, F32), jax.ShapeDtypeStruct(xs.shape, F32)],
        scratch_shapes=[pltpu.VMEM((2, tm, D_MODEL), F32),
                        pltpu.VMEM((2, tm, D_MODEL), F32),
                        pltpu.VMEM((tm, D_FF), BF16),
                        pltpu.SemaphoreType.DMA((2,)),
                        pltpu.SemaphoreType.DMA((2,))],
        compiler_params=pltpu.CompilerParams(dimension_semantics=("arbitrary",),
                                             vmem_limit_bytes=VMEM_LIMIT),
        name="ffn",
    )(xp, xs, *wts)


N_S5_OUT = 6
N_S5_SCRATCH = 4


def _prep_body(lre_ref, lim_ref, logdt_ref, btre_ref, btim_ref, cre_ref, cim_ref, *refs):
    n_cast = (len(refs) - N_S5_OUT - N_S5_SCRATCH) // 2
    cast_in, refs = refs[:n_cast], refs[n_cast:]
    s5_out, refs = refs[:N_S5_OUT], refs[N_S5_OUT:]
    cast_out, s5_scratch = refs[:n_cast], refs[n_cast:]
    for src, dst in zip(cast_in, cast_out):
        dst[...] = src[...].astype(BF16)
    pl.when(pl.program_id(0) == 0)(functools.partial(
        _s5_matrices, lre_ref, lim_ref, logdt_ref, btre_ref, btim_ref, cre_ref, cim_ref, *s5_out, *s5_scratch))


def _s5_matrices(lre_ref, lim_ref, logdt_ref, btre_ref, btim_ref, cre_ref, cim_ref,
                 lam2r_ref, lam2i_ref, bc2_ref, cc_ref, ccl_ref, d0_ref, bscr, lbscr, cscr, clscr):
    lre = lre_ref[...]
    lim = lim_ref[...]
    grp = (lax.broadcasted_iota(jnp.int32, (N_GROUPS, N_GROUPS), 0)
           == lax.broadcasted_iota(jnp.int32, (N_GROUPS, N_GROUPS), 1))
    logdt = jnp.sum(jnp.where(grp, jnp.broadcast_to(logdt_ref[...], (N_GROUPS, N_GROUPS)), 0.0),
                    axis=1, keepdims=True)
    dt = jnp.exp(logdt)
    mag = jnp.exp(lre * dt)
    lbr = mag * jnp.cos(lim * dt)
    lbi = mag * jnp.sin(lim * dt)
    l2r = lbr * lbr - lbi * lbi
    l2i = 2.0 * (lbr * lbi)
    den = lre * lre + lim * lim
    fr = ((lbr - 1.0) * lre + lbi * lim) / den
    fi = (lbi * lre - (lbr - 1.0) * lim) / den
    frb, fib = fr[:, None, :], fi[:, None, :]
    lrb, lib = lbr[:, None, :], lbi[:, None, :]
    bbr = frb * btre_ref[...] - fib * btim_ref[...]
    bbi = frb * btim_ref[...] + fib * btre_ref[...]
    lbbr = lrb * bbr - lib * bbi
    lbbi = lrb * bbi + lib * bbr
    cre = cre_ref[...]
    cim = cim_ref[...]
    clr = cre * lrb - cim * lib
    cli = cre * lib + cim * lrb

    for scr in (bscr, lbscr, cscr, clscr):
        scr[...] = jnp.zeros_like(scr)
    for g in range(N_GROUPS):
        n, k = divmod(g, GROUPS_PER_HALF)
        chans = slice(k * SSM_H, (k + 1) * SSM_H)
        states = slice(k * STATE_P, (k + 1) * STATE_P)
        im_states = slice(S_HALF + k * STATE_P, S_HALF + (k + 1) * STATE_P)
        for scr, re, im in ((bscr, bbr, bbi), (lbscr, lbbr, lbbi), (cscr, cre, -cim), (clscr, clr, -cli)):
            scr[n, chans, states] = re[g]
            scr[n, chans, im_states] = im[g]
        lam2r_ref[n, :, states] = jnp.broadcast_to(l2r[g:g + 1, :], (SUBLANES, STATE_P))
        lam2i_ref[n, :, states] = jnp.broadcast_to(l2i[g:g + 1, :], (SUBLANES, STATE_P))
    for n in range(N_HALVES):
        bc2_ref[n, 0:U_HALF, :] = lbscr[n].astype(BF16)
        bc2_ref[n, U_HALF:, :] = bscr[n].astype(BF16)
        c_t = cscr[n].T
        cc_ref[n] = c_t.astype(BF16)
        ccl_ref[n] = clscr[n].T.astype(BF16)
        b_hi, b_lo = _split_bf16(bscr[n])
        c_hi, c_lo = _split_bf16(c_t)
        d0_ref[n] = (_dot(b_hi, c_hi) + _dot(b_hi, c_lo) + _dot(b_lo, c_hi)).astype(BF16)


PREP_STEPS = 2


def _prep(lam_re, lam_im, log_dt, b_re, b_im, c_re, c_im, cast):
    s5_in = (lam_re, lam_im, log_dt.reshape(1, N_GROUPS), b_re.transpose(0, 2, 1), b_im.transpose(0, 2, 1),
             c_re, c_im)
    s5_shapes = [jax.ShapeDtypeStruct((N_HALVES, SUBLANES, S_HALF), F32),
                 jax.ShapeDtypeStruct((N_HALVES, SUBLANES, S_HALF), F32),
                 jax.ShapeDtypeStruct((N_HALVES, 2 * U_HALF, 2 * S_HALF), BF16),
                 jax.ShapeDtypeStruct((N_HALVES, 2 * S_HALF, U_HALF), BF16),
                 jax.ShapeDtypeStruct((N_HALVES, 2 * S_HALF, U_HALF), BF16),
                 jax.ShapeDtypeStruct((N_HALVES, U_HALF, U_HALF), BF16)]
    assert len(s5_shapes) == N_S5_OUT
    for w in cast:
        assert w.shape[0] % (PREP_STEPS * BF16_ROWS) == 0, w.shape
    cast_specs = [pl.BlockSpec((w.shape[0] // PREP_STEPS, w.shape[1]), lambda i: (i, 0)) for w in cast]
    whole = lambda a: pl.BlockSpec(a.shape, lambda i: (0,) * len(a.shape))
    return pl.pallas_call(
        _prep_body,
        grid=(PREP_STEPS,),
        in_specs=[whole(a) for a in s5_in] + cast_specs,
        out_specs=[whole(a) for a in s5_shapes] + cast_specs,
        out_shape=s5_shapes + [jax.ShapeDtypeStruct(w.shape, BF16) for w in cast],
        scratch_shapes=[pltpu.VMEM((N_HALVES, U_HALF, 2 * S_HALF), F32)] * N_S5_SCRATCH,
        compiler_params=pltpu.CompilerParams(dimension_semantics=("arbitrary",)),
        name="prep",
    )(*s5_in, *cast)


def kernel(x_prompt, x_sample, cache_conv, state_ssm_re, state_ssm_im, norm_mix, w_in, conv_w, ssm_lam_re, ssm_lam_im, ssm_log_dt, ssm_b_re, ssm_b_im, ssm_c_re, ssm_c_im, ssm_d, w_glu, b_glu, w_out, norm_ffn, w_gate, w_up, w_down, norm_final):
    assert norm_mix.shape[0] == 1, "single-layer trunk"
    n_st = N_GROUPS * STATE_P
    lam2r, lam2i, bc2, cc, ccl, d0, w_in_b, w_glu_b, w_out_b = _prep(
        ssm_lam_re[0], ssm_lam_im[0], ssm_log_dt[0], ssm_b_re[0], ssm_b_im[0], ssm_c_re[0], ssm_c_im[0],
        cast=(w_in[0], w_glu[0], w_out[0]))
    mix_w = (norm_mix, w_in_b, conv_w[0][:, None, :], lam2r, lam2i, bc2, cc, ccl, d0, ssm_d, w_glu_b, b_glu,
             w_out_b)

    bp, lp, _ = x_prompt.shape
    bs, ls, _ = x_sample.shape

    x1p, convp, rep, imp, wg, wu, wd = _mixer(x_prompt, mix_w, cast=(w_gate[0], w_up[0], w_down[0]),
                                              nb=SUBLANES, tc=128)
    ffn_w = (norm_ffn, wg, wu, wd, norm_final.reshape(1, D_MODEL))
    def state_major(h):
        return h[0].transpose(1, 2, 0).reshape(n_st, h.shape[1])

    sample_state = (cache_conv, state_major(state_ssm_re), state_major(state_ssm_im))
    x1s, convs, res, ims = _mixer(x_sample, mix_w, sample_state, nb=128, tc=SUBLANES)

    yp, ys = _ffn(x1p.reshape(bp * lp, D_MODEL), x1s.reshape(bs * ls, D_MODEL), ffn_w, tm=1024)
    yp = yp.reshape(bp, lp, D_MODEL)
    ys = ys.reshape(bs, ls, D_MODEL)

    def seq_major(h):
        return h.reshape(N_GROUPS, STATE_P, h.shape[1]).transpose(2, 0, 1)[None]

    return (yp, ys, convp, rep.reshape(1, bp, N_GROUPS, STATE_P), imp.reshape(1, bp, N_GROUPS, STATE_P),
            convs, seq_major(res), seq_major(ims))
```

```python
import functools

import jax
import jax.numpy as jnp
from jax import lax
from jax.experimental import pallas as pl
from jax.experimental.pallas import tpu as pltpu

D_MODEL = 1024
W_CONV = 512
W_SSM = 512
CONV_W = 3
N_GROUPS = 32
SSM_H = 16
STATE_P = 64
D_FF = 2816
EPS = 1e-5

SUBLANES = 8
BF16_ROWS = 16
GROUPS_PER_HALF = 16
N_HALVES = N_GROUPS // GROUPS_PER_HALF
U_HALF = GROUPS_PER_HALF * SSM_H
S_HALF = GROUPS_PER_HALF * STATE_P
S_COLS = 2 * N_GROUPS * STATE_P
SCAN_LANES = 512
MXU_COLS = 256
FFN_CHUNKS = tuple((s, MXU_COLS) for s in range(0, D_FF, MXU_COLS))
VMEM_LIMIT = 60 * 1024 * 1024

BF16 = jnp.bfloat16
F32 = jnp.float32


def _rms(x, g):
    y = x * lax.rsqrt(jnp.mean(x * x, axis=-1, keepdims=True) + EPS)
    return y * g


def _dot(a, b):
    return jnp.dot(a, b, preferred_element_type=F32)


def _split_bf16(a):
    hi = a.astype(BF16)
    return hi, (a - hi.astype(F32)).astype(BF16)


def _tile_copies(hbm, buf, sem, step, slot, *, nb, tc, per_seq, to_vmem):
    copies = []
    for k in range(SUBLANES):
        if per_seq:
            h = hbm.at[k, pl.ds(step * tc, tc), :]
            v = buf.at[slot, :, k, :]
        else:
            h = hbm.at[pl.ds(step * nb, nb), k, :]
            v = buf.at[slot, k]
        src, dst = (h, v) if to_vmem else (v, h)
        copies.append(pltpu.make_async_copy(src, dst, sem.at[slot, k]))
    return copies


def _mixer_body(x_hbm, *refs, nb, tc, n_steps, carry, n_cast):
    if carry:
        cache_ref = re0_ref = im0_ref = None
    else:
        cache_ref, re0_ref, im0_ref, *refs = refs
    (nmix_ref, win_ref, convw_ref, lam2r_ref, lam2i_ref, bc2_ref, cc_ref, ccl_ref, d0_ref, dskip_ref,
     wglu_ref, bglu_ref, wout_ref, *refs) = refs
    cast_in, refs = refs[:n_cast], refs[n_cast:]
    x1_hbm, convnew_hbm, hre_ref, him_ref, *refs = refs
    cast_out, refs = refs[:n_cast], refs[n_cast:]
    xbuf, obuf, gbuf, hb_odd, hb_prev, hstate, vbuf, insem, outsem, cachesem = refs
    tm = nb * tc
    n_pair = tc // 2
    tmh = n_pair * nb

    i = pl.program_id(0)
    slot = lax.rem(i, 2)
    cp = functools.partial(_tile_copies, nb=nb, tc=tc, per_seq=carry)
    in_copies = functools.partial(cp, x_hbm, xbuf, insem, to_vmem=True)
    out_copies = functools.partial(cp, x1_hbm, obuf, outsem, to_vmem=False)

    def cache_copies(hbm, row0, to_vmem):
        copies = []
        for k in range(CONV_W - 1):
            h = hbm.at[0, pl.ds(0 if carry else i * nb, nb), k, :]
            v = vbuf.at[pl.ds(row0 + k * nb, nb), :]
            src, dst = (h, v) if to_vmem else (v, h)
            copies.append(pltpu.make_async_copy(src, dst, cachesem.at[int(to_vmem), k]))
        return copies

    @pl.when(i == 0)
    def _():
        for c in in_copies(0, 0):
            c.start()

    if not carry:
        for c in cache_copies(cache_ref, 0, True):
            c.start()
    for c in in_copies(i, slot):
        c.wait()
    if not carry:
        for c in cache_copies(cache_ref, 0, True):
            c.wait()

    @pl.when(i + 1 < n_steps)
    def _():
        for c in in_copies(i + 1, 1 - slot):
            c.start()

    @pl.when(i >= 2)
    def _():
        for c in out_copies(i - 2, slot):
            c.wait()

    if carry:
        @pl.when(i == 0)
        def _():
            vbuf[0:2 * nb, :] = jnp.zeros((2 * nb, W_CONV), F32)
            hstate[...] = jnp.zeros((nb, S_COLS), F32)
    else:
        for n in range(N_HALVES):
            hstate[:, 2 * n * S_HALF:(2 * n + 1) * S_HALF] = re0_ref[n * S_HALF:(n + 1) * S_HALF, :].T
            hstate[:, (2 * n + 1) * S_HALF:(2 * n + 2) * S_HALF] = im0_ref[n * S_HALF:(n + 1) * S_HALF, :].T

    x = xbuf[slot].reshape(tm, D_MODEL)
    inv_rms = lax.rsqrt(jnp.mean(x * x, axis=-1, keepdims=True) + EPS)
    xg = (x * nmix_ref[...]).astype(BF16)
    xn = (x * inv_rms * nmix_ref[...]).astype(BF16)

    u = inv_rms * _dot(xg, win_ref[:, 3 * W_CONV:])
    u_pairs = u.reshape(n_pair, 2, nb, W_SSM)
    u_e = u_pairs[:, 0].reshape(tmh, W_SSM)
    u_o = u_pairs[:, 1].reshape(tmh, W_SSM)
    ub_e = u_e.astype(BF16)
    ub_o = u_o.astype(BF16)
    for n in range(N_HALVES):
        cols = slice(n * U_HALF, (n + 1) * U_HALF)
        gbuf[:, 2 * n * S_HALF:(2 * n + 2) * S_HALF] = _dot(
            jnp.concatenate([ub_e[:, cols], ub_o[:, cols]], axis=1), bc2_ref[n])

    for src, dst in zip(cast_in, cast_out):
        dst[...] = src[...].astype(BF16)

    def scan_step(h, rows, lr, li, re_cols, im_cols):
        hr, hi = h
        return (lr * hr - li * hi + gbuf[rows, re_cols], lr * hi + li * hr + gbuf[rows, im_cols])

    def store_pair(buf, row0, pair, re_cols, im_cols):
        rows = slice(row0, row0 + BF16_ROWS)
        buf[rows, re_cols] = jnp.concatenate([h[0] for h in pair], axis=0).astype(BF16)
        buf[rows, im_cols] = jnp.concatenate([h[1] for h in pair], axis=0).astype(BF16)

    for n in range(N_HALVES):
        for c0 in range(0, S_HALF, SCAN_LANES):
            re_cols = slice(2 * n * S_HALF + c0, 2 * n * S_HALF + c0 + SCAN_LANES)
            im_cols = slice((2 * n + 1) * S_HALF + c0, (2 * n + 1) * S_HALF + c0 + SCAN_LANES)
            lr = lam2r_ref[n, :, c0:c0 + SCAN_LANES]
            li = lam2i_ref[n, :, c0:c0 + SCAN_LANES]
            step = functools.partial(scan_step, lr=lr, li=li, re_cols=re_cols, im_cols=im_cols)
            if nb == SUBLANES:
                h = (hstate[:, re_cols], hstate[:, im_cols])
                for k in range(0, n_pair, 2):
                    h0 = step(h, slice(k * nb, (k + 1) * nb))
                    h1 = step(h0, slice((k + 1) * nb, (k + 2) * nb))
                    store_pair(hb_odd, k * nb, (h0, h1), re_cols, im_cols)
                    store_pair(hb_prev, k * nb, (h, h0), re_cols, im_cols)
                    h = h1
                hstate[:, re_cols], hstate[:, im_cols] = h
            else:
                for r0 in range(0, nb, BF16_ROWS):
                    seqs = [slice(r0 + j * SUBLANES, r0 + (j + 1) * SUBLANES) for j in range(2)]
                    pair = [(hstate[sq, re_cols], hstate[sq, im_cols]) for sq in seqs]
                    for k in range(n_pair):
                        store_pair(hb_prev, k * nb + r0, pair, re_cols, im_cols)
                        pair = [step(h, slice(k * nb + sq.start, k * nb + sq.stop)) for h, sq in zip(pair, seqs)]
                        store_pair(hb_odd, k * nb + r0, pair, re_cols, im_cols)
                    for h, sq in zip(pair, seqs):
                        hstate[sq, re_cols], hstate[sq, im_cols] = h

    conv_chunks = []
    for c0 in range(0, W_CONV, MXU_COLS):
        cols = slice(c0, c0 + MXU_COLS)
        b_g = _dot(xn, win_ref[:, c0:c0 + MXU_COLS])
        c_g = _dot(xn, win_ref[:, W_CONV + c0:W_CONV + c0 + MXU_COLS])
        xc = _dot(xn, win_ref[:, 2 * W_CONV + c0:2 * W_CONV + c0 + MXU_COLS])
        v = c_g * xc
        vbuf[2 * nb:2 * nb + tm, cols] = v
        conv = (convw_ref[0, :, cols] * vbuf[0:tm, cols] + convw_ref[1, :, cols] * vbuf[nb:nb + tm, cols]
                + convw_ref[2, :, cols] * v)
        conv_chunks.append((b_g * conv).astype(BF16))
    conv_out = jnp.concatenate(conv_chunks, axis=-1)
    if carry:
        vbuf[0:2 * nb, :] = vbuf[tm:tm + 2 * nb, :]
    x1 = x + _dot(conv_out, wout_ref[0:W_CONV, :])

    last = hstate[...]
    for n in range(N_HALVES):
        last_re = last[:, 2 * n * S_HALF:(2 * n + 1) * S_HALF]
        last_im = last[:, (2 * n + 1) * S_HALF:(2 * n + 2) * S_HALF]
        if carry:
            hre_ref[:, n * S_HALF:(n + 1) * S_HALF] = last_re
            him_ref[:, n * S_HALF:(n + 1) * S_HALF] = last_im
        else:
            hre_ref[n * S_HALF:(n + 1) * S_HALF, :] = last_re.T
            him_ref[n * S_HALF:(n + 1) * S_HALF, :] = last_im.T

    zs = []
    for n in range(N_HALVES):
        cols = slice(n * U_HALF, (n + 1) * U_HALF)
        st_cols = slice(2 * n * S_HALF, (2 * n + 2) * S_HALF)
        y_o = _dot(hb_odd[:, st_cols], cc_ref[n]) + dskip_ref[:, cols] * u_o[:, cols]
        y_e = (_dot(hb_prev[:, st_cols], ccl_ref[n]) + _dot(ub_e[:, cols], d0_ref[n])
               + dskip_ref[:, cols] * u_e[:, cols])
        y = jnp.stack([y_e.reshape(n_pair, nb, U_HALF), y_o.reshape(n_pair, nb, U_HALF)],
                      axis=1).reshape(tm, U_HALF)
        zs.append(jax.nn.gelu(y))
    zb = jnp.concatenate([z.astype(BF16) for z in zs], axis=-1)
    ssm_chunks = []
    for n, z in enumerate(zs):
        cols = slice(n * U_HALF, (n + 1) * U_HALF)
        gate = jax.nn.sigmoid(_dot(zb, wglu_ref[:, cols]) + bglu_ref[:, cols])
        ssm_chunks.append((z * gate).astype(BF16))
    ssm_out = jnp.concatenate(ssm_chunks, axis=-1)

    x1 = x1 + _dot(ssm_out, wout_ref[W_CONV:, :])

    obuf[slot] = x1.reshape(tc, nb, D_MODEL)
    for c in out_copies(i, slot):
        c.start()

    if not carry:
        for c in cache_copies(convnew_hbm, tm, False):
            c.start()
        for c in cache_copies(convnew_hbm, tm, False):
            c.wait()

    @pl.when(i == n_steps - 1)
    def _():
        if carry:
            for c in cache_copies(convnew_hbm, tm, False):
                c.start()
            for c in cache_copies(convnew_hbm, tm, False):
                c.wait()
        if n_steps >= 2:
            for c in out_copies(i - 1, 1 - slot):
                c.wait()
        for c in out_copies(i, slot):
            c.wait()


def _resident(shape):
    nd = len(shape)
    return pl.BlockSpec(shape, lambda i: (0,) * nd, pipeline_mode=pl.Buffered(1))


def _mixer(x, wts, state=None, cast=(), *, nb, tc):
    n_seq, n_time, _ = x.shape
    carry = state is None
    if carry:
        assert n_seq == nb == SUBLANES and n_time % tc == 0 and tc % 4 == 0
        n_steps = n_time // tc
    else:
        assert n_time == tc == SUBLANES and n_seq % nb == 0 and nb % BF16_ROWS == 0
        n_steps = n_seq // nb
    tm = nb * tc
    n_st = N_GROUPS * STATE_P
    body = functools.partial(_mixer_body, nb=nb, tc=tc, n_steps=n_steps, carry=carry, n_cast=len(cast))
    any_spec = pl.BlockSpec(memory_space=pl.ANY)
    if carry:
        st_spec = pl.BlockSpec((nb, n_st), lambda i: (0, 0))
        st_shape = jax.ShapeDtypeStruct((n_seq, n_st), F32)
    else:
        st_spec = pl.BlockSpec((n_st, nb), lambda i: (0, i))
        st_shape = jax.ShapeDtypeStruct((n_st, n_seq), F32)
    state_specs = [] if carry else [any_spec, st_spec, st_spec]
    for w in cast:
        assert w.shape[0] % (n_steps * BF16_ROWS) == 0, w.shape
    cast_specs = [pl.BlockSpec((w.shape[0] // n_steps, w.shape[1]), lambda i: (i, 0)) for w in cast]
    return pl.pallas_call(
        body,
        grid=(n_steps,),
        in_specs=[any_spec] + state_specs + [_resident(w.shape) for w in wts] + cast_specs,
        out_specs=[any_spec, any_spec, st_spec, st_spec] + cast_specs,
        out_shape=[jax.ShapeDtypeStruct(x.shape, F32),
                   jax.ShapeDtypeStruct((1, n_seq, CONV_W - 1, W_CONV), F32),
                   st_shape, st_shape]
                  + [jax.ShapeDtypeStruct(w.shape, BF16) for w in cast],
        scratch_shapes=[pltpu.VMEM((2, tc, nb, D_MODEL), F32),
                        pltpu.VMEM((2, tc, nb, D_MODEL), F32),
                        pltpu.VMEM((tm // 2, S_COLS), F32),
                        pltpu.VMEM((tm // 2, S_COLS), BF16),
                        pltpu.VMEM((tm // 2, S_COLS), BF16),
                        pltpu.VMEM((nb, S_COLS), F32),
                        pltpu.VMEM((2 * nb + tm, W_CONV), F32),
                        pltpu.SemaphoreType.DMA((2, SUBLANES)),
                        pltpu.SemaphoreType.DMA((2, SUBLANES)),
                        pltpu.SemaphoreType.DMA((2, CONV_W - 1))],
        compiler_params=pltpu.CompilerParams(dimension_semantics=("arbitrary",),
                                             vmem_limit_bytes=VMEM_LIMIT),
        name="mixer_carry" if carry else "mixer_step",
    )(x, *(() if carry else state), *wts, *cast)


def _ffn_body(xp_hbm, xs_hbm, nffn_ref, wg_ref, wu_ref, wd_ref, nfin_ref, yp_hbm, ys_hbm,
              xbuf, obuf, abuf, insem, outsem, *, n_p, n_s, tm):
    i = pl.program_id(0)
    n_steps = n_p + n_s
    slot = lax.rem(i, 2)

    def tile_copy(step, buf_slot, to_vmem, first):
        hbm = (xp_hbm if first else xs_hbm) if to_vmem else (yp_hbm if first else ys_hbm)
        h = hbm.at[pl.ds((step if first else step - n_p) * tm, tm), :]
        if to_vmem:
            return pltpu.make_async_copy(h, xbuf.at[buf_slot], insem.at[buf_slot])
        return pltpu.make_async_copy(obuf.at[buf_slot], h, outsem.at[buf_slot])

    def start(step, buf_slot, to_vmem):
        pl.when(step < n_p)(lambda: tile_copy(step, buf_slot, to_vmem, True).start())
        pl.when(step >= n_p)(lambda: tile_copy(step, buf_slot, to_vmem, False).start())

    def wait(buf_slot, to_vmem):
        tile_copy(0, buf_slot, to_vmem, True).wait()

    pl.when(i == 0)(lambda: start(0, 0, True))
    wait(slot, True)
    pl.when(i + 1 < n_steps)(lambda: start(i + 1, 1 - slot, True))
    pl.when(i >= 2)(lambda: wait(slot, False))

    x = xbuf[slot]
    hn = _rms(x, nffn_ref[...]).astype(BF16)
    for s, w in FFN_CHUNKS:
        g = _dot(hn, wg_ref[:, s:s + w])
        up = _dot(hn, wu_ref[:, s:s + w])
        abuf[:, s:s + w] = (g * jax.nn.sigmoid(g) * up).astype(BF16)
    acc = x + _dot(abuf[...], wd_ref[...])
    obuf[slot] = _rms(acc, nfin_ref[...])
    start(i, slot, False)

    @pl.when(i == n_steps - 1)
    def _():
        if n_steps >= 2:
            wait(1 - slot, False)
        wait(slot, False)


def _ffn(xp, xs, wts, *, tm):
    n_p, n_s = xp.shape[0] // tm, xs.shape[0] // tm
    assert xp.shape[0] == n_p * tm and xs.shape[0] == n_s * tm
    any_spec = pl.BlockSpec(memory_space=pl.ANY)
    return pl.pallas_call(
        functools.partial(_ffn_body, n_p=n_p, n_s=n_s, tm=tm),
        grid=(n_p + n_s,),
        in_specs=[any_spec, any_spec] + [_resident(w.shape) for w in wts],
        out_specs=[any_spec, any_spec],
        out_shape=[jax.ShapeDtypeStruct(xp.shape, F32), jax.ShapeDtypeStruct(xs.shape, F32)],
        scratch_shapes=[pltpu.VMEM((2, tm, D_MODEL), F32),
                        pltpu.VMEM((2, tm, D_MODEL), F32),
                        pltpu.VMEM((tm, D_FF), BF16),
                        pltpu.SemaphoreType.DMA((2,)),
                        pltpu.SemaphoreType.DMA((2,))],
        compiler_params=pltpu.CompilerParams(dimension_semantics=("arbitrary",),
                                             vmem_limit_bytes=VMEM_LIMIT),
        name="ffn",
    )(xp, xs, *wts)


N_S5_OUT = 6
N_S5_SCRATCH = 4


def _prep_body(lre_ref, lim_ref, logdt_ref, btre_ref, btim_ref, cre_ref, cim_ref, *refs):
    n_cast = (len(refs) - N_S5_OUT - N_S5_SCRATCH) // 2
    cast_in, refs = refs[:n_cast], refs[n_cast:]
    s5_out, refs = refs[:N_S5_OUT], refs[N_S5_OUT:]
    cast_out, s5_scratch = refs[:n_cast], refs[n_cast:]
    for src, dst in zip(cast_in, cast_out):
        dst[...] = src[...].astype(BF16)
    pl.when(pl.program_id(0) == 0)(functools.partial(
        _s5_matrices, lre_ref, lim_ref, logdt_ref, btre_ref, btim_ref, cre_ref, cim_ref, *s5_out, *s5_scratch))


def _s5_matrices(lre_ref, lim_ref, logdt_ref, btre_ref, btim_ref, cre_ref, cim_ref,
                 lam2r_ref, lam2i_ref, bc2_ref, cc_ref, ccl_ref, d0_ref, bscr, lbscr, cscr, clscr):
    lre = lre_ref[...]
    lim = lim_ref[...]
    grp = (lax.broadcasted_iota(jnp.int32, (N_GROUPS, N_GROUPS), 0)
           == lax.broadcasted_iota(jnp.int32, (N_GROUPS, N_GROUPS), 1))
    logdt = jnp.sum(jnp.where(grp, jnp.broadcast_to(logdt_ref[...], (N_GROUPS, N_GROUPS)), 0.0),
                    axis=1, keepdims=True)
    dt = jnp.exp(logdt)
    mag = jnp.exp(lre * dt)
    lbr = mag * jnp.cos(lim * dt)
    lbi = mag * jnp.sin(lim * dt)
    l2r = lbr * lbr - lbi * lbi
    l2i = 2.0 * (lbr * lbi)
    den = lre * lre + lim * lim
    fr = ((lbr - 1.0) * lre + lbi * lim) / den
    fi = (lbi * lre - (lbr - 1.0) * lim) / den
    frb, fib = fr[:, None, :], fi[:, None, :]
    lrb, lib = lbr[:, None, :], lbi[:, None, :]
    bbr = frb * btre_ref[...] - fib * btim_ref[...]
    bbi = frb * btim_ref[...] + fib * btre_ref[...]
    lbbr = lrb * bbr - lib * bbi
    lbbi = lrb * bbi + lib * bbr
    cre = cre_ref[...]
    cim = cim_ref[...]
    clr = cre * lrb - cim * lib
    cli = cre * lib + cim * lrb

    for scr in (bscr, lbscr, cscr, clscr):
        scr[...] = jnp.zeros_like(scr)
    for g in range(N_GROUPS):
        n, k = divmod(g, GROUPS_PER_HALF)
        chans = slice(k * SSM_H, (k + 1) * SSM_H)
        states = slice(k * STATE_P, (k + 1) * STATE_P)
        im_states = slice(S_HALF + k * STATE_P, S_HALF + (k + 1) * STATE_P)
        for scr, re, im in ((bscr, bbr, bbi), (lbscr, lbbr, lbbi), (cscr, cre, -cim), (clscr, clr, -cli)):
            scr[n, chans, states] = re[g]
            scr[n, chans, im_states] = im[g]
        lam2r_ref[n, :, states] = jnp.broadcast_to(l2r[g:g + 1, :], (SUBLANES, STATE_P))
        lam2i_ref[n, :, states] = jnp.broadcast_to(l2i[g:g + 1, :], (SUBLANES, STATE_P))
    for n in range(N_HALVES):
        bc2_ref[n, 0:U_HALF, :] = lbscr[n].astype(BF16)
        bc2_ref[n, U_HALF:, :] = bscr[n].astype(BF16)
        c_t = cscr[n].T
        cc_ref[n] = c_t.astype(BF16)
        ccl_ref[n] = clscr[n].T.astype(BF16)
        b_hi, b_lo = _split_bf16(bscr[n])
        c_hi, c_lo = _split_bf16(c_t)
        d0_ref[n] = (_dot(b_hi, c_hi) + _dot(b_hi, c_lo) + _dot(b_lo, c_hi)).astype(BF16)


PREP_STEPS = 2


def _prep(lam_re, lam_im, log_dt, b_re, b_im, c_re, c_im, cast):
    s5_in = (lam_re, lam_im, log_dt.reshape(1, N_GROUPS), b_re.transpose(0, 2, 1), b_im.transpose(0, 2, 1),
             c_re, c_im)
    s5_shapes = [jax.ShapeDtypeStruct((N_HALVES, SUBLANES, S_HALF), F32),
                 jax.ShapeDtypeStruct((N_HALVES, SUBLANES, S_HALF), F32),
                 jax.ShapeDtypeStruct((N_HALVES, 2 * U_HALF, 2 * S_HALF), BF16),
                 jax.ShapeDtypeStruct((N_HALVES, 2 * S_HALF, U_HALF), BF16),
                 jax.ShapeDtypeStruct((N_HALVES, 2 * S_HALF, U_HALF), BF16),
                 jax.ShapeDtypeStruct((N_HALVES, U_HALF, U_HALF), BF16)]
    assert len(s5_shapes) == N_S5_OUT
    for w in cast:
        assert w.shape[0] % (PREP_STEPS * BF16_ROWS) == 0, w.shape
    cast_specs = [pl.BlockSpec((w.shape[0] // PREP_STEPS, w.shape[1]), lambda i: (i, 0)) for w in cast]
    whole = lambda a: pl.BlockSpec(a.shape, lambda i: (0,) * len(a.shape))
    return pl.pallas_call(
        _prep_body,
        grid=(PREP_STEPS,),
        in_specs=[whole(a) for a in s5_in] + cast_specs,
        out_specs=[whole(a) for a in s5_shapes] + cast_specs,
        out_shape=s5_shapes + [jax.ShapeDtypeStruct(w.shape, BF16) for w in cast],
        scratch_shapes=[pltpu.VMEM((N_HALVES, U_HALF, 2 * S_HALF), F32)] * N_S5_SCRATCH,
        compiler_params=pltpu.CompilerParams(dimension_semantics=("arbitrary",)),
        name="prep",
    )(*s5_in, *cast)


def kernel(x_prompt, x_sample, cache_conv, state_ssm_re, state_ssm_im, norm_mix, w_in, conv_w, ssm_lam_re, ssm_lam_im, ssm_log_dt, ssm_b_re, ssm_b_im, ssm_c_re, ssm_c_im, ssm_d, w_glu, b_glu, w_out, norm_ffn, w_gate, w_up, w_down, norm_final):
    assert norm_mix.shape[0] == 1, "single-layer trunk"
    n_st = N_GROUPS * STATE_P
    lam2r, lam2i, bc2, cc, ccl, d0, w_in_b, w_glu_b, w_out_b = _prep(
        ssm_lam_re[0], ssm_lam_im[0], ssm_log_dt[0], ssm_b_re[0], ssm_b_im[0], ssm_c_re[0], ssm_c_im[0],
        cast=(w_in[0], w_glu[0], w_out[0]))
    mix_w = (norm_mix, w_in_b, conv_w[0][:, None, :], lam2r, lam2i, bc2, cc, ccl, d0, ssm_d, w_glu_b, b_glu,
             w_out_b)

    bp, lp, _ = x_prompt.shape
    bs, ls, _ = x_sample.shape

    x1p, convp, rep, imp, wg, wu, wd = _mixer(x_prompt, mix_w, cast=(w_gate[0], w_up[0], w_down[0]),
                                              nb=SUBLANES, tc=128)
    ffn_w = (norm_ffn, wg, wu, wd, norm_final.reshape(1, D_MODEL))
    def state_major(h):
        return h[0].transpose(1, 2, 0).reshape(n_st, h.shape[1])

    sample_state = (cache_conv, state_major(state_ssm_re), state_major(state_ssm_im))
    x1s, convs, res, ims = _mixer(x_sample, mix_w, sample_state, nb=128, tc=SUBLANES)

    yp, ys = _ffn(x1p.reshape(bp * lp, D_MODEL), x1s.reshape(bs * ls, D_MODEL), ffn_w, tm=1024)
    yp = yp.reshape(bp, lp, D_MODEL)
    ys = ys.reshape(bs, ls, D_MODEL)

    def seq_major(h):
        return h.reshape(N_GROUPS, STATE_P, h.shape[1]).transpose(2, 0, 1)[None]

    return (yp, ys, convp, rep.reshape(1, bp, N_GROUPS, STATE_P), imp.reshape(1, bp, N_GROUPS, STATE_P),
            convs, seq_major(res), seq_major(ims))
```

```python
import functools

import jax
import jax.numpy as jnp
from jax import lax
from jax.experimental import pallas as pl
from jax.experimental.pallas import tpu as pltpu

D_MODEL = 1024
W_CONV = 512
W_SSM = 512
CONV_W = 3
N_GROUPS = 32
SSM_H = 16
STATE_P = 64
D_FF = 2816
EPS = 1e-5

SUBLANES = 8
BF16_ROWS = 16
GROUPS_PER_HALF = 16
N_HALVES = N_GROUPS // GROUPS_PER_HALF
U_HALF = GROUPS_PER_HALF * SSM_H
S_HALF = GROUPS_PER_HALF * STATE_P
S_COLS = 2 * N_GROUPS * STATE_P
SCAN_LANES = 512
MXU_COLS = 256
FFN_CHUNKS = tuple((s, MXU_COLS) for s in range(0, D_FF, MXU_COLS))
VMEM_LIMIT = 60 * 1024 * 1024

BF16 = jnp.bfloat16
F32 = jnp.float32


def _rms(x, g):
    y = x * lax.rsqrt(jnp.mean(x * x, axis=-1, keepdims=True) + EPS)
    return y * g


def _dot(a, b):
    return jnp.dot(a, b, preferred_element_type=F32)


def _split_bf16(a):
    hi = a.astype(BF16)
    return hi, (a - hi.astype(F32)).astype(BF16)


def _tile_copies(hbm, buf, sem, step, slot, *, nb, tc, per_seq, to_vmem):
    copies = []
    for k in range(SUBLANES):
        if per_seq:
            h = hbm.at[k, pl.ds(step * tc, tc), :]
            v = buf.at[slot, :, k, :]
        else:
            h = hbm.at[pl.ds(step * nb, nb), k, :]
            v = buf.at[slot, k]
        src, dst = (h, v) if to_vmem else (v, h)
        copies.append(pltpu.make_async_copy(src, dst, sem.at[slot, k]))
    return copies


def _mixer_body(x_hbm, *refs, nb, tc, n_steps, carry, n_cast):
    if carry:
        cache_ref = re0_ref = im0_ref = None
    else:
        cache_ref, re0_ref, im0_ref, *refs = refs
    (nmix_ref, win_ref, convw_ref, lam2r_ref, lam2i_ref, bc2_ref, cc_ref, ccl_ref, d0_ref, dskip_ref,
     wglu_ref, bglu_ref, wout_ref, *refs) = refs
    cast_in, refs = refs[:n_cast], refs[n_cast:]
    x1_hbm, convnew_hbm, hre_ref, him_ref, *refs = refs
    cast_out, refs = refs[:n_cast], refs[n_cast:]
    xbuf, obuf, gbuf, hb_odd, hb_prev, hstate, vbuf, insem, outsem, cachesem = refs
    tm = nb * tc
    n_pair = tc // 2
    tmh = n_pair * nb

    i = pl.program_id(0)
    slot = lax.rem(i, 2)
    cp = functools.partial(_tile_copies, nb=nb, tc=tc, per_seq=carry)
    in_copies = functools.partial(cp, x_hbm, xbuf, insem, to_vmem=True)
    out_copies = functools.partial(cp, x1_hbm, obuf, outsem, to_vmem=False)

    def cache_copies(hbm, row0, to_vmem):
        copies = []
        for k in range(CONV_W - 1):
            h = hbm.at[0, pl.ds(0 if carry else i * nb, nb), k, :]
            v = vbuf.at[pl.ds(row0 + k * nb, nb), :]
            src, dst = (h, v) if to_vmem else (v, h)
            copies.append(pltpu.make_async_copy(src, dst, cachesem.at[int(to_vmem), k]))
        return copies

    @pl.when(i == 0)
    def _():
        for c in in_copies(0, 0):
            c.start()

    if not carry:
        for c in cache_copies(cache_ref, 0, True):
            c.start()
    for c in in_copies(i, slot):
        c.wait()
    if not carry:
        for c in cache_copies(cache_ref, 0, True):
            c.wait()

    @pl.when(i + 1 < n_steps)
    def _():
        for c in in_copies(i + 1, 1 - slot):
            c.start()

    @pl.when(i >= 2)
    def _():
        for c in out_copies(i - 2, slot):
            c.wait()

    if carry:
        @pl.when(i == 0)
        def _():
            vbuf[0:2 * nb, :] = jnp.zeros((2 * nb, W_CONV), F32)
            hstate[...] = jnp.zeros((nb, S_COLS), F32)
    else:
        for n in range(N_HALVES):
            hstate[:, 2 * n * S_HALF:(2 * n + 1) * S_HALF] = re0_ref[n * S_HALF:(n + 1) * S_HALF, :].T
            hstate[:, (2 * n + 1) * S_HALF:(2 * n + 2) * S_HALF] = im0_ref[n * S_HALF:(n + 1) * S_HALF, :].T

    x = xbuf[slot].reshape(tm, D_MODEL)
    inv_rms = lax.rsqrt(jnp.mean(x * x, axis=-1, keepdims=True) + EPS)
    xg = (x * nmix_ref[...]).astype(BF16)
    xn = (x * inv_rms * nmix_ref[...]).astype(BF16)

    u = inv_rms * _dot(xg, win_ref[:, 3 * W_CONV:])
    u_pairs = u.reshape(n_pair, 2, nb, W_SSM)
    u_e = u_pairs[:, 0].reshape(tmh, W_SSM)
    u_o = u_pairs[:, 1].reshape(tmh, W_SSM)
    ub_e = u_e.astype(BF16)
    ub_o = u_o.astype(BF16)
    for n in range(N_HALVES):
        cols = slice(n * U_HALF, (n + 1) * U_HALF)
        gbuf[:, 2 * n * S_HALF:(2 * n + 2) * S_HALF] = _dot(
            jnp.concatenate([ub_e[:, cols], ub_o[:, cols]], axis=1), bc2_ref[n])

    for src, dst in zip(cast_in, cast_out):
        dst[...] = src[...].astype(BF16)

    def scan_step(h, rows, lr, li, re_cols, im_cols):
        hr, hi = h
        return (lr * hr - li * hi + gbuf[rows, re_cols], lr * hi + li * hr + gbuf[rows, im_cols])

    def store_pair(buf, row0, pair, re_cols, im_cols):
        rows = slice(row0, row0 + BF16_ROWS)
        buf[rows, re_cols] = jnp.concatenate([h[0] for h in pair], axis=0).astype(BF16)
        buf[rows, im_cols] = jnp.concatenate([h[1] for h in pair], axis=0).astype(BF16)

    for n in range(N_HALVES):
        for c0 in range(0, S_HALF, SCAN_LANES):
            re_cols = slice(2 * n * S_HALF + c0, 2 * n * S_HALF + c0 + SCAN_LANES)
            im_cols = slice((2 * n + 1) * S_HALF + c0, (2 * n + 1) * S_HALF + c0 + SCAN_LANES)
            lr = lam2r_ref[n, :, c0:c0 + SCAN_LANES]
            li = lam2i_ref[n, :, c0:c0 + SCAN_LANES]
            step = functools.partial(scan_step, lr=lr, li=li, re_cols=re_cols, im_cols=im_cols)
            if nb == SUBLANES:
                h = (hstate[:, re_cols], hstate[:, im_cols])
                for k in range(0, n_pair, 2):
                    h0 = step(h, slice(k * nb, (k + 1) * nb))
                    h1 = step(h0, slice((k + 1) * nb, (k + 2) * nb))
                    store_pair(hb_odd, k * nb, (h0, h1), re_cols, im_cols)
                    store_pair(hb_prev, k * nb, (h, h0), re_cols, im_cols)
                    h = h1
                hstate[:, re_cols], hstate[:, im_cols] = h
            else:
                for r0 in range(0, nb, BF16_ROWS):
                    seqs = [slice(r0 + j * SUBLANES, r0 + (j + 1) * SUBLANES) for j in range(2)]
                    pair = [(hstate[sq, re_cols], hstate[sq, im_cols]) for sq in seqs]
                    for k in range(n_pair):
                        store_pair(hb_prev, k * nb + r0, pair, re_cols, im_cols)
                        pair = [step(h, slice(k * nb + sq.start, k * nb + sq.stop)) for h, sq in zip(pair, seqs)]
                        store_pair(hb_odd, k * nb + r0, pair, re_cols, im_cols)
                    for h, sq in zip(pair, seqs):
                        hstate[sq, re_cols], hstate[sq, im_cols] = h

    conv_chunks = []
    for c0 in range(0, W_CONV, MXU_COLS):
        cols = slice(c0, c0 + MXU_COLS)
        b_g = _dot(xn, win_ref[:, c0:c0 + MXU_COLS])
        c_g = _dot(xn, win_ref[:, W_CONV + c0:W_CONV + c0 + MXU_COLS])
        xc = _dot(xn, win_ref[:, 2 * W_CONV + c0:2 * W_CONV + c0 + MXU_COLS])
        v = c_g * xc
        vbuf[2 * nb:2 * nb + tm, cols] = v
        conv = (convw_ref[0, :, cols] * vbuf[0:tm, cols] + convw_ref[1, :, cols] * vbuf[nb:nb + tm, cols]
                + convw_ref[2, :, cols] * v)
        conv_chunks.append((b_g * conv).astype(BF16))
    conv_out = jnp.concatenate(conv_chunks, axis=-1)
    if carry:
        vbuf[0:2 * nb, :] = vbuf[tm:tm + 2 * nb, :]
    x1 = x + _dot(conv_out, wout_ref[0:W_CONV, :])

    last = hstate[...]
    for n in range(N_HALVES):
        last_re = last[:, 2 * n * S_HALF:(2 * n + 1) * S_HALF]
        last_im = last[:, (2 * n + 1) * S_HALF:(2 * n + 2) * S_HALF]
        if carry:
            hre_ref[:, n * S_HALF:(n + 1) * S_HALF] = last_re
            him_ref[:, n * S_HALF:(n + 1) * S_HALF] = last_im
        else:
            hre_ref[n * S_HALF:(n + 1) * S_HALF, :] = last_re.T
            him_ref[n * S_HALF:(n + 1) * S_HALF, :] = last_im.T

    zs = []
    for n in range(N_HALVES):
        cols = slice(n * U_HALF, (n + 1) * U_HALF)
        st_cols = slice(2 * n * S_HALF, (2 * n + 2) * S_HALF)
        y_o = _dot(hb_odd[:, st_cols], cc_ref[n]) + dskip_ref[:, cols] * u_o[:, cols]
        y_e = (_dot(hb_prev[:, st_cols], ccl_ref[n]) + _dot(ub_e[:, cols], d0_ref[n])
               + dskip_ref[:, cols] * u_e[:, cols])
        y = jnp.stack([y_e.reshape(n_pair, nb, U_HALF), y_o.reshape(n_pair, nb, U_HALF)],
                      axis=1).reshape(tm, U_HALF)
        zs.append(jax.nn.gelu(y))
    zb = jnp.concatenate([z.astype(BF16) for z in zs], axis=-1)
    ssm_chunks = []
    for n, z in enumerate(zs):
        cols = slice(n * U_HALF, (n + 1) * U_HALF)
        gate = jax.nn.sigmoid(_dot(zb, wglu_ref[:, cols]) + bglu_ref[:, cols])
        ssm_chunks.append((z * gate).astype(BF16))
    ssm_out = jnp.concatenate(ssm_chunks, axis=-1)

    x1 = x1 + _dot(ssm_out, wout_ref[W_CONV:, :])

    obuf[slot] = x1.reshape(tc, nb, D_MODEL)
    for c in out_copies(i, slot):
        c.start()

    if not carry:
        for c in cache_copies(convnew_hbm, tm, False):
            c.start()
        for c in cache_copies(convnew_hbm, tm, False):
            c.wait()

    @pl.when(i == n_steps - 1)
    def _():
        if carry:
            for c in cache_copies(convnew_hbm, tm, False):
                c.start()
            for c in cache_copies(convnew_hbm, tm, False):
                c.wait()
        if n_steps >= 2:
            for c in out_copies(i - 1, 1 - slot):
                c.wait()
        for c in out_copies(i, slot):
            c.wait()


def _resident(shape):
    nd = len(shape)
    return pl.BlockSpec(shape, lambda i: (0,) * nd, pipeline_mode=pl.Buffered(1))


def _mixer(x, wts, state=None, cast=(), *, nb, tc):
    n_seq, n_time, _ = x.shape
    carry = state is None
    if carry:
        assert n_seq == nb == SUBLANES and n_time % tc == 0 and tc % 4 == 0
        n_steps = n_time // tc
    else:
        assert n_time == tc == SUBLANES and n_seq % nb == 0 and nb % BF16_ROWS == 0
        n_steps = n_seq // nb
    tm = nb * tc
    n_st = N_GROUPS * STATE_P
    body = functools.partial(_mixer_body, nb=nb, tc=tc, n_steps=n_steps, carry=carry, n_cast=len(cast))
    any_spec = pl.BlockSpec(memory_space=pl.ANY)
    if carry:
        st_spec = pl.BlockSpec((nb, n_st), lambda i: (0, 0))
        st_shape = jax.ShapeDtypeStruct((n_seq, n_st), F32)
    else:
        st_spec = pl.BlockSpec((n_st, nb), lambda i: (0, i))
        st_shape = jax.ShapeDtypeStruct((n_st, n_seq), F32)
    state_specs = [] if carry else [any_spec, st_spec, st_spec]
    for w in cast:
        assert w.shape[0] % (n_steps * BF16_ROWS) == 0, w.shape
    cast_specs = [pl.BlockSpec((w.shape[0] // n_steps, w.shape[1]), lambda i: (i, 0)) for w in cast]
    return pl.pallas_call(
        body,
        grid=(n_steps,),
        in_specs=[any_spec] + state_specs + [_resident(w.shape) for w in wts] + cast_specs,
        out_specs=[any_spec, any_spec, st_spec, st_spec] + cast_specs,
        out_shape=[jax.ShapeDtypeStruct(x.shape, F32),
                   jax.ShapeDtypeStruct((1, n_seq, CONV_W - 1, W_CONV), F32),
                   st_shape, st_shape]
                  + [jax.ShapeDtypeStruct(w.shape, BF16) for w in cast],
        scratch_shapes=[pltpu.VMEM((2, tc, nb, D_MODEL), F32),
                        pltpu.VMEM((2, tc, nb, D_MODEL), F32),
                        pltpu.VMEM((tm // 2, S_COLS), F32),
                        pltpu.VMEM((tm // 2, S_COLS), BF16),
                        pltpu.VMEM((tm // 2, S_COLS), BF16),
                        pltpu.VMEM((nb, S_COLS), F32),
                        pltpu.VMEM((2 * nb + tm, W_CONV), F32),
                        pltpu.SemaphoreType.DMA((2, SUBLANES)),
                        pltpu.SemaphoreType.DMA((2, SUBLANES)),
                        pltpu.SemaphoreType.DMA((2, CONV_W - 1))],
        compiler_params=pltpu.CompilerParams(dimension_semantics=("arbitrary",),
                                             vmem_limit_bytes=VMEM_LIMIT),
        name="mixer_carry" if carry else "mixer_step",
    )(x, *(() if carry else state), *wts, *cast)


FFN_LOOKAHEAD = 2


def _ffn_body(xp_hbm, xs_hbm, nffn_ref, wg_hbm, wu_hbm, wd_hbm, nfin_ref, yp_hbm, ys_hbm,
              xbuf, obuf, abuf, wg_ref, wu_ref, wd_ref, insem, outsem, wsem, *, n_p, n_s, tm):
    i = pl.program_id(0)
    n_steps = n_p + n_s
    slot = lax.rem(i, 2)
    n_chunks = len(FFN_CHUNKS)

    def tile_copy(step, buf_slot, to_vmem, first):
        hbm = (xp_hbm if first else xs_hbm) if to_vmem else (yp_hbm if first else ys_hbm)
        h = hbm.at[pl.ds((step if first else step - n_p) * tm, tm), :]
        if to_vmem:
            return pltpu.make_async_copy(h, xbuf.at[buf_slot], insem.at[buf_slot])
        return pltpu.make_async_copy(obuf.at[buf_slot], h, outsem.at[buf_slot])

    def start(step, buf_slot, to_vmem):
        pl.when(step < n_p)(lambda: tile_copy(step, buf_slot, to_vmem, True).start())
        pl.when(step >= n_p)(lambda: tile_copy(step, buf_slot, to_vmem, False).start())

    def wait(buf_slot, to_vmem):
        tile_copy(0, buf_slot, to_vmem, True).wait()

    def weight_copies(c):
        s, w = FFN_CHUNKS[c]
        return ([pltpu.make_async_copy(hbm.at[:, pl.ds(s, w)], ref.at[:, pl.ds(s, w)], wsem.at[0, c])
                 for hbm, ref in ((wg_hbm, wg_ref), (wu_hbm, wu_ref))],
                pltpu.make_async_copy(wd_hbm.at[pl.ds(s, w), :], wd_ref.at[pl.ds(s, w), :], wsem.at[1, c]))

    @pl.when(i == 0)
    def _():
        start(0, 0, True)
        for c in range(min(FFN_LOOKAHEAD, n_chunks)):
            for cp in weight_copies(c)[0]:
                cp.start()

    wait(slot, True)
    pl.when(i + 1 < n_steps)(lambda: start(i + 1, 1 - slot, True))
    pl.when(i >= 2)(lambda: wait(slot, False))

    def tile(fetch_weights):
        x = xbuf[slot]
        hn = _rms(x, nffn_ref[...]).astype(BF16)
        for c, (s, w) in enumerate(FFN_CHUNKS):
            if fetch_weights:
                if c + FFN_LOOKAHEAD < n_chunks:
                    for cp in weight_copies(c + FFN_LOOKAHEAD)[0]:
                        cp.start()
                weight_copies(c)[1].start()
                for cp in weight_copies(c)[0]:
                    cp.wait()
            g = _dot(hn, wg_ref[:, s:s + w])
            up = _dot(hn, wu_ref[:, s:s + w])
            abuf[:, s:s + w] = (g * jax.nn.sigmoid(g) * up).astype(BF16)
        if fetch_weights:
            for c in range(n_chunks):
                weight_copies(c)[1].wait()
        acc = x + _dot(abuf[...], wd_ref[...])
        obuf[slot] = _rms(acc, nfin_ref[...])

    pl.when(i == 0)(lambda: tile(True))
    pl.when(i > 0)(lambda: tile(False))
    start(i, slot, False)

    @pl.when(i == n_steps - 1)
    def _():
        if n_steps >= 2:
            wait(1 - slot, False)
        wait(slot, False)


def _ffn(xp, xs, wts, *, tm):
    n_p, n_s = xp.shape[0] // tm, xs.shape[0] // tm
    assert xp.shape[0] == n_p * tm and xs.shape[0] == n_s * tm
    nffn, wg, wu, wd, nfin = wts
    assert wd.shape[0] == D_FF == FFN_CHUNKS[-1][0] + FFN_CHUNKS[-1][1]
    any_spec = pl.BlockSpec(memory_space=pl.ANY)
    return pl.pallas_call(
        functools.partial(_ffn_body, n_p=n_p, n_s=n_s, tm=tm),
        grid=(n_p + n_s,),
        in_specs=[any_spec, any_spec, _resident(nffn.shape), any_spec, any_spec, any_spec, _resident(nfin.shape)],
        out_specs=[any_spec, any_spec],
        out_shape=[jax.ShapeDtypeStruct(xp.shape, F32), jax.ShapeDtypeStruct(xs.shape, F32)],
        scratch_shapes=[pltpu.VMEM((2, tm, D_MODEL), F32),
                        pltpu.VMEM((2, tm, D_MODEL), F32),
                        pltpu.VMEM((tm, D_FF), BF16),
                        pltpu.VMEM(wg.shape, BF16),
                        pltpu.VMEM(wu.shape, BF16),
                        pltpu.VMEM(wd.shape, BF16),
                        pltpu.SemaphoreType.DMA((2,)),
                        pltpu.SemaphoreType.DMA((2,)),
                        pltpu.SemaphoreType.DMA((2, len(FFN_CHUNKS)))],
        compiler_params=pltpu.CompilerParams(dimension_semantics=("arbitrary",),
                                             vmem_limit_bytes=VMEM_LIMIT),
        name="ffn",
    )(xp, xs, *wts)


N_S5_OUT = 6
N_S5_SCRATCH = 4


def _prep_body(lre_ref, lim_ref, logdt_ref, btre_ref, btim_ref, cre_ref, cim_ref, *refs):
    n_cast = (len(refs) - N_S5_OUT - N_S5_SCRATCH) // 2
    cast_in, refs = refs[:n_cast], refs[n_cast:]
    s5_out, refs = refs[:N_S5_OUT], refs[N_S5_OUT:]
    cast_out, s5_scratch = refs[:n_cast], refs[n_cast:]
    for src, dst in zip(cast_in, cast_out):
        dst[...] = src[...].astype(BF16)
    pl.when(pl.program_id(0) == 0)(functools.partial(
        _s5_matrices, lre_ref, lim_ref, logdt_ref, btre_ref, btim_ref, cre_ref, cim_ref, *s5_out, *s5_scratch))


def _s5_matrices(lre_ref, lim_ref, logdt_ref, btre_ref, btim_ref, cre_ref, cim_ref,
                 lam2r_ref, lam2i_ref, bc2_ref, cc_ref, ccl_ref, d0_ref, bscr, lbscr, cscr, clscr):
    lre = lre_ref[...]
    lim = lim_ref[...]
    grp = (lax.broadcasted_iota(jnp.int32, (N_GROUPS, N_GROUPS), 0)
           == lax.broadcasted_iota(jnp.int32, (N_GROUPS, N_GROUPS), 1))
    logdt = jnp.sum(jnp.where(grp, jnp.broadcast_to(logdt_ref[...], (N_GROUPS, N_GROUPS)), 0.0),
                    axis=1, keepdims=True)
    dt = jnp.exp(logdt)
    mag = jnp.exp(lre * dt)
    lbr = mag * jnp.cos(lim * dt)
    lbi = mag * jnp.sin(lim * dt)
    l2r = lbr * lbr - lbi * lbi
    l2i = 2.0 * (lbr * lbi)
    den = lre * lre + lim * lim
    fr = ((lbr - 1.0) * lre + lbi * lim) / den
    fi = (lbi * lre - (lbr - 1.0) * lim) / den
    frb, fib = fr[:, None, :], fi[:, None, :]
    lrb, lib = lbr[:, None, :], lbi[:, None, :]
    bbr = frb * btre_ref[...] - fib * btim_ref[...]
    bbi = frb * btim_ref[...] + fib * btre_ref[...]
    lbbr = lrb * bbr - lib * bbi
    lbbi = lrb * bbi + lib * bbr
    cre = cre_ref[...]
    cim = cim_ref[...]
    clr = cre * lrb - cim * lib
    cli = cre * lib + cim * lrb

    for scr in (bscr, lbscr, cscr, clscr):
        scr[...] = jnp.zeros_like(scr)
    for g in range(N_GROUPS):
        n, k = divmod(g, GROUPS_PER_HALF)
        chans = slice(k * SSM_H, (k + 1) * SSM_H)
        states = slice(k * STATE_P, (k + 1) * STATE_P)
        im_states = slice(S_HALF + k * STATE_P, S_HALF + (k + 1) * STATE_P)
        for scr, re, im in ((bscr, bbr, bbi), (lbscr, lbbr, lbbi), (cscr, cre, -cim), (clscr, clr, -cli)):
            scr[n, chans, states] = re[g]
            scr[n, chans, im_states] = im[g]
        lam2r_ref[n, :, states] = jnp.broadcast_to(l2r[g:g + 1, :], (SUBLANES, STATE_P))
        lam2i_ref[n, :, states] = jnp.broadcast_to(l2i[g:g + 1, :], (SUBLANES, STATE_P))
    for n in range(N_HALVES):
        bc2_ref[n, 0:U_HALF, :] = lbscr[n].astype(BF16)
        bc2_ref[n, U_HALF:, :] = bscr[n].astype(BF16)
        c_t = cscr[n].T
        cc_ref[n] = c_t.astype(BF16)
        ccl_ref[n] = clscr[n].T.astype(BF16)
        b_hi, b_lo = _split_bf16(bscr[n])
        c_hi, c_lo = _split_bf16(c_t)
        d0_ref[n] = (_dot(b_hi, c_hi) + _dot(b_hi, c_lo) + _dot(b_lo, c_hi)).astype(BF16)


PREP_STEPS = 2


def _prep(lam_re, lam_im, log_dt, b_re, b_im, c_re, c_im, cast):
    s5_in = (lam_re, lam_im, log_dt.reshape(1, N_GROUPS), b_re.transpose(0, 2, 1), b_im.transpose(0, 2, 1),
             c_re, c_im)
    s5_shapes = [jax.ShapeDtypeStruct((N_HALVES, SUBLANES, S_HALF), F32),
                 jax.ShapeDtypeStruct((N_HALVES, SUBLANES, S_HALF), F32),
                 jax.ShapeDtypeStruct((N_HALVES, 2 * U_HALF, 2 * S_HALF), BF16),
                 jax.ShapeDtypeStruct((N_HALVES, 2 * S_HALF, U_HALF), BF16),
                 jax.ShapeDtypeStruct((N_HALVES, 2 * S_HALF, U_HALF), BF16),
                 jax.ShapeDtypeStruct((N_HALVES, U_HALF, U_HALF), BF16)]
    assert len(s5_shapes) == N_S5_OUT
    for w in cast:
        assert w.shape[0] % (PREP_STEPS * BF16_ROWS) == 0, w.shape
    cast_specs = [pl.BlockSpec((w.shape[0] // PREP_STEPS, w.shape[1]), lambda i: (i, 0)) for w in cast]
    whole = lambda a: pl.BlockSpec(a.shape, lambda i: (0,) * len(a.shape))
    return pl.pallas_call(
        _prep_body,
        grid=(PREP_STEPS,),
        in_specs=[whole(a) for a in s5_in] + cast_specs,
        out_specs=[whole(a) for a in s5_shapes] + cast_specs,
        out_shape=s5_shapes + [jax.ShapeDtypeStruct(w.shape, BF16) for w in cast],
        scratch_shapes=[pltpu.VMEM((N_HALVES, U_HALF, 2 * S_HALF), F32)] * N_S5_SCRATCH,
        compiler_params=pltpu.CompilerParams(dimension_semantics=("arbitrary",)),
        name="prep",
    )(*s5_in, *cast)


def kernel(x_prompt, x_sample, cache_conv, state_ssm_re, state_ssm_im, norm_mix, w_in, conv_w, ssm_lam_re, ssm_lam_im, ssm_log_dt, ssm_b_re, ssm_b_im, ssm_c_re, ssm_c_im, ssm_d, w_glu, b_glu, w_out, norm_ffn, w_gate, w_up, w_down, norm_final):
    assert norm_mix.shape[0] == 1, "single-layer trunk"
    n_st = N_GROUPS * STATE_P
    lam2r, lam2i, bc2, cc, ccl, d0, w_in_b, w_glu_b, w_out_b = _prep(
        ssm_lam_re[0], ssm_lam_im[0], ssm_log_dt[0], ssm_b_re[0], ssm_b_im[0], ssm_c_re[0], ssm_c_im[0],
        cast=(w_in[0], w_glu[0], w_out[0]))
    mix_w = (norm_mix, w_in_b, conv_w[0][:, None, :], lam2r, lam2i, bc2, cc, ccl, d0, ssm_d, w_glu_b, b_glu,
             w_out_b)

    bp, lp, _ = x_prompt.shape
    bs, ls, _ = x_sample.shape

    x1p, convp, rep, imp, wg, wu, wd = _mixer(x_prompt, mix_w, cast=(w_gate[0], w_up[0], w_down[0]),
                                              nb=SUBLANES, tc=128)
    ffn_w = (norm_ffn, wg, wu, wd, norm_final.reshape(1, D_MODEL))
    def state_major(h):
        return h[0].transpose(1, 2, 0).reshape(n_st, h.shape[1])

    sample_state = (cache_conv, state_major(state_ssm_re), state_major(state_ssm_im))
    x1s, convs, res, ims = _mixer(x_sample, mix_w, sample_state, nb=128, tc=SUBLANES)

    yp, ys = _ffn(x1p.reshape(bp * lp, D_MODEL), x1s.reshape(bs * ls, D_MODEL), ffn_w, tm=1024)
    yp = yp.reshape(bp, lp, D_MODEL)
    ys = ys.reshape(bs, ls, D_MODEL)

    def seq_major(h):
        return h.reshape(N_GROUPS, STATE_P, h.shape[1]).transpose(2, 0, 1)[None]

    return (yp, ys, convp, rep.reshape(1, bp, N_GROUPS, STATE_P), imp.reshape(1, bp, N_GROUPS, STATE_P),
            convs, seq_major(res), seq_major(ims))
```

```python
import functools

import jax
import jax.numpy as jnp
from jax import lax
from jax.experimental import pallas as pl
from jax.experimental.pallas import tpu as pltpu

D_MODEL = 1024
W_CONV = 512
W_SSM = 512
CONV_W = 3
N_GROUPS = 32
SSM_H = 16
STATE_P = 64
D_FF = 2816
EPS = 1e-5

SUBLANES = 8
BF16_ROWS = 16
GROUPS_PER_HALF = 16
N_HALVES = N_GROUPS // GROUPS_PER_HALF
U_HALF = GROUPS_PER_HALF * SSM_H
S_HALF = GROUPS_PER_HALF * STATE_P
S_COLS = 2 * N_GROUPS * STATE_P
SCAN_LANES = 512
MXU_COLS = 256
FFN_CHUNKS = tuple((s, MXU_COLS) for s in range(0, D_FF, MXU_COLS))
VMEM_LIMIT = 60 * 1024 * 1024

BF16 = jnp.bfloat16
F32 = jnp.float32


def _rms(x, g):
    y = x * lax.rsqrt(jnp.mean(x * x, axis=-1, keepdims=True) + EPS)
    return y * g


def _dot(a, b):
    return jnp.dot(a, b, preferred_element_type=F32)


def _split_bf16(a):
    hi = a.astype(BF16)
    return hi, (a - hi.astype(F32)).astype(BF16)


def _tile_copies(hbm, buf, sem, step, slot, *, nb, tc, per_seq, to_vmem):
    copies = []
    for k in range(SUBLANES):
        if per_seq:
            h = hbm.at[k, pl.ds(step * tc, tc), :]
            v = buf.at[slot, :, k, :]
        else:
            h = hbm.at[pl.ds(step * nb, nb), k, :]
            v = buf.at[slot, k]
        src, dst = (h, v) if to_vmem else (v, h)
        copies.append(pltpu.make_async_copy(src, dst, sem.at[slot, k]))
    return copies


def _mixer_body(x_hbm, *refs, nb, tc, n_steps, carry, n_cast):
    if carry:
        cache_ref = re0_ref = im0_ref = None
    else:
        cache_ref, re0_ref, im0_ref, *refs = refs
    (nmix_ref, win_ref, convw_ref, lam2r_ref, lam2i_ref, bc2_ref, cc_ref, ccl_ref, d0_ref, dskip_ref,
     wglu_ref, bglu_ref, wout_ref, *refs) = refs
    cast_in, refs = refs[:n_cast], refs[n_cast:]
    x1_hbm, convnew_hbm, hre_ref, him_ref, *refs = refs
    cast_out, refs = refs[:n_cast], refs[n_cast:]
    xbuf, obuf, gbuf, hb_odd, hb_prev, hstate, vbuf, insem, outsem, cachesem = refs
    tm = nb * tc
    n_pair = tc // 2
    tmh = n_pair * nb

    i = pl.program_id(0)
    slot = lax.rem(i, 2)
    cp = functools.partial(_tile_copies, nb=nb, tc=tc, per_seq=carry)
    in_copies = functools.partial(cp, x_hbm, xbuf, insem, to_vmem=True)
    out_copies = functools.partial(cp, x1_hbm, obuf, outsem, to_vmem=False)

    def cache_copies(hbm, row0, to_vmem):
        copies = []
        for k in range(CONV_W - 1):
            h = hbm.at[0, pl.ds(0 if carry else i * nb, nb), k, :]
            v = vbuf.at[pl.ds(row0 + k * nb, nb), :]
            src, dst = (h, v) if to_vmem else (v, h)
            copies.append(pltpu.make_async_copy(src, dst, cachesem.at[int(to_vmem), k]))
        return copies

    @pl.when(i == 0)
    def _():
        for c in in_copies(0, 0):
            c.start()

    if not carry:
        for c in cache_copies(cache_ref, 0, True):
            c.start()
    for c in in_copies(i, slot):
        c.wait()
    if not carry:
        for c in cache_copies(cache_ref, 0, True):
            c.wait()

    @pl.when(i + 1 < n_steps)
    def _():
        for c in in_copies(i + 1, 1 - slot):
            c.start()

    @pl.when(i >= 2)
    def _():
        for c in out_copies(i - 2, slot):
            c.wait()

    if carry:
        @pl.when(i == 0)
        def _():
            vbuf[0:2 * nb, :] = jnp.zeros((2 * nb, W_CONV), F32)
            hstate[...] = jnp.zeros((nb, S_COLS), F32)
    else:
        for n in range(N_HALVES):
            hstate[:, 2 * n * S_HALF:(2 * n + 1) * S_HALF] = re0_ref[n * S_HALF:(n + 1) * S_HALF, :].T
            hstate[:, (2 * n + 1) * S_HALF:(2 * n + 2) * S_HALF] = im0_ref[n * S_HALF:(n + 1) * S_HALF, :].T

    x = xbuf[slot].reshape(tm, D_MODEL)
    inv_rms = lax.rsqrt(jnp.mean(x * x, axis=-1, keepdims=True) + EPS)
    xg = (x * nmix_ref[...]).astype(BF16)
    xn = (x * inv_rms * nmix_ref[...]).astype(BF16)

    u = inv_rms * _dot(xg, win_ref[:, 3 * W_CONV:])
    u_pairs = u.reshape(n_pair, 2, nb, W_SSM)
    u_e = u_pairs[:, 0].reshape(tmh, W_SSM)
    u_o = u_pairs[:, 1].reshape(tmh, W_SSM)
    ub_e = u_e.astype(BF16)
    ub_o = u_o.astype(BF16)
    for n in range(N_HALVES):
        cols = slice(n * U_HALF, (n + 1) * U_HALF)
        gbuf[:, 2 * n * S_HALF:(2 * n + 2) * S_HALF] = _dot(
            jnp.concatenate([ub_e[:, cols], ub_o[:, cols]], axis=1), bc2_ref[n])

    for src, dst in zip(cast_in, cast_out):
        if len(dst.shape) == 3:
            for c in range(dst.shape[0]):
                dst[c] = src[:, c * dst.shape[2]:(c + 1) * dst.shape[2]].astype(BF16)
        else:
            dst[...] = src[...].astype(BF16)

    def scan_step(h, rows, lr, li, re_cols, im_cols):
        hr, hi = h
        return (lr * hr - li * hi + gbuf[rows, re_cols], lr * hi + li * hr + gbuf[rows, im_cols])

    def store_pair(buf, row0, pair, re_cols, im_cols):
        rows = slice(row0, row0 + BF16_ROWS)
        buf[rows, re_cols] = jnp.concatenate([h[0] for h in pair], axis=0).astype(BF16)
        buf[rows, im_cols] = jnp.concatenate([h[1] for h in pair], axis=0).astype(BF16)

    for n in range(N_HALVES):
        for c0 in range(0, S_HALF, SCAN_LANES):
            re_cols = slice(2 * n * S_HALF + c0, 2 * n * S_HALF + c0 + SCAN_LANES)
            im_cols = slice((2 * n + 1) * S_HALF + c0, (2 * n + 1) * S_HALF + c0 + SCAN_LANES)
            lr = lam2r_ref[n, :, c0:c0 + SCAN_LANES]
            li = lam2i_ref[n, :, c0:c0 + SCAN_LANES]
            step = functools.partial(scan_step, lr=lr, li=li, re_cols=re_cols, im_cols=im_cols)
            if nb == SUBLANES:
                h = (hstate[:, re_cols], hstate[:, im_cols])
                for k in range(0, n_pair, 2):
                    h0 = step(h, slice(k * nb, (k + 1) * nb))
                    h1 = step(h0, slice((k + 1) * nb, (k + 2) * nb))
                    store_pair(hb_odd, k * nb, (h0, h1), re_cols, im_cols)
                    store_pair(hb_prev, k * nb, (h, h0), re_cols, im_cols)
                    h = h1
                hstate[:, re_cols], hstate[:, im_cols] = h
            else:
                for r0 in range(0, nb, BF16_ROWS):
                    seqs = [slice(r0 + j * SUBLANES, r0 + (j + 1) * SUBLANES) for j in range(2)]
                    pair = [(hstate[sq, re_cols], hstate[sq, im_cols]) for sq in seqs]
                    for k in range(n_pair):
                        store_pair(hb_prev, k * nb + r0, pair, re_cols, im_cols)
                        pair = [step(h, slice(k * nb + sq.start, k * nb + sq.stop)) for h, sq in zip(pair, seqs)]
                        store_pair(hb_odd, k * nb + r0, pair, re_cols, im_cols)
                    for h, sq in zip(pair, seqs):
                        hstate[sq, re_cols], hstate[sq, im_cols] = h

    conv_chunks = []
    for c0 in range(0, W_CONV, MXU_COLS):
        cols = slice(c0, c0 + MXU_COLS)
        b_g = _dot(xn, win_ref[:, c0:c0 + MXU_COLS])
        c_g = _dot(xn, win_ref[:, W_CONV + c0:W_CONV + c0 + MXU_COLS])
        xc = _dot(xn, win_ref[:, 2 * W_CONV + c0:2 * W_CONV + c0 + MXU_COLS])
        v = c_g * xc
        vbuf[2 * nb:2 * nb + tm, cols] = v
        conv = (convw_ref[0, :, cols] * vbuf[0:tm, cols] + convw_ref[1, :, cols] * vbuf[nb:nb + tm, cols]
                + convw_ref[2, :, cols] * v)
        conv_chunks.append((b_g * conv).astype(BF16))
    conv_out = jnp.concatenate(conv_chunks, axis=-1)
    if carry:
        vbuf[0:2 * nb, :] = vbuf[tm:tm + 2 * nb, :]
    x1 = x + _dot(conv_out, wout_ref[0:W_CONV, :])

    last = hstate[...]
    for n in range(N_HALVES):
        last_re = last[:, 2 * n * S_HALF:(2 * n + 1) * S_HALF]
        last_im = last[:, (2 * n + 1) * S_HALF:(2 * n + 2) * S_HALF]
        if carry:
            hre_ref[:, n * S_HALF:(n + 1) * S_HALF] = last_re
            him_ref[:, n * S_HALF:(n + 1) * S_HALF] = last_im
        else:
            hre_ref[n * S_HALF:(n + 1) * S_HALF, :] = last_re.T
            him_ref[n * S_HALF:(n + 1) * S_HALF, :] = last_im.T

    zs = []
    for n in range(N_HALVES):
        cols = slice(n * U_HALF, (n + 1) * U_HALF)
        st_cols = slice(2 * n * S_HALF, (2 * n + 2) * S_HALF)
        y_o = _dot(hb_odd[:, st_cols], cc_ref[n]) + dskip_ref[:, cols] * u_o[:, cols]
        y_e = (_dot(hb_prev[:, st_cols], ccl_ref[n]) + _dot(ub_e[:, cols], d0_ref[n])
               + dskip_ref[:, cols] * u_e[:, cols])
        y = jnp.stack([y_e.reshape(n_pair, nb, U_HALF), y_o.reshape(n_pair, nb, U_HALF)],
                      axis=1).reshape(tm, U_HALF)
        zs.append(jax.nn.gelu(y))
    zb = jnp.concatenate([z.astype(BF16) for z in zs], axis=-1)
    ssm_chunks = []
    for n, z in enumerate(zs):
        cols = slice(n * U_HALF, (n + 1) * U_HALF)
        gate = jax.nn.sigmoid(_dot(zb, wglu_ref[:, cols]) + bglu_ref[:, cols])
        ssm_chunks.append((z * gate).astype(BF16))
    ssm_out = jnp.concatenate(ssm_chunks, axis=-1)

    x1 = x1 + _dot(ssm_out, wout_ref[W_CONV:, :])

    obuf[slot] = x1.reshape(tc, nb, D_MODEL)
    for c in out_copies(i, slot):
        c.start()

    if not carry:
        for c in cache_copies(convnew_hbm, tm, False):
            c.start()
        for c in cache_copies(convnew_hbm, tm, False):
            c.wait()

    @pl.when(i == n_steps - 1)
    def _():
        if carry:
            for c in cache_copies(convnew_hbm, tm, False):
                c.start()
            for c in cache_copies(convnew_hbm, tm, False):
                c.wait()
        if n_steps >= 2:
            for c in out_copies(i - 1, 1 - slot):
                c.wait()
        for c in out_copies(i, slot):
            c.wait()


def _resident(shape):
    nd = len(shape)
    return pl.BlockSpec(shape, lambda i: (0,) * nd, pipeline_mode=pl.Buffered(1))


def _mixer(x, wts, state=None, cast=(), *, nb, tc):
    n_seq, n_time, _ = x.shape
    carry = state is None
    if carry:
        assert n_seq == nb == SUBLANES and n_time % tc == 0 and tc % 4 == 0
        n_steps = n_time // tc
    else:
        assert n_time == tc == SUBLANES and n_seq % nb == 0 and nb % BF16_ROWS == 0
        n_steps = n_seq // nb
    tm = nb * tc
    n_st = N_GROUPS * STATE_P
    body = functools.partial(_mixer_body, nb=nb, tc=tc, n_steps=n_steps, carry=carry, n_cast=len(cast))
    any_spec = pl.BlockSpec(memory_space=pl.ANY)
    if carry:
        st_spec = pl.BlockSpec((nb, n_st), lambda i: (0, 0))
        st_shape = jax.ShapeDtypeStruct((n_seq, n_st), F32)
    else:
        st_spec = pl.BlockSpec((n_st, nb), lambda i: (0, i))
        st_shape = jax.ShapeDtypeStruct((n_st, n_seq), F32)
    state_specs = [] if carry else [any_spec, st_spec, st_spec]
    cast_in_specs, cast_out_specs, cast_shapes = [], [], []
    for w, chunk in cast:
        rows, cols = w.shape
        assert rows % (n_steps * BF16_ROWS) == 0, w.shape
        cast_in_specs.append(pl.BlockSpec((rows // n_steps, cols), lambda i: (i, 0)))
        if chunk is None:
            cast_out_specs.append(cast_in_specs[-1])
            cast_shapes.append(jax.ShapeDtypeStruct(w.shape, BF16))
        else:
            assert cols % chunk == 0
            cast_out_specs.append(pl.BlockSpec((cols // chunk, rows // n_steps, chunk), lambda i: (0, i, 0)))
            cast_shapes.append(jax.ShapeDtypeStruct((cols // chunk, rows, chunk), BF16))
    return pl.pallas_call(
        body,
        grid=(n_steps,),
        in_specs=[any_spec] + state_specs + [_resident(w.shape) for w in wts] + cast_in_specs,
        out_specs=[any_spec, any_spec, st_spec, st_spec] + cast_out_specs,
        out_shape=[jax.ShapeDtypeStruct(x.shape, F32),
                   jax.ShapeDtypeStruct((1, n_seq, CONV_W - 1, W_CONV), F32),
                   st_shape, st_shape] + cast_shapes,
        scratch_shapes=[pltpu.VMEM((2, tc, nb, D_MODEL), F32),
                        pltpu.VMEM((2, tc, nb, D_MODEL), F32),
                        pltpu.VMEM((tm // 2, S_COLS), F32),
                        pltpu.VMEM((tm // 2, S_COLS), BF16),
                        pltpu.VMEM((tm // 2, S_COLS), BF16),
                        pltpu.VMEM((nb, S_COLS), F32),
                        pltpu.VMEM((2 * nb + tm, W_CONV), F32),
                        pltpu.SemaphoreType.DMA((2, SUBLANES)),
                        pltpu.SemaphoreType.DMA((2, SUBLANES)),
                        pltpu.SemaphoreType.DMA((2, CONV_W - 1))],
        compiler_params=pltpu.CompilerParams(dimension_semantics=("arbitrary",),
                                             vmem_limit_bytes=VMEM_LIMIT),
        name="mixer_carry" if carry else "mixer_step",
    )(x, *(() if carry else state), *wts, *(w for w, _ in cast))


def _ffn_body(xp_hbm, xs_hbm, nffn_ref, wg_hbm, wu_hbm, wd_hbm, nfin_ref, yp_hbm, ys_hbm,
              xbuf, obuf, abuf, wg_ref, wu_ref, wd_ref, insem, outsem, wsem, *, n_p, n_s, tm):
    i = pl.program_id(0)
    n_steps = n_p + n_s
    slot = lax.rem(i, 2)
    n_chunks = len(FFN_CHUNKS)

    def tile_copy(step, buf_slot, to_vmem, first):
        hbm = (xp_hbm if first else xs_hbm) if to_vmem else (yp_hbm if first else ys_hbm)
        h = hbm.at[pl.ds((step if first else step - n_p) * tm, tm), :]
        if to_vmem:
            return pltpu.make_async_copy(h, xbuf.at[buf_slot], insem.at[buf_slot])
        return pltpu.make_async_copy(obuf.at[buf_slot], h, outsem.at[buf_slot])

    def start(step, buf_slot, to_vmem):
        pl.when(step < n_p)(lambda: tile_copy(step, buf_slot, to_vmem, True).start())
        pl.when(step >= n_p)(lambda: tile_copy(step, buf_slot, to_vmem, False).start())

    def wait(buf_slot, to_vmem):
        tile_copy(0, buf_slot, to_vmem, True).wait()

    def gate_up_copies(c):
        return [pltpu.make_async_copy(hbm.at[c], ref.at[c], wsem.at[0, c])
                for hbm, ref in ((wg_hbm, wg_ref), (wu_hbm, wu_ref))]

    def down_copy(c):
        s, w = FFN_CHUNKS[c]
        return pltpu.make_async_copy(wd_hbm.at[pl.ds(s, w), :], wd_ref.at[pl.ds(s, w), :], wsem.at[1, c])

    @pl.when(i == 0)
    def _():
        start(0, 0, True)
        for c in range(n_chunks):
            for cp in gate_up_copies(c):
                cp.start()
        for c in range(n_chunks):
            down_copy(c).start()

    wait(slot, True)
    pl.when(i + 1 < n_steps)(lambda: start(i + 1, 1 - slot, True))
    pl.when(i >= 2)(lambda: wait(slot, False))

    def tile(fetch_weights):
        x = xbuf[slot]
        hn = _rms(x, nffn_ref[...]).astype(BF16)
        for c, (s, w) in enumerate(FFN_CHUNKS):
            if fetch_weights:
                for cp in gate_up_copies(c):
                    cp.wait()
            g = _dot(hn, wg_ref[c])
            up = _dot(hn, wu_ref[c])
            abuf[:, s:s + w] = (g * jax.nn.sigmoid(g) * up).astype(BF16)
        if fetch_weights:
            for c in range(n_chunks):
                down_copy(c).wait()
        acc = x + _dot(abuf[...], wd_ref[...])
        obuf[slot] = _rms(acc, nfin_ref[...])

    pl.when(i == 0)(lambda: tile(True))
    pl.when(i > 0)(lambda: tile(False))
    start(i, slot, False)

    @pl.when(i == n_steps - 1)
    def _():
        if n_steps >= 2:
            wait(1 - slot, False)
        wait(slot, False)


def _ffn(xp, xs, wts, *, tm):
    n_p, n_s = xp.shape[0] // tm, xs.shape[0] // tm
    assert xp.shape[0] == n_p * tm and xs.shape[0] == n_s * tm
    nffn, wg, wu, wd, nfin = wts
    assert wg.shape == wu.shape == (len(FFN_CHUNKS), D_MODEL, MXU_COLS) and wd.shape == (D_FF, D_MODEL)
    any_spec = pl.BlockSpec(memory_space=pl.ANY)
    return pl.pallas_call(
        functools.partial(_ffn_body, n_p=n_p, n_s=n_s, tm=tm),
        grid=(n_p + n_s,),
        in_specs=[any_spec, any_spec, _resident(nffn.shape), any_spec, any_spec, any_spec, _resident(nfin.shape)],
        out_specs=[any_spec, any_spec],
        out_shape=[jax.ShapeDtypeStruct(xp.shape, F32), jax.ShapeDtypeStruct(xs.shape, F32)],
        scratch_shapes=[pltpu.VMEM((2, tm, D_MODEL), F32),
                        pltpu.VMEM((2, tm, D_MODEL), F32),
                        pltpu.VMEM((tm, D_FF), BF16),
                        pltpu.VMEM(wg.shape, BF16),
                        pltpu.VMEM(wu.shape, BF16),
                        pltpu.VMEM(wd.shape, BF16),
                        pltpu.SemaphoreType.DMA((2,)),
                        pltpu.SemaphoreType.DMA((2,)),
                        pltpu.SemaphoreType.DMA((2, len(FFN_CHUNKS)))],
        compiler_params=pltpu.CompilerParams(dimension_semantics=("arbitrary",),
                                             vmem_limit_bytes=VMEM_LIMIT),
        name="ffn",
    )(xp, xs, *wts)


N_S5_OUT = 6
N_S5_SCRATCH = 4


def _prep_body(lre_ref, lim_ref, logdt_ref, btre_ref, btim_ref, cre_ref, cim_ref, *refs):
    n_cast = (len(refs) - N_S5_OUT - N_S5_SCRATCH) // 2
    cast_in, refs = refs[:n_cast], refs[n_cast:]
    s5_out, refs = refs[:N_S5_OUT], refs[N_S5_OUT:]
    cast_out, s5_scratch = refs[:n_cast], refs[n_cast:]
    for src, dst in zip(cast_in, cast_out):
        dst[...] = src[...].astype(BF16)
    pl.when(pl.program_id(0) == 0)(functools.partial(
        _s5_matrices, lre_ref, lim_ref, logdt_ref, btre_ref, btim_ref, cre_ref, cim_ref, *s5_out, *s5_scratch))


def _s5_matrices(lre_ref, lim_ref, logdt_ref, btre_ref, btim_ref, cre_ref, cim_ref,
                 lam2r_ref, lam2i_ref, bc2_ref, cc_ref, ccl_ref, d0_ref, bscr, lbscr, cscr, clscr):
    lre = lre_ref[...]
    lim = lim_ref[...]
    grp = (lax.broadcasted_iota(jnp.int32, (N_GROUPS, N_GROUPS), 0)
           == lax.broadcasted_iota(jnp.int32, (N_GROUPS, N_GROUPS), 1))
    logdt = jnp.sum(jnp.where(grp, jnp.broadcast_to(logdt_ref[...], (N_GROUPS, N_GROUPS)), 0.0),
                    axis=1, keepdims=True)
    dt = jnp.exp(logdt)
    mag = jnp.exp(lre * dt)
    lbr = mag * jnp.cos(lim * dt)
    lbi = mag * jnp.sin(lim * dt)
    l2r = lbr * lbr - lbi * lbi
    l2i = 2.0 * (lbr * lbi)
    den = lre * lre + lim * lim
    fr = ((lbr - 1.0) * lre + lbi * lim) / den
    fi = (lbi * lre - (lbr - 1.0) * lim) / den
    frb, fib = fr[:, None, :], fi[:, None, :]
    lrb, lib = lbr[:, None, :], lbi[:, None, :]
    bbr = frb * btre_ref[...] - fib * btim_ref[...]
    bbi = frb * btim_ref[...] + fib * btre_ref[...]
    lbbr = lrb * bbr - lib * bbi
    lbbi = lrb * bbi + lib * bbr
    cre = cre_ref[...]
    cim = cim_ref[...]
    clr = cre * lrb - cim * lib
    cli = cre * lib + cim * lrb

    for scr in (bscr, lbscr, cscr, clscr):
        scr[...] = jnp.zeros_like(scr)
    for g in range(N_GROUPS):
        n, k = divmod(g, GROUPS_PER_HALF)
        chans = slice(k * SSM_H, (k + 1) * SSM_H)
        states = slice(k * STATE_P, (k + 1) * STATE_P)
        im_states = slice(S_HALF + k * STATE_P, S_HALF + (k + 1) * STATE_P)
        for scr, re, im in ((bscr, bbr, bbi), (lbscr, lbbr, lbbi), (cscr, cre, -cim), (clscr, clr, -cli)):
            scr[n, chans, states] = re[g]
            scr[n, chans, im_states] = im[g]
        lam2r_ref[n, :, states] = jnp.broadcast_to(l2r[g:g + 1, :], (SUBLANES, STATE_P))
        lam2i_ref[n, :, states] = jnp.broadcast_to(l2i[g:g + 1, :], (SUBLANES, STATE_P))
    for n in range(N_HALVES):
        bc2_ref[n, 0:U_HALF, :] = lbscr[n].astype(BF16)
        bc2_ref[n, U_HALF:, :] = bscr[n].astype(BF16)
        c_t = cscr[n].T
        cc_ref[n] = c_t.astype(BF16)
        ccl_ref[n] = clscr[n].T.astype(BF16)
        b_hi, b_lo = _split_bf16(bscr[n])
        c_hi, c_lo = _split_bf16(c_t)
        d0_ref[n] = (_dot(b_hi, c_hi) + _dot(b_hi, c_lo) + _dot(b_lo, c_hi)).astype(BF16)


PREP_STEPS = 2


def _prep(lam_re, lam_im, log_dt, b_re, b_im, c_re, c_im, cast):
    s5_in = (lam_re, lam_im, log_dt.reshape(1, N_GROUPS), b_re.transpose(0, 2, 1), b_im.transpose(0, 2, 1),
             c_re, c_im)
    s5_shapes = [jax.ShapeDtypeStruct((N_HALVES, SUBLANES, S_HALF), F32),
                 jax.ShapeDtypeStruct((N_HALVES, SUBLANES, S_HALF), F32),
                 jax.ShapeDtypeStruct((N_HALVES, 2 * U_HALF, 2 * S_HALF), BF16),
                 jax.ShapeDtypeStruct((N_HALVES, 2 * S_HALF, U_HALF), BF16),
                 jax.ShapeDtypeStruct((N_HALVES, 2 * S_HALF, U_HALF), BF16),
                 jax.ShapeDtypeStruct((N_HALVES, U_HALF, U_HALF), BF16)]
    assert len(s5_shapes) == N_S5_OUT
    for w in cast:
        assert w.shape[0] % (PREP_STEPS * BF16_ROWS) == 0, w.shape
    cast_specs = [pl.BlockSpec((w.shape[0] // PREP_STEPS, w.shape[1]), lambda i: (i, 0)) for w in cast]
    whole = lambda a: pl.BlockSpec(a.shape, lambda i: (0,) * len(a.shape))
    return pl.pallas_call(
        _prep_body,
        grid=(PREP_STEPS,),
        in_specs=[whole(a) for a in s5_in] + cast_specs,
        out_specs=[whole(a) for a in s5_shapes] + cast_specs,
        out_shape=s5_shapes + [jax.ShapeDtypeStruct(w.shape, BF16) for w in cast],
        scratch_shapes=[pltpu.VMEM((N_HALVES, U_HALF, 2 * S_HALF), F32)] * N_S5_SCRATCH,
        compiler_params=pltpu.CompilerParams(dimension_semantics=("arbitrary",)),
        name="prep",
    )(*s5_in, *cast)


def kernel(x_prompt, x_sample, cache_conv, state_ssm_re, state_ssm_im, norm_mix, w_in, conv_w, ssm_lam_re, ssm_lam_im, ssm_log_dt, ssm_b_re, ssm_b_im, ssm_c_re, ssm_c_im, ssm_d, w_glu, b_glu, w_out, norm_ffn, w_gate, w_up, w_down, norm_final):
    assert norm_mix.shape[0] == 1, "single-layer trunk"
    n_st = N_GROUPS * STATE_P
    lam2r, lam2i, bc2, cc, ccl, d0, w_in_b, w_glu_b, w_out_b = _prep(
        ssm_lam_re[0], ssm_lam_im[0], ssm_log_dt[0], ssm_b_re[0], ssm_b_im[0], ssm_c_re[0], ssm_c_im[0],
        cast=(w_in[0], w_glu[0], w_out[0]))
    mix_w = (norm_mix, w_in_b, conv_w[0][:, None, :], lam2r, lam2i, bc2, cc, ccl, d0, ssm_d, w_glu_b, b_glu,
             w_out_b)

    bp, lp, _ = x_prompt.shape
    bs, ls, _ = x_sample.shape

    x1p, convp, rep, imp, wg, wu, wd = _mixer(
        x_prompt, mix_w, cast=((w_gate[0], MXU_COLS), (w_up[0], MXU_COLS), (w_down[0], None)),
        nb=SUBLANES, tc=128)
    ffn_w = (norm_ffn, wg, wu, wd, norm_final.reshape(1, D_MODEL))
    def state_major(h):
        return h[0].transpose(1, 2, 0).reshape(n_st, h.shape[1])

    sample_state = (cache_conv, state_major(state_ssm_re), state_major(state_ssm_im))
    x1s, convs, res, ims = _mixer(x_sample, mix_w, sample_state, nb=128, tc=SUBLANES)

    yp, ys = _ffn(x1p.reshape(bp * lp, D_MODEL), x1s.reshape(bs * ls, D_MODEL), ffn_w, tm=1024)
    yp = yp.reshape(bp, lp, D_MODEL)
    ys = ys.reshape(bs, ls, D_MODEL)

    def seq_major(h):
        return h.reshape(N_GROUPS, STATE_P, h.shape[1]).transpose(2, 0, 1)[None]

    return (yp, ys, convp, rep.reshape(1, bp, N_GROUPS, STATE_P), imp.reshape(1, bp, N_GROUPS, STATE_P),
            convs, seq_major(res), seq_major(ims))
```

```python
import functools

import jax
import jax.numpy as jnp
from jax import lax
from jax.experimental import pallas as pl
from jax.experimental.pallas import tpu as pltpu

D_MODEL = 1024
W_CONV = 512
W_SSM = 512
CONV_W = 3
N_GROUPS = 32
SSM_H = 16
STATE_P = 64
D_FF = 2816
EPS = 1e-5

SUBLANES = 8
BF16_ROWS = 16
GROUPS_PER_HALF = 16
N_HALVES = N_GROUPS // GROUPS_PER_HALF
U_HALF = GROUPS_PER_HALF * SSM_H
S_HALF = GROUPS_PER_HALF * STATE_P
S_COLS = 2 * N_GROUPS * STATE_P
SCAN_LANES = 512
MXU_COLS = 256
FFN_CHUNKS = tuple((s, MXU_COLS) for s in range(0, D_FF, MXU_COLS))
VMEM_LIMIT = 60 * 1024 * 1024

BF16 = jnp.bfloat16
F32 = jnp.float32


def _rms(x, g):
    y = x * lax.rsqrt(jnp.mean(x * x, axis=-1, keepdims=True) + EPS)
    return y * g


def _dot(a, b):
    return jnp.dot(a, b, preferred_element_type=F32)


def _split_bf16(a):
    hi = a.astype(BF16)
    return hi, (a - hi.astype(F32)).astype(BF16)


def _tile_copies(hbm, buf, sem, step, slot, *, nb, tc, per_seq, to_vmem):
    copies = []
    for k in range(SUBLANES):
        if per_seq:
            h = hbm.at[k, pl.ds(step * tc, tc), :]
            v = buf.at[slot, :, k, :]
        else:
            h = hbm.at[pl.ds(step * nb, nb), k, :]
            v = buf.at[slot, k]
        src, dst = (h, v) if to_vmem else (v, h)
        copies.append(pltpu.make_async_copy(src, dst, sem.at[slot, k]))
    return copies


def _mixer_body(x_hbm, *refs, nb, tc, n_steps, carry, n_cast):
    if carry:
        cache_ref = re0_ref = im0_ref = None
    else:
        cache_ref, re0_ref, im0_ref, *refs = refs
    (nmix_ref, win_ref, convw_ref, lam2r_ref, lam2i_ref, bc2_ref, cc_ref, ccl_ref, d0_ref, dskip_ref,
     wglu_ref, bglu_ref, wout_ref, *refs) = refs
    cast_in, refs = refs[:n_cast], refs[n_cast:]
    x1_hbm, convnew_hbm, hre_ref, him_ref, *refs = refs
    cast_out, refs = refs[:n_cast], refs[n_cast:]
    xbuf, obuf, gbuf, hb_odd, hb_prev, hstate, vbuf, insem, outsem, cachesem = refs
    tm = nb * tc
    n_pair = tc // 2
    tmh = n_pair * nb

    i = pl.program_id(0)
    slot = lax.rem(i, 2)
    cp = functools.partial(_tile_copies, nb=nb, tc=tc, per_seq=carry)
    in_copies = functools.partial(cp, x_hbm, xbuf, insem, to_vmem=True)
    out_copies = functools.partial(cp, x1_hbm, obuf, outsem, to_vmem=False)

    def cache_copies(hbm, row0, to_vmem):
        copies = []
        for k in range(CONV_W - 1):
            h = hbm.at[0, pl.ds(0 if carry else i * nb, nb), k, :]
            v = vbuf.at[pl.ds(row0 + k * nb, nb), :]
            src, dst = (h, v) if to_vmem else (v, h)
            copies.append(pltpu.make_async_copy(src, dst, cachesem.at[int(to_vmem), k]))
        return copies

    @pl.when(i == 0)
    def _():
        for c in in_copies(0, 0):
            c.start()

    if not carry:
        for c in cache_copies(cache_ref, 0, True):
            c.start()
    for c in in_copies(i, slot):
        c.wait()
    if not carry:
        for c in cache_copies(cache_ref, 0, True):
            c.wait()

    @pl.when(i + 1 < n_steps)
    def _():
        for c in in_copies(i + 1, 1 - slot):
            c.start()

    @pl.when(i >= 2)
    def _():
        for c in out_copies(i - 2, slot):
            c.wait()

    if carry:
        @pl.when(i == 0)
        def _():
            vbuf[0:2 * nb, :] = jnp.zeros((2 * nb, W_CONV), F32)
            hstate[...] = jnp.zeros((nb, S_COLS), F32)
    else:
        for n in range(N_HALVES):
            hstate[:, 2 * n * S_HALF:(2 * n + 1) * S_HALF] = re0_ref[n * S_HALF:(n + 1) * S_HALF, :].T
            hstate[:, (2 * n + 1) * S_HALF:(2 * n + 2) * S_HALF] = im0_ref[n * S_HALF:(n + 1) * S_HALF, :].T

    x = xbuf[slot].reshape(tm, D_MODEL)
    inv_rms = lax.rsqrt(jnp.mean(x * x, axis=-1, keepdims=True) + EPS)
    xg = (x * nmix_ref[...]).astype(BF16)
    xn = (x * inv_rms * nmix_ref[...]).astype(BF16)

    u = inv_rms * _dot(xg, win_ref[:, 3 * W_CONV:])
    u_pairs = u.reshape(n_pair, 2, nb, W_SSM)
    u_e = u_pairs[:, 0].reshape(tmh, W_SSM)
    u_o = u_pairs[:, 1].reshape(tmh, W_SSM)
    ub_e = u_e.astype(BF16)
    ub_o = u_o.astype(BF16)
    for n in range(N_HALVES):
        cols = slice(n * U_HALF, (n + 1) * U_HALF)
        gbuf[:, 2 * n * S_HALF:(2 * n + 2) * S_HALF] = _dot(
            jnp.concatenate([ub_e[:, cols], ub_o[:, cols]], axis=1), bc2_ref[n])

    for src, dst in zip(cast_in, cast_out):
        dst[...] = src[...].astype(BF16)

    def scan_step(h, rows, lr, li, re_cols, im_cols):
        hr, hi = h
        return (lr * hr - li * hi + gbuf[rows, re_cols], lr * hi + li * hr + gbuf[rows, im_cols])

    def store_pair(buf, row0, pair, re_cols, im_cols):
        rows = slice(row0, row0 + BF16_ROWS)
        buf[rows, re_cols] = jnp.concatenate([h[0] for h in pair], axis=0).astype(BF16)
        buf[rows, im_cols] = jnp.concatenate([h[1] for h in pair], axis=0).astype(BF16)

    for n in range(N_HALVES):
        for c0 in range(0, S_HALF, SCAN_LANES):
            re_cols = slice(2 * n * S_HALF + c0, 2 * n * S_HALF + c0 + SCAN_LANES)
            im_cols = slice((2 * n + 1) * S_HALF + c0, (2 * n + 1) * S_HALF + c0 + SCAN_LANES)
            lr = lam2r_ref[n, :, c0:c0 + SCAN_LANES]
            li = lam2i_ref[n, :, c0:c0 + SCAN_LANES]
            step = functools.partial(scan_step, lr=lr, li=li, re_cols=re_cols, im_cols=im_cols)
            if nb == SUBLANES:
                h = (hstate[:, re_cols], hstate[:, im_cols])
                for k in range(0, n_pair, 2):
                    h0 = step(h, slice(k * nb, (k + 1) * nb))
                    h1 = step(h0, slice((k + 1) * nb, (k + 2) * nb))
                    store_pair(hb_odd, k * nb, (h0, h1), re_cols, im_cols)
                    store_pair(hb_prev, k * nb, (h, h0), re_cols, im_cols)
                    h = h1
                hstate[:, re_cols], hstate[:, im_cols] = h
            else:
                for r0 in range(0, nb, BF16_ROWS):
                    seqs = [slice(r0 + j * SUBLANES, r0 + (j + 1) * SUBLANES) for j in range(2)]
                    pair = [(hstate[sq, re_cols], hstate[sq, im_cols]) for sq in seqs]
                    for k in range(n_pair):
                        store_pair(hb_prev, k * nb + r0, pair, re_cols, im_cols)
                        pair = [step(h, slice(k * nb + sq.start, k * nb + sq.stop)) for h, sq in zip(pair, seqs)]
                        store_pair(hb_odd, k * nb + r0, pair, re_cols, im_cols)
                    for h, sq in zip(pair, seqs):
                        hstate[sq, re_cols], hstate[sq, im_cols] = h

    conv_chunks = []
    for c0 in range(0, W_CONV, MXU_COLS):
        cols = slice(c0, c0 + MXU_COLS)
        b_g = _dot(xn, win_ref[:, c0:c0 + MXU_COLS])
        c_g = _dot(xn, win_ref[:, W_CONV + c0:W_CONV + c0 + MXU_COLS])
        xc = _dot(xn, win_ref[:, 2 * W_CONV + c0:2 * W_CONV + c0 + MXU_COLS])
        v = c_g * xc
        vbuf[2 * nb:2 * nb + tm, cols] = v
        conv = (convw_ref[0, :, cols] * vbuf[0:tm, cols] + convw_ref[1, :, cols] * vbuf[nb:nb + tm, cols]
                + convw_ref[2, :, cols] * v)
        conv_chunks.append((b_g * conv).astype(BF16))
    conv_out = jnp.concatenate(conv_chunks, axis=-1)
    if carry:
        vbuf[0:2 * nb, :] = vbuf[tm:tm + 2 * nb, :]
    x1 = x + _dot(conv_out, wout_ref[0:W_CONV, :])

    last = hstate[...]
    for n in range(N_HALVES):
        last_re = last[:, 2 * n * S_HALF:(2 * n + 1) * S_HALF]
        last_im = last[:, (2 * n + 1) * S_HALF:(2 * n + 2) * S_HALF]
        if carry:
            hre_ref[:, n * S_HALF:(n + 1) * S_HALF] = last_re
            him_ref[:, n * S_HALF:(n + 1) * S_HALF] = last_im
        else:
            hre_ref[n * S_HALF:(n + 1) * S_HALF, :] = last_re.T
            him_ref[n * S_HALF:(n + 1) * S_HALF, :] = last_im.T

    zs = []
    for n in range(N_HALVES):
        cols = slice(n * U_HALF, (n + 1) * U_HALF)
        st_cols = slice(2 * n * S_HALF, (2 * n + 2) * S_HALF)
        y_o = _dot(hb_odd[:, st_cols], cc_ref[n]) + dskip_ref[:, cols] * u_o[:, cols]
        y_e = (_dot(hb_prev[:, st_cols], ccl_ref[n]) + _dot(ub_e[:, cols], d0_ref[n])
               + dskip_ref[:, cols] * u_e[:, cols])
        y = jnp.stack([y_e.reshape(n_pair, nb, U_HALF), y_o.reshape(n_pair, nb, U_HALF)],
                      axis=1).reshape(tm, U_HALF)
        zs.append(jax.nn.gelu(y))
    zb = jnp.concatenate([z.astype(BF16) for z in zs], axis=-1)
    ssm_chunks = []
    for n, z in enumerate(zs):
        cols = slice(n * U_HALF, (n + 1) * U_HALF)
        gate = jax.nn.sigmoid(_dot(zb, wglu_ref[:, cols]) + bglu_ref[:, cols])
        ssm_chunks.append((z * gate).astype(BF16))
    ssm_out = jnp.concatenate(ssm_chunks, axis=-1)

    x1 = x1 + _dot(ssm_out, wout_ref[W_CONV:, :])

    obuf[slot] = x1.reshape(tc, nb, D_MODEL)
    for c in out_copies(i, slot):
        c.start()

    if not carry:
        for c in cache_copies(convnew_hbm, tm, False):
            c.start()
        for c in cache_copies(convnew_hbm, tm, False):
            c.wait()

    @pl.when(i == n_steps - 1)
    def _():
        if carry:
            for c in cache_copies(convnew_hbm, tm, False):
                c.start()
            for c in cache_copies(convnew_hbm, tm, False):
                c.wait()
        if n_steps >= 2:
            for c in out_copies(i - 1, 1 - slot):
                c.wait()
        for c in out_copies(i, slot):
            c.wait()


def _resident(shape):
    nd = len(shape)
    return pl.BlockSpec(shape, lambda i: (0,) * nd, pipeline_mode=pl.Buffered(1))


def _mixer(x, wts, state=None, cast=(), *, nb, tc):
    n_seq, n_time, _ = x.shape
    carry = state is None
    if carry:
        assert n_seq == nb == SUBLANES and n_time % tc == 0 and tc % 4 == 0
        n_steps = n_time // tc
    else:
        assert n_time == tc == SUBLANES and n_seq % nb == 0 and nb % BF16_ROWS == 0
        n_steps = n_seq // nb
    tm = nb * tc
    n_st = N_GROUPS * STATE_P
    body = functools.partial(_mixer_body, nb=nb, tc=tc, n_steps=n_steps, carry=carry, n_cast=len(cast))
    any_spec = pl.BlockSpec(memory_space=pl.ANY)
    if carry:
        st_spec = pl.BlockSpec((nb, n_st), lambda i: (0, 0))
        st_shape = jax.ShapeDtypeStruct((n_seq, n_st), F32)
    else:
        st_spec = pl.BlockSpec((n_st, nb), lambda i: (0, i))
        st_shape = jax.ShapeDtypeStruct((n_st, n_seq), F32)
    state_specs = [] if carry else [any_spec, st_spec, st_spec]
    for w in cast:
        assert w.shape[0] % (n_steps * BF16_ROWS) == 0, w.shape
    cast_specs = [pl.BlockSpec((w.shape[0] // n_steps, w.shape[1]), lambda i: (i, 0)) for w in cast]
    return pl.pallas_call(
        body,
        grid=(n_steps,),
        in_specs=[any_spec] + state_specs + [_resident(w.shape) for w in wts] + cast_specs,
        out_specs=[any_spec, any_spec, st_spec, st_spec] + cast_specs,
        out_shape=[jax.ShapeDtypeStruct(x.shape, F32),
                   jax.ShapeDtypeStruct((1, n_seq, CONV_W - 1, W_CONV), F32),
                   st_shape, st_shape]
                  + [jax.ShapeDtypeStruct(w.shape, BF16) for w in cast],
        scratch_shapes=[pltpu.VMEM((2, tc, nb, D_MODEL), F32),
                        pltpu.VMEM((2, tc, nb, D_MODEL), F32),
                        pltpu.VMEM((tm // 2, S_COLS), F32),
                        pltpu.VMEM((tm // 2, S_COLS), BF16),
                        pltpu.VMEM((tm // 2, S_COLS), BF16),
                        pltpu.VMEM((nb, S_COLS), F32),
                        pltpu.VMEM((2 * nb + tm, W_CONV), F32),
                        pltpu.SemaphoreType.DMA((2, SUBLANES)),
                        pltpu.SemaphoreType.DMA((2, SUBLANES)),
                        pltpu.SemaphoreType.DMA((2, CONV_W - 1))],
        compiler_params=pltpu.CompilerParams(dimension_semantics=("arbitrary",),
                                             vmem_limit_bytes=VMEM_LIMIT),
        name="mixer_carry" if carry else "mixer_step",
    )(x, *(() if carry else state), *wts, *cast)


def _ffn_body(xp_hbm, xs_hbm, nffn_ref, wg_ref, wu_ref, wd_hbm, nfin_ref, yp_hbm, ys_hbm,
              xbuf, obuf, abuf, wd_ref, insem, outsem, wsem, *, n_p, n_s, tm):
    i = pl.program_id(0)
    n_steps = n_p + n_s
    slot = lax.rem(i, 2)
    wd_copy = pltpu.make_async_copy(wd_hbm, wd_ref, wsem.at[0])

    def tile_copy(step, buf_slot, to_vmem, first):
        hbm = (xp_hbm if first else xs_hbm) if to_vmem else (yp_hbm if first else ys_hbm)
        h = hbm.at[pl.ds((step if first else step - n_p) * tm, tm), :]
        if to_vmem:
            return pltpu.make_async_copy(h, xbuf.at[buf_slot], insem.at[buf_slot])
        return pltpu.make_async_copy(obuf.at[buf_slot], h, outsem.at[buf_slot])

    def start(step, buf_slot, to_vmem):
        pl.when(step < n_p)(lambda: tile_copy(step, buf_slot, to_vmem, True).start())
        pl.when(step >= n_p)(lambda: tile_copy(step, buf_slot, to_vmem, False).start())

    def wait(buf_slot, to_vmem):
        tile_copy(0, buf_slot, to_vmem, True).wait()

    @pl.when(i == 0)
    def _():
        start(0, 0, True)
        wd_copy.start()

    wait(slot, True)
    pl.when(i + 1 < n_steps)(lambda: start(i + 1, 1 - slot, True))
    pl.when(i >= 2)(lambda: wait(slot, False))

    x = xbuf[slot]
    hn = _rms(x, nffn_ref[...]).astype(BF16)
    for s, w in FFN_CHUNKS:
        g = _dot(hn, wg_ref[:, s:s + w])
        up = _dot(hn, wu_ref[:, s:s + w])
        abuf[:, s:s + w] = (g * jax.nn.sigmoid(g) * up).astype(BF16)
    pl.when(i == 0)(lambda: wd_copy.wait())
    acc = x + _dot(abuf[...], wd_ref[...])
    obuf[slot] = _rms(acc, nfin_ref[...])
    start(i, slot, False)

    @pl.when(i == n_steps - 1)
    def _():
        if n_steps >= 2:
            wait(1 - slot, False)
        wait(slot, False)


def _ffn(xp, xs, wts, *, tm):
    n_p, n_s = xp.shape[0] // tm, xs.shape[0] // tm
    assert xp.shape[0] == n_p * tm and xs.shape[0] == n_s * tm
    nffn, wg, wu, wd, nfin = wts
    any_spec = pl.BlockSpec(memory_space=pl.ANY)
    return pl.pallas_call(
        functools.partial(_ffn_body, n_p=n_p, n_s=n_s, tm=tm),
        grid=(n_p + n_s,),
        in_specs=[any_spec, any_spec, _resident(nffn.shape), _resident(wg.shape), _resident(wu.shape), any_spec,
                  _resident(nfin.shape)],
        out_specs=[any_spec, any_spec],
        out_shape=[jax.ShapeDtypeStruct(xp.shape, F32), jax.ShapeDtypeStruct(xs.shape, F32)],
        scratch_shapes=[pltpu.VMEM((2, tm, D_MODEL), F32),
                        pltpu.VMEM((2, tm, D_MODEL), F32),
                        pltpu.VMEM((tm, D_FF), BF16),
                        pltpu.VMEM(wd.shape, BF16),
                        pltpu.SemaphoreType.DMA((2,)),
                        pltpu.SemaphoreType.DMA((2,)),
                        pltpu.SemaphoreType.DMA((1,))],
        compiler_params=pltpu.CompilerParams(dimension_semantics=("arbitrary",),
                                             vmem_limit_bytes=VMEM_LIMIT),
        name="ffn",
    )(xp, xs, *wts)


N_S5_OUT = 6
N_S5_SCRATCH = 4


def _prep_body(lre_ref, lim_ref, logdt_ref, btre_ref, btim_ref, cre_ref, cim_ref, *refs):
    n_cast = (len(refs) - N_S5_OUT - N_S5_SCRATCH) // 2
    cast_in, refs = refs[:n_cast], refs[n_cast:]
    s5_out, refs = refs[:N_S5_OUT], refs[N_S5_OUT:]
    cast_out, s5_scratch = refs[:n_cast], refs[n_cast:]
    for src, dst in zip(cast_in, cast_out):
        dst[...] = src[...].astype(BF16)
    pl.when(pl.program_id(0) == 0)(functools.partial(
        _s5_matrices, lre_ref, lim_ref, logdt_ref, btre_ref, btim_ref, cre_ref, cim_ref, *s5_out, *s5_scratch))


def _s5_matrices(lre_ref, lim_ref, logdt_ref, btre_ref, btim_ref, cre_ref, cim_ref,
                 lam2r_ref, lam2i_ref, bc2_ref, cc_ref, ccl_ref, d0_ref, bscr, lbscr, cscr, clscr):
    lre = lre_ref[...]
    lim = lim_ref[...]
    grp = (lax.broadcasted_iota(jnp.int32, (N_GROUPS, N_GROUPS), 0)
           == lax.broadcasted_iota(jnp.int32, (N_GROUPS, N_GROUPS), 1))
    logdt = jnp.sum(jnp.where(grp, jnp.broadcast_to(logdt_ref[...], (N_GROUPS, N_GROUPS)), 0.0),
                    axis=1, keepdims=True)
    dt = jnp.exp(logdt)
    mag = jnp.exp(lre * dt)
    lbr = mag * jnp.cos(lim * dt)
    lbi = mag * jnp.sin(lim * dt)
    l2r = lbr * lbr - lbi * lbi
    l2i = 2.0 * (lbr * lbi)
    den = lre * lre + lim * lim
    fr = ((lbr - 1.0) * lre + lbi * lim) / den
    fi = (lbi * lre - (lbr - 1.0) * lim) / den
    frb, fib = fr[:, None, :], fi[:, None, :]
    lrb, lib = lbr[:, None, :], lbi[:, None, :]
    bbr = frb * btre_ref[...] - fib * btim_ref[...]
    bbi = frb * btim_ref[...] + fib * btre_ref[...]
    lbbr = lrb * bbr - lib * bbi
    lbbi = lrb * bbi + lib * bbr
    cre = cre_ref[...]
    cim = cim_ref[...]
    clr = cre * lrb - cim * lib
    cli = cre * lib + cim * lrb

    for scr in (bscr, lbscr, cscr, clscr):
        scr[...] = jnp.zeros_like(scr)
    for g in range(N_GROUPS):
        n, k = divmod(g, GROUPS_PER_HALF)
        chans = slice(k * SSM_H, (k + 1) * SSM_H)
        states = slice(k * STATE_P, (k + 1) * STATE_P)
        im_states = slice(S_HALF + k * STATE_P, S_HALF + (k + 1) * STATE_P)
        for scr, re, im in ((bscr, bbr, bbi), (lbscr, lbbr, lbbi), (cscr, cre, -cim), (clscr, clr, -cli)):
            scr[n, chans, states] = re[g]
            scr[n, chans, im_states] = im[g]
        lam2r_ref[n, :, states] = jnp.broadcast_to(l2r[g:g + 1, :], (SUBLANES, STATE_P))
        lam2i_ref[n, :, states] = jnp.broadcast_to(l2i[g:g + 1, :], (SUBLANES, STATE_P))
    for n in range(N_HALVES):
        bc2_ref[n, 0:U_HALF, :] = lbscr[n].astype(BF16)
        bc2_ref[n, U_HALF:, :] = bscr[n].astype(BF16)
        c_t = cscr[n].T
        cc_ref[n] = c_t.astype(BF16)
        ccl_ref[n] = clscr[n].T.astype(BF16)
        b_hi, b_lo = _split_bf16(bscr[n])
        c_hi, c_lo = _split_bf16(c_t)
        d0_ref[n] = (_dot(b_hi, c_hi) + _dot(b_hi, c_lo) + _dot(b_lo, c_hi)).astype(BF16)


PREP_STEPS = 2


def _prep(lam_re, lam_im, log_dt, b_re, b_im, c_re, c_im, cast):
    s5_in = (lam_re, lam_im, log_dt.reshape(1, N_GROUPS), b_re.transpose(0, 2, 1), b_im.transpose(0, 2, 1),
             c_re, c_im)
    s5_shapes = [jax.ShapeDtypeStruct((N_HALVES, SUBLANES, S_HALF), F32),
                 jax.ShapeDtypeStruct((N_HALVES, SUBLANES, S_HALF), F32),
                 jax.ShapeDtypeStruct((N_HALVES, 2 * U_HALF, 2 * S_HALF), BF16),
                 jax.ShapeDtypeStruct((N_HALVES, 2 * S_HALF, U_HALF), BF16),
                 jax.ShapeDtypeStruct((N_HALVES, 2 * S_HALF, U_HALF), BF16),
                 jax.ShapeDtypeStruct((N_HALVES, U_HALF, U_HALF), BF16)]
    assert len(s5_shapes) == N_S5_OUT
    for w in cast:
        assert w.shape[0] % (PREP_STEPS * BF16_ROWS) == 0, w.shape
    cast_specs = [pl.BlockSpec((w.shape[0] // PREP_STEPS, w.shape[1]), lambda i: (i, 0)) for w in cast]
    whole = lambda a: pl.BlockSpec(a.shape, lambda i: (0,) * len(a.shape))
    return pl.pallas_call(
        _prep_body,
        grid=(PREP_STEPS,),
        in_specs=[whole(a) for a in s5_in] + cast_specs,
        out_specs=[whole(a) for a in s5_shapes] + cast_specs,
        out_shape=s5_shapes + [jax.ShapeDtypeStruct(w.shape, BF16) for w in cast],
        scratch_shapes=[pltpu.VMEM((N_HALVES, U_HALF, 2 * S_HALF), F32)] * N_S5_SCRATCH,
        compiler_params=pltpu.CompilerParams(dimension_semantics=("arbitrary",)),
        name="prep",
    )(*s5_in, *cast)


def kernel(x_prompt, x_sample, cache_conv, state_ssm_re, state_ssm_im, norm_mix, w_in, conv_w, ssm_lam_re, ssm_lam_im, ssm_log_dt, ssm_b_re, ssm_b_im, ssm_c_re, ssm_c_im, ssm_d, w_glu, b_glu, w_out, norm_ffn, w_gate, w_up, w_down, norm_final):
    assert norm_mix.shape[0] == 1, "single-layer trunk"
    n_st = N_GROUPS * STATE_P
    lam2r, lam2i, bc2, cc, ccl, d0, w_in_b, w_glu_b, w_out_b = _prep(
        ssm_lam_re[0], ssm_lam_im[0], ssm_log_dt[0], ssm_b_re[0], ssm_b_im[0], ssm_c_re[0], ssm_c_im[0],
        cast=(w_in[0], w_glu[0], w_out[0]))
    mix_w = (norm_mix, w_in_b, conv_w[0][:, None, :], lam2r, lam2i, bc2, cc, ccl, d0, ssm_d, w_glu_b, b_glu,
             w_out_b)

    bp, lp, _ = x_prompt.shape
    bs, ls, _ = x_sample.shape

    x1p, convp, rep, imp, wg, wu, wd = _mixer(x_prompt, mix_w, cast=(w_gate[0], w_up[0], w_down[0]),
                                              nb=SUBLANES, tc=128)
    ffn_w = (norm_ffn, wg, wu, wd, norm_final.reshape(1, D_MODEL))
    def state_major(h):
        return h[0].transpose(1, 2, 0).reshape(n_st, h.shape[1])

    sample_state = (cache_conv, state_major(state_ssm_re), state_major(state_ssm_im))
    x1s, convs, res, ims = _mixer(x_sample, mix_w, sample_state, nb=128, tc=SUBLANES)

    yp, ys = _ffn(x1p.reshape(bp * lp, D_MODEL), x1s.reshape(bs * ls, D_MODEL), ffn_w, tm=1024)
    yp = yp.reshape(bp, lp, D_MODEL)
    ys = ys.reshape(bs, ls, D_MODEL)

    def seq_major(h):
        return h.reshape(N_GROUPS, STATE_P, h.shape[1]).transpose(2, 0, 1)[None]

    return (yp, ys, convp, rep.reshape(1, bp, N_GROUPS, STATE_P), imp.reshape(1, bp, N_GROUPS, STATE_P),
            convs, seq_major(res), seq_major(ims))
```

```python
import functools

import jax
import jax.numpy as jnp
from jax import lax
from jax.experimental import pallas as pl
from jax.experimental.pallas import tpu as pltpu

D_MODEL = 1024
W_CONV = 512
W_SSM = 512
CONV_W = 3
N_GROUPS = 32
SSM_H = 16
STATE_P = 64
D_FF = 2816
EPS = 1e-5

SUBLANES = 8
BF16_ROWS = 16
GROUPS_PER_HALF = 16
N_HALVES = N_GROUPS // GROUPS_PER_HALF
U_HALF = GROUPS_PER_HALF * SSM_H
S_HALF = GROUPS_PER_HALF * STATE_P
S_COLS = 2 * N_GROUPS * STATE_P
SCAN_LANES = 512
MXU_COLS = 256
FFN_CHUNKS = tuple((s, MXU_COLS) for s in range(0, D_FF, MXU_COLS))
VMEM_LIMIT = 60 * 1024 * 1024

BF16 = jnp.bfloat16
F32 = jnp.float32


def _rms(x, g):
    y = x * lax.rsqrt(jnp.mean(x * x, axis=-1, keepdims=True) + EPS)
    return y * g


def _dot(a, b):
    return jnp.dot(a, b, preferred_element_type=F32)


def _split_bf16(a):
    hi = a.astype(BF16)
    return hi, (a - hi.astype(F32)).astype(BF16)


def _tile_copies(hbm, buf, sem, step, slot, *, nb, tc, per_seq, to_vmem):
    copies = []
    for k in range(SUBLANES):
        if per_seq:
            h = hbm.at[k, pl.ds(step * tc, tc), :]
            v = buf.at[slot, :, k, :]
        else:
            h = hbm.at[pl.ds(step * nb, nb), k, :]
            v = buf.at[slot, k]
        src, dst = (h, v) if to_vmem else (v, h)
        copies.append(pltpu.make_async_copy(src, dst, sem.at[slot, k]))
    return copies


N_MIX_W = 13
W_IN, BC2, CC, CCL, D0, W_GLU, W_OUT = 1, 5, 6, 7, 8, 10, 12
STREAM_ORDER = (BC2, W_OUT, CC, CCL, D0, W_GLU)


def _mixer_body(x_hbm, *refs, nb, tc, n_steps, carry, n_cast):
    if carry:
        cache_ref = re0_ref = im0_ref = None
    else:
        cache_ref, re0_ref, im0_ref, *refs = refs
    wts, refs = list(refs[:N_MIX_W]), refs[N_MIX_W:]
    cast_in, refs = refs[:n_cast], refs[n_cast:]
    x1_hbm, convnew_hbm, hre_ref, him_ref, *refs = refs
    cast_out, refs = refs[:n_cast], refs[n_cast:]
    xbuf, obuf, gbuf, hb_odd, hb_prev, hstate, vbuf, insem, outsem, cachesem, *refs = refs
    if carry:
        w_copies = {}
    else:
        assert n_steps == 1
        *wbufs, wsem = refs
        win_hbm, win_buf = wts[W_IN], wbufs[0]
        u_cols, rest_cols = pl.ds(3 * W_CONV, W_SSM), pl.ds(0, 3 * W_CONV)
        w_copies = {"u": pltpu.make_async_copy(win_hbm.at[:, u_cols], win_buf.at[:, u_cols], wsem.at[0]),
                    W_IN: pltpu.make_async_copy(win_hbm.at[:, rest_cols], win_buf.at[:, rest_cols], wsem.at[1])}
        wts[W_IN] = win_buf
        for j, (k, buf) in enumerate(zip(STREAM_ORDER, wbufs[1:])):
            w_copies[k] = pltpu.make_async_copy(wts[k], buf, wsem.at[2 + j])
            wts[k] = buf
    (nmix_ref, win_ref, convw_ref, lam2r_ref, lam2i_ref, bc2_ref, cc_ref, ccl_ref, d0_ref, dskip_ref,
     wglu_ref, bglu_ref, wout_ref) = wts

    def wait_w(*keys):
        for k in keys:
            if k in w_copies:
                w_copies[k].wait()
    tm = nb * tc
    n_pair = tc // 2
    tmh = n_pair * nb

    i = pl.program_id(0)
    slot = lax.rem(i, 2)
    cp = functools.partial(_tile_copies, nb=nb, tc=tc, per_seq=carry)
    in_copies = functools.partial(cp, x_hbm, xbuf, insem, to_vmem=True)
    out_copies = functools.partial(cp, x1_hbm, obuf, outsem, to_vmem=False)

    def cache_copies(hbm, row0, to_vmem):
        copies = []
        for k in range(CONV_W - 1):
            h = hbm.at[0, pl.ds(0 if carry else i * nb, nb), k, :]
            v = vbuf.at[pl.ds(row0 + k * nb, nb), :]
            src, dst = (h, v) if to_vmem else (v, h)
            copies.append(pltpu.make_async_copy(src, dst, cachesem.at[int(to_vmem), k]))
        return copies

    @pl.when(i == 0)
    def _():
        for c in in_copies(0, 0):
            c.start()

    if not carry:
        for c in cache_copies(cache_ref, 0, True):
            c.start()
        for k in ("u", BC2, W_IN) + STREAM_ORDER[1:]:
            w_copies[k].start()
    for c in in_copies(i, slot):
        c.wait()
    if not carry:
        for c in cache_copies(cache_ref, 0, True):
            c.wait()

    @pl.when(i + 1 < n_steps)
    def _():
        for c in in_copies(i + 1, 1 - slot):
            c.start()

    @pl.when(i >= 2)
    def _():
        for c in out_copies(i - 2, slot):
            c.wait()

    if carry:
        @pl.when(i == 0)
        def _():
            vbuf[0:2 * nb, :] = jnp.zeros((2 * nb, W_CONV), F32)
            hstate[...] = jnp.zeros((nb, S_COLS), F32)
    else:
        for n in range(N_HALVES):
            hstate[:, 2 * n * S_HALF:(2 * n + 1) * S_HALF] = re0_ref[n * S_HALF:(n + 1) * S_HALF, :].T
            hstate[:, (2 * n + 1) * S_HALF:(2 * n + 2) * S_HALF] = im0_ref[n * S_HALF:(n + 1) * S_HALF, :].T

    x = xbuf[slot].reshape(tm, D_MODEL)
    inv_rms = lax.rsqrt(jnp.mean(x * x, axis=-1, keepdims=True) + EPS)
    xg = (x * nmix_ref[...]).astype(BF16)
    xn = (x * inv_rms * nmix_ref[...]).astype(BF16)

    wait_w("u")
    u = inv_rms * _dot(xg, win_ref[:, 3 * W_CONV:])
    u_pairs = u.reshape(n_pair, 2, nb, W_SSM)
    u_e = u_pairs[:, 0].reshape(tmh, W_SSM)
    u_o = u_pairs[:, 1].reshape(tmh, W_SSM)
    ub_e = u_e.astype(BF16)
    ub_o = u_o.astype(BF16)
    wait_w(BC2, W_IN)
    for n in range(N_HALVES):
        cols = slice(n * U_HALF, (n + 1) * U_HALF)
        gbuf[:, 2 * n * S_HALF:(2 * n + 2) * S_HALF] = _dot(
            jnp.concatenate([ub_e[:, cols], ub_o[:, cols]], axis=1), bc2_ref[n])

    for src, dst in zip(cast_in, cast_out):
        dst[...] = src[...].astype(BF16)

    def scan_step(h, rows, lr, li, re_cols, im_cols):
        hr, hi = h
        return (lr * hr - li * hi + gbuf[rows, re_cols], lr * hi + li * hr + gbuf[rows, im_cols])

    def store_pair(buf, row0, pair, re_cols, im_cols):
        rows = slice(row0, row0 + BF16_ROWS)
        buf[rows, re_cols] = jnp.concatenate([h[0] for h in pair], axis=0).astype(BF16)
        buf[rows, im_cols] = jnp.concatenate([h[1] for h in pair], axis=0).astype(BF16)

    for n in range(N_HALVES):
        for c0 in range(0, S_HALF, SCAN_LANES):
            re_cols = slice(2 * n * S_HALF + c0, 2 * n * S_HALF + c0 + SCAN_LANES)
            im_cols = slice((2 * n + 1) * S_HALF + c0, (2 * n + 1) * S_HALF + c0 + SCAN_LANES)
            lr = lam2r_ref[n, :, c0:c0 + SCAN_LANES]
            li = lam2i_ref[n, :, c0:c0 + SCAN_LANES]
            step = functools.partial(scan_step, lr=lr, li=li, re_cols=re_cols, im_cols=im_cols)
            if nb == SUBLANES:
                h = (hstate[:, re_cols], hstate[:, im_cols])
                for k in range(0, n_pair, 2):
                    h0 = step(h, slice(k * nb, (k + 1) * nb))
                    h1 = step(h0, slice((k + 1) * nb, (k + 2) * nb))
                    store_pair(hb_odd, k * nb, (h0, h1), re_cols, im_cols)
                    store_pair(hb_prev, k * nb, (h, h0), re_cols, im_cols)
                    h = h1
                hstate[:, re_cols], hstate[:, im_cols] = h
            else:
                for r0 in range(0, nb, BF16_ROWS):
                    seqs = [slice(r0 + j * SUBLANES, r0 + (j + 1) * SUBLANES) for j in range(2)]
                    pair = [(hstate[sq, re_cols], hstate[sq, im_cols]) for sq in seqs]
                    for k in range(n_pair):
                        store_pair(hb_prev, k * nb + r0, pair, re_cols, im_cols)
                        pair = [step(h, slice(k * nb + sq.start, k * nb + sq.stop)) for h, sq in zip(pair, seqs)]
                        store_pair(hb_odd, k * nb + r0, pair, re_cols, im_cols)
                    for h, sq in zip(pair, seqs):
                        hstate[sq, re_cols], hstate[sq, im_cols] = h

    conv_chunks = []
    for c0 in range(0, W_CONV, MXU_COLS):
        cols = slice(c0, c0 + MXU_COLS)
        b_g = _dot(xn, win_ref[:, c0:c0 + MXU_COLS])
        c_g = _dot(xn, win_ref[:, W_CONV + c0:W_CONV + c0 + MXU_COLS])
        xc = _dot(xn, win_ref[:, 2 * W_CONV + c0:2 * W_CONV + c0 + MXU_COLS])
        v = c_g * xc
        vbuf[2 * nb:2 * nb + tm, cols] = v
        conv = (convw_ref[0, :, cols] * vbuf[0:tm, cols] + convw_ref[1, :, cols] * vbuf[nb:nb + tm, cols]
                + convw_ref[2, :, cols] * v)
        conv_chunks.append((b_g * conv).astype(BF16))
    conv_out = jnp.concatenate(conv_chunks, axis=-1)
    if carry:
        vbuf[0:2 * nb, :] = vbuf[tm:tm + 2 * nb, :]
    wait_w(*STREAM_ORDER[1:])
    x1 = x + _dot(conv_out, wout_ref[0:W_CONV, :])

    last = hstate[...]
    for n in range(N_HALVES):
        last_re = last[:, 2 * n * S_HALF:(2 * n + 1) * S_HALF]
        last_im = last[:, (2 * n + 1) * S_HALF:(2 * n + 2) * S_HALF]
        if carry:
            hre_ref[:, n * S_HALF:(n + 1) * S_HALF] = last_re
            him_ref[:, n * S_HALF:(n + 1) * S_HALF] = last_im
        else:
            hre_ref[n * S_HALF:(n + 1) * S_HALF, :] = last_re.T
            him_ref[n * S_HALF:(n + 1) * S_HALF, :] = last_im.T

    zs = []
    for n in range(N_HALVES):
        cols = slice(n * U_HALF, (n + 1) * U_HALF)
        st_cols = slice(2 * n * S_HALF, (2 * n + 2) * S_HALF)
        y_o = _dot(hb_odd[:, st_cols], cc_ref[n]) + dskip_ref[:, cols] * u_o[:, cols]
        y_e = (_dot(hb_prev[:, st_cols], ccl_ref[n]) + _dot(ub_e[:, cols], d0_ref[n])
               + dskip_ref[:, cols] * u_e[:, cols])
        y = jnp.stack([y_e.reshape(n_pair, nb, U_HALF), y_o.reshape(n_pair, nb, U_HALF)],
                      axis=1).reshape(tm, U_HALF)
        zs.append(jax.nn.gelu(y))
    zb = jnp.concatenate([z.astype(BF16) for z in zs], axis=-1)
    ssm_chunks = []
    for n, z in enumerate(zs):
        cols = slice(n * U_HALF, (n + 1) * U_HALF)
        gate = jax.nn.sigmoid(_dot(zb, wglu_ref[:, cols]) + bglu_ref[:, cols])
        ssm_chunks.append((z * gate).astype(BF16))
    ssm_out = jnp.concatenate(ssm_chunks, axis=-1)

    x1 = x1 + _dot(ssm_out, wout_ref[W_CONV:, :])

    obuf[slot] = x1.reshape(tc, nb, D_MODEL)
    for c in out_copies(i, slot):
        c.start()

    if not carry:
        for c in cache_copies(convnew_hbm, tm, False):
            c.start()
        for c in cache_copies(convnew_hbm, tm, False):
            c.wait()

    @pl.when(i == n_steps - 1)
    def _():
        if carry:
            for c in cache_copies(convnew_hbm, tm, False):
                c.start()
            for c in cache_copies(convnew_hbm, tm, False):
                c.wait()
        if n_steps >= 2:
            for c in out_copies(i - 1, 1 - slot):
                c.wait()
        for c in out_copies(i, slot):
            c.wait()


def _resident(shape):
    nd = len(shape)
    return pl.BlockSpec(shape, lambda i: (0,) * nd, pipeline_mode=pl.Buffered(1))


def _mixer(x, wts, state=None, cast=(), *, nb, tc):
    n_seq, n_time, _ = x.shape
    carry = state is None
    if carry:
        assert n_seq == nb == SUBLANES and n_time % tc == 0 and tc % 4 == 0
        n_steps = n_time // tc
    else:
        assert n_time == tc == SUBLANES and n_seq % nb == 0 and nb % BF16_ROWS == 0
        n_steps = n_seq // nb
    tm = nb * tc
    n_st = N_GROUPS * STATE_P
    body = functools.partial(_mixer_body, nb=nb, tc=tc, n_steps=n_steps, carry=carry, n_cast=len(cast))
    any_spec = pl.BlockSpec(memory_space=pl.ANY)
    if carry:
        st_spec = pl.BlockSpec((nb, n_st), lambda i: (0, 0))
        st_shape = jax.ShapeDtypeStruct((n_seq, n_st), F32)
    else:
        st_spec = pl.BlockSpec((n_st, nb), lambda i: (0, i))
        st_shape = jax.ShapeDtypeStruct((n_st, n_seq), F32)
    state_specs = [] if carry else [any_spec, st_spec, st_spec]
    assert len(wts) == N_MIX_W
    streamed = () if carry else (W_IN,) + STREAM_ORDER
    w_specs = [any_spec if k in streamed else _resident(w.shape) for k, w in enumerate(wts)]
    w_scratch = [pltpu.VMEM(wts[k].shape, BF16) for k in streamed]
    if streamed:
        w_scratch.append(pltpu.SemaphoreType.DMA((1 + len(streamed),)))
    for w in cast:
        assert w.shape[0] % (n_steps * BF16_ROWS) == 0, w.shape
    cast_specs = [pl.BlockSpec((w.shape[0] // n_steps, w.shape[1]), lambda i: (i, 0)) for w in cast]
    return pl.pallas_call(
        body,
        grid=(n_steps,),
        in_specs=[any_spec] + state_specs + w_specs + cast_specs,
        out_specs=[any_spec, any_spec, st_spec, st_spec] + cast_specs,
        out_shape=[jax.ShapeDtypeStruct(x.shape, F32),
                   jax.ShapeDtypeStruct((1, n_seq, CONV_W - 1, W_CONV), F32),
                   st_shape, st_shape]
                  + [jax.ShapeDtypeStruct(w.shape, BF16) for w in cast],
        scratch_shapes=[pltpu.VMEM((2, tc, nb, D_MODEL), F32),
                        pltpu.VMEM((2, tc, nb, D_MODEL), F32),
                        pltpu.VMEM((tm // 2, S_COLS), F32),
                        pltpu.VMEM((tm // 2, S_COLS), BF16),
                        pltpu.VMEM((tm // 2, S_COLS), BF16),
                        pltpu.VMEM((nb, S_COLS), F32),
                        pltpu.VMEM((2 * nb + tm, W_CONV), F32),
                        pltpu.SemaphoreType.DMA((2, SUBLANES)),
                        pltpu.SemaphoreType.DMA((2, SUBLANES)),
                        pltpu.SemaphoreType.DMA((2, CONV_W - 1))] + w_scratch,
        compiler_params=pltpu.CompilerParams(dimension_semantics=("arbitrary",),
                                             vmem_limit_bytes=VMEM_LIMIT),
        name="mixer_carry" if carry else "mixer_step",
    )(x, *(() if carry else state), *wts, *cast)


def _ffn_body(xp_hbm, xs_hbm, nffn_ref, wg_ref, wu_ref, wd_ref, nfin_ref, yp_hbm, ys_hbm,
              xbuf, obuf, abuf, insem, outsem, *, n_p, n_s, tm):
    i = pl.program_id(0)
    n_steps = n_p + n_s
    slot = lax.rem(i, 2)

    def tile_copy(step, buf_slot, to_vmem, first):
        hbm = (xp_hbm if first else xs_hbm) if to_vmem else (yp_hbm if first else ys_hbm)
        h = hbm.at[pl.ds((step if first else step - n_p) * tm, tm), :]
        if to_vmem:
            return pltpu.make_async_copy(h, xbuf.at[buf_slot], insem.at[buf_slot])
        return pltpu.make_async_copy(obuf.at[buf_slot], h, outsem.at[buf_slot])

    def start(step, buf_slot, to_vmem):
        pl.when(step < n_p)(lambda: tile_copy(step, buf_slot, to_vmem, True).start())
        pl.when(step >= n_p)(lambda: tile_copy(step, buf_slot, to_vmem, False).start())

    def wait(buf_slot, to_vmem):
        tile_copy(0, buf_slot, to_vmem, True).wait()

    pl.when(i == 0)(lambda: start(0, 0, True))
    wait(slot, True)
    pl.when(i + 1 < n_steps)(lambda: start(i + 1, 1 - slot, True))
    pl.when(i >= 2)(lambda: wait(slot, False))

    x = xbuf[slot]
    hn = _rms(x, nffn_ref[...]).astype(BF16)
    for s, w in FFN_CHUNKS:
        g = _dot(hn, wg_ref[:, s:s + w])
        up = _dot(hn, wu_ref[:, s:s + w])
        abuf[:, s:s + w] = (g * jax.nn.sigmoid(g) * up).astype(BF16)
    acc = x + _dot(abuf[...], wd_ref[...])
    obuf[slot] = _rms(acc, nfin_ref[...])
    start(i, slot, False)

    @pl.when(i == n_steps - 1)
    def _():
        if n_steps >= 2:
            wait(1 - slot, False)
        wait(slot, False)


def _ffn(xp, xs, wts, *, tm):
    n_p, n_s = xp.shape[0] // tm, xs.shape[0] // tm
    assert xp.shape[0] == n_p * tm and xs.shape[0] == n_s * tm
    any_spec = pl.BlockSpec(memory_space=pl.ANY)
    return pl.pallas_call(
        functools.partial(_ffn_body, n_p=n_p, n_s=n_s, tm=tm),
        grid=(n_p + n_s,),
        in_specs=[any_spec, any_spec] + [_resident(w.shape) for w in wts],
        out_specs=[any_spec, any_spec],
        out_shape=[jax.ShapeDtypeStruct(xp.shape, F32), jax.ShapeDtypeStruct(xs.shape, F32)],
        scratch_shapes=[pltpu.VMEM((2, tm, D_MODEL), F32),
                        pltpu.VMEM((2, tm, D_MODEL), F32),
                        pltpu.VMEM((tm, D_FF), BF16),
                        pltpu.SemaphoreType.DMA((2,)),
                        pltpu.SemaphoreType.DMA((2,))],
        compiler_params=pltpu.CompilerParams(dimension_semantics=("arbitrary",),
                                             vmem_limit_bytes=VMEM_LIMIT),
        name="ffn",
    )(xp, xs, *wts)


N_S5_OUT = 6
N_S5_SCRATCH = 4


def _prep_body(lre_ref, lim_ref, logdt_ref, btre_ref, btim_ref, cre_ref, cim_ref, *refs):
    n_cast = (len(refs) - N_S5_OUT - N_S5_SCRATCH) // 2
    cast_in, refs = refs[:n_cast], refs[n_cast:]
    s5_out, refs = refs[:N_S5_OUT], refs[N_S5_OUT:]
    cast_out, s5_scratch = refs[:n_cast], refs[n_cast:]
    for src, dst in zip(cast_in, cast_out):
        dst[...] = src[...].astype(BF16)
    pl.when(pl.program_id(0) == 0)(functools.partial(
        _s5_matrices, lre_ref, lim_ref, logdt_ref, btre_ref, btim_ref, cre_ref, cim_ref, *s5_out, *s5_scratch))


def _s5_matrices(lre_ref, lim_ref, logdt_ref, btre_ref, btim_ref, cre_ref, cim_ref,
                 lam2r_ref, lam2i_ref, bc2_ref, cc_ref, ccl_ref, d0_ref, bscr, lbscr, cscr, clscr):
    lre = lre_ref[...]
    lim = lim_ref[...]
    grp = (lax.broadcasted_iota(jnp.int32, (N_GROUPS, N_GROUPS), 0)
           == lax.broadcasted_iota(jnp.int32, (N_GROUPS, N_GROUPS), 1))
    logdt = jnp.sum(jnp.where(grp, jnp.broadcast_to(logdt_ref[...], (N_GROUPS, N_GROUPS)), 0.0),
                    axis=1, keepdims=True)
    dt = jnp.exp(logdt)
    mag = jnp.exp(lre * dt)
    lbr = mag * jnp.cos(lim * dt)
    lbi = mag * jnp.sin(lim * dt)
    l2r = lbr * lbr - lbi * lbi
    l2i = 2.0 * (lbr * lbi)
    den = lre * lre + lim * lim
    fr = ((lbr - 1.0) * lre + lbi * lim) / den
    fi = (lbi * lre - (lbr - 1.0) * lim) / den
    frb, fib = fr[:, None, :], fi[:, None, :]
    lrb, lib = lbr[:, None, :], lbi[:, None, :]
    bbr = frb * btre_ref[...] - fib * btim_ref[...]
    bbi = frb * btim_ref[...] + fib * btre_ref[...]
    lbbr = lrb * bbr - lib * bbi
    lbbi = lrb * bbi + lib * bbr
    cre = cre_ref[...]
    cim = cim_ref[...]
    clr = cre * lrb - cim * lib
    cli = cre * lib + cim * lrb

    for scr in (bscr, lbscr, cscr, clscr):
        scr[...] = jnp.zeros_like(scr)
    for g in range(N_GROUPS):
        n, k = divmod(g, GROUPS_PER_HALF)
        chans = slice(k * SSM_H, (k + 1) * SSM_H)
        states = slice(k * STATE_P, (k + 1) * STATE_P)
        im_states = slice(S_HALF + k * STATE_P, S_HALF + (k + 1) * STATE_P)
        for scr, re, im in ((bscr, bbr, bbi), (lbscr, lbbr, lbbi), (cscr, cre, -cim), (clscr, clr, -cli)):
            scr[n, chans, states] = re[g]
            scr[n, chans, im_states] = im[g]
        lam2r_ref[n, :, states] = jnp.broadcast_to(l2r[g:g + 1, :], (SUBLANES, STATE_P))
        lam2i_ref[n, :, states] = jnp.broadcast_to(l2i[g:g + 1, :], (SUBLANES, STATE_P))
    for n in range(N_HALVES):
        bc2_ref[n, 0:U_HALF, :] = lbscr[n].astype(BF16)
        bc2_ref[n, U_HALF:, :] = bscr[n].astype(BF16)
        c_t = cscr[n].T
        cc_ref[n] = c_t.astype(BF16)
        ccl_ref[n] = clscr[n].T.astype(BF16)
        b_hi, b_lo = _split_bf16(bscr[n])
        c_hi, c_lo = _split_bf16(c_t)
        d0_ref[n] = (_dot(b_hi, c_hi) + _dot(b_hi, c_lo) + _dot(b_lo, c_hi)).astype(BF16)


PREP_STEPS = 2


def _prep(lam_re, lam_im, log_dt, b_re, b_im, c_re, c_im, cast):
    s5_in = (lam_re, lam_im, log_dt.reshape(1, N_GROUPS), b_re.transpose(0, 2, 1), b_im.transpose(0, 2, 1),
             c_re, c_im)
    s5_shapes = [jax.ShapeDtypeStruct((N_HALVES, SUBLANES, S_HALF), F32),
                 jax.ShapeDtypeStruct((N_HALVES, SUBLANES, S_HALF), F32),
                 jax.ShapeDtypeStruct((N_HALVES, 2 * U_HALF, 2 * S_HALF), BF16),
                 jax.ShapeDtypeStruct((N_HALVES, 2 * S_HALF, U_HALF), BF16),
                 jax.ShapeDtypeStruct((N_HALVES, 2 * S_HALF, U_HALF), BF16),
                 jax.ShapeDtypeStruct((N_HALVES, U_HALF, U_HALF), BF16)]
    assert len(s5_shapes) == N_S5_OUT
    for w in cast:
        assert w.shape[0] % (PREP_STEPS * BF16_ROWS) == 0, w.shape
    cast_specs = [pl.BlockSpec((w.shape[0] // PREP_STEPS, w.shape[1]), lambda i: (i, 0)) for w in cast]
    whole = lambda a: pl.BlockSpec(a.shape, lambda i: (0,) * len(a.shape))
    return pl.pallas_call(
        _prep_body,
        grid=(PREP_STEPS,),
        in_specs=[whole(a) for a in s5_in] + cast_specs,
        out_specs=[whole(a) for a in s5_shapes] + cast_specs,
        out_shape=s5_shapes + [jax.ShapeDtypeStruct(w.shape, BF16) for w in cast],
        scratch_shapes=[pltpu.VMEM((N_HALVES, U_HALF, 2 * S_HALF), F32)] * N_S5_SCRATCH,
        compiler_params=pltpu.CompilerParams(dimension_semantics=("arbitrary",)),
        name="prep",
    )(*s5_in, *cast)


def kernel(x_prompt, x_sample, cache_conv, state_ssm_re, state_ssm_im, norm_mix, w_in, conv_w, ssm_lam_re, ssm_lam_im, ssm_log_dt, ssm_b_re, ssm_b_im, ssm_c_re, ssm_c_im, ssm_d, w_glu, b_glu, w_out, norm_ffn, w_gate, w_up, w_down, norm_final):
    assert norm_mix.shape[0] == 1, "single-layer trunk"
    n_st = N_GROUPS * STATE_P
    lam2r, lam2i, bc2, cc, ccl, d0, w_in_b, w_glu_b, w_out_b = _prep(
        ssm_lam_re[0], ssm_lam_im[0], ssm_log_dt[0], ssm_b_re[0], ssm_b_im[0], ssm_c_re[0], ssm_c_im[0],
        cast=(w_in[0], w_glu[0], w_out[0]))
    mix_w = (norm_mix, w_in_b, conv_w[0][:, None, :], lam2r, lam2i, bc2, cc, ccl, d0, ssm_d, w_glu_b, b_glu,
             w_out_b)

    bp, lp, _ = x_prompt.shape
    bs, ls, _ = x_sample.shape

    x1p, convp, rep, imp, wg, wu, wd = _mixer(x_prompt, mix_w, cast=(w_gate[0], w_up[0], w_down[0]),
                                              nb=SUBLANES, tc=128)
    ffn_w = (norm_ffn, wg, wu, wd, norm_final.reshape(1, D_MODEL))
    def state_major(h):
        return h[0].transpose(1, 2, 0).reshape(n_st, h.shape[1])

    sample_state = (cache_conv, state_major(state_ssm_re), state_major(state_ssm_im))
    x1s, convs, res, ims = _mixer(x_sample, mix_w, sample_state, nb=128, tc=SUBLANES)

    yp, ys = _ffn(x1p.reshape(bp * lp, D_MODEL), x1s.reshape(bs * ls, D_MODEL), ffn_w, tm=1024)
    yp = yp.reshape(bp, lp, D_MODEL)
    ys = ys.reshape(bs, ls, D_MODEL)

    def seq_major(h):
        return h.reshape(N_GROUPS, STATE_P, h.shape[1]).transpose(2, 0, 1)[None]

    return (yp, ys, convp, rep.reshape(1, bp, N_GROUPS, STATE_P), imp.reshape(1, bp, N_GROUPS, STATE_P),
            convs, seq_major(res), seq_major(ims))
```

```python
import functools

import jax
import jax.numpy as jnp
from jax import lax
from jax.experimental import pallas as pl
from jax.experimental.pallas import tpu as pltpu

D_MODEL = 1024
W_CONV = 512
W_SSM = 512
CONV_W = 3
N_GROUPS = 32
SSM_H = 16
STATE_P = 64
D_FF = 2816
EPS = 1e-5

SUBLANES = 8
BF16_ROWS = 16
GROUPS_PER_HALF = 16
N_HALVES = N_GROUPS // GROUPS_PER_HALF
U_HALF = GROUPS_PER_HALF * SSM_H
S_HALF = GROUPS_PER_HALF * STATE_P
S_COLS = 2 * N_GROUPS * STATE_P
SCAN_LANES = 512
MXU_COLS = 256
FFN_CHUNKS = tuple((s, MXU_COLS) for s in range(0, D_FF, MXU_COLS))
VMEM_LIMIT = 60 * 1024 * 1024

BF16 = jnp.bfloat16
F32 = jnp.float32


def _rms(x, g):
    y = x * lax.rsqrt(jnp.mean(x * x, axis=-1, keepdims=True) + EPS)
    return y * g


def _dot(a, b):
    return jnp.dot(a, b, preferred_element_type=F32)


def _split_bf16(a):
    hi = a.astype(BF16)
    return hi, (a - hi.astype(F32)).astype(BF16)


def _tile_copies(hbm, buf, sem, step, slot, *, nb, tc, per_seq, to_vmem):
    copies = []
    for k in range(SUBLANES):
        if per_seq:
            h = hbm.at[k, pl.ds(step * tc, tc), :]
            v = buf.at[slot, :, k, :]
        else:
            h = hbm.at[pl.ds(step * nb, nb), k, :]
            v = buf.at[slot, k]
        src, dst = (h, v) if to_vmem else (v, h)
        copies.append(pltpu.make_async_copy(src, dst, sem.at[slot, k]))
    return copies


N_MIX_W = 13
W_IN, BC2, CC, CCL, D0, W_GLU, W_OUT = 1, 5, 6, 7, 8, 10, 12
STREAM_ORDER = (BC2, W_OUT, CC, CCL, D0, W_GLU)


def _mixer_body(x_hbm, *refs, nb, tc, n_steps, carry, n_cast):
    if carry:
        cache_ref = re0_ref = im0_ref = None
    else:
        cache_ref, re0_ref, im0_ref, *refs = refs
    wts, refs = list(refs[:N_MIX_W]), refs[N_MIX_W:]
    cast_in, refs = refs[:n_cast], refs[n_cast:]
    x1_hbm, convnew_hbm, hre_ref, him_ref, *refs = refs
    cast_out, refs = refs[:n_cast], refs[n_cast:]
    xbuf, obuf, gbuf, hb_odd, hb_prev, hstate, vbuf, insem, outsem, cachesem, *refs = refs
    if carry:
        w_copies = {}
    else:
        assert n_steps == 1
        *wbufs, stbuf, wsem, stsem = refs
        win_hbm, win_buf = wts[W_IN], wbufs[0]
        u_cols, rest_cols = pl.ds(3 * W_CONV, W_SSM), pl.ds(0, 3 * W_CONV)
        w_copies = {"u": pltpu.make_async_copy(win_hbm.at[:, u_cols], win_buf.at[:, u_cols], wsem.at[0]),
                    W_IN: pltpu.make_async_copy(win_hbm.at[:, rest_cols], win_buf.at[:, rest_cols], wsem.at[1])}
        wts[W_IN] = win_buf
        for j, (k, buf) in enumerate(zip(STREAM_ORDER, wbufs[1:])):
            w_copies[k] = pltpu.make_async_copy(wts[k], buf, wsem.at[2 + j])
            wts[k] = buf
    (nmix_ref, win_ref, convw_ref, lam2r_ref, lam2i_ref, bc2_ref, cc_ref, ccl_ref, d0_ref, dskip_ref,
     wglu_ref, bglu_ref, wout_ref) = wts

    def wait_w(*keys):
        for k in keys:
            if k in w_copies:
                w_copies[k].wait()

    def state_copies(to_vmem):
        pairs = ((re0_ref, stbuf.at[0]), (im0_ref, stbuf.at[1])) if to_vmem else (
            (stbuf.at[0], hre_ref), (stbuf.at[1], him_ref))
        return [pltpu.make_async_copy(src, dst, stsem.at[j]) for j, (src, dst) in enumerate(pairs)]
    tm = nb * tc
    n_pair = tc // 2
    tmh = n_pair * nb

    i = pl.program_id(0)
    slot = lax.rem(i, 2)
    cp = functools.partial(_tile_copies, nb=nb, tc=tc, per_seq=carry)
    in_copies = functools.partial(cp, x_hbm, xbuf, insem, to_vmem=True)
    out_copies = functools.partial(cp, x1_hbm, obuf, outsem, to_vmem=False)

    def cache_copies(hbm, row0, to_vmem):
        copies = []
        for k in range(CONV_W - 1):
            h = hbm.at[0, pl.ds(0 if carry else i * nb, nb), k, :]
            v = vbuf.at[pl.ds(row0 + k * nb, nb), :]
            src, dst = (h, v) if to_vmem else (v, h)
            copies.append(pltpu.make_async_copy(src, dst, cachesem.at[int(to_vmem), k]))
        return copies

    @pl.when(i == 0)
    def _():
        for c in in_copies(0, 0):
            c.start()

    if not carry:
        for c in cache_copies(cache_ref, 0, True):
            c.start()
        w_copies["u"].start()
        for c in state_copies(True):
            c.start()
        for k in (BC2, W_IN) + STREAM_ORDER[1:]:
            w_copies[k].start()
    for c in in_copies(i, slot):
        c.wait()
    if not carry:
        for c in cache_copies(cache_ref, 0, True):
            c.wait()

    @pl.when(i + 1 < n_steps)
    def _():
        for c in in_copies(i + 1, 1 - slot):
            c.start()

    @pl.when(i >= 2)
    def _():
        for c in out_copies(i - 2, slot):
            c.wait()

    if carry:
        @pl.when(i == 0)
        def _():
            vbuf[0:2 * nb, :] = jnp.zeros((2 * nb, W_CONV), F32)
            hstate[...] = jnp.zeros((nb, S_COLS), F32)

    x = xbuf[slot].reshape(tm, D_MODEL)
    inv_rms = lax.rsqrt(jnp.mean(x * x, axis=-1, keepdims=True) + EPS)
    xg = (x * nmix_ref[...]).astype(BF16)
    xn = (x * inv_rms * nmix_ref[...]).astype(BF16)

    wait_w("u")
    u = inv_rms * _dot(xg, win_ref[:, 3 * W_CONV:])
    u_pairs = u.reshape(n_pair, 2, nb, W_SSM)
    u_e = u_pairs[:, 0].reshape(tmh, W_SSM)
    u_o = u_pairs[:, 1].reshape(tmh, W_SSM)
    ub_e = u_e.astype(BF16)
    ub_o = u_o.astype(BF16)
    wait_w(BC2, W_IN)
    if not carry:
        for c in state_copies(True):
            c.wait()
        for n in range(N_HALVES):
            hstate[:, 2 * n * S_HALF:(2 * n + 1) * S_HALF] = stbuf[0, n * S_HALF:(n + 1) * S_HALF, :].T
            hstate[:, (2 * n + 1) * S_HALF:(2 * n + 2) * S_HALF] = stbuf[1, n * S_HALF:(n + 1) * S_HALF, :].T
    for n in range(N_HALVES):
        cols = slice(n * U_HALF, (n + 1) * U_HALF)
        gbuf[:, 2 * n * S_HALF:(2 * n + 2) * S_HALF] = _dot(
            jnp.concatenate([ub_e[:, cols], ub_o[:, cols]], axis=1), bc2_ref[n])

    for src, dst in zip(cast_in, cast_out):
        dst[...] = src[...].astype(BF16)

    def scan_step(h, rows, lr, li, re_cols, im_cols):
        hr, hi = h
        return (lr * hr - li * hi + gbuf[rows, re_cols], lr * hi + li * hr + gbuf[rows, im_cols])

    def store_pair(buf, row0, pair, re_cols, im_cols):
        rows = slice(row0, row0 + BF16_ROWS)
        buf[rows, re_cols] = jnp.concatenate([h[0] for h in pair], axis=0).astype(BF16)
        buf[rows, im_cols] = jnp.concatenate([h[1] for h in pair], axis=0).astype(BF16)

    for n in range(N_HALVES):
        for c0 in range(0, S_HALF, SCAN_LANES):
            re_cols = slice(2 * n * S_HALF + c0, 2 * n * S_HALF + c0 + SCAN_LANES)
            im_cols = slice((2 * n + 1) * S_HALF + c0, (2 * n + 1) * S_HALF + c0 + SCAN_LANES)
            lr = lam2r_ref[n, :, c0:c0 + SCAN_LANES]
            li = lam2i_ref[n, :, c0:c0 + SCAN_LANES]
            step = functools.partial(scan_step, lr=lr, li=li, re_cols=re_cols, im_cols=im_cols)
            if nb == SUBLANES:
                h = (hstate[:, re_cols], hstate[:, im_cols])
                for k in range(0, n_pair, 2):
                    h0 = step(h, slice(k * nb, (k + 1) * nb))
                    h1 = step(h0, slice((k + 1) * nb, (k + 2) * nb))
                    store_pair(hb_odd, k * nb, (h0, h1), re_cols, im_cols)
                    store_pair(hb_prev, k * nb, (h, h0), re_cols, im_cols)
                    h = h1
                hstate[:, re_cols], hstate[:, im_cols] = h
            else:
                for r0 in range(0, nb, BF16_ROWS):
                    seqs = [slice(r0 + j * SUBLANES, r0 + (j + 1) * SUBLANES) for j in range(2)]
                    pair = [(hstate[sq, re_cols], hstate[sq, im_cols]) for sq in seqs]
                    for k in range(n_pair):
                        store_pair(hb_prev, k * nb + r0, pair, re_cols, im_cols)
                        pair = [step(h, slice(k * nb + sq.start, k * nb + sq.stop)) for h, sq in zip(pair, seqs)]
                        store_pair(hb_odd, k * nb + r0, pair, re_cols, im_cols)
                    for h, sq in zip(pair, seqs):
                        hstate[sq, re_cols], hstate[sq, im_cols] = h

    conv_chunks = []
    for c0 in range(0, W_CONV, MXU_COLS):
        cols = slice(c0, c0 + MXU_COLS)
        b_g = _dot(xn, win_ref[:, c0:c0 + MXU_COLS])
        c_g = _dot(xn, win_ref[:, W_CONV + c0:W_CONV + c0 + MXU_COLS])
        xc = _dot(xn, win_ref[:, 2 * W_CONV + c0:2 * W_CONV + c0 + MXU_COLS])
        v = c_g * xc
        vbuf[2 * nb:2 * nb + tm, cols] = v
        conv = (convw_ref[0, :, cols] * vbuf[0:tm, cols] + convw_ref[1, :, cols] * vbuf[nb:nb + tm, cols]
                + convw_ref[2, :, cols] * v)
        conv_chunks.append((b_g * conv).astype(BF16))
    conv_out = jnp.concatenate(conv_chunks, axis=-1)
    if carry:
        vbuf[0:2 * nb, :] = vbuf[tm:tm + 2 * nb, :]
    wait_w(*STREAM_ORDER[1:])
    x1 = x + _dot(conv_out, wout_ref[0:W_CONV, :])

    last = hstate[...]
    for n in range(N_HALVES):
        last_re = last[:, 2 * n * S_HALF:(2 * n + 1) * S_HALF]
        last_im = last[:, (2 * n + 1) * S_HALF:(2 * n + 2) * S_HALF]
        if carry:
            hre_ref[:, n * S_HALF:(n + 1) * S_HALF] = last_re
            him_ref[:, n * S_HALF:(n + 1) * S_HALF] = last_im
        else:
            stbuf[0, n * S_HALF:(n + 1) * S_HALF, :] = last_re.T
            stbuf[1, n * S_HALF:(n + 1) * S_HALF, :] = last_im.T
    if not carry:
        for c in state_copies(False):
            c.start()

    zs = []
    for n in range(N_HALVES):
        cols = slice(n * U_HALF, (n + 1) * U_HALF)
        st_cols = slice(2 * n * S_HALF, (2 * n + 2) * S_HALF)
        y_o = _dot(hb_odd[:, st_cols], cc_ref[n]) + dskip_ref[:, cols] * u_o[:, cols]
        y_e = (_dot(hb_prev[:, st_cols], ccl_ref[n]) + _dot(ub_e[:, cols], d0_ref[n])
               + dskip_ref[:, cols] * u_e[:, cols])
        y = jnp.stack([y_e.reshape(n_pair, nb, U_HALF), y_o.reshape(n_pair, nb, U_HALF)],
                      axis=1).reshape(tm, U_HALF)
        zs.append(jax.nn.gelu(y))
    zb = jnp.concatenate([z.astype(BF16) for z in zs], axis=-1)
    ssm_chunks = []
    for n, z in enumerate(zs):
        cols = slice(n * U_HALF, (n + 1) * U_HALF)
        gate = jax.nn.sigmoid(_dot(zb, wglu_ref[:, cols]) + bglu_ref[:, cols])
        ssm_chunks.append((z * gate).astype(BF16))
    ssm_out = jnp.concatenate(ssm_chunks, axis=-1)

    x1 = x1 + _dot(ssm_out, wout_ref[W_CONV:, :])

    obuf[slot] = x1.reshape(tc, nb, D_MODEL)
    for c in out_copies(i, slot):
        c.start()

    if not carry:
        for c in cache_copies(convnew_hbm, tm, False):
            c.start()
        for c in cache_copies(convnew_hbm, tm, False) + state_copies(False):
            c.wait()

    @pl.when(i == n_steps - 1)
    def _():
        if carry:
            for c in cache_copies(convnew_hbm, tm, False):
                c.start()
            for c in cache_copies(convnew_hbm, tm, False):
                c.wait()
        if n_steps >= 2:
            for c in out_copies(i - 1, 1 - slot):
                c.wait()
        for c in out_copies(i, slot):
            c.wait()


def _resident(shape):
    nd = len(shape)
    return pl.BlockSpec(shape, lambda i: (0,) * nd, pipeline_mode=pl.Buffered(1))


def _mixer(x, wts, state=None, cast=(), *, nb, tc):
    n_seq, n_time, _ = x.shape
    carry = state is None
    if carry:
        assert n_seq == nb == SUBLANES and n_time % tc == 0 and tc % 4 == 0
        n_steps = n_time // tc
    else:
        assert n_time == tc == SUBLANES and n_seq % nb == 0 and nb % BF16_ROWS == 0
        n_steps = n_seq // nb
    tm = nb * tc
    n_st = N_GROUPS * STATE_P
    body = functools.partial(_mixer_body, nb=nb, tc=tc, n_steps=n_steps, carry=carry, n_cast=len(cast))
    any_spec = pl.BlockSpec(memory_space=pl.ANY)
    if carry:
        st_spec = pl.BlockSpec((nb, n_st), lambda i: (0, 0))
        st_shape = jax.ShapeDtypeStruct((n_seq, n_st), F32)
    else:
        assert n_steps == 1
        st_spec = any_spec
        st_shape = jax.ShapeDtypeStruct((n_st, n_seq), F32)
    state_specs = [] if carry else [any_spec, any_spec, any_spec]
    assert len(wts) == N_MIX_W
    streamed = () if carry else (W_IN,) + STREAM_ORDER
    w_specs = [any_spec if k in streamed else _resident(w.shape) for k, w in enumerate(wts)]
    w_scratch = [pltpu.VMEM(wts[k].shape, BF16) for k in streamed]
    if streamed:
        w_scratch += [pltpu.VMEM((2, n_st, nb), F32), pltpu.SemaphoreType.DMA((1 + len(streamed),)),
                      pltpu.SemaphoreType.DMA((2,))]
    for w in cast:
        assert w.shape[0] % (n_steps * BF16_ROWS) == 0, w.shape
    cast_specs = [pl.BlockSpec((w.shape[0] // n_steps, w.shape[1]), lambda i: (i, 0)) for w in cast]
    return pl.pallas_call(
        body,
        grid=(n_steps,),
        in_specs=[any_spec] + state_specs + w_specs + cast_specs,
        out_specs=[any_spec, any_spec, st_spec, st_spec] + cast_specs,
        out_shape=[jax.ShapeDtypeStruct(x.shape, F32),
                   jax.ShapeDtypeStruct((1, n_seq, CONV_W - 1, W_CONV), F32),
                   st_shape, st_shape]
                  + [jax.ShapeDtypeStruct(w.shape, BF16) for w in cast],
        scratch_shapes=[pltpu.VMEM((2, tc, nb, D_MODEL), F32),
                        pltpu.VMEM((2, tc, nb, D_MODEL), F32),
                        pltpu.VMEM((tm // 2, S_COLS), F32),
                        pltpu.VMEM((tm // 2, S_COLS), BF16),
                        pltpu.VMEM((tm // 2, S_COLS), BF16),
                        pltpu.VMEM((nb, S_COLS), F32),
                        pltpu.VMEM((2 * nb + tm, W_CONV), F32),
                        pltpu.SemaphoreType.DMA((2, SUBLANES)),
                        pltpu.SemaphoreType.DMA((2, SUBLANES)),
                        pltpu.SemaphoreType.DMA((2, CONV_W - 1))] + w_scratch,
        compiler_params=pltpu.CompilerParams(dimension_semantics=("arbitrary",),
                                             vmem_limit_bytes=VMEM_LIMIT),
        name="mixer_carry" if carry else "mixer_step",
    )(x, *(() if carry else state), *wts, *cast)


def _ffn_body(xp_hbm, xs_hbm, nffn_ref, wg_ref, wu_ref, wd_ref, nfin_ref, yp_hbm, ys_hbm,
              xbuf, obuf, abuf, insem, outsem, *, n_p, n_s, tm):
    i = pl.program_id(0)
    n_steps = n_p + n_s
    slot = lax.rem(i, 2)

    def tile_copy(step, buf_slot, to_vmem, first):
        hbm = (xp_hbm if first else xs_hbm) if to_vmem else (yp_hbm if first else ys_hbm)
        h = hbm.at[pl.ds((step if first else step - n_p) * tm, tm), :]
        if to_vmem:
            return pltpu.make_async_copy(h, xbuf.at[buf_slot], insem.at[buf_slot])
        return pltpu.make_async_copy(obuf.at[buf_slot], h, outsem.at[buf_slot])

    def start(step, buf_slot, to_vmem):
        pl.when(step < n_p)(lambda: tile_copy(step, buf_slot, to_vmem, True).start())
        pl.when(step >= n_p)(lambda: tile_copy(step, buf_slot, to_vmem, False).start())

    def wait(buf_slot, to_vmem):
        tile_copy(0, buf_slot, to_vmem, True).wait()

    pl.when(i == 0)(lambda: start(0, 0, True))
    wait(slot, True)
    pl.when(i + 1 < n_steps)(lambda: start(i + 1, 1 - slot, True))
    pl.when(i >= 2)(lambda: wait(slot, False))

    x = xbuf[slot]
    hn = _rms(x, nffn_ref[...]).astype(BF16)
    for s, w in FFN_CHUNKS:
        g = _dot(hn, wg_ref[:, s:s + w])
        up = _dot(hn, wu_ref[:, s:s + w])
        abuf[:, s:s + w] = (g * jax.nn.sigmoid(g) * up).astype(BF16)
    acc = x + _dot(abuf[...], wd_ref[...])
    obuf[slot] = _rms(acc, nfin_ref[...])
    start(i, slot, False)

    @pl.when(i == n_steps - 1)
    def _():
        if n_steps >= 2:
            wait(1 - slot, False)
        wait(slot, False)


def _ffn(xp, xs, wts, *, tm):
    n_p, n_s = xp.shape[0] // tm, xs.shape[0] // tm
    assert xp.shape[0] == n_p * tm and xs.shape[0] == n_s * tm
    any_spec = pl.BlockSpec(memory_space=pl.ANY)
    return pl.pallas_call(
        functools.partial(_ffn_body, n_p=n_p, n_s=n_s, tm=tm),
        grid=(n_p + n_s,),
        in_specs=[any_spec, any_spec] + [_resident(w.shape) for w in wts],
        out_specs=[any_spec, any_spec],
        out_shape=[jax.ShapeDtypeStruct(xp.shape, F32), jax.ShapeDtypeStruct(xs.shape, F32)],
        scratch_shapes=[pltpu.VMEM((2, tm, D_MODEL), F32),
                        pltpu.VMEM((2, tm, D_MODEL), F32),
                        pltpu.VMEM((tm, D_FF), BF16),
                        pltpu.SemaphoreType.DMA((2,)),
                        pltpu.SemaphoreType.DMA((2,))],
        compiler_params=pltpu.CompilerParams(dimension_semantics=("arbitrary",),
                                             vmem_limit_bytes=VMEM_LIMIT),
        name="ffn",
    )(xp, xs, *wts)


N_S5_OUT = 6
N_S5_SCRATCH = 4


def _prep_body(lre_ref, lim_ref, logdt_ref, btre_ref, btim_ref, cre_ref, cim_ref, *refs):
    n_cast = (len(refs) - N_S5_OUT - N_S5_SCRATCH) // 2
    cast_in, refs = refs[:n_cast], refs[n_cast:]
    s5_out, refs = refs[:N_S5_OUT], refs[N_S5_OUT:]
    cast_out, s5_scratch = refs[:n_cast], refs[n_cast:]
    for src, dst in zip(cast_in, cast_out):
        dst[...] = src[...].astype(BF16)
    pl.when(pl.program_id(0) == 0)(functools.partial(
        _s5_matrices, lre_ref, lim_ref, logdt_ref, btre_ref, btim_ref, cre_ref, cim_ref, *s5_out, *s5_scratch))


def _s5_matrices(lre_ref, lim_ref, logdt_ref, btre_ref, btim_ref, cre_ref, cim_ref,
                 lam2r_ref, lam2i_ref, bc2_ref, cc_ref, ccl_ref, d0_ref, bscr, lbscr, cscr, clscr):
    lre = lre_ref[...]
    lim = lim_ref[...]
    grp = (lax.broadcasted_iota(jnp.int32, (N_GROUPS, N_GROUPS), 0)
           == lax.broadcasted_iota(jnp.int32, (N_GROUPS, N_GROUPS), 1))
    logdt = jnp.sum(jnp.where(grp, jnp.broadcast_to(logdt_ref[...], (N_GROUPS, N_GROUPS)), 0.0),
                    axis=1, keepdims=True)
    dt = jnp.exp(logdt)
    mag = jnp.exp(lre * dt)
    lbr = mag * jnp.cos(lim * dt)
    lbi = mag * jnp.sin(lim * dt)
    l2r = lbr * lbr - lbi * lbi
    l2i = 2.0 * (lbr * lbi)
    den = lre * lre + lim * lim
    fr = ((lbr - 1.0) * lre + lbi * lim) / den
    fi = (lbi * lre - (lbr - 1.0) * lim) / den
    frb, fib = fr[:, None, :], fi[:, None, :]
    lrb, lib = lbr[:, None, :], lbi[:, None, :]
    bbr = frb * btre_ref[...] - fib * btim_ref[...]
    bbi = frb * btim_ref[...] + fib * btre_ref[...]
    lbbr = lrb * bbr - lib * bbi
    lbbi = lrb * bbi + lib * bbr
    cre = cre_ref[...]
    cim = cim_ref[...]
    clr = cre * lrb - cim * lib
    cli = cre * lib + cim * lrb

    for scr in (bscr, lbscr, cscr, clscr):
        scr[...] = jnp.zeros_like(scr)
    for g in range(N_GROUPS):
        n, k = divmod(g, GROUPS_PER_HALF)
        chans = slice(k * SSM_H, (k + 1) * SSM_H)
        states = slice(k * STATE_P, (k + 1) * STATE_P)
        im_states = slice(S_HALF + k * STATE_P, S_HALF + (k + 1) * STATE_P)
        for scr, re, im in ((bscr, bbr, bbi), (lbscr, lbbr, lbbi), (cscr, cre, -cim), (clscr, clr, -cli)):
            scr[n, chans, states] = re[g]
            scr[n, chans, im_states] = im[g]
        lam2r_ref[n, :, states] = jnp.broadcast_to(l2r[g:g + 1, :], (SUBLANES, STATE_P))
        lam2i_ref[n, :, states] = jnp.broadcast_to(l2i[g:g + 1, :], (SUBLANES, STATE_P))
    for n in range(N_HALVES):
        bc2_ref[n, 0:U_HALF, :] = lbscr[n].astype(BF16)
        bc2_ref[n, U_HALF:, :] = bscr[n].astype(BF16)
        c_t = cscr[n].T
        cc_ref[n] = c_t.astype(BF16)
        ccl_ref[n] = clscr[n].T.astype(BF16)
        b_hi, b_lo = _split_bf16(bscr[n])
        c_hi, c_lo = _split_bf16(c_t)
        d0_ref[n] = (_dot(b_hi, c_hi) + _dot(b_hi, c_lo) + _dot(b_lo, c_hi)).astype(BF16)


PREP_STEPS = 2


def _prep(lam_re, lam_im, log_dt, b_re, b_im, c_re, c_im, cast):
    s5_in = (lam_re, lam_im, log_dt.reshape(1, N_GROUPS), b_re.transpose(0, 2, 1), b_im.transpose(0, 2, 1),
             c_re, c_im)
    s5_shapes = [jax.ShapeDtypeStruct((N_HALVES, SUBLANES, S_HALF), F32),
                 jax.ShapeDtypeStruct((N_HALVES, SUBLANES, S_HALF), F32),
                 jax.ShapeDtypeStruct((N_HALVES, 2 * U_HALF, 2 * S_HALF), BF16),
                 jax.ShapeDtypeStruct((N_HALVES, 2 * S_HALF, U_HALF), BF16),
                 jax.ShapeDtypeStruct((N_HALVES, 2 * S_HALF, U_HALF), BF16),
                 jax.ShapeDtypeStruct((N_HALVES, U_HALF, U_HALF), BF16)]
    assert len(s5_shapes) == N_S5_OUT
    for w in cast:
        assert w.shape[0] % (PREP_STEPS * BF16_ROWS) == 0, w.shape
    cast_specs = [pl.BlockSpec((w.shape[0] // PREP_STEPS, w.shape[1]), lambda i: (i, 0)) for w in cast]
    whole = lambda a: pl.BlockSpec(a.shape, lambda i: (0,) * len(a.shape))
    return pl.pallas_call(
        _prep_body,
        grid=(PREP_STEPS,),
        in_specs=[whole(a) for a in s5_in] + cast_specs,
        out_specs=[whole(a) for a in s5_shapes] + cast_specs,
        out_shape=s5_shapes + [jax.ShapeDtypeStruct(w.shape, BF16) for w in cast],
        scratch_shapes=[pltpu.VMEM((N_HALVES, U_HALF, 2 * S_HALF), F32)] * N_S5_SCRATCH,
        compiler_params=pltpu.CompilerParams(dimension_semantics=("arbitrary",)),
        name="prep",
    )(*s5_in, *cast)


def kernel(x_prompt, x_sample, cache_conv, state_ssm_re, state_ssm_im, norm_mix, w_in, conv_w, ssm_lam_re, ssm_lam_im, ssm_log_dt, ssm_b_re, ssm_b_im, ssm_c_re, ssm_c_im, ssm_d, w_glu, b_glu, w_out, norm_ffn, w_gate, w_up, w_down, norm_final):
    assert norm_mix.shape[0] == 1, "single-layer trunk"
    n_st = N_GROUPS * STATE_P
    lam2r, lam2i, bc2, cc, ccl, d0, w_in_b, w_glu_b, w_out_b = _prep(
        ssm_lam_re[0], ssm_lam_im[0], ssm_log_dt[0], ssm_b_re[0], ssm_b_im[0], ssm_c_re[0], ssm_c_im[0],
        cast=(w_in[0], w_glu[0], w_out[0]))
    mix_w = (norm_mix, w_in_b, conv_w[0][:, None, :], lam2r, lam2i, bc2, cc, ccl, d0, ssm_d, w_glu_b, b_glu,
             w_out_b)

    bp, lp, _ = x_prompt.shape
    bs, ls, _ = x_sample.shape

    x1p, convp, rep, imp, wg, wu, wd = _mixer(x_prompt, mix_w, cast=(w_gate[0], w_up[0], w_down[0]),
                                              nb=SUBLANES, tc=128)
    ffn_w = (norm_ffn, wg, wu, wd, norm_final.reshape(1, D_MODEL))
    def state_major(h):
        return h[0].transpose(1, 2, 0).reshape(n_st, h.shape[1])

    sample_state = (cache_conv, state_major(state_ssm_re), state_major(state_ssm_im))
    x1s, convs, res, ims = _mixer(x_sample, mix_w, sample_state, nb=128, tc=SUBLANES)

    yp, ys = _ffn(x1p.reshape(bp * lp, D_MODEL), x1s.reshape(bs * ls, D_MODEL), ffn_w, tm=1024)
    yp = yp.reshape(bp, lp, D_MODEL)
    ys = ys.reshape(bs, ls, D_MODEL)

    def seq_major(h):
        return h.reshape(N_GROUPS, STATE_P, h.shape[1]).transpose(2, 0, 1)[None]

    return (yp, ys, convp, rep.reshape(1, bp, N_GROUPS, STATE_P), imp.reshape(1, bp, N_GROUPS, STATE_P),
            convs, seq_major(res), seq_major(ims))
```

```python
import functools

import jax
import jax.numpy as jnp
from jax import lax
from jax.experimental import pallas as pl
from jax.experimental.pallas import tpu as pltpu

D_MODEL = 1024
W_CONV = 512
W_SSM = 512
CONV_W = 3
N_GROUPS = 32
SSM_H = 16
STATE_P = 64
D_FF = 2816
EPS = 1e-5

SUBLANES = 8
BF16_ROWS = 16
GROUPS_PER_HALF = 16
N_HALVES = N_GROUPS // GROUPS_PER_HALF
U_HALF = GROUPS_PER_HALF * SSM_H
S_HALF = GROUPS_PER_HALF * STATE_P
S_COLS = 2 * N_GROUPS * STATE_P
SCAN_LANES = 512
MXU_COLS = 256
FFN_CHUNKS = tuple((s, MXU_COLS) for s in range(0, D_FF, MXU_COLS))
VMEM_LIMIT = 60 * 1024 * 1024

BF16 = jnp.bfloat16
F32 = jnp.float32


def _rms(x, g):
    y = x * lax.rsqrt(jnp.mean(x * x, axis=-1, keepdims=True) + EPS)
    return y * g


def _dot(a, b):
    return jnp.dot(a, b, preferred_element_type=F32)


def _split_bf16(a):
    hi = a.astype(BF16)
    return hi, (a - hi.astype(F32)).astype(BF16)


def _tile_copies(hbm, buf, sem, step, slot, *, nb, tc, per_seq, to_vmem):
    copies = []
    for k in range(SUBLANES):
        if per_seq:
            h = hbm.at[k, pl.ds(step * tc, tc), :]
            v = buf.at[slot, :, k, :]
        else:
            h = hbm.at[pl.ds(step * nb, nb), k, :]
            v = buf.at[slot, k]
        src, dst = (h, v) if to_vmem else (v, h)
        copies.append(pltpu.make_async_copy(src, dst, sem.at[slot, k]))
    return copies


N_MIX_W = 13
W_IN, BC2, CC, CCL, D0, W_GLU, W_OUT = 1, 5, 6, 7, 8, 10, 12
STREAM_ORDER = (BC2, W_OUT, CC, CCL, D0, W_GLU)


def _mixer_body(x_hbm, *refs, nb, tc, n_steps, carry, n_cast):
    if carry:
        cache_ref = re0_ref = im0_ref = None
    else:
        cache_ref, re0_ref, im0_ref, *refs = refs
    wts, refs = list(refs[:N_MIX_W]), refs[N_MIX_W:]
    cast_in, refs = refs[:n_cast], refs[n_cast:]
    x1_hbm, convnew_hbm, hre_ref, him_ref, *refs = refs
    cast_out, refs = refs[:n_cast], refs[n_cast:]
    xbuf, obuf, gbuf, hb_odd, hb_prev, hstate, vbuf, insem, outsem, cachesem, *refs = refs
    if carry:
        w_copies = {}
    else:
        assert n_steps == 1
        *wbufs, wsem = refs
        win_hbm, win_buf = wts[W_IN], wbufs[0]
        u_cols, rest_cols = pl.ds(3 * W_CONV, W_SSM), pl.ds(0, 3 * W_CONV)
        w_copies = {"u": pltpu.make_async_copy(win_hbm.at[:, u_cols], win_buf.at[:, u_cols], wsem.at[0]),
                    W_IN: pltpu.make_async_copy(win_hbm.at[:, rest_cols], win_buf.at[:, rest_cols], wsem.at[1])}
        wts[W_IN] = win_buf
        for j, (k, buf) in enumerate(zip(STREAM_ORDER, wbufs[1:])):
            w_copies[k] = pltpu.make_async_copy(wts[k], buf, wsem.at[2 + j])
            wts[k] = buf
    (nmix_ref, win_ref, convw_ref, lam2r_ref, lam2i_ref, bc2_ref, cc_ref, ccl_ref, d0_ref, dskip_ref,
     wglu_ref, bglu_ref, wout_ref) = wts

    def wait_w(*keys):
        for k in keys:
            if k in w_copies:
                w_copies[k].wait()
    tm = nb * tc
    n_pair = tc // 2
    tmh = n_pair * nb

    i = pl.program_id(0)
    slot = lax.rem(i, 2)
    cp = functools.partial(_tile_copies, nb=nb, tc=tc, per_seq=carry)
    in_copies = functools.partial(cp, x_hbm, xbuf, insem, to_vmem=True)
    out_copies = functools.partial(cp, x1_hbm, obuf, outsem, to_vmem=False)

    def cache_copies(hbm, row0, to_vmem):
        copies = []
        for k in range(CONV_W - 1):
            h = hbm.at[0, pl.ds(0 if carry else i * nb, nb), k, :]
            v = vbuf.at[pl.ds(row0 + k * nb, nb), :]
            src, dst = (h, v) if to_vmem else (v, h)
            copies.append(pltpu.make_async_copy(src, dst, cachesem.at[int(to_vmem), k]))
        return copies

    @pl.when(i == 0)
    def _():
        for c in in_copies(0, 0):
            c.start()

    if not carry:
        for c in cache_copies(cache_ref, 0, True):
            c.start()
        for k in ("u", BC2, W_IN) + STREAM_ORDER[1:]:
            w_copies[k].start()
    for c in in_copies(i, slot):
        c.wait()
    if not carry:
        for c in cache_copies(cache_ref, 0, True):
            c.wait()

    @pl.when(i + 1 < n_steps)
    def _():
        for c in in_copies(i + 1, 1 - slot):
            c.start()

    @pl.when(i >= 2)
    def _():
        for c in out_copies(i - 2, slot):
            c.wait()

    if carry:
        @pl.when(i == 0)
        def _():
            vbuf[0:2 * nb, :] = jnp.zeros((2 * nb, W_CONV), F32)
            hstate[...] = jnp.zeros((nb, S_COLS), F32)
    else:
        for n in range(N_HALVES):
            hstate[:, 2 * n * S_HALF:(2 * n + 1) * S_HALF] = re0_ref[n * S_HALF:(n + 1) * S_HALF, :].T
            hstate[:, (2 * n + 1) * S_HALF:(2 * n + 2) * S_HALF] = im0_ref[n * S_HALF:(n + 1) * S_HALF, :].T

    x = xbuf[slot].reshape(tm, D_MODEL)
    inv_rms = lax.rsqrt(jnp.mean(x * x, axis=-1, keepdims=True) + EPS)
    xg = (x * nmix_ref[...]).astype(BF16)
    xn = (x * inv_rms * nmix_ref[...]).astype(BF16)

    wait_w("u")
    u = inv_rms * _dot(xg, win_ref[:, 3 * W_CONV:])
    u_pairs = u.reshape(n_pair, 2, nb, W_SSM)
    u_e = u_pairs[:, 0].reshape(tmh, W_SSM)
    u_o = u_pairs[:, 1].reshape(tmh, W_SSM)
    ub_e = u_e.astype(BF16)
    ub_o = u_o.astype(BF16)
    wait_w(BC2, W_IN)
    for n in range(N_HALVES):
        cols = slice(n * U_HALF, (n + 1) * U_HALF)
        gbuf[:, 2 * n * S_HALF:(2 * n + 2) * S_HALF] = _dot(
            jnp.concatenate([ub_e[:, cols], ub_o[:, cols]], axis=1), bc2_ref[n])

    for src, dst in zip(cast_in, cast_out):
        dst[...] = src[...].astype(BF16)

    def scan_step(h, rows, lr, li, re_cols, im_cols):
        hr, hi = h
        return (lr * hr - li * hi + gbuf[rows, re_cols], lr * hi + li * hr + gbuf[rows, im_cols])

    def store_pair(buf, row0, pair, re_cols, im_cols):
        rows = slice(row0, row0 + BF16_ROWS)
        buf[rows, re_cols] = jnp.concatenate([h[0] for h in pair], axis=0).astype(BF16)
        buf[rows, im_cols] = jnp.concatenate([h[1] for h in pair], axis=0).astype(BF16)

    for n in range(N_HALVES):
        for c0 in range(0, S_HALF, SCAN_LANES):
            re_cols = slice(2 * n * S_HALF + c0, 2 * n * S_HALF + c0 + SCAN_LANES)
            im_cols = slice((2 * n + 1) * S_HALF + c0, (2 * n + 1) * S_HALF + c0 + SCAN_LANES)
            lr = lam2r_ref[n, :, c0:c0 + SCAN_LANES]
            li = lam2i_ref[n, :, c0:c0 + SCAN_LANES]
            step = functools.partial(scan_step, lr=lr, li=li, re_cols=re_cols, im_cols=im_cols)
            if nb == SUBLANES:
                h = (hstate[:, re_cols], hstate[:, im_cols])
                for k in range(0, n_pair, 2):
                    h0 = step(h, slice(k * nb, (k + 1) * nb))
                    h1 = step(h0, slice((k + 1) * nb, (k + 2) * nb))
                    store_pair(hb_odd, k * nb, (h0, h1), re_cols, im_cols)
                    store_pair(hb_prev, k * nb, (h, h0), re_cols, im_cols)
                    h = h1
                hstate[:, re_cols], hstate[:, im_cols] = h
            else:
                for r0 in range(0, nb, BF16_ROWS):
                    seqs = [slice(r0 + j * SUBLANES, r0 + (j + 1) * SUBLANES) for j in range(2)]
                    pair = [(hstate[sq, re_cols], hstate[sq, im_cols]) for sq in seqs]
                    for k in range(n_pair):
                        store_pair(hb_prev, k * nb + r0, pair, re_cols, im_cols)
                        pair = [step(h, slice(k * nb + sq.start, k * nb + sq.stop)) for h, sq in zip(pair, seqs)]
                        store_pair(hb_odd, k * nb + r0, pair, re_cols, im_cols)
                    for h, sq in zip(pair, seqs):
                        hstate[sq, re_cols], hstate[sq, im_cols] = h

    conv_chunks = []
    for c0 in range(0, W_CONV, MXU_COLS):
        cols = slice(c0, c0 + MXU_COLS)
        b_g = _dot(xn, win_ref[:, c0:c0 + MXU_COLS])
        c_g = _dot(xn, win_ref[:, W_CONV + c0:W_CONV + c0 + MXU_COLS])
        xc = _dot(xn, win_ref[:, 2 * W_CONV + c0:2 * W_CONV + c0 + MXU_COLS])
        v = c_g * xc
        vbuf[2 * nb:2 * nb + tm, cols] = v
        conv = (convw_ref[0, :, cols] * vbuf[0:tm, cols] + convw_ref[1, :, cols] * vbuf[nb:nb + tm, cols]
                + convw_ref[2, :, cols] * v)
        conv_chunks.append((b_g * conv).astype(BF16))
    conv_out = jnp.concatenate(conv_chunks, axis=-1)
    if carry:
        vbuf[0:2 * nb, :] = vbuf[tm:tm + 2 * nb, :]
    wait_w(W_OUT)
    x1 = x + _dot(conv_out, wout_ref[0:W_CONV, :])

    last = hstate[...]
    for n in range(N_HALVES):
        last_re = last[:, 2 * n * S_HALF:(2 * n + 1) * S_HALF]
        last_im = last[:, (2 * n + 1) * S_HALF:(2 * n + 2) * S_HALF]
        if carry:
            hre_ref[:, n * S_HALF:(n + 1) * S_HALF] = last_re
            him_ref[:, n * S_HALF:(n + 1) * S_HALF] = last_im
        else:
            hre_ref[n * S_HALF:(n + 1) * S_HALF, :] = last_re.T
            him_ref[n * S_HALF:(n + 1) * S_HALF, :] = last_im.T

    wait_w(CC, CCL, D0)
    zs = []
    for n in range(N_HALVES):
        cols = slice(n * U_HALF, (n + 1) * U_HALF)
        st_cols = slice(2 * n * S_HALF, (2 * n + 2) * S_HALF)
        y_o = _dot(hb_odd[:, st_cols], cc_ref[n]) + dskip_ref[:, cols] * u_o[:, cols]
        y_e = (_dot(hb_prev[:, st_cols], ccl_ref[n]) + _dot(ub_e[:, cols], d0_ref[n])
               + dskip_ref[:, cols] * u_e[:, cols])
        y = jnp.stack([y_e.reshape(n_pair, nb, U_HALF), y_o.reshape(n_pair, nb, U_HALF)],
                      axis=1).reshape(tm, U_HALF)
        zs.append(jax.nn.gelu(y))
    zb = jnp.concatenate([z.astype(BF16) for z in zs], axis=-1)
    wait_w(W_GLU)
    ssm_chunks = []
    for n, z in enumerate(zs):
        cols = slice(n * U_HALF, (n + 1) * U_HALF)
        gate = jax.nn.sigmoid(_dot(zb, wglu_ref[:, cols]) + bglu_ref[:, cols])
        ssm_chunks.append((z * gate).astype(BF16))
    ssm_out = jnp.concatenate(ssm_chunks, axis=-1)

    x1 = x1 + _dot(ssm_out, wout_ref[W_CONV:, :])

    obuf[slot] = x1.reshape(tc, nb, D_MODEL)
    for c in out_copies(i, slot):
        c.start()

    if not carry:
        for c in cache_copies(convnew_hbm, tm, False):
            c.start()
        for c in cache_copies(convnew_hbm, tm, False):
            c.wait()

    @pl.when(i == n_steps - 1)
    def _():
        if carry:
            for c in cache_copies(convnew_hbm, tm, False):
                c.start()
            for c in cache_copies(convnew_hbm, tm, False):
                c.wait()
        if n_steps >= 2:
            for c in out_copies(i - 1, 1 - slot):
                c.wait()
        for c in out_copies(i, slot):
            c.wait()


def _resident(shape):
    nd = len(shape)
    return pl.BlockSpec(shape, lambda i: (0,) * nd, pipeline_mode=pl.Buffered(1))


def _mixer(x, wts, state=None, cast=(), *, nb, tc):
    n_seq, n_time, _ = x.shape
    carry = state is None
    if carry:
        assert n_seq == nb == SUBLANES and n_time % tc == 0 and tc % 4 == 0
        n_steps = n_time // tc
    else:
        assert n_time == tc == SUBLANES and n_seq % nb == 0 and nb % BF16_ROWS == 0
        n_steps = n_seq // nb
    tm = nb * tc
    n_st = N_GROUPS * STATE_P
    body = functools.partial(_mixer_body, nb=nb, tc=tc, n_steps=n_steps, carry=carry, n_cast=len(cast))
    any_spec = pl.BlockSpec(memory_space=pl.ANY)
    if carry:
        st_spec = pl.BlockSpec((nb, n_st), lambda i: (0, 0))
        st_shape = jax.ShapeDtypeStruct((n_seq, n_st), F32)
    else:
        st_spec = pl.BlockSpec((n_st, nb), lambda i: (0, i))
        st_shape = jax.ShapeDtypeStruct((n_st, n_seq), F32)
    state_specs = [] if carry else [any_spec, st_spec, st_spec]
    assert len(wts) == N_MIX_W
    streamed = () if carry else (W_IN,) + STREAM_ORDER
    w_specs = [any_spec if k in streamed else _resident(w.shape) for k, w in enumerate(wts)]
    w_scratch = [pltpu.VMEM(wts[k].shape, BF16) for k in streamed]
    if streamed:
        w_scratch.append(pltpu.SemaphoreType.DMA((1 + len(streamed),)))
    for w in cast:
        assert w.shape[0] % (n_steps * BF16_ROWS) == 0, w.shape
    cast_specs = [pl.BlockSpec((w.shape[0] // n_steps, w.shape[1]), lambda i: (i, 0)) for w in cast]
    return pl.pallas_call(
        body,
        grid=(n_steps,),
        in_specs=[any_spec] + state_specs + w_specs + cast_specs,
        out_specs=[any_spec, any_spec, st_spec, st_spec] + cast_specs,
        out_shape=[jax.ShapeDtypeStruct(x.shape, F32),
                   jax.ShapeDtypeStruct((1, n_seq, CONV_W - 1, W_CONV), F32),
                   st_shape, st_shape]
                  + [jax.ShapeDtypeStruct(w.shape, BF16) for w in cast],
        scratch_shapes=[pltpu.VMEM((2, tc, nb, D_MODEL), F32),
                        pltpu.VMEM((2, tc, nb, D_MODEL), F32),
                        pltpu.VMEM((tm // 2, S_COLS), F32),
                        pltpu.VMEM((tm // 2, S_COLS), BF16),
                        pltpu.VMEM((tm // 2, S_COLS), BF16),
                        pltpu.VMEM((nb, S_COLS), F32),
                        pltpu.VMEM((2 * nb + tm, W_CONV), F32),
                        pltpu.SemaphoreType.DMA((2, SUBLANES)),
                        pltpu.SemaphoreType.DMA((2, SUBLANES)),
                        pltpu.SemaphoreType.DMA((2, CONV_W - 1))] + w_scratch,
        compiler_params=pltpu.CompilerParams(dimension_semantics=("arbitrary",),
                                             vmem_limit_bytes=VMEM_LIMIT),
        name="mixer_carry" if carry else "mixer_step",
    )(x, *(() if carry else state), *wts, *cast)


def _ffn_body(xp_hbm, xs_hbm, nffn_ref, wg_ref, wu_ref, wd_ref, nfin_ref, yp_hbm, ys_hbm,
              xbuf, obuf, abuf, insem, outsem, *, n_p, n_s, tm):
    i = pl.program_id(0)
    n_steps = n_p + n_s
    slot = lax.rem(i, 2)

    def tile_copy(step, buf_slot, to_vmem, first):
        hbm = (xp_hbm if first else xs_hbm) if to_vmem else (yp_hbm if first else ys_hbm)
        h = hbm.at[pl.ds((step if first else step - n_p) * tm, tm), :]
        if to_vmem:
            return pltpu.make_async_copy(h, xbuf.at[buf_slot], insem.at[buf_slot])
        return pltpu.make_async_copy(obuf.at[buf_slot], h, outsem.at[buf_slot])

    def start(step, buf_slot, to_vmem):
        pl.when(step < n_p)(lambda: tile_copy(step, buf_slot, to_vmem, True).start())
        pl.when(step >= n_p)(lambda: tile_copy(step, buf_slot, to_vmem, False).start())

    def wait(buf_slot, to_vmem):
        tile_copy(0, buf_slot, to_vmem, True).wait()

    pl.when(i == 0)(lambda: start(0, 0, True))
    wait(slot, True)
    pl.when(i + 1 < n_steps)(lambda: start(i + 1, 1 - slot, True))
    pl.when(i >= 2)(lambda: wait(slot, False))

    x = xbuf[slot]
    hn = _rms(x, nffn_ref[...]).astype(BF16)
    for s, w in FFN_CHUNKS:
        g = _dot(hn, wg_ref[:, s:s + w])
        up = _dot(hn, wu_ref[:, s:s + w])
        abuf[:, s:s + w] = (g * jax.nn.sigmoid(g) * up).astype(BF16)
    acc = x + _dot(abuf[...], wd_ref[...])
    obuf[slot] = _rms(acc, nfin_ref[...])
    start(i, slot, False)

    @pl.when(i == n_steps - 1)
    def _():
        if n_steps >= 2:
            wait(1 - slot, False)
        wait(slot, False)


def _ffn(xp, xs, wts, *, tm):
    n_p, n_s = xp.shape[0] // tm, xs.shape[0] // tm
    assert xp.shape[0] == n_p * tm and xs.shape[0] == n_s * tm
    any_spec = pl.BlockSpec(memory_space=pl.ANY)
    return pl.pallas_call(
        functools.partial(_ffn_body, n_p=n_p, n_s=n_s, tm=tm),
        grid=(n_p + n_s,),
        in_specs=[any_spec, any_spec] + [_resident(w.shape) for w in wts],
        out_specs=[any_spec, any_spec],
        out_shape=[jax.ShapeDtypeStruct(xp.shape, F32), jax.ShapeDtypeStruct(xs.shape, F32)],
        scratch_shapes=[pltpu.VMEM((2, tm, D_MODEL), F32),
                        pltpu.VMEM((2, tm, D_MODEL), F32),
                        pltpu.VMEM((tm, D_FF), BF16),
                        pltpu.SemaphoreType.DMA((2,)),
                        pltpu.SemaphoreType.DMA((2,))],
        compiler_params=pltpu.CompilerParams(dimension_semantics=("arbitrary",),
                                             vmem_limit_bytes=VMEM_LIMIT),
        name="ffn",
    )(xp, xs, *wts)


N_S5_OUT = 6
N_S5_SCRATCH = 4


def _prep_body(lre_ref, lim_ref, logdt_ref, btre_ref, btim_ref, cre_ref, cim_ref, *refs):
    n_cast = (len(refs) - N_S5_OUT - N_S5_SCRATCH) // 2
    cast_in, refs = refs[:n_cast], refs[n_cast:]
    s5_out, refs = refs[:N_S5_OUT], refs[N_S5_OUT:]
    cast_out, s5_scratch = refs[:n_cast], refs[n_cast:]
    for src, dst in zip(cast_in, cast_out):
        dst[...] = src[...].astype(BF16)
    pl.when(pl.program_id(0) == 0)(functools.partial(
        _s5_matrices, lre_ref, lim_ref, logdt_ref, btre_ref, btim_ref, cre_ref, cim_ref, *s5_out, *s5_scratch))


def _s5_matrices(lre_ref, lim_ref, logdt_ref, btre_ref, btim_ref, cre_ref, cim_ref,
                 lam2r_ref, lam2i_ref, bc2_ref, cc_ref, ccl_ref, d0_ref, bscr, lbscr, cscr, clscr):
    lre = lre_ref[...]
    lim = lim_ref[...]
    grp = (lax.broadcasted_iota(jnp.int32, (N_GROUPS, N_GROUPS), 0)
           == lax.broadcasted_iota(jnp.int32, (N_GROUPS, N_GROUPS), 1))
    logdt = jnp.sum(jnp.where(grp, jnp.broadcast_to(logdt_ref[...], (N_GROUPS, N_GROUPS)), 0.0),
                    axis=1, keepdims=True)
    dt = jnp.exp(logdt)
    mag = jnp.exp(lre * dt)
    lbr = mag * jnp.cos(lim * dt)
    lbi = mag * jnp.sin(lim * dt)
    l2r = lbr * lbr - lbi * lbi
    l2i = 2.0 * (lbr * lbi)
    den = lre * lre + lim * lim
    fr = ((lbr - 1.0) * lre + lbi * lim) / den
    fi = (lbi * lre - (lbr - 1.0) * lim) / den
    frb, fib = fr[:, None, :], fi[:, None, :]
    lrb, lib = lbr[:, None, :], lbi[:, None, :]
    bbr = frb * btre_ref[...] - fib * btim_ref[...]
    bbi = frb * btim_ref[...] + fib * btre_ref[...]
    lbbr = lrb * bbr - lib * bbi
    lbbi = lrb * bbi + lib * bbr
    cre = cre_ref[...]
    cim = cim_ref[...]
    clr = cre * lrb - cim * lib
    cli = cre * lib + cim * lrb

    for scr in (bscr, lbscr, cscr, clscr):
        scr[...] = jnp.zeros_like(scr)
    for g in range(N_GROUPS):
        n, k = divmod(g, GROUPS_PER_HALF)
        chans = slice(k * SSM_H, (k + 1) * SSM_H)
        states = slice(k * STATE_P, (k + 1) * STATE_P)
        im_states = slice(S_HALF + k * STATE_P, S_HALF + (k + 1) * STATE_P)
        for scr, re, im in ((bscr, bbr, bbi), (lbscr, lbbr, lbbi), (cscr, cre, -cim), (clscr, clr, -cli)):
            scr[n, chans, states] = re[g]
            scr[n, chans, im_states] = im[g]
        lam2r_ref[n, :, states] = jnp.broadcast_to(l2r[g:g + 1, :], (SUBLANES, STATE_P))
        lam2i_ref[n, :, states] = jnp.broadcast_to(l2i[g:g + 1, :], (SUBLANES, STATE_P))
    for n in range(N_HALVES):
        bc2_ref[n, 0:U_HALF, :] = lbscr[n].astype(BF16)
        bc2_ref[n, U_HALF:, :] = bscr[n].astype(BF16)
        c_t = cscr[n].T
        cc_ref[n] = c_t.astype(BF16)
        ccl_ref[n] = clscr[n].T.astype(BF16)
        b_hi, b_lo = _split_bf16(bscr[n])
        c_hi, c_lo = _split_bf16(c_t)
        d0_ref[n] = (_dot(b_hi, c_hi) + _dot(b_hi, c_lo) + _dot(b_lo, c_hi)).astype(BF16)


PREP_STEPS = 2


def _prep(lam_re, lam_im, log_dt, b_re, b_im, c_re, c_im, cast):
    s5_in = (lam_re, lam_im, log_dt.reshape(1, N_GROUPS), b_re.transpose(0, 2, 1), b_im.transpose(0, 2, 1),
             c_re, c_im)
    s5_shapes = [jax.ShapeDtypeStruct((N_HALVES, SUBLANES, S_HALF), F32),
                 jax.ShapeDtypeStruct((N_HALVES, SUBLANES, S_HALF), F32),
                 jax.ShapeDtypeStruct((N_HALVES, 2 * U_HALF, 2 * S_HALF), BF16),
                 jax.ShapeDtypeStruct((N_HALVES, 2 * S_HALF, U_HALF), BF16),
                 jax.ShapeDtypeStruct((N_HALVES, 2 * S_HALF, U_HALF), BF16),
                 jax.ShapeDtypeStruct((N_HALVES, U_HALF, U_HALF), BF16)]
    assert len(s5_shapes) == N_S5_OUT
    for w in cast:
        assert w.shape[0] % (PREP_STEPS * BF16_ROWS) == 0, w.shape
    cast_specs = [pl.BlockSpec((w.shape[0] // PREP_STEPS, w.shape[1]), lambda i: (i, 0)) for w in cast]
    whole = lambda a: pl.BlockSpec(a.shape, lambda i: (0,) * len(a.shape))
    return pl.pallas_call(
        _prep_body,
        grid=(PREP_STEPS,),
        in_specs=[whole(a) for a in s5_in] + cast_specs,
        out_specs=[whole(a) for a in s5_shapes] + cast_specs,
        out_shape=s5_shapes + [jax.ShapeDtypeStruct(w.shape, BF16) for w in cast],
        scratch_shapes=[pltpu.VMEM((N_HALVES, U_HALF, 2 * S_HALF), F32)] * N_S5_SCRATCH,
        compiler_params=pltpu.CompilerParams(dimension_semantics=("arbitrary",)),
        name="prep",
    )(*s5_in, *cast)


def kernel(x_prompt, x_sample, cache_conv, state_ssm_re, state_ssm_im, norm_mix, w_in, conv_w, ssm_lam_re, ssm_lam_im, ssm_log_dt, ssm_b_re, ssm_b_im, ssm_c_re, ssm_c_im, ssm_d, w_glu, b_glu, w_out, norm_ffn, w_gate, w_up, w_down, norm_final):
    assert norm_mix.shape[0] == 1, "single-layer trunk"
    n_st = N_GROUPS * STATE_P
    lam2r, lam2i, bc2, cc, ccl, d0, w_in_b, w_glu_b, w_out_b = _prep(
        ssm_lam_re[0], ssm_lam_im[0], ssm_log_dt[0], ssm_b_re[0], ssm_b_im[0], ssm_c_re[0], ssm_c_im[0],
        cast=(w_in[0], w_glu[0], w_out[0]))
    mix_w = (norm_mix, w_in_b, conv_w[0][:, None, :], lam2r, lam2i, bc2, cc, ccl, d0, ssm_d, w_glu_b, b_glu,
             w_out_b)

    bp, lp, _ = x_prompt.shape
    bs, ls, _ = x_sample.shape

    x1p, convp, rep, imp, wg, wu, wd = _mixer(x_prompt, mix_w, cast=(w_gate[0], w_up[0], w_down[0]),
                                              nb=SUBLANES, tc=128)
    ffn_w = (norm_ffn, wg, wu, wd, norm_final.reshape(1, D_MODEL))
    def state_major(h):
        return h[0].transpose(1, 2, 0).reshape(n_st, h.shape[1])

    sample_state = (cache_conv, state_major(state_ssm_re), state_major(state_ssm_im))
    x1s, convs, res, ims = _mixer(x_sample, mix_w, sample_state, nb=128, tc=SUBLANES)

    yp, ys = _ffn(x1p.reshape(bp * lp, D_MODEL), x1s.reshape(bs * ls, D_MODEL), ffn_w, tm=1024)
    yp = yp.reshape(bp, lp, D_MODEL)
    ys = ys.reshape(bs, ls, D_MODEL)

    def seq_major(h):
        return h.reshape(N_GROUPS, STATE_P, h.shape[1]).transpose(2, 0, 1)[None]

    return (yp, ys, convp, rep.reshape(1, bp, N_GROUPS, STATE_P), imp.reshape(1, bp, N_GROUPS, STATE_P),
            convs, seq_major(res), seq_major(ims))
```

```python
import functools

import jax
import jax.numpy as jnp
from jax import lax
from jax.experimental import pallas as pl
from jax.experimental.pallas import tpu as pltpu

D_MODEL = 1024
W_CONV = 512
W_SSM = 512
CONV_W = 3
N_GROUPS = 32
SSM_H = 16
STATE_P = 64
D_FF = 2816
EPS = 1e-5

SUBLANES = 8
BF16_ROWS = 16
GROUPS_PER_HALF = 16
N_HALVES = N_GROUPS // GROUPS_PER_HALF
U_HALF = GROUPS_PER_HALF * SSM_H
S_HALF = GROUPS_PER_HALF * STATE_P
S_COLS = 2 * N_GROUPS * STATE_P
SCAN_LANES = 512
MXU_COLS = 256
FFN_CHUNKS = tuple((s, MXU_COLS) for s in range(0, D_FF, MXU_COLS))
VMEM_LIMIT = 60 * 1024 * 1024

BF16 = jnp.bfloat16
F32 = jnp.float32


def _rms(x, g):
    y = x * lax.rsqrt(jnp.mean(x * x, axis=-1, keepdims=True) + EPS)
    return y * g


def _dot(a, b):
    return jnp.dot(a, b, preferred_element_type=F32)


def _split_bf16(a):
    hi = a.astype(BF16)
    return hi, (a - hi.astype(F32)).astype(BF16)


def _tile_copies(hbm, buf, sem, step, slot, *, nb, tc, per_seq, to_vmem):
    copies = []
    for k in range(SUBLANES):
        if per_seq:
            h = hbm.at[k, pl.ds(step * tc, tc), :]
            v = buf.at[slot, :, k, :]
        else:
            h = hbm.at[pl.ds(step * nb, nb), k, :]
            v = buf.at[slot, k]
        src, dst = (h, v) if to_vmem else (v, h)
        copies.append(pltpu.make_async_copy(src, dst, sem.at[slot, k]))
    return copies


N_MIX_W = 13
W_IN, BC2, CC, CCL, D0, W_GLU, W_OUT = 1, 5, 6, 7, 8, 10, 12
STREAM_ORDER = (BC2, W_OUT, CC, CCL, D0, W_GLU)


def _mixer_body(x_hbm, *refs, nb, tc, n_steps, carry, n_cast):
    if carry:
        cache_ref = re0_ref = im0_ref = None
    else:
        cache_ref, re0_ref, im0_ref, *refs = refs
    wts, refs = list(refs[:N_MIX_W]), refs[N_MIX_W:]
    cast_in, refs = refs[:n_cast], refs[n_cast:]
    x1_hbm, convnew_hbm, hre_ref, him_ref, *refs = refs
    cast_out, refs = refs[:n_cast], refs[n_cast:]
    xbuf, obuf, gbuf, hb_odd, hb_prev, hstate, vbuf, insem, outsem, cachesem, *refs = refs
    if carry:
        w_copies = {}
    else:
        assert n_steps == 1
        *wbufs, wsem = refs
        win_hbm, win_buf = wts[W_IN], wbufs[0]
        u_cols, rest_cols = pl.ds(3 * W_CONV, W_SSM), pl.ds(0, 3 * W_CONV)
        w_copies = {"u": pltpu.make_async_copy(win_hbm.at[:, u_cols], win_buf.at[:, u_cols], wsem.at[0]),
                    W_IN: pltpu.make_async_copy(win_hbm.at[:, rest_cols], win_buf.at[:, rest_cols], wsem.at[1])}
        wts[W_IN] = win_buf
        for j, (k, buf) in enumerate(zip(STREAM_ORDER, wbufs[1:])):
            w_copies[k] = pltpu.make_async_copy(wts[k], buf, wsem.at[2 + j])
            wts[k] = buf
    (nmix_ref, win_ref, convw_ref, lam2r_ref, lam2i_ref, bc2_ref, cc_ref, ccl_ref, d0_ref, dskip_ref,
     wglu_ref, bglu_ref, wout_ref) = wts

    def wait_w(*keys):
        for k in keys:
            if k in w_copies:
                w_copies[k].wait()
    tm = nb * tc
    n_pair = tc // 2
    tmh = n_pair * nb

    i = pl.program_id(0)
    slot = lax.rem(i, 2)
    cp = functools.partial(_tile_copies, nb=nb, tc=tc, per_seq=carry)
    in_copies = functools.partial(cp, x_hbm, xbuf, insem, to_vmem=True)
    out_copies = functools.partial(cp, x1_hbm, obuf, outsem, to_vmem=False)

    def cache_copies(hbm, row0, to_vmem):
        copies = []
        for k in range(CONV_W - 1):
            h = hbm.at[0, pl.ds(0 if carry else i * nb, nb), k, :]
            v = vbuf.at[pl.ds(row0 + k * nb, nb), :]
            src, dst = (h, v) if to_vmem else (v, h)
            copies.append(pltpu.make_async_copy(src, dst, cachesem.at[int(to_vmem), k]))
        return copies

    @pl.when(i == 0)
    def _():
        for c in in_copies(0, 0):
            c.start()

    if not carry:
        for c in cache_copies(cache_ref, 0, True):
            c.start()
        for k in ("u", BC2, W_IN) + STREAM_ORDER[1:]:
            w_copies[k].start()
    for c in in_copies(i, slot):
        c.wait()
    if not carry:
        for c in cache_copies(cache_ref, 0, True):
            c.wait()

    @pl.when(i + 1 < n_steps)
    def _():
        for c in in_copies(i + 1, 1 - slot):
            c.start()

    @pl.when(i >= 2)
    def _():
        for c in out_copies(i - 2, slot):
            c.wait()

    if carry:
        @pl.when(i == 0)
        def _():
            vbuf[0:2 * nb, :] = jnp.zeros((2 * nb, W_CONV), F32)
            hstate[...] = jnp.zeros((nb, S_COLS), F32)
    else:
        for n in range(N_HALVES):
            hstate[:, 2 * n * S_HALF:(2 * n + 1) * S_HALF] = re0_ref[n * S_HALF:(n + 1) * S_HALF, :].T
            hstate[:, (2 * n + 1) * S_HALF:(2 * n + 2) * S_HALF] = im0_ref[n * S_HALF:(n + 1) * S_HALF, :].T

    x = xbuf[slot].reshape(tm, D_MODEL)
    inv_rms = lax.rsqrt(jnp.mean(x * x, axis=-1, keepdims=True) + EPS)
    xg = (x * nmix_ref[...]).astype(BF16)
    xn = (x * inv_rms * nmix_ref[...]).astype(BF16)

    wait_w("u")
    u = inv_rms * _dot(xg, win_ref[:, 3 * W_CONV:])
    u_pairs = u.reshape(n_pair, 2, nb, W_SSM)
    u_e = u_pairs[:, 0].reshape(tmh, W_SSM)
    u_o = u_pairs[:, 1].reshape(tmh, W_SSM)
    ub_e = u_e.astype(BF16)
    ub_o = u_o.astype(BF16)
    wait_w(BC2, W_IN)
    for n in range(N_HALVES):
        cols = slice(n * U_HALF, (n + 1) * U_HALF)
        gbuf[:, 2 * n * S_HALF:(2 * n + 2) * S_HALF] = _dot(
            jnp.concatenate([ub_e[:, cols], ub_o[:, cols]], axis=1), bc2_ref[n])

    for src, dst in zip(cast_in, cast_out):
        dst[...] = src[...].astype(BF16)

    def scan_step(h, rows, lr, li, re_cols, im_cols):
        hr, hi = h
        return (lr * hr - li * hi + gbuf[rows, re_cols], lr * hi + li * hr + gbuf[rows, im_cols])

    def store_pair(buf, row0, pair, re_cols, im_cols):
        rows = slice(row0, row0 + BF16_ROWS)
        buf[rows, re_cols] = jnp.concatenate([h[0] for h in pair], axis=0).astype(BF16)
        buf[rows, im_cols] = jnp.concatenate([h[1] for h in pair], axis=0).astype(BF16)

    for n in range(N_HALVES):
        for c0 in range(0, S_HALF, SCAN_LANES):
            re_cols = slice(2 * n * S_HALF + c0, 2 * n * S_HALF + c0 + SCAN_LANES)
            im_cols = slice((2 * n + 1) * S_HALF + c0, (2 * n + 1) * S_HALF + c0 + SCAN_LANES)
            lr = lam2r_ref[n, :, c0:c0 + SCAN_LANES]
            li = lam2i_ref[n, :, c0:c0 + SCAN_LANES]
            step = functools.partial(scan_step, lr=lr, li=li, re_cols=re_cols, im_cols=im_cols)
            if nb == SUBLANES:
                h = (hstate[:, re_cols], hstate[:, im_cols])
                for k in range(0, n_pair, 2):
                    h0 = step(h, slice(k * nb, (k + 1) * nb))
                    h1 = step(h0, slice((k + 1) * nb, (k + 2) * nb))
                    store_pair(hb_odd, k * nb, (h0, h1), re_cols, im_cols)
                    store_pair(hb_prev, k * nb, (h, h0), re_cols, im_cols)
                    h = h1
                hstate[:, re_cols], hstate[:, im_cols] = h
            else:
                for r0 in range(0, nb, BF16_ROWS):
                    seqs = [slice(r0 + j * SUBLANES, r0 + (j + 1) * SUBLANES) for j in range(2)]
                    pair = [(hstate[sq, re_cols], hstate[sq, im_cols]) for sq in seqs]
                    for k in range(n_pair):
                        store_pair(hb_prev, k * nb + r0, pair, re_cols, im_cols)
                        pair = [step(h, slice(k * nb + sq.start, k * nb + sq.stop)) for h, sq in zip(pair, seqs)]
                        store_pair(hb_odd, k * nb + r0, pair, re_cols, im_cols)
                    for h, sq in zip(pair, seqs):
                        hstate[sq, re_cols], hstate[sq, im_cols] = h

    conv_chunks = []
    for c0 in range(0, W_CONV, MXU_COLS):
        cols = slice(c0, c0 + MXU_COLS)
        b_g = _dot(xn, win_ref[:, c0:c0 + MXU_COLS])
        c_g = _dot(xn, win_ref[:, W_CONV + c0:W_CONV + c0 + MXU_COLS])
        xc = _dot(xn, win_ref[:, 2 * W_CONV + c0:2 * W_CONV + c0 + MXU_COLS])
        v = c_g * xc
        vbuf[2 * nb:2 * nb + tm, cols] = v
        conv = (convw_ref[0, :, cols] * vbuf[0:tm, cols] + convw_ref[1, :, cols] * vbuf[nb:nb + tm, cols]
                + convw_ref[2, :, cols] * v)
        conv_chunks.append((b_g * conv).astype(BF16))
    conv_out = jnp.concatenate(conv_chunks, axis=-1)
    if carry:
        vbuf[0:2 * nb, :] = vbuf[tm:tm + 2 * nb, :]
    wait_w(*STREAM_ORDER[1:])
    x1 = x + _dot(conv_out, wout_ref[0:W_CONV, :])

    last = hstate[...]
    for n in range(N_HALVES):
        last_re = last[:, 2 * n * S_HALF:(2 * n + 1) * S_HALF]
        last_im = last[:, (2 * n + 1) * S_HALF:(2 * n + 2) * S_HALF]
        if not carry:
            hre_ref[n * S_HALF:(n + 1) * S_HALF, :] = last_re.T
            him_ref[n * S_HALF:(n + 1) * S_HALF, :] = last_im.T

    zs = []
    for n in range(N_HALVES):
        cols = slice(n * U_HALF, (n + 1) * U_HALF)
        st_cols = slice(2 * n * S_HALF, (2 * n + 2) * S_HALF)
        y_o = _dot(hb_odd[:, st_cols], cc_ref[n]) + dskip_ref[:, cols] * u_o[:, cols]
        y_e = (_dot(hb_prev[:, st_cols], ccl_ref[n]) + _dot(ub_e[:, cols], d0_ref[n])
               + dskip_ref[:, cols] * u_e[:, cols])
        y = jnp.stack([y_e.reshape(n_pair, nb, U_HALF), y_o.reshape(n_pair, nb, U_HALF)],
                      axis=1).reshape(tm, U_HALF)
        zs.append(jax.nn.gelu(y))
    zb = jnp.concatenate([z.astype(BF16) for z in zs], axis=-1)
    ssm_chunks = []
    for n, z in enumerate(zs):
        cols = slice(n * U_HALF, (n + 1) * U_HALF)
        gate = jax.nn.sigmoid(_dot(zb, wglu_ref[:, cols]) + bglu_ref[:, cols])
        ssm_chunks.append((z * gate).astype(BF16))
    ssm_out = jnp.concatenate(ssm_chunks, axis=-1)

    x1 = x1 + _dot(ssm_out, wout_ref[W_CONV:, :])

    obuf[slot] = x1.reshape(tc, nb, D_MODEL)
    for c in out_copies(i, slot):
        c.start()

    if not carry:
        for c in cache_copies(convnew_hbm, tm, False):
            c.start()
        for c in cache_copies(convnew_hbm, tm, False):
            c.wait()

    @pl.when(i == n_steps - 1)
    def _():
        if carry:
            for c in cache_copies(convnew_hbm, tm, False):
                c.start()
            rows128 = jnp.concatenate([hstate[...]] * (128 // nb), axis=0)
            for n in range(N_HALVES):
                rows = slice(n * S_HALF, (n + 1) * S_HALF)
                hre_ref[rows, :] = rows128[:, 2 * n * S_HALF:(2 * n + 1) * S_HALF].T[:, 0:nb]
                him_ref[rows, :] = rows128[:, (2 * n + 1) * S_HALF:(2 * n + 2) * S_HALF].T[:, 0:nb]
            for c in cache_copies(convnew_hbm, tm, False):
                c.wait()
        if n_steps >= 2:
            for c in out_copies(i - 1, 1 - slot):
                c.wait()
        for c in out_copies(i, slot):
            c.wait()


def _resident(shape):
    nd = len(shape)
    return pl.BlockSpec(shape, lambda i: (0,) * nd, pipeline_mode=pl.Buffered(1))


def _mixer(x, wts, state=None, cast=(), *, nb, tc):
    n_seq, n_time, _ = x.shape
    carry = state is None
    if carry:
        assert n_seq == nb == SUBLANES and n_time % tc == 0 and tc % 4 == 0
        n_steps = n_time // tc
    else:
        assert n_time == tc == SUBLANES and n_seq % nb == 0 and nb % BF16_ROWS == 0
        n_steps = n_seq // nb
    tm = nb * tc
    n_st = N_GROUPS * STATE_P
    body = functools.partial(_mixer_body, nb=nb, tc=tc, n_steps=n_steps, carry=carry, n_cast=len(cast))
    any_spec = pl.BlockSpec(memory_space=pl.ANY)
    if carry:
        st_spec = pl.BlockSpec((n_st, nb), lambda i: (0, 0))
        st_shape = jax.ShapeDtypeStruct((n_st, n_seq), F32)
    else:
        st_spec = pl.BlockSpec((n_st, nb), lambda i: (0, i))
        st_shape = jax.ShapeDtypeStruct((n_st, n_seq), F32)
    state_specs = [] if carry else [any_spec, st_spec, st_spec]
    assert len(wts) == N_MIX_W
    streamed = () if carry else (W_IN,) + STREAM_ORDER
    w_specs = [any_spec if k in streamed else _resident(w.shape) for k, w in enumerate(wts)]
    w_scratch = [pltpu.VMEM(wts[k].shape, BF16) for k in streamed]
    if streamed:
        w_scratch.append(pltpu.SemaphoreType.DMA((1 + len(streamed),)))
    for w in cast:
        assert w.shape[0] % (n_steps * BF16_ROWS) == 0, w.shape
    cast_specs = [pl.BlockSpec((w.shape[0] // n_steps, w.shape[1]), lambda i: (i, 0)) for w in cast]
    return pl.pallas_call(
        body,
        grid=(n_steps,),
        in_specs=[any_spec] + state_specs + w_specs + cast_specs,
        out_specs=[any_spec, any_spec, st_spec, st_spec] + cast_specs,
        out_shape=[jax.ShapeDtypeStruct(x.shape, F32),
                   jax.ShapeDtypeStruct((1, n_seq, CONV_W - 1, W_CONV), F32),
                   st_shape, st_shape]
                  + [jax.ShapeDtypeStruct(w.shape, BF16) for w in cast],
        scratch_shapes=[pltpu.VMEM((2, tc, nb, D_MODEL), F32),
                        pltpu.VMEM((2, tc, nb, D_MODEL), F32),
                        pltpu.VMEM((tm // 2, S_COLS), F32),
                        pltpu.VMEM((tm // 2, S_COLS), BF16),
                        pltpu.VMEM((tm // 2, S_COLS), BF16),
                        pltpu.VMEM((nb, S_COLS), F32),
                        pltpu.VMEM((2 * nb + tm, W_CONV), F32),
                        pltpu.SemaphoreType.DMA((2, SUBLANES)),
                        pltpu.SemaphoreType.DMA((2, SUBLANES)),
                        pltpu.SemaphoreType.DMA((2, CONV_W - 1))] + w_scratch,
        compiler_params=pltpu.CompilerParams(dimension_semantics=("arbitrary",),
                                             vmem_limit_bytes=VMEM_LIMIT),
        name="mixer_carry" if carry else "mixer_step",
    )(x, *(() if carry else state), *wts, *cast)


def _ffn_body(xp_hbm, xs_hbm, nffn_ref, wg_ref, wu_ref, wd_ref, nfin_ref, yp_hbm, ys_hbm,
              xbuf, obuf, abuf, insem, outsem, *, n_p, n_s, tm):
    i = pl.program_id(0)
    n_steps = n_p + n_s
    slot = lax.rem(i, 2)

    def tile_copy(step, buf_slot, to_vmem, first):
        hbm = (xp_hbm if first else xs_hbm) if to_vmem else (yp_hbm if first else ys_hbm)
        h = hbm.at[pl.ds((step if first else step - n_p) * tm, tm), :]
        if to_vmem:
            return pltpu.make_async_copy(h, xbuf.at[buf_slot], insem.at[buf_slot])
        return pltpu.make_async_copy(obuf.at[buf_slot], h, outsem.at[buf_slot])

    def start(step, buf_slot, to_vmem):
        pl.when(step < n_p)(lambda: tile_copy(step, buf_slot, to_vmem, True).start())
        pl.when(step >= n_p)(lambda: tile_copy(step, buf_slot, to_vmem, False).start())

    def wait(buf_slot, to_vmem):
        tile_copy(0, buf_slot, to_vmem, True).wait()

    pl.when(i == 0)(lambda: start(0, 0, True))
    wait(slot, True)
    pl.when(i + 1 < n_steps)(lambda: start(i + 1, 1 - slot, True))
    pl.when(i >= 2)(lambda: wait(slot, False))

    x = xbuf[slot]
    hn = _rms(x, nffn_ref[...]).astype(BF16)
    for s, w in FFN_CHUNKS:
        g = _dot(hn, wg_ref[:, s:s + w])
        up = _dot(hn, wu_ref[:, s:s + w])
        abuf[:, s:s + w] = (g * jax.nn.sigmoid(g) * up).astype(BF16)
    acc = x + _dot(abuf[...], wd_ref[...])
    obuf[slot] = _rms(acc, nfin_ref[...])
    start(i, slot, False)

    @pl.when(i == n_steps - 1)
    def _():
        if n_steps >= 2:
            wait(1 - slot, False)
        wait(slot, False)


def _ffn(xp, xs, wts, *, tm):
    n_p, n_s = xp.shape[0] // tm, xs.shape[0] // tm
    assert xp.shape[0] == n_p * tm and xs.shape[0] == n_s * tm
    any_spec = pl.BlockSpec(memory_space=pl.ANY)
    return pl.pallas_call(
        functools.partial(_ffn_body, n_p=n_p, n_s=n_s, tm=tm),
        grid=(n_p + n_s,),
        in_specs=[any_spec, any_spec] + [_resident(w.shape) for w in wts],
        out_specs=[any_spec, any_spec],
        out_shape=[jax.ShapeDtypeStruct(xp.shape, F32), jax.ShapeDtypeStruct(xs.shape, F32)],
        scratch_shapes=[pltpu.VMEM((2, tm, D_MODEL), F32),
                        pltpu.VMEM((2, tm, D_MODEL), F32),
                        pltpu.VMEM((tm, D_FF), BF16),
                        pltpu.SemaphoreType.DMA((2,)),
                        pltpu.SemaphoreType.DMA((2,))],
        compiler_params=pltpu.CompilerParams(dimension_semantics=("arbitrary",),
                                             vmem_limit_bytes=VMEM_LIMIT),
        name="ffn",
    )(xp, xs, *wts)


N_S5_OUT = 6
N_S5_SCRATCH = 4


def _prep_body(lre_ref, lim_ref, logdt_ref, btre_ref, btim_ref, cre_ref, cim_ref, *refs):
    n_cast = (len(refs) - N_S5_OUT - N_S5_SCRATCH) // 2
    cast_in, refs = refs[:n_cast], refs[n_cast:]
    s5_out, refs = refs[:N_S5_OUT], refs[N_S5_OUT:]
    cast_out, s5_scratch = refs[:n_cast], refs[n_cast:]
    for src, dst in zip(cast_in, cast_out):
        dst[...] = src[...].astype(BF16)
    pl.when(pl.program_id(0) == 0)(functools.partial(
        _s5_matrices, lre_ref, lim_ref, logdt_ref, btre_ref, btim_ref, cre_ref, cim_ref, *s5_out, *s5_scratch))


def _s5_matrices(lre_ref, lim_ref, logdt_ref, btre_ref, btim_ref, cre_ref, cim_ref,
                 lam2r_ref, lam2i_ref, bc2_ref, cc_ref, ccl_ref, d0_ref, bscr, lbscr, cscr, clscr):
    lre = lre_ref[...]
    lim = lim_ref[...]
    grp = (lax.broadcasted_iota(jnp.int32, (N_GROUPS, N_GROUPS), 0)
           == lax.broadcasted_iota(jnp.int32, (N_GROUPS, N_GROUPS), 1))
    logdt = jnp.sum(jnp.where(grp, jnp.broadcast_to(logdt_ref[...], (N_GROUPS, N_GROUPS)), 0.0),
                    axis=1, keepdims=True)
    dt = jnp.exp(logdt)
    mag = jnp.exp(lre * dt)
    lbr = mag * jnp.cos(lim * dt)
    lbi = mag * jnp.sin(lim * dt)
    l2r = lbr * lbr - lbi * lbi
    l2i = 2.0 * (lbr * lbi)
    den = lre * lre + lim * lim
    fr = ((lbr - 1.0) * lre + lbi * lim) / den
    fi = (lbi * lre - (lbr - 1.0) * lim) / den
    frb, fib = fr[:, None, :], fi[:, None, :]
    lrb, lib = lbr[:, None, :], lbi[:, None, :]
    bbr = frb * btre_ref[...] - fib * btim_ref[...]
    bbi = frb * btim_ref[...] + fib * btre_ref[...]
    lbbr = lrb * bbr - lib * bbi
    lbbi = lrb * bbi + lib * bbr
    cre = cre_ref[...]
    cim = cim_ref[...]
    clr = cre * lrb - cim * lib
    cli = cre * lib + cim * lrb

    for scr in (bscr, lbscr, cscr, clscr):
        scr[...] = jnp.zeros_like(scr)
    for g in range(N_GROUPS):
        n, k = divmod(g, GROUPS_PER_HALF)
        chans = slice(k * SSM_H, (k + 1) * SSM_H)
        states = slice(k * STATE_P, (k + 1) * STATE_P)
        im_states = slice(S_HALF + k * STATE_P, S_HALF + (k + 1) * STATE_P)
        for scr, re, im in ((bscr, bbr, bbi), (lbscr, lbbr, lbbi), (cscr, cre, -cim), (clscr, clr, -cli)):
            scr[n, chans, states] = re[g]
            scr[n, chans, im_states] = im[g]
        lam2r_ref[n, :, states] = jnp.broadcast_to(l2r[g:g + 1, :], (SUBLANES, STATE_P))
        lam2i_ref[n, :, states] = jnp.broadcast_to(l2i[g:g + 1, :], (SUBLANES, STATE_P))
    for n in range(N_HALVES):
        bc2_ref[n, 0:U_HALF, :] = lbscr[n].astype(BF16)
        bc2_ref[n, U_HALF:, :] = bscr[n].astype(BF16)
        c_t = cscr[n].T
        cc_ref[n] = c_t.astype(BF16)
        ccl_ref[n] = clscr[n].T.astype(BF16)
        b_hi, b_lo = _split_bf16(bscr[n])
        c_hi, c_lo = _split_bf16(c_t)
        d0_ref[n] = (_dot(b_hi, c_hi) + _dot(b_hi, c_lo) + _dot(b_lo, c_hi)).astype(BF16)


PREP_STEPS = 2


def _prep(lam_re, lam_im, log_dt, b_re, b_im, c_re, c_im, cast):
    s5_in = (lam_re, lam_im, log_dt.reshape(1, N_GROUPS), b_re.transpose(0, 2, 1), b_im.transpose(0, 2, 1),
             c_re, c_im)
    s5_shapes = [jax.ShapeDtypeStruct((N_HALVES, SUBLANES, S_HALF), F32),
                 jax.ShapeDtypeStruct((N_HALVES, SUBLANES, S_HALF), F32),
                 jax.ShapeDtypeStruct((N_HALVES, 2 * U_HALF, 2 * S_HALF), BF16),
                 jax.ShapeDtypeStruct((N_HALVES, 2 * S_HALF, U_HALF), BF16),
                 jax.ShapeDtypeStruct((N_HALVES, 2 * S_HALF, U_HALF), BF16),
                 jax.ShapeDtypeStruct((N_HALVES, U_HALF, U_HALF), BF16)]
    assert len(s5_shapes) == N_S5_OUT
    for w in cast:
        assert w.shape[0] % (PREP_STEPS * BF16_ROWS) == 0, w.shape
    cast_specs = [pl.BlockSpec((w.shape[0] // PREP_STEPS, w.shape[1]), lambda i: (i, 0)) for w in cast]
    whole = lambda a: pl.BlockSpec(a.shape, lambda i: (0,) * len(a.shape))
    return pl.pallas_call(
        _prep_body,
        grid=(PREP_STEPS,),
        in_specs=[whole(a) for a in s5_in] + cast_specs,
        out_specs=[whole(a) for a in s5_shapes] + cast_specs,
        out_shape=s5_shapes + [jax.ShapeDtypeStruct(w.shape, BF16) for w in cast],
        scratch_shapes=[pltpu.VMEM((N_HALVES, U_HALF, 2 * S_HALF), F32)] * N_S5_SCRATCH,
        compiler_params=pltpu.CompilerParams(dimension_semantics=("arbitrary",)),
        name="prep",
    )(*s5_in, *cast)


def kernel(x_prompt, x_sample, cache_conv, state_ssm_re, state_ssm_im, norm_mix, w_in, conv_w, ssm_lam_re, ssm_lam_im, ssm_log_dt, ssm_b_re, ssm_b_im, ssm_c_re, ssm_c_im, ssm_d, w_glu, b_glu, w_out, norm_ffn, w_gate, w_up, w_down, norm_final):
    assert norm_mix.shape[0] == 1, "single-layer trunk"
    n_st = N_GROUPS * STATE_P
    lam2r, lam2i, bc2, cc, ccl, d0, w_in_b, w_glu_b, w_out_b = _prep(
        ssm_lam_re[0], ssm_lam_im[0], ssm_log_dt[0], ssm_b_re[0], ssm_b_im[0], ssm_c_re[0], ssm_c_im[0],
        cast=(w_in[0], w_glu[0], w_out[0]))
    mix_w = (norm_mix, w_in_b, conv_w[0][:, None, :], lam2r, lam2i, bc2, cc, ccl, d0, ssm_d, w_glu_b, b_glu,
             w_out_b)

    bp, lp, _ = x_prompt.shape
    bs, ls, _ = x_sample.shape

    x1p, convp, rep, imp, wg, wu, wd = _mixer(x_prompt, mix_w, cast=(w_gate[0], w_up[0], w_down[0]),
                                              nb=SUBLANES, tc=128)
    ffn_w = (norm_ffn, wg, wu, wd, norm_final.reshape(1, D_MODEL))
    def state_major(h):
        return h[0].transpose(1, 2, 0).reshape(n_st, h.shape[1])

    sample_state = (cache_conv, state_major(state_ssm_re), state_major(state_ssm_im))
    x1s, convs, res, ims = _mixer(x_sample, mix_w, sample_state, nb=128, tc=SUBLANES)

    yp, ys = _ffn(x1p.reshape(bp * lp, D_MODEL), x1s.reshape(bs * ls, D_MODEL), ffn_w, tm=1024)
    yp = yp.reshape(bp, lp, D_MODEL)
    ys = ys.reshape(bs, ls, D_MODEL)

    def seq_major(h):
        return h.reshape(N_GROUPS, STATE_P, h.shape[1]).transpose(2, 0, 1)[None]

    return (yp, ys, convp, seq_major(rep), seq_major(imp), convs, seq_major(res), seq_major(ims))
```

```python
import functools

import jax
import jax.numpy as jnp
from jax import lax
from jax.experimental import pallas as pl
from jax.experimental.pallas import tpu as pltpu

D_MODEL = 1024
W_CONV = 512
W_SSM = 512
CONV_W = 3
N_GROUPS = 32
SSM_H = 16
STATE_P = 64
D_FF = 2816
EPS = 1e-5

SUBLANES = 8
BF16_ROWS = 16
GROUPS_PER_HALF = 16
N_HALVES = N_GROUPS // GROUPS_PER_HALF
U_HALF = GROUPS_PER_HALF * SSM_H
S_HALF = GROUPS_PER_HALF * STATE_P
S_COLS = 2 * N_GROUPS * STATE_P
SCAN_LANES = 512
MXU_COLS = 256
FFN_CHUNKS = tuple((s, MXU_COLS) for s in range(0, D_FF, MXU_COLS))
VMEM_LIMIT = 60 * 1024 * 1024

BF16 = jnp.bfloat16
F32 = jnp.float32


def _rms(x, g):
    y = x * lax.rsqrt(jnp.mean(x * x, axis=-1, keepdims=True) + EPS)
    return y * g


def _dot(a, b):
    return jnp.dot(a, b, preferred_element_type=F32)


def _split_bf16(a):
    hi = a.astype(BF16)
    return hi, (a - hi.astype(F32)).astype(BF16)


def _tile_copies(hbm, buf, sem, step, slot, *, nb, tc, per_seq, to_vmem):
    copies = []
    for k in range(SUBLANES):
        if per_seq:
            h = hbm.at[k, pl.ds(step * tc, tc), :]
            v = buf.at[slot, :, k, :]
        else:
            h = hbm.at[pl.ds(step * nb, nb), k, :]
            v = buf.at[slot, k]
        src, dst = (h, v) if to_vmem else (v, h)
        copies.append(pltpu.make_async_copy(src, dst, sem.at[slot, k]))
    return copies


N_MIX_W = 13
W_IN, BC2, CC, CCL, D0, W_GLU, W_OUT = 1, 5, 6, 7, 8, 10, 12
STREAM_ORDER = (BC2, W_OUT, CC, CCL, D0, W_GLU)


def _mixer_body(x_hbm, *refs, nb, tc, n_steps, carry, n_cast):
    if carry:
        cache_ref = re0_ref = im0_ref = None
    else:
        cache_ref, re0_ref, im0_ref, *refs = refs
    wts, refs = list(refs[:N_MIX_W]), refs[N_MIX_W:]
    cast_in, refs = refs[:n_cast], refs[n_cast:]
    x1_hbm, convnew_hbm, hre_ref, him_ref, *refs = refs
    cast_out, refs = refs[:n_cast], refs[n_cast:]
    xbuf, obuf, gbuf, hb_odd, hb_prev, hstate, vbuf, insem, outsem, cachesem, *refs = refs
    if carry:
        w_copies = {}
    else:
        assert n_steps == 1
        *wbufs, wsem = refs
        win_hbm, win_buf = wts[W_IN], wbufs[0]
        u_cols, rest_cols = pl.ds(3 * W_CONV, W_SSM), pl.ds(0, 3 * W_CONV)
        w_copies = {"u": pltpu.make_async_copy(win_hbm.at[:, u_cols], win_buf.at[:, u_cols], wsem.at[0]),
                    W_IN: pltpu.make_async_copy(win_hbm.at[:, rest_cols], win_buf.at[:, rest_cols], wsem.at[1])}
        wts[W_IN] = win_buf
        for j, (k, buf) in enumerate(zip(STREAM_ORDER, wbufs[1:])):
            w_copies[k] = pltpu.make_async_copy(wts[k], buf, wsem.at[2 + j])
            wts[k] = buf
    (nmix_ref, win_ref, convw_ref, lam2r_ref, lam2i_ref, bc2_ref, cc_ref, ccl_ref, d0_ref, dskip_ref,
     wglu_ref, bglu_ref, wout_ref) = wts

    def wait_w(*keys):
        for k in keys:
            if k in w_copies:
                w_copies[k].wait()
    tm = nb * tc
    n_pair = tc // 2
    tmh = n_pair * nb

    i = pl.program_id(0)
    slot = lax.rem(i, 2)
    cp = functools.partial(_tile_copies, nb=nb, tc=tc, per_seq=carry)
    in_copies = functools.partial(cp, x_hbm, xbuf, insem, to_vmem=True)
    out_copies = functools.partial(cp, x1_hbm, obuf, outsem, to_vmem=False)

    def cache_copies(hbm, row0, to_vmem):
        copies = []
        for k in range(CONV_W - 1):
            h = hbm.at[0, pl.ds(0 if carry else i * nb, nb), k, :]
            v = vbuf.at[pl.ds(row0 + k * nb, nb), :]
            src, dst = (h, v) if to_vmem else (v, h)
            copies.append(pltpu.make_async_copy(src, dst, cachesem.at[int(to_vmem), k]))
        return copies

    @pl.when(i == 0)
    def _():
        for c in in_copies(0, 0):
            c.start()

    if not carry:
        for c in cache_copies(cache_ref, 0, True):
            c.start()
        for k in ("u", BC2, W_IN) + STREAM_ORDER[1:]:
            w_copies[k].start()
    for c in in_copies(i, slot):
        c.wait()
    if not carry:
        for c in cache_copies(cache_ref, 0, True):
            c.wait()

    @pl.when(i + 1 < n_steps)
    def _():
        for c in in_copies(i + 1, 1 - slot):
            c.start()

    @pl.when(i >= 2)
    def _():
        for c in out_copies(i - 2, slot):
            c.wait()

    if carry:
        @pl.when(i == 0)
        def _():
            vbuf[0:2 * nb, :] = jnp.zeros((2 * nb, W_CONV), F32)
            hstate[...] = jnp.zeros((nb, S_COLS), F32)
    else:
        for n in range(N_HALVES):
            hstate[:, 2 * n * S_HALF:(2 * n + 1) * S_HALF] = re0_ref[n * S_HALF:(n + 1) * S_HALF, :].T
            hstate[:, (2 * n + 1) * S_HALF:(2 * n + 2) * S_HALF] = im0_ref[n * S_HALF:(n + 1) * S_HALF, :].T

    x = xbuf[slot].reshape(tm, D_MODEL)
    inv_rms = lax.rsqrt(jnp.mean(x * x, axis=-1, keepdims=True) + EPS)
    xg = (x * nmix_ref[...]).astype(BF16)
    xn = (x * inv_rms * nmix_ref[...]).astype(BF16)

    wait_w("u", BC2, W_IN)
    u = inv_rms * _dot(xg, win_ref[:, 3 * W_CONV:])
    u_pairs = u.reshape(n_pair, 2, nb, W_SSM)
    u_e = u_pairs[:, 0].reshape(tmh, W_SSM)
    u_o = u_pairs[:, 1].reshape(tmh, W_SSM)
    ub_e = u_e.astype(BF16)
    ub_o = u_o.astype(BF16)
    for n in range(N_HALVES):
        cols = slice(n * U_HALF, (n + 1) * U_HALF)
        gbuf[:, 2 * n * S_HALF:(2 * n + 2) * S_HALF] = _dot(
            jnp.concatenate([ub_e[:, cols], ub_o[:, cols]], axis=1), bc2_ref[n])

    for src, dst in zip(cast_in, cast_out):
        dst[...] = src[...].astype(BF16)

    def scan_step(h, rows, lr, li, re_cols, im_cols):
        hr, hi = h
        return (lr * hr - li * hi + gbuf[rows, re_cols], lr * hi + li * hr + gbuf[rows, im_cols])

    def store_pair(buf, row0, pair, re_cols, im_cols):
        rows = slice(row0, row0 + BF16_ROWS)
        buf[rows, re_cols] = jnp.concatenate([h[0] for h in pair], axis=0).astype(BF16)
        buf[rows, im_cols] = jnp.concatenate([h[1] for h in pair], axis=0).astype(BF16)

    for n in range(N_HALVES):
        for c0 in range(0, S_HALF, SCAN_LANES):
            re_cols = slice(2 * n * S_HALF + c0, 2 * n * S_HALF + c0 + SCAN_LANES)
            im_cols = slice((2 * n + 1) * S_HALF + c0, (2 * n + 1) * S_HALF + c0 + SCAN_LANES)
            lr = lam2r_ref[n, :, c0:c0 + SCAN_LANES]
            li = lam2i_ref[n, :, c0:c0 + SCAN_LANES]
            step = functools.partial(scan_step, lr=lr, li=li, re_cols=re_cols, im_cols=im_cols)
            if nb == SUBLANES:
                h = (hstate[:, re_cols], hstate[:, im_cols])
                for k in range(0, n_pair, 2):
                    h0 = step(h, slice(k * nb, (k + 1) * nb))
                    h1 = step(h0, slice((k + 1) * nb, (k + 2) * nb))
                    store_pair(hb_odd, k * nb, (h0, h1), re_cols, im_cols)
                    store_pair(hb_prev, k * nb, (h, h0), re_cols, im_cols)
                    h = h1
                hstate[:, re_cols], hstate[:, im_cols] = h
            else:
                for r0 in range(0, nb, BF16_ROWS):
                    seqs = [slice(r0 + j * SUBLANES, r0 + (j + 1) * SUBLANES) for j in range(2)]
                    pair = [(hstate[sq, re_cols], hstate[sq, im_cols]) for sq in seqs]
                    for k in range(n_pair):
                        store_pair(hb_prev, k * nb + r0, pair, re_cols, im_cols)
                        pair = [step(h, slice(k * nb + sq.start, k * nb + sq.stop)) for h, sq in zip(pair, seqs)]
                        store_pair(hb_odd, k * nb + r0, pair, re_cols, im_cols)
                    for h, sq in zip(pair, seqs):
                        hstate[sq, re_cols], hstate[sq, im_cols] = h

    conv_chunks = []
    for c0 in range(0, W_CONV, MXU_COLS):
        cols = slice(c0, c0 + MXU_COLS)
        b_g = _dot(xn, win_ref[:, c0:c0 + MXU_COLS])
        c_g = _dot(xn, win_ref[:, W_CONV + c0:W_CONV + c0 + MXU_COLS])
        xc = _dot(xn, win_ref[:, 2 * W_CONV + c0:2 * W_CONV + c0 + MXU_COLS])
        v = c_g * xc
        vbuf[2 * nb:2 * nb + tm, cols] = v
        conv = (convw_ref[0, :, cols] * vbuf[0:tm, cols] + convw_ref[1, :, cols] * vbuf[nb:nb + tm, cols]
                + convw_ref[2, :, cols] * v)
        conv_chunks.append((b_g * conv).astype(BF16))
    conv_out = jnp.concatenate(conv_chunks, axis=-1)
    if carry:
        vbuf[0:2 * nb, :] = vbuf[tm:tm + 2 * nb, :]
    wait_w(*STREAM_ORDER[1:])
    x1 = x + _dot(conv_out, wout_ref[0:W_CONV, :])

    last = hstate[...]
    for n in range(N_HALVES):
        last_re = last[:, 2 * n * S_HALF:(2 * n + 1) * S_HALF]
        last_im = last[:, (2 * n + 1) * S_HALF:(2 * n + 2) * S_HALF]
        if carry:
            hre_ref[:, n * S_HALF:(n + 1) * S_HALF] = last_re
            him_ref[:, n * S_HALF:(n + 1) * S_HALF] = last_im
        else:
            hre_ref[n * S_HALF:(n + 1) * S_HALF, :] = last_re.T
            him_ref[n * S_HALF:(n + 1) * S_HALF, :] = last_im.T

    zs = []
    for n in range(N_HALVES):
        cols = slice(n * U_HALF, (n + 1) * U_HALF)
        st_cols = slice(2 * n * S_HALF, (2 * n + 2) * S_HALF)
        y_o = _dot(hb_odd[:, st_cols], cc_ref[n]) + dskip_ref[:, cols] * u_o[:, cols]
        y_e = (_dot(hb_prev[:, st_cols], ccl_ref[n]) + _dot(ub_e[:, cols], d0_ref[n])
               + dskip_ref[:, cols] * u_e[:, cols])
        y = jnp.stack([y_e.reshape(n_pair, nb, U_HALF), y_o.reshape(n_pair, nb, U_HALF)],
                      axis=1).reshape(tm, U_HALF)
        zs.append(jax.nn.gelu(y))
    zb = jnp.concatenate([z.astype(BF16) for z in zs], axis=-1)
    ssm_chunks = []
    for n, z in enumerate(zs):
        cols = slice(n * U_HALF, (n + 1) * U_HALF)
        gate = jax.nn.sigmoid(_dot(zb, wglu_ref[:, cols]) + bglu_ref[:, cols])
        ssm_chunks.append((z * gate).astype(BF16))
    ssm_out = jnp.concatenate(ssm_chunks, axis=-1)

    x1 = x1 + _dot(ssm_out, wout_ref[W_CONV:, :])

    obuf[slot] = x1.reshape(tc, nb, D_MODEL)
    for c in out_copies(i, slot):
        c.start()

    if not carry:
        for c in cache_copies(convnew_hbm, tm, False):
            c.start()
        for c in cache_copies(convnew_hbm, tm, False):
            c.wait()

    @pl.when(i == n_steps - 1)
    def _():
        if carry:
            for c in cache_copies(convnew_hbm, tm, False):
                c.start()
            for c in cache_copies(convnew_hbm, tm, False):
                c.wait()
        if n_steps >= 2:
            for c in out_copies(i - 1, 1 - slot):
                c.wait()
        for c in out_copies(i, slot):
            c.wait()


def _resident(shape):
    nd = len(shape)
    return pl.BlockSpec(shape, lambda i: (0,) * nd, pipeline_mode=pl.Buffered(1))


def _mixer(x, wts, state=None, cast=(), *, nb, tc):
    n_seq, n_time, _ = x.shape
    carry = state is None
    if carry:
        assert n_seq == nb == SUBLANES and n_time % tc == 0 and tc % 4 == 0
        n_steps = n_time // tc
    else:
        assert n_time == tc == SUBLANES and n_seq % nb == 0 and nb % BF16_ROWS == 0
        n_steps = n_seq // nb
    tm = nb * tc
    n_st = N_GROUPS * STATE_P
    body = functools.partial(_mixer_body, nb=nb, tc=tc, n_steps=n_steps, carry=carry, n_cast=len(cast))
    any_spec = pl.BlockSpec(memory_space=pl.ANY)
    if carry:
        st_spec = pl.BlockSpec((nb, n_st), lambda i: (0, 0))
        st_shape = jax.ShapeDtypeStruct((n_seq, n_st), F32)
    else:
        st_spec = pl.BlockSpec((n_st, nb), lambda i: (0, i))
        st_shape = jax.ShapeDtypeStruct((n_st, n_seq), F32)
    state_specs = [] if carry else [any_spec, st_spec, st_spec]
    assert len(wts) == N_MIX_W
    streamed = () if carry else (W_IN,) + STREAM_ORDER
    w_specs = [any_spec if k in streamed else _resident(w.shape) for k, w in enumerate(wts)]
    w_scratch = [pltpu.VMEM(wts[k].shape, BF16) for k in streamed]
    if streamed:
        w_scratch.append(pltpu.SemaphoreType.DMA((1 + len(streamed),)))
    for w in cast:
        assert w.shape[0] % (n_steps * BF16_ROWS) == 0, w.shape
    cast_specs = [pl.BlockSpec((w.shape[0] // n_steps, w.shape[1]), lambda i: (i, 0)) for w in cast]
    return pl.pallas_call(
        body,
        grid=(n_steps,),
        in_specs=[any_spec] + state_specs + w_specs + cast_specs,
        out_specs=[any_spec, any_spec, st_spec, st_spec] + cast_specs,
        out_shape=[jax.ShapeDtypeStruct(x.shape, F32),
                   jax.ShapeDtypeStruct((1, n_seq, CONV_W - 1, W_CONV), F32),
                   st_shape, st_shape]
                  + [jax.ShapeDtypeStruct(w.shape, BF16) for w in cast],
        scratch_shapes=[pltpu.VMEM((2, tc, nb, D_MODEL), F32),
                        pltpu.VMEM((2, tc, nb, D_MODEL), F32),
                        pltpu.VMEM((tm // 2, S_COLS), F32),
                        pltpu.VMEM((tm // 2, S_COLS), BF16),
                        pltpu.VMEM((tm // 2, S_COLS), BF16),
                        pltpu.VMEM((nb, S_COLS), F32),
                        pltpu.VMEM((2 * nb + tm, W_CONV), F32),
                        pltpu.SemaphoreType.DMA((2, SUBLANES)),
                        pltpu.SemaphoreType.DMA((2, SUBLANES)),
                        pltpu.SemaphoreType.DMA((2, CONV_W - 1))] + w_scratch,
        compiler_params=pltpu.CompilerParams(dimension_semantics=("arbitrary",),
                                             vmem_limit_bytes=VMEM_LIMIT),
        name="mixer_carry" if carry else "mixer_step",
    )(x, *(() if carry else state), *wts, *cast)


def _ffn_body(xp_hbm, xs_hbm, nffn_ref, wg_ref, wu_ref, wd_ref, nfin_ref, yp_hbm, ys_hbm,
              xbuf, obuf, abuf, insem, outsem, *, n_p, n_s, tm):
    i = pl.program_id(0)
    n_steps = n_p + n_s
    slot = lax.rem(i, 2)

    def tile_copy(step, buf_slot, to_vmem, first):
        hbm = (xp_hbm if first else xs_hbm) if to_vmem else (yp_hbm if first else ys_hbm)
        h = hbm.at[pl.ds((step if first else step - n_p) * tm, tm), :]
        if to_vmem:
            return pltpu.make_async_copy(h, xbuf.at[buf_slot], insem.at[buf_slot])
        return pltpu.make_async_copy(obuf.at[buf_slot], h, outsem.at[buf_slot])

    def start(step, buf_slot, to_vmem):
        pl.when(step < n_p)(lambda: tile_copy(step, buf_slot, to_vmem, True).start())
        pl.when(step >= n_p)(lambda: tile_copy(step, buf_slot, to_vmem, False).start())

    def wait(buf_slot, to_vmem):
        tile_copy(0, buf_slot, to_vmem, True).wait()

    pl.when(i == 0)(lambda: start(0, 0, True))
    wait(slot, True)
    pl.when(i + 1 < n_steps)(lambda: start(i + 1, 1 - slot, True))
    pl.when(i >= 2)(lambda: wait(slot, False))

    x = xbuf[slot]
    hn = _rms(x, nffn_ref[...]).astype(BF16)
    for s, w in FFN_CHUNKS:
        g = _dot(hn, wg_ref[:, s:s + w])
        up = _dot(hn, wu_ref[:, s:s + w])
        abuf[:, s:s + w] = (g * jax.nn.sigmoid(g) * up).astype(BF16)
    acc = x + _dot(abuf[...], wd_ref[...])
    obuf[slot] = _rms(acc, nfin_ref[...])
    start(i, slot, False)

    @pl.when(i == n_steps - 1)
    def _():
        if n_steps >= 2:
            wait(1 - slot, False)
        wait(slot, False)


def _ffn(xp, xs, wts, *, tm):
    n_p, n_s = xp.shape[0] // tm, xs.shape[0] // tm
    assert xp.shape[0] == n_p * tm and xs.shape[0] == n_s * tm
    any_spec = pl.BlockSpec(memory_space=pl.ANY)
    return pl.pallas_call(
        functools.partial(_ffn_body, n_p=n_p, n_s=n_s, tm=tm),
        grid=(n_p + n_s,),
        in_specs=[any_spec, any_spec] + [_resident(w.shape) for w in wts],
        out_specs=[any_spec, any_spec],
        out_shape=[jax.ShapeDtypeStruct(xp.shape, F32), jax.ShapeDtypeStruct(xs.shape, F32)],
        scratch_shapes=[pltpu.VMEM((2, tm, D_MODEL), F32),
                        pltpu.VMEM((2, tm, D_MODEL), F32),
                        pltpu.VMEM((tm, D_FF), BF16),
                        pltpu.SemaphoreType.DMA((2,)),
                        pltpu.SemaphoreType.DMA((2,))],
        compiler_params=pltpu.CompilerParams(dimension_semantics=("arbitrary",),
                                             vmem_limit_bytes=VMEM_LIMIT),
        name="ffn",
    )(xp, xs, *wts)


N_S5_OUT = 6
N_S5_SCRATCH = 4


def _prep_body(lre_ref, lim_ref, logdt_ref, btre_ref, btim_ref, cre_ref, cim_ref, *refs):
    n_cast = (len(refs) - N_S5_OUT - N_S5_SCRATCH) // 2
    cast_in, refs = refs[:n_cast], refs[n_cast:]
    s5_out, refs = refs[:N_S5_OUT], refs[N_S5_OUT:]
    cast_out, s5_scratch = refs[:n_cast], refs[n_cast:]
    for src, dst in zip(cast_in, cast_out):
        dst[...] = src[...].astype(BF16)
    pl.when(pl.program_id(0) == 0)(functools.partial(
        _s5_matrices, lre_ref, lim_ref, logdt_ref, btre_ref, btim_ref, cre_ref, cim_ref, *s5_out, *s5_scratch))


def _s5_matrices(lre_ref, lim_ref, logdt_ref, btre_ref, btim_ref, cre_ref, cim_ref,
                 lam2r_ref, lam2i_ref, bc2_ref, cc_ref, ccl_ref, d0_ref, bscr, lbscr, cscr, clscr):
    lre = lre_ref[...]
    lim = lim_ref[...]
    grp = (lax.broadcasted_iota(jnp.int32, (N_GROUPS, N_GROUPS), 0)
           == lax.broadcasted_iota(jnp.int32, (N_GROUPS, N_GROUPS), 1))
    logdt = jnp.sum(jnp.where(grp, jnp.broadcast_to(logdt_ref[...], (N_GROUPS, N_GROUPS)), 0.0),
                    axis=1, keepdims=True)
    dt = jnp.exp(logdt)
    mag = jnp.exp(lre * dt)
    lbr = mag * jnp.cos(lim * dt)
    lbi = mag * jnp.sin(lim * dt)
    l2r = lbr * lbr - lbi * lbi
    l2i = 2.0 * (lbr * lbi)
    den = lre * lre + lim * lim
    fr = ((lbr - 1.0) * lre + lbi * lim) / den
    fi = (lbi * lre - (lbr - 1.0) * lim) / den
    frb, fib = fr[:, None, :], fi[:, None, :]
    lrb, lib = lbr[:, None, :], lbi[:, None, :]
    bbr = frb * btre_ref[...] - fib * btim_ref[...]
    bbi = frb * btim_ref[...] + fib * btre_ref[...]
    lbbr = lrb * bbr - lib * bbi
    lbbi = lrb * bbi + lib * bbr
    cre = cre_ref[...]
    cim = cim_ref[...]
    clr = cre * lrb - cim * lib
    cli = cre * lib + cim * lrb

    for scr in (bscr, lbscr, cscr, clscr):
        scr[...] = jnp.zeros_like(scr)
    for g in range(N_GROUPS):
        n, k = divmod(g, GROUPS_PER_HALF)
        chans = slice(k * SSM_H, (k + 1) * SSM_H)
        states = slice(k * STATE_P, (k + 1) * STATE_P)
        im_states = slice(S_HALF + k * STATE_P, S_HALF + (k + 1) * STATE_P)
        for scr, re, im in ((bscr, bbr, bbi), (lbscr, lbbr, lbbi), (cscr, cre, -cim), (clscr, clr, -cli)):
            scr[n, chans, states] = re[g]
            scr[n, chans, im_states] = im[g]
        lam2r_ref[n, :, states] = jnp.broadcast_to(l2r[g:g + 1, :], (SUBLANES, STATE_P))
        lam2i_ref[n, :, states] = jnp.broadcast_to(l2i[g:g + 1, :], (SUBLANES, STATE_P))
    for n in range(N_HALVES):
        bc2_ref[n, 0:U_HALF, :] = lbscr[n].astype(BF16)
        bc2_ref[n, U_HALF:, :] = bscr[n].astype(BF16)
        c_t = cscr[n].T
        cc_ref[n] = c_t.astype(BF16)
        ccl_ref[n] = clscr[n].T.astype(BF16)
        b_hi, b_lo = _split_bf16(bscr[n])
        c_hi, c_lo = _split_bf16(c_t)
        d0_ref[n] = (_dot(b_hi, c_hi) + _dot(b_hi, c_lo) + _dot(b_lo, c_hi)).astype(BF16)


PREP_STEPS = 2


def _prep(lam_re, lam_im, log_dt, b_re, b_im, c_re, c_im, cast):
    s5_in = (lam_re, lam_im, log_dt.reshape(1, N_GROUPS), b_re.transpose(0, 2, 1), b_im.transpose(0, 2, 1),
             c_re, c_im)
    s5_shapes = [jax.ShapeDtypeStruct((N_HALVES, SUBLANES, S_HALF), F32),
                 jax.ShapeDtypeStruct((N_HALVES, SUBLANES, S_HALF), F32),
                 jax.ShapeDtypeStruct((N_HALVES, 2 * U_HALF, 2 * S_HALF), BF16),
                 jax.ShapeDtypeStruct((N_HALVES, 2 * S_HALF, U_HALF), BF16),
                 jax.ShapeDtypeStruct((N_HALVES, 2 * S_HALF, U_HALF), BF16),
                 jax.ShapeDtypeStruct((N_HALVES, U_HALF, U_HALF), BF16)]
    assert len(s5_shapes) == N_S5_OUT
    for w in cast:
        assert w.shape[0] % (PREP_STEPS * BF16_ROWS) == 0, w.shape
    cast_specs = [pl.BlockSpec((w.shape[0] // PREP_STEPS, w.shape[1]), lambda i: (i, 0)) for w in cast]
    whole = lambda a: pl.BlockSpec(a.shape, lambda i: (0,) * len(a.shape))
    return pl.pallas_call(
        _prep_body,
        grid=(PREP_STEPS,),
        in_specs=[whole(a) for a in s5_in] + cast_specs,
        out_specs=[whole(a) for a in s5_shapes] + cast_specs,
        out_shape=s5_shapes + [jax.ShapeDtypeStruct(w.shape, BF16) for w in cast],
        scratch_shapes=[pltpu.VMEM((N_HALVES, U_HALF, 2 * S_HALF), F32)] * N_S5_SCRATCH,
        compiler_params=pltpu.CompilerParams(dimension_semantics=("arbitrary",)),
        name="prep",
    )(*s5_in, *cast)


def kernel(x_prompt, x_sample, cache_conv, state_ssm_re, state_ssm_im, norm_mix, w_in, conv_w, ssm_lam_re, ssm_lam_im, ssm_log_dt, ssm_b_re, ssm_b_im, ssm_c_re, ssm_c_im, ssm_d, w_glu, b_glu, w_out, norm_ffn, w_gate, w_up, w_down, norm_final):
    assert norm_mix.shape[0] == 1, "single-layer trunk"
    n_st = N_GROUPS * STATE_P
    lam2r, lam2i, bc2, cc, ccl, d0, w_in_b, w_glu_b, w_out_b = _prep(
        ssm_lam_re[0], ssm_lam_im[0], ssm_log_dt[0], ssm_b_re[0], ssm_b_im[0], ssm_c_re[0], ssm_c_im[0],
        cast=(w_in[0], w_glu[0], w_out[0]))
    mix_w = (norm_mix, w_in_b, conv_w[0][:, None, :], lam2r, lam2i, bc2, cc, ccl, d0, ssm_d, w_glu_b, b_glu,
             w_out_b)

    bp, lp, _ = x_prompt.shape
    bs, ls, _ = x_sample.shape

    x1p, convp, rep, imp, wg, wu, wd = _mixer(x_prompt, mix_w, cast=(w_gate[0], w_up[0], w_down[0]),
                                              nb=SUBLANES, tc=128)
    ffn_w = (norm_ffn, wg, wu, wd, norm_final.reshape(1, D_MODEL))
    def state_major(h):
        return h[0].transpose(1, 2, 0).reshape(n_st, h.shape[1])

    sample_state = (cache_conv, state_major(state_ssm_re), state_major(state_ssm_im))
    x1s, convs, res, ims = _mixer(x_sample, mix_w, sample_state, nb=128, tc=SUBLANES)

    yp, ys = _ffn(x1p.reshape(bp * lp, D_MODEL), x1s.reshape(bs * ls, D_MODEL), ffn_w, tm=1024)
    yp = yp.reshape(bp, lp, D_MODEL)
    ys = ys.reshape(bs, ls, D_MODEL)

    def seq_major(h):
        return h.reshape(N_GROUPS, STATE_P, h.shape[1]).transpose(2, 0, 1)[None]

    return (yp, ys, convp, rep.reshape(1, bp, N_GROUPS, STATE_P), imp.reshape(1, bp, N_GROUPS, STATE_P),
            convs, seq_major(res), seq_major(ims))
```
